```python
import jax, jax.numpy as jnp
from jax import lax
import numpy as np

D_MODEL = 1024
BATCH = 2
SEQ = 8192
DEPTH = 1

PLE_DIM = 256
EPS = 1e-6
HG_HEADS = 4
HG_DK = 128
HG_DV = 128
HG_WIDTH = HG_HEADS * HG_DK
HG_CHUNK = 64
RET_HEADS = 4
RET_DK = 128
RET_DV = 256
RET_QK_WIDTH = RET_HEADS * RET_DK
RET_V_WIDTH = RET_HEADS * RET_DV
RET_CHUNK = 128
ROPE_BASE = 10000.0
IN_SPLITS = [HG_WIDTH, HG_WIDTH, HG_WIDTH, HG_WIDTH,
             RET_QK_WIDTH, RET_QK_WIDTH, RET_V_WIDTH, RET_V_WIDTH,
             D_MODEL, D_MODEL]
IN_TOTAL = sum(IN_SPLITS)
N_GROUPS = 4
EXPERTS_PER_GROUP = 8
N_EXPERTS = N_GROUPS * EXPERTS_PER_GROUP
TOP_K_IN_GROUP = 2
D_EXPERT = 256

kernel_name = 'hybrid_hgrn2_retention_hmoe_block'


def rmsnorm(x, g):
    xf = x.astype(jnp.float32)
    y = xf * lax.rsqrt(jnp.mean(xf * xf, axis=-1, keepdims=True) + EPS)
    return (y * g.astype(jnp.float32)).astype(x.dtype)


def _chunks(t, n_heads, chunk):
    b, s, w = t.shape
    return t.reshape(b, s // chunk, chunk, n_heads, w // n_heads).transpose(1, 0, 3, 2, 4)


def _unchunks(t):
    nc, b, h, c, d = t.shape
    return t.transpose(1, 0, 3, 2, 4).reshape(b, nc * c, h, d)


def hgrn2_lower_bound(lb_logits, layer):
    sm = jax.nn.softmax(lb_logits.astype(jnp.float32), axis=0)
    return jnp.cumsum(sm, axis=0)[layer]


def hgrn2_mix(q, f_logit, i, g, lb, norm_g):
    B, S, _ = q.shape
    C = HG_CHUNK
    qf = jax.nn.silu(q.astype(jnp.float32))
    z = f_logit.astype(jnp.float32)
    log_f = jnp.logaddexp(jnp.log(lb), jnp.log1p(-lb) + jax.nn.log_sigmoid(z))
    k = -jnp.expm1(log_f)
    qc = _chunks(qf, HG_HEADS, C)
    kc = _chunks(k, HG_HEADS, C)
    lc = _chunks(log_f, HG_HEADS, C)
    vc = _chunks(i.astype(jnp.float32), HG_HEADS, C)
    causal = jnp.tril(jnp.ones((C, C), dtype=bool))[:, :, None]

    def step(state, inp):
        q_, k_, lf, v_ = inp
        b = jnp.cumsum(lf, axis=2)
        diff = b[:, :, :, None, :] - b[:, :, None, :, :]
        decay = jnp.exp(jnp.where(causal, diff, -jnp.inf))
        scores = jnp.einsum('bhtk,bhsk,bhtsk->bhts', q_, k_, decay)
        o = (jnp.einsum('bhts,bhsv->bhtv', scores, v_)
             + jnp.einsum('bhtk,bhkv->bhtv', q_ * jnp.exp(b), state))
        b_last = b[:, :, -1, :]
        new_state = (jnp.exp(b_last)[..., None] * state
                     + jnp.einsum('bhsk,bhsv->bhkv', k_ * jnp.exp(b_last[:, :, None, :] - b), v_))
        return new_state, o

    s0 = jnp.zeros((B, HG_HEADS, HG_DK, HG_DV), jnp.float32)
    _, o = lax.scan(step, s0, (qc, kc, lc, vc))
    o = rmsnorm(_unchunks(o), norm_g).reshape(B, S, HG_HEADS * HG_DV)
    return (o * jax.nn.silu(g.astype(jnp.float32))).astype(q.dtype)


def rotary(t, pos):
    d = t.shape[-1]
    inv = ROPE_BASE ** (-jnp.arange(0, d, 2, dtype=jnp.float32) / d)
    ang = pos[:, None] * inv[None, :]
    cos = jnp.cos(ang)[None, :, None, :]
    sin = jnp.sin(ang)[None, :, None, :]
    t1, t2 = t[..., : d // 2], t[..., d // 2:]
    return jnp.concatenate([t1 * cos - t2 * sin, t1 * sin + t2 * cos], axis=-1)


def retention_mix(q, k, v, g, pos, norm_g):
    B, S, _ = q.shape
    C = RET_CHUNK
    qh = rotary(q.astype(jnp.float32).reshape(B, S, RET_HEADS, RET_DK), pos) * (RET_DK ** -0.5)
    kh = rotary(k.astype(jnp.float32).reshape(B, S, RET_HEADS, RET_DK), pos)
    qc = _chunks(qh.reshape(B, S, RET_QK_WIDTH), RET_HEADS, C)
    kc = _chunks(kh.reshape(B, S, RET_QK_WIDTH), RET_HEADS, C)
    vc = _chunks(v.astype(jnp.float32), RET_HEADS, C)
    lg = jnp.log1p(-jnp.exp2(-5.0 - jnp.arange(RET_HEADS, dtype=jnp.float32)))
    idx = jnp.arange(C, dtype=jnp.float32)
    rel = idx[:, None] - idx[None, :]
    intra = jnp.exp(jnp.where(rel >= 0, rel[None] * lg[:, None, None], -jnp.inf))
    inter = jnp.exp((idx[None, :] + 1.0) * lg[:, None])
    to_state = jnp.exp((C - 1.0 - idx[None, :]) * lg[:, None])
    chunk_decay = jnp.exp(C * lg)

    def step(R, inp):
        q_, k_, v_ = inp
        att = jnp.einsum('bhtk,bhsk->bhts', q_, k_) * intra
        o = (jnp.einsum('bhts,bhsv->bhtv', att, v_)
             + jnp.einsum('bhtk,bhkv->bhtv', q_, R) * inter[:, :, None])
        R = (chunk_decay[:, None, None] * R
             + jnp.einsum('bhsk,bhsv->bhkv', k_ * to_state[:, :, None], v_))
        return R, o

    r0 = jnp.zeros((B, RET_HEADS, RET_DK, RET_DV), jnp.float32)
    _, o = lax.scan(step, r0, (qc, kc, vc))
    o = rmsnorm(_unchunks(o), norm_g).reshape(B, S, RET_V_WIDTH)
    return (o * jax.nn.silu(g.astype(jnp.float32))).astype(q.dtype)


def hier_moe(h, w_rg, b_rg, w_re, b_re, w_gate, w_up, w_down):
    T, D = h.shape
    g_logits = (h @ w_rg).astype(jnp.float32) + b_rg.astype(jnp.float32)
    g_prob = jax.nn.softmax(g_logits, axis=-1)
    g_idx = jnp.argmax(g_logits, axis=-1)
    g_w = jnp.take_along_axis(g_prob, g_idx[:, None], axis=-1)[:, 0]
    e_logits = ((h @ w_re).astype(jnp.float32) + b_re.astype(jnp.float32)).reshape(T, N_GROUPS, EXPERTS_PER_GROUP)
    e_logits = jnp.take_along_axis(e_logits, g_idx[:, None, None], axis=1)[:, 0]
    top_l, top_i = lax.top_k(e_logits, TOP_K_IN_GROUP)
    w = jax.nn.softmax(top_l, axis=-1) * g_w[:, None]
    eid = (g_idx[:, None] * EXPERTS_PER_GROUP + top_i).reshape(-1)
    order = jnp.argsort(eid)
    xs = h[order // TOP_K_IN_GROUP]
    sizes = jnp.bincount(eid, length=N_EXPERTS).astype(jnp.int32)
    a = lax.ragged_dot(xs, w_gate, sizes)
    u = lax.ragged_dot(xs, w_up, sizes)
    ys = lax.ragged_dot(jax.nn.silu(a) * u, w_down, sizes)
    y = jnp.zeros_like(ys).at[order].set(ys).reshape(T, TOP_K_IN_GROUP, D)
    return jnp.einsum('tkd,tk->td', y, w.astype(y.dtype))


def setup_inputs(seed: int = 0) -> dict:
    key = jax.random.key(seed)
    ks = jax.random.split(key, 24)
    f32 = jnp.float32

    def nrm(k, shape, fan):
        return jax.random.normal(k, shape, f32) * (fan ** -0.5)

    def gain(k, shape):
        return 1.0 + 0.02 * jax.random.normal(k, shape, f32)

    return {
        'x': jax.random.normal(ks[0], (BATCH, SEQ, D_MODEL), f32),
        'p': jax.random.normal(ks[1], (DEPTH, BATCH, SEQ, PLE_DIM), f32),
        'mix_norm': gain(ks[2], (DEPTH, D_MODEL)),
        'w_in': nrm(ks[3], (DEPTH, D_MODEL, IN_TOTAL), D_MODEL),
        'hg_lb_logits': 0.5 * jax.random.normal(ks[4], (DEPTH + 1, HG_WIDTH), f32),
        'hg_norm': gain(ks[5], (DEPTH, HG_DV)),
        'ret_norm': gain(ks[6], (DEPTH, RET_DV)),
        'w_branch_a': nrm(ks[7], (DEPTH, HG_WIDTH, D_MODEL), HG_WIDTH),
        'w_branch_b': nrm(ks[8], (DEPTH, RET_V_WIDTH, D_MODEL), RET_V_WIDTH),
        'w_out': nrm(ks[9], (DEPTH, D_MODEL, D_MODEL), D_MODEL),
        'ffn_norm': gain(ks[10], (DEPTH, D_MODEL)),
        'w_router_group': nrm(ks[11], (DEPTH, D_MODEL, N_GROUPS), D_MODEL),
        'b_router_group': 0.01 * jax.random.normal(ks[12], (DEPTH, N_GROUPS), f32),
        'w_router_expert': nrm(ks[13], (DEPTH, D_MODEL, N_EXPERTS), D_MODEL),
        'b_router_expert': 0.01 * jax.random.normal(ks[14], (DEPTH, N_EXPERTS), f32),
        'w_expert_gate': nrm(ks[15], (DEPTH, N_EXPERTS, D_MODEL, D_EXPERT), D_MODEL),
        'w_expert_up': nrm(ks[16], (DEPTH, N_EXPERTS, D_MODEL, D_EXPERT), D_MODEL),
        'w_expert_down': nrm(ks[17], (DEPTH, N_EXPERTS, D_EXPERT, D_MODEL), D_EXPERT),
        'ple_norm': gain(ks[18], (DEPTH, D_MODEL)),
        'w_ple_gate': nrm(ks[19], (DEPTH, D_MODEL, D_MODEL), D_MODEL),
        'w_ple_proj': nrm(ks[20], (DEPTH, PLE_DIM, D_MODEL), PLE_DIM),
        'final_norm': gain(ks[21], (D_MODEL,)),
    }


def reference(x, p, mix_norm, w_in, hg_lb_logits, hg_norm, ret_norm, w_branch_a, w_branch_b, w_out,
              ffn_norm, w_router_group, b_router_group, w_router_expert, b_router_expert,
              w_expert_gate, w_expert_up, w_expert_down, ple_norm, w_ple_gate, w_ple_proj, final_norm):
    B, S, D = x.shape
    pos = jnp.arange(S, dtype=jnp.float32)
    split_at = [int(v) for v in np.cumsum(IN_SPLITS)[:-1]]
    for layer in range(DEPTH):
        h = rmsnorm(x, mix_norm[layer])
        proj = h @ w_in[layer]
        hq, hf, hi, hg, rq, rk, rv, rg, ga, gb = jnp.split(proj, split_at, axis=-1)
        lb = hgrn2_lower_bound(hg_lb_logits, layer)
        ya = hgrn2_mix(hq, hf, hi, hg, lb, hg_norm[layer])
        yb = retention_mix(rq, rk, rv, rg, pos, ret_norm[layer])
        merged = (jax.nn.sigmoid(ga) * (ya @ w_branch_a[layer])
                  + jax.nn.sigmoid(gb) * (yb @ w_branch_b[layer]))
        x = x + merged @ w_out[layer]
        h2 = rmsnorm(x, ffn_norm[layer]).reshape(B * S, D)
        x = x + hier_moe(h2, w_router_group[layer], b_router_group[layer], w_router_expert[layer],
                         b_router_expert[layer], w_expert_gate[layer], w_expert_up[layer],
                         w_expert_down[layer]).reshape(B, S, D)
        hp = rmsnorm(x, ple_norm[layer])
        x = x + jax.nn.sigmoid(hp @ w_ple_gate[layer]) * (p[layer] @ w_ple_proj[layer])
    return rmsnorm(x, final_norm)
```

```python
import functools

import jax
import jax.numpy as jnp
from jax import lax
from jax.experimental import pallas as pl
from jax.experimental.pallas import tpu as pltpu

F32 = jnp.float32
BF16 = jnp.bfloat16
I32 = jnp.int32
U32 = jnp.uint32

EPS = 1e-6
D_MODEL = 1024
PLE_DIM = 256
HG_HEADS = 4
HG_DK = 128
HG_WIDTH = HG_HEADS * HG_DK
RET_HEADS = 4
RET_DK = 128
RET_DV = 256
ROPE_BASE = 10000.0
IN_TOTAL = 7168
N_GROUPS = 4
EXPERTS_PER_GROUP = 8
N_EXPERTS = 32
D_EXPERT = 256

COL_HQ, COL_HF, COL_HI, COL_HG = 0, 4, 8, 12
COL_RQ, COL_RK = 16, 20
COL_RV, COL_RG = 24, 32
COL_GA, COL_GB = 5, 6

LANES = 128
VMEM_LIMIT = 56 * 1024 * 1024

HG_CHUNK = 64
HG_SUB = 8
RET_CHUNK = 128
SEQ_TILE = 1024
TOK_TILE = 512
EXP_TILE = 256
SCAT_TILE = 1024


def _cparams(sem):
    return pltpu.CompilerParams(dimension_semantics=sem, vmem_limit_bytes=VMEM_LIMIT)


def _rms(x, g):
    return x * lax.rsqrt(jnp.mean(x * x, axis=-1, keepdims=True) + EPS) * g


def _sigmoid(x):
    return 1.0 / (1.0 + jnp.exp(-x))


def _silu(x):
    return x * _sigmoid(x)


def _split3(x):
    hi = x.astype(BF16)
    r1 = x - hi.astype(F32)
    mid = r1.astype(BF16)
    lo = (r1 - mid.astype(F32)).astype(BF16)
    return hi, mid, lo


def _dot(a, b):
    return jnp.dot(a, b, preferred_element_type=F32)


def _dot_nt(a, b):
    return lax.dot_general(a, b, (((1,), (1,)), ((), ())), preferred_element_type=F32)


def _dot_tn(a, b):
    return lax.dot_general(a, b, (((0,), (0,)), ((), ())), preferred_element_type=F32)


def _inproj_kernel(x_ref, g_ref, w_ref, proj_ref, hf_ref):
    h = _rms(x_ref[...], g_ref[...]).astype(BF16)
    tn = HG_WIDTH
    for j in range(IN_TOTAL // tn):
        acc = _dot(h, w_ref[:, j * tn:(j + 1) * tn])
        proj_ref[:, j * tn:(j + 1) * tn] = acc.astype(BF16)
        if j * tn == COL_HF * LANES:
            hf_ref[...] = acc


def _inproj(x2d, gain, w_bf16):
    t = x2d.shape[0]
    return pl.pallas_call(
        _inproj_kernel,
        grid=(t // TOK_TILE,),
        in_specs=[
            pl.BlockSpec((TOK_TILE, D_MODEL), lambda i: (i, 0)),
            pl.BlockSpec((1, D_MODEL), lambda i: (0, 0)),
            pl.BlockSpec((D_MODEL, IN_TOTAL), lambda i: (0, 0), pipeline_mode=pl.Buffered(1)),
        ],
        out_specs=[
            pl.BlockSpec((TOK_TILE, IN_TOTAL), lambda i: (i, 0)),
            pl.BlockSpec((TOK_TILE, HG_WIDTH), lambda i: (i, 0)),
        ],
        out_shape=[
            jax.ShapeDtypeStruct((t, IN_TOTAL), BF16),
            jax.ShapeDtypeStruct((t, HG_WIDTH), F32),
        ],
        compiler_params=_cparams(("arbitrary",)),
        name="inproj",
    )(x2d, gain, w_bf16)


def _hgrn_kernel(lbl_ref, q_ref, f_ref, i_ref, g_ref, ng_ref, o_ref, st_ref):
    c = HG_CHUNK
    nsub = c // HG_SUB

    @pl.when(pl.program_id(2) == 0)
    def _():
        st_ref[...] = jnp.zeros_like(st_ref)

    logits = lbl_ref[...]
    e = jnp.exp(logits - jnp.max(logits, axis=0, keepdims=True))
    lb = e[0:1] / jnp.sum(e, axis=0, keepdims=True)
    log_lb = jnp.log(lb)
    log_1m = jnp.log1p(-lb)
    ng = ng_ref[...]

    row = lax.broadcasted_iota(I32, (c, c), 0)
    col = lax.broadcasted_iota(I32, (c, c), 1)
    tri = jnp.where(row >= col, 1.0, 0.0).astype(BF16)
    row_k = lax.broadcasted_iota(I32, (c, HG_DK), 0)
    sub_s = lax.broadcasted_iota(I32, (HG_SUB, 1), 0)
    sub_row = lax.broadcasted_iota(I32, (HG_SUB, HG_DK), 0)

    def chunk(ci, carry):
        r0 = pl.multiple_of(ci * c, c)
        z = f_ref[pl.ds(r0, c), :]
        logsig = jnp.minimum(z, 0.0) - jnp.log1p(jnp.exp(-jnp.abs(z)))
        bb = log_1m + logsig
        logf = jnp.maximum(log_lb, bb) + jnp.log1p(jnp.exp(-jnp.abs(log_lb - bb)))
        kk = jnp.exp(log_1m + logsig - z)
        q = _silu(q_ref[pl.ds(r0, c), :].astype(F32))
        v = i_ref[pl.ds(r0, c), :].astype(F32)
        v16 = v.astype(BF16)

        hi, mid, lo = _split3(logf)
        b = (_dot(tri, lo) + _dot(tri, mid)) + _dot(tri, hi)
        blast = b[c - 1:c, :]

        st = st_ref[...]
        o = _dot_nt((q * jnp.exp(b)).astype(BF16), st.astype(BF16))

        a_rows = [jnp.zeros((HG_SUB, c), F32)]
        for i in range(1, nsub):
            bi = b[i * HG_SUB - 1:i * HG_SUB, :]
            bq = b[i * HG_SUB:(i + 1) * HG_SUB, :]
            qi = q[i * HG_SUB:(i + 1) * HG_SUB, :] * jnp.exp(bq - bi)
            ki = jnp.where(row_k < i * HG_SUB, kk * jnp.exp(jnp.minimum(bi - b, 0.0)), 0.0)
            a_rows.append(_dot_nt(qi.astype(BF16), ki.astype(BF16)))
        a_off = jnp.concatenate(a_rows, axis=0)
        o = o + _dot(a_off.astype(BF16), v16)

        d_blocks = []
        for i in range(nsub):
            sl = slice(i * HG_SUB, (i + 1) * HG_SUB)
            bs, ks, vs, qs = b[sl, :], kk[sl, :], v[sl, :], q[sl, :]
            blk = jnp.zeros((HG_SUB, HG_DK), F32)
            for t in range(HG_SUB):
                w = jnp.exp(jnp.minimum(bs[t:t + 1, :] - bs, 0.0)) * ks * qs[t:t + 1, :]
                a = jnp.sum(w, axis=-1, keepdims=True)
                a = jnp.where(sub_s <= t, a, 0.0)
                blk = jnp.where(sub_row == t, jnp.sum(a * vs, axis=0, keepdims=True), blk)
            d_blocks.append(blk)
        o = o + jnp.concatenate(d_blocks, axis=0)

        kd = kk * jnp.exp(blast - b)
        st_ref[...] = st * jnp.exp(blast) + _dot_tn(v16, kd.astype(BF16))

        y = _rms(o, ng) * _silu(g_ref[pl.ds(r0, c), :].astype(F32))
        o_ref[pl.ds(r0, c), :] = y.astype(o_ref.dtype)
        return carry

    lax.fori_loop(0, q_ref.shape[0] // c, chunk, 0)


def _hgrn(proj, hf, lb_logits, norm_g, batch, seq):
    ns = seq // SEQ_TILE
    tok = lambda b, h, s: b * ns + s
    return pl.pallas_call(
        _hgrn_kernel,
        grid=(batch, HG_HEADS, ns),
        in_specs=[
            pl.BlockSpec((2, HG_DK), lambda b, h, s: (0, h)),
            pl.BlockSpec((SEQ_TILE, HG_DK), lambda b, h, s: (tok(b, h, s), COL_HQ + h)),
            pl.BlockSpec((SEQ_TILE, HG_DK), lambda b, h, s: (tok(b, h, s), h)),
            pl.BlockSpec((SEQ_TILE, HG_DK), lambda b, h, s: (tok(b, h, s), COL_HI + h)),
            pl.BlockSpec((SEQ_TILE, HG_DK), lambda b, h, s: (tok(b, h, s), COL_HG + h)),
            pl.BlockSpec((1, HG_DK), lambda b, h, s: (0, 0)),
        ],
        out_specs=pl.BlockSpec((SEQ_TILE, HG_DK), lambda b, h, s: (tok(b, h, s), h)),
        out_shape=jax.ShapeDtypeStruct((batch * seq, HG_WIDTH), BF16),
        scratch_shapes=[pltpu.VMEM((HG_DK, HG_DK), F32)],
        compiler_params=_cparams(("arbitrary", "arbitrary", "arbitrary")),
        name="hgrn2",
    )(lb_logits, proj, hf, proj, proj, norm_g)


def _ret_kernel(q_ref, k_ref, v_ref, g_ref, cos_ref, sin_ref, ng_ref, o_ref, r_ref):
    c = RET_CHUNK

    @pl.when(pl.program_id(2) == 0)
    def _():
        r_ref[...] = jnp.zeros_like(r_ref)

    hf = jnp.full((1, 1), pl.program_id(1), I32).astype(F32)
    lg = jnp.log1p(-jnp.exp2(-5.0 - hf))
    ti = lax.broadcasted_iota(I32, (c, c), 0)
    si = lax.broadcasted_iota(I32, (c, c), 1)
    rel = (ti - si).astype(F32)
    intra = jnp.where(ti >= si, jnp.exp(jnp.maximum(rel, 0.0) * lg), 0.0)
    idx = lax.broadcasted_iota(I32, (c, 1), 0).astype(F32)
    inter = jnp.exp((idx + 1.0) * lg)
    to_state = jnp.exp((c - 1.0 - idx) * lg)
    chunk_decay = jnp.exp(float(c) * lg)
    ng = ng_ref[...]
    half = RET_DK // 2

    def chunk(ci, carry):
        r0 = pl.multiple_of(ci * c, c)
        cos = cos_ref[pl.ds(r0, c), :]
        sin = sin_ref[pl.ds(r0, c), :]
        q = q_ref[pl.ds(r0, c), :].astype(F32)
        k = k_ref[pl.ds(r0, c), :].astype(F32)
        qr = (q * cos + pltpu.roll(q, half, 1) * sin) * (RET_DK ** -0.5)
        kr = k * cos + pltpu.roll(k, half, 1) * sin
        q16 = qr.astype(BF16)
        v16 = v_ref[pl.ds(r0, c), :]
        att = _dot_nt(q16, kr.astype(BF16)) * intra
        r = r_ref[...]
        o = _dot(att.astype(BF16), v16) + _dot(q16, r.astype(BF16)) * inter
        r_ref[...] = chunk_decay * r + _dot_tn((kr * to_state).astype(BF16), v16)
        y = _rms(o, ng) * _silu(g_ref[pl.ds(r0, c), :].astype(F32))
        o_ref[pl.ds(r0, c), :] = y.astype(o_ref.dtype)
        return carry

    lax.fori_loop(0, q_ref.shape[0] // c, chunk, 0)


def _retention(proj, cos2, sin2, norm_g, batch, seq):
    ns = seq // SEQ_TILE
    tok = lambda b, h, s: b * ns + s
    return pl.pallas_call(
        _ret_kernel,
        grid=(batch, RET_HEADS, ns),
        in_specs=[
            pl.BlockSpec((SEQ_TILE, RET_DK), lambda b, h, s: (tok(b, h, s), COL_RQ + h)),
            pl.BlockSpec((SEQ_TILE, RET_DK), lambda b, h, s: (tok(b, h, s), COL_RK + h)),
            pl.BlockSpec((SEQ_TILE, RET_DV), lambda b, h, s: (tok(b, h, s), COL_RV // 2 + h)),
            pl.BlockSpec((SEQ_TILE, RET_DV), lambda b, h, s: (tok(b, h, s), COL_RG // 2 + h)),
            pl.BlockSpec((SEQ_TILE, RET_DK), lambda b, h, s: (s, 0)),
            pl.BlockSpec((SEQ_TILE, RET_DK), lambda b, h, s: (s, 0)),
            pl.BlockSpec((1, RET_DV), lambda b, h, s: (0, 0)),
        ],
        out_specs=pl.BlockSpec((SEQ_TILE, RET_DV), lambda b, h, s: (tok(b, h, s), h)),
        out_shape=jax.ShapeDtypeStruct((batch * seq, RET_HEADS * RET_DV), BF16),
        scratch_shapes=[pltpu.VMEM((RET_DK, RET_DV), F32)],
        compiler_params=_cparams(("arbitrary", "arbitrary", "arbitrary")),
        name="retention",
    )(proj, proj, proj, proj, cos2, sin2, norm_g)


def _merge_kernel(x_ref, ya_ref, yb_ref, ga_ref, gb_ref, wa_ref, wb_ref, wo_ref, fg_ref,
                  wr_ref, br_ref, x1_ref, h2_ref, route_ref, cnt_ref, carry_ref):
    tm = x_ref.shape[0]

    @pl.when(pl.program_id(0) == 0)
    def _():
        carry_ref[...] = jnp.zeros_like(carry_ref)

    merged = (_sigmoid(ga_ref[...].astype(F32)) * _dot(ya_ref[...], wa_ref[...])
              + _sigmoid(gb_ref[...].astype(F32)) * _dot(yb_ref[...], wb_ref[...]))
    x1 = x_ref[...] + _dot(merged.astype(BF16), wo_ref[...])
    x1_ref[...] = x1
    h2 = _rms(x1, fg_ref[...])
    h2_ref[...] = h2

    hh, hm, _ = _split3(h2)
    wh, wm = wr_ref[0], wr_ref[1]
    logits = (_dot(hm, wh) + _dot(hh, wm)) + _dot(hh, wh) + br_ref[...]

    lane = lax.broadcasted_iota(I32, (tm, LANES), 1)
    neg = jnp.float32(-jnp.inf)
    big = jnp.int32(1 << 30)
    is_g = lane < N_GROUPS
    gl = jnp.where(is_g, logits, neg)
    gmax = jnp.max(gl, axis=-1, keepdims=True)
    g_idx = jnp.min(jnp.where(gl == gmax, lane, big), axis=-1, keepdims=True)
    g_w = 1.0 / jnp.sum(jnp.where(is_g, jnp.exp(gl - gmax), 0.0), axis=-1, keepdims=True)

    ex = lane - N_GROUPS
    in_grp = (ex >= g_idx * EXPERTS_PER_GROUP) & (ex < (g_idx + 1) * EXPERTS_PER_GROUP)
    el = jnp.where(in_grp, logits, neg)
    m1 = jnp.max(el, axis=-1, keepdims=True)
    e1 = jnp.min(jnp.where(el == m1, ex, big), axis=-1, keepdims=True)
    el2 = jnp.where(ex == e1, neg, el)
    m2 = jnp.max(el2, axis=-1, keepdims=True)
    e2 = jnp.min(jnp.where(el2 == m2, ex, big), axis=-1, keepdims=True)
    p2 = jnp.exp(m2 - m1)
    w1 = g_w / (1.0 + p2)
    w2 = g_w * p2 / (1.0 + p2)

    oh1 = ex == e1
    oh2 = ex == e2
    oh = jnp.where(oh1 | oh2, 1.0, 0.0)
    ri = lax.broadcasted_iota(I32, (tm, tm), 0)
    ci = lax.broadcasted_iota(I32, (tm, tm), 1)
    strict = jnp.where(ri > ci, 1.0, 0.0).astype(BF16)
    cum = _dot(strict, oh.astype(BF16)) + carry_ref[...]
    rank1 = jnp.sum(jnp.where(oh1, cum, 0.0), axis=-1, keepdims=True)
    rank2 = jnp.sum(jnp.where(oh2, cum, 0.0), axis=-1, keepdims=True)
    total = carry_ref[...] + jnp.sum(oh, axis=0, keepdims=True)
    carry_ref[...] = total
    cnt_ref[...] = total

    route = jnp.where(lane == 0, e1.astype(F32), 0.0)
    route = jnp.where(lane == 1, e2.astype(F32), route)
    route = jnp.where(lane == 2, w1, route)
    route = jnp.where(lane == 3, w2, route)
    route = jnp.where(lane == 4, rank1, route)
    route = jnp.where(lane == 5, rank2, route)
    route_ref[...] = route


def _merge(x2d, ya, yb, proj, wa, wb, wo, ffn_g, w_router, b_router):
    t = x2d.shape[0]
    const = lambda *shape: pl.BlockSpec(shape, lambda i: (0,) * len(shape))
    return pl.pallas_call(
        _merge_kernel,
        grid=(t // TOK_TILE,),
        in_specs=[
            pl.BlockSpec((TOK_TILE, D_MODEL), lambda i: (i, 0)),
            pl.BlockSpec((TOK_TILE, HG_WIDTH), lambda i: (i, 0)),
            pl.BlockSpec((TOK_TILE, D_MODEL), lambda i: (i, 0)),
            pl.BlockSpec((TOK_TILE, D_MODEL), lambda i: (i, COL_GA)),
            pl.BlockSpec((TOK_TILE, D_MODEL), lambda i: (i, COL_GB)),
            const(HG_WIDTH, D_MODEL),
            const(D_MODEL, D_MODEL),
            const(D_MODEL, D_MODEL),
            const(1, D_MODEL),
            const(2, D_MODEL, LANES),
            const(1, LANES),
        ],
        out_specs=[
            pl.BlockSpec((TOK_TILE, D_MODEL), lambda i: (i, 0)),
            pl.BlockSpec((TOK_TILE, D_MODEL), lambda i: (i, 0)),
            pl.BlockSpec((TOK_TILE, LANES), lambda i: (i, 0)),
            pl.BlockSpec((1, LANES), lambda i: (0, 0)),
        ],
        out_shape=[
            jax.ShapeDtypeStruct((t, D_MODEL), F32),
            jax.ShapeDtypeStruct((t, D_MODEL), F32),
            jax.ShapeDtypeStruct((t, LANES), F32),
            jax.ShapeDtypeStruct((1, LANES), F32),
        ],
        scratch_shapes=[pltpu.VMEM((1, LANES), F32)],
        compiler_params=_cparams(("arbitrary",)),
        name="merge_route",
    )(x2d, ya, yb, proj, proj, wa, wb, wo, ffn_g, w_router, b_router)


def _scatter_kernel(te_ref, nt_ref, dest_ref, h2_ref, xs_ref, zbuf, sem, zsem):
    n = dest_ref.shape[1]
    base = pl.program_id(0) * n
    n_tiles = xs_ref.shape[0] // EXP_TILE

    @pl.when(pl.program_id(0) == 0)
    def _():
        zbuf[...] = jnp.zeros_like(zbuf)
        nt = nt_ref[0]

        def needs_zero(i):
            nxt = te_ref[jnp.minimum(i + 1, n_tiles - 1)]
            return (i >= nt - 1) | (nxt != te_ref[i])

        def zcopy(i):
            return pltpu.make_async_copy(zbuf, xs_ref.at[pl.ds(i * EXP_TILE, EXP_TILE)], zsem)

        def zstart(i, carry):
            @pl.when(needs_zero(i))
            def _():
                zcopy(i).start()
            return carry

        def zwait(i, carry):
            @pl.when(needs_zero(i))
            def _():
                zcopy(i).wait()
            return carry

        lax.fori_loop(0, n_tiles, zstart, 0)
        lax.fori_loop(0, n_tiles, zwait, 0)

    def copy(t, j):
        return pltpu.make_async_copy(
            h2_ref.at[pl.ds(base + t, 1)], xs_ref.at[pl.ds(dest_ref[j, t], 1)], sem)

    def start(t, carry):
        copy(t, 0).start()
        copy(t, 1).start()
        return carry

    def wait(t, carry):
        copy(t, 0).wait()
        copy(t, 1).wait()
        return carry

    lax.fori_loop(0, n, start, 0)
    lax.fori_loop(0, n, wait, 0)


def _scatter_rows(tile_expert, n_tiles_used, dest_t, h2, n_rows):
    t = h2.shape[0]
    grid_spec = pltpu.PrefetchScalarGridSpec(
        num_scalar_prefetch=2,
        grid=(t // SCAT_TILE,),
        in_specs=[
            pl.BlockSpec((2, SCAT_TILE), lambda i, te, nt: (0, i), memory_space=pltpu.SMEM),
            pl.BlockSpec(memory_space=pl.ANY),
        ],
        out_specs=pl.BlockSpec(memory_space=pl.ANY),
        scratch_shapes=[pltpu.VMEM((EXP_TILE, D_MODEL), F32), pltpu.SemaphoreType.DMA(()),
                        pltpu.SemaphoreType.DMA(())],
    )
    return pl.pallas_call(
        _scatter_kernel,
        grid_spec=grid_spec,
        out_shape=jax.ShapeDtypeStruct((n_rows, D_MODEL), F32),
        compiler_params=_cparams(("arbitrary",)),
        name="scatter_rows",
    )(tile_expert, n_tiles_used, dest_t, h2)


def _expert_kernel(te_ref, nt_ref, xs_ref, wg_ref, wu_ref, wd_ref, ys_ref):
    del te_ref

    @pl.when(pl.program_id(0) < nt_ref[0])
    def _():
        x = xs_ref[...].astype(BF16)
        a = _dot(x, wg_ref[0])
        u = _dot(x, wu_ref[0])
        ys_ref[...] = _dot((_silu(a) * u).astype(BF16), wd_ref[0])

    @pl.when(pl.program_id(0) >= nt_ref[0])
    def _():
        ys_ref[...] = jnp.zeros_like(ys_ref)


def _experts(tile_expert, n_tiles_used, xs, wg, wu, wd):
    nr = xs.shape[0]
    grid_spec = pltpu.PrefetchScalarGridSpec(
        num_scalar_prefetch=2,
        grid=(nr // EXP_TILE,),
        in_specs=[
            pl.BlockSpec((EXP_TILE, D_MODEL), lambda i, te, nt: (i, 0)),
            pl.BlockSpec((1, D_MODEL, D_EXPERT), lambda i, te, nt: (te[i], 0, 0)),
            pl.BlockSpec((1, D_MODEL, D_EXPERT), lambda i, te, nt: (te[i], 0, 0)),
            pl.BlockSpec((1, D_EXPERT, D_MODEL), lambda i, te, nt: (te[i], 0, 0)),
        ],
        out_specs=pl.BlockSpec((EXP_TILE, D_MODEL), lambda i, te, nt: (i, 0)),
    )
    return pl.pallas_call(
        _expert_kernel,
        grid_spec=grid_spec,
        out_shape=jax.ShapeDtypeStruct((nr, D_MODEL), F32),
        compiler_params=_cparams(("arbitrary",)),
        name="experts",
    )(tile_expert, n_tiles_used, xs, wg, wu, wd)


def _final_kernel(dest_ref, x1_ref, route_ref, p_ref, ys_ref, pg_ref, wpg_ref, wpp_ref, fg_ref,
                  o_ref, ybuf, sem):
    tm = x1_ref.shape[0]

    def copy(t, j):
        return pltpu.make_async_copy(
            ys_ref.at[pl.ds(dest_ref[j, t], 1)], ybuf.at[j, pl.ds(t, 1)], sem)

    def start(t, carry):
        copy(t, 0).start()
        copy(t, 1).start()
        return carry

    def wait(t, carry):
        copy(t, 0).wait()
        copy(t, 1).wait()
        return carry

    lax.fori_loop(0, tm, start, 0)
    lax.fori_loop(0, tm, wait, 0)

    route = route_ref[...]
    w1 = route[:, 2:3]
    w2 = route[:, 3:4]
    x2 = x1_ref[...] + (w1 * ybuf[0] + w2 * ybuf[1])
    hp = _rms(x2, pg_ref[...]).astype(BF16)
    gate = _sigmoid(_dot(hp, wpg_ref[...]))
    x3 = x2 + gate * _dot(p_ref[...].astype(BF16), wpp_ref[...])
    o_ref[...] = _rms(x3, fg_ref[...])


def _final(dest_t, x1, route, p2d, ys, ple_g, wpg, wpp, final_g):
    t = x1.shape[0]
    const = lambda *shape: pl.BlockSpec(shape, lambda i: (0,) * len(shape))
    return pl.pallas_call(
        _final_kernel,
        grid=(t // TOK_TILE,),
        in_specs=[
            pl.BlockSpec((2, TOK_TILE), lambda i: (0, i), memory_space=pltpu.SMEM),
            pl.BlockSpec((TOK_TILE, D_MODEL), lambda i: (i, 0)),
            pl.BlockSpec((TOK_TILE, LANES), lambda i: (i, 0)),
            pl.BlockSpec((TOK_TILE, PLE_DIM), lambda i: (i, 0)),
            pl.BlockSpec(memory_space=pl.ANY),
            const(1, D_MODEL),
            const(D_MODEL, D_MODEL),
            const(PLE_DIM, D_MODEL),
            const(1, D_MODEL),
        ],
        out_specs=pl.BlockSpec((TOK_TILE, D_MODEL), lambda i: (i, 0)),
        out_shape=jax.ShapeDtypeStruct((t, D_MODEL), F32),
        scratch_shapes=[pltpu.VMEM((2, TOK_TILE, D_MODEL), F32), pltpu.SemaphoreType.DMA(())],
        compiler_params=_cparams(("arbitrary",)),
        name="combine_ple_final",
    )(dest_t, x1, route, p2d, ys, ple_g, wpg, wpp, final_g)


def _rotary_tables(seq):
    inv = ROPE_BASE ** (-jnp.arange(0, RET_DK, 2, dtype=F32) / RET_DK)
    ang = jnp.arange(seq, dtype=F32)[:, None] * inv[None, :]
    cos, sin = jnp.cos(ang), jnp.sin(ang)
    return jnp.concatenate([cos, cos], axis=1), jnp.concatenate([-sin, sin], axis=1)


def _layer(x2d, p2d, batch, seq, mix_norm, w_in, hg_lb_logits, hg_norm, ret_norm, w_branch_a,
           w_branch_b, w_out, ffn_norm, w_rg, b_rg, w_re, b_re, w_gate, w_up, w_down, ple_norm,
           w_ple_gate, w_ple_proj, out_gain):
    t = x2d.shape[0]
    row = lambda v: v.reshape(1, -1).astype(F32)

    proj, hf = _inproj(x2d, row(mix_norm), w_in.astype(BF16))
    ya = _hgrn(proj, hf, hg_lb_logits.astype(F32), row(hg_norm), batch, seq)
    cos2, sin2 = _rotary_tables(seq)
    yb = _retention(proj, cos2, sin2, row(ret_norm), batch, seq)

    n_r = N_GROUPS + N_EXPERTS
    w_router = jnp.zeros((D_MODEL, LANES), F32).at[:, :n_r].set(jnp.concatenate([w_rg, w_re], axis=1))
    wr_hi = w_router.astype(BF16)
    wr_mid = (w_router - wr_hi.astype(F32)).astype(BF16)
    b_router = jnp.zeros((1, LANES), F32).at[0, :n_r].set(jnp.concatenate([b_rg, b_re]))
    x1, h2, route, counts = _merge(
        x2d, ya, yb, proj, w_branch_a.astype(BF16), w_branch_b.astype(BF16), w_out.astype(BF16),
        row(ffn_norm), jnp.stack([wr_hi, wr_mid]), b_router)

    n_rows = 2 * t + N_EXPERTS * EXP_TILE
    cnt = counts[0, N_GROUPS:N_GROUPS + N_EXPERTS].astype(I32)
    tiles_per = (cnt + EXP_TILE - 1) // EXP_TILE
    tile_end = jnp.cumsum(tiles_per)
    offsets = (tile_end - tiles_per) * EXP_TILE
    eid = route[:, 0:2].astype(I32)
    dest_t = (jnp.take(offsets, eid) + route[:, 4:6].astype(I32)).T
    tile_ids = jnp.arange(n_rows // EXP_TILE, dtype=I32)
    tile_expert = jnp.minimum(jnp.searchsorted(tile_end, tile_ids, side="right"), N_EXPERTS - 1).astype(I32)

    n_used = tile_end[-1:].astype(I32)
    xs = _scatter_rows(tile_expert, n_used, dest_t, h2, n_rows)
    ys = _experts(tile_expert, n_used, xs,
                  w_gate.astype(BF16), w_up.astype(BF16), w_down.astype(BF16))
    return _final(dest_t, x1, route, p2d, ys, row(ple_norm), w_ple_gate.astype(BF16),
                  w_ple_proj.astype(BF16), out_gain)


def kernel(x, p, mix_norm, w_in, hg_lb_logits, hg_norm, ret_norm, w_branch_a, w_branch_b, w_out,
           ffn_norm, w_router_group, b_router_group, w_router_expert, b_router_expert,
           w_expert_gate, w_expert_up, w_expert_down, ple_norm, w_ple_gate, w_ple_proj, final_norm):
    batch, seq, d = x.shape
    depth = p.shape[0]
    assert depth == 1, "the final rmsnorm is fused into the single layer"
    x2d = x.reshape(batch * seq, d)
    out = _layer(x2d, p[0].reshape(batch * seq, -1), batch, seq, mix_norm[0], w_in[0], hg_lb_logits,
                 hg_norm[0], ret_norm[0], w_branch_a[0], w_branch_b[0], w_out[0], ffn_norm[0],
                 w_router_group[0], b_router_group[0], w_router_expert[0], b_router_expert[0],
                 w_expert_gate[0], w_expert_up[0], w_expert_down[0], ple_norm[0], w_ple_gate[0],
                 w_ple_proj[0], final_norm.reshape(1, -1).astype(F32))
    return out.reshape(batch, seq, d)
```

```python
import functools

import jax
import jax.numpy as jnp
from jax import lax
from jax.experimental import pallas as pl
from jax.experimental.pallas import tpu as pltpu

F32 = jnp.float32
BF16 = jnp.bfloat16
I32 = jnp.int32
U32 = jnp.uint32

EPS = 1e-6
D_MODEL = 1024
PLE_DIM = 256
HG_HEADS = 4
HG_DK = 128
HG_WIDTH = HG_HEADS * HG_DK
RET_HEADS = 4
RET_DK = 128
RET_DV = 256
ROPE_BASE = 10000.0
IN_TOTAL = 7168
N_GROUPS = 4
EXPERTS_PER_GROUP = 8
N_EXPERTS = 32
D_EXPERT = 256

COL_HQ, COL_HF, COL_HI, COL_HG = 0, 4, 8, 12
COL_RQ, COL_RK = 16, 20
COL_RV, COL_RG = 24, 32
COL_GA, COL_GB = 5, 6

LANES = 128
VMEM_LIMIT = 56 * 1024 * 1024

HG_CHUNK = 64
HG_SUB = 8
RET_CHUNK = 128
SEQ_TILE = 1024
TOK_TILE = 512
EXP_TILE = 256
SCAT_TILE = 1024


def _cparams(sem):
    return pltpu.CompilerParams(dimension_semantics=sem, vmem_limit_bytes=VMEM_LIMIT)


def _rms(x, g):
    return x * lax.rsqrt(jnp.mean(x * x, axis=-1, keepdims=True) + EPS) * g


def _sigmoid(x):
    return 1.0 / (1.0 + jnp.exp(-x))


def _silu(x):
    return x * _sigmoid(x)


def _split3(x):
    hi = x.astype(BF16)
    r1 = x - hi.astype(F32)
    mid = r1.astype(BF16)
    lo = (r1 - mid.astype(F32)).astype(BF16)
    return hi, mid, lo


def _dot(a, b):
    return jnp.dot(a, b, preferred_element_type=F32)


def _dot_nt(a, b):
    return lax.dot_general(a, b, (((1,), (1,)), ((), ())), preferred_element_type=F32)


def _dot_tn(a, b):
    return lax.dot_general(a, b, (((0,), (0,)), ((), ())), preferred_element_type=F32)


def _inproj_kernel(x_ref, g_ref, w_ref, proj_ref, hf_ref):
    h = _rms(x_ref[...], g_ref[...]).astype(BF16)
    tn = HG_WIDTH
    for j in range(IN_TOTAL // tn):
        acc = _dot(h, w_ref[:, j * tn:(j + 1) * tn])
        proj_ref[:, j * tn:(j + 1) * tn] = acc.astype(BF16)
        if j * tn == COL_HF * LANES:
            hf_ref[...] = acc


def _inproj(x2d, gain, w_bf16):
    t = x2d.shape[0]
    return pl.pallas_call(
        _inproj_kernel,
        grid=(t // TOK_TILE,),
        in_specs=[
            pl.BlockSpec((TOK_TILE, D_MODEL), lambda i: (i, 0)),
            pl.BlockSpec((1, D_MODEL), lambda i: (0, 0)),
            pl.BlockSpec((D_MODEL, IN_TOTAL), lambda i: (0, 0), pipeline_mode=pl.Buffered(1)),
        ],
        out_specs=[
            pl.BlockSpec((TOK_TILE, IN_TOTAL), lambda i: (i, 0)),
            pl.BlockSpec((TOK_TILE, HG_WIDTH), lambda i: (i, 0)),
        ],
        out_shape=[
            jax.ShapeDtypeStruct((t, IN_TOTAL), BF16),
            jax.ShapeDtypeStruct((t, HG_WIDTH), F32),
        ],
        compiler_params=_cparams(("arbitrary",)),
        name="inproj",
    )(x2d, gain, w_bf16)


def _hgrn_kernel(lbl_ref, q_ref, f_ref, i_ref, g_ref, ng_ref, o_ref, st_ref):
    c = HG_CHUNK
    nsub = c // HG_SUB

    @pl.when(pl.program_id(2) == 0)
    def _():
        st_ref[...] = jnp.zeros_like(st_ref)

    logits = lbl_ref[...]
    e = jnp.exp(logits - jnp.max(logits, axis=0, keepdims=True))
    lb = e[0:1] / jnp.sum(e, axis=0, keepdims=True)
    log_lb = jnp.log(lb)
    log_1m = jnp.log1p(-lb)
    ng = ng_ref[...]

    row = lax.broadcasted_iota(I32, (c, c), 0)
    col = lax.broadcasted_iota(I32, (c, c), 1)
    tri = jnp.where(row >= col, 1.0, 0.0).astype(BF16)
    row_k = lax.broadcasted_iota(I32, (c, HG_DK), 0)
    sub_s = lax.broadcasted_iota(I32, (HG_SUB, 1), 0)
    sub_row = lax.broadcasted_iota(I32, (HG_SUB, HG_DK), 0)

    def chunk(ci, carry):
        r0 = pl.multiple_of(ci * c, c)
        z = f_ref[pl.ds(r0, c), :]
        logsig = jnp.minimum(z, 0.0) - jnp.log1p(jnp.exp(-jnp.abs(z)))
        bb = log_1m + logsig
        logf = jnp.maximum(log_lb, bb) + jnp.log1p(jnp.exp(-jnp.abs(log_lb - bb)))
        kk = jnp.exp(log_1m + logsig - z)
        q = _silu(q_ref[pl.ds(r0, c), :].astype(F32))
        v = i_ref[pl.ds(r0, c), :].astype(F32)
        v16 = v.astype(BF16)

        hi, mid, lo = _split3(logf)
        b = (_dot(tri, lo) + _dot(tri, mid)) + _dot(tri, hi)
        blast = b[c - 1:c, :]

        st = st_ref[...]
        o = _dot_nt((q * jnp.exp(b)).astype(BF16), st.astype(BF16))

        a_rows = [jnp.zeros((HG_SUB, c), F32)]
        for i in range(1, nsub):
            bi = b[i * HG_SUB - 1:i * HG_SUB, :]
            bq = b[i * HG_SUB:(i + 1) * HG_SUB, :]
            qi = q[i * HG_SUB:(i + 1) * HG_SUB, :] * jnp.exp(bq - bi)
            ki = jnp.where(row_k < i * HG_SUB, kk * jnp.exp(jnp.minimum(bi - b, 0.0)), 0.0)
            a_rows.append(_dot_nt(qi.astype(BF16), ki.astype(BF16)))
        a_off = jnp.concatenate(a_rows, axis=0)
        o = o + _dot(a_off.astype(BF16), v16)

        d_blocks = []
        for i in range(nsub):
            sl = slice(i * HG_SUB, (i + 1) * HG_SUB)
            bs, ks, vs, qs = b[sl, :], kk[sl, :], v[sl, :], q[sl, :]
            blk = jnp.zeros((HG_SUB, HG_DK), F32)
            for t in range(HG_SUB):
                w = jnp.exp(jnp.minimum(bs[t:t + 1, :] - bs, 0.0)) * ks * qs[t:t + 1, :]
                a = jnp.sum(w, axis=-1, keepdims=True)
                a = jnp.where(sub_s <= t, a, 0.0)
                blk = jnp.where(sub_row == t, jnp.sum(a * vs, axis=0, keepdims=True), blk)
            d_blocks.append(blk)
        o = o + jnp.concatenate(d_blocks, axis=0)

        kd = kk * jnp.exp(blast - b)
        st_ref[...] = st * jnp.exp(blast) + _dot_tn(v16, kd.astype(BF16))

        y = _rms(o, ng) * _silu(g_ref[pl.ds(r0, c), :].astype(F32))
        o_ref[pl.ds(r0, c), :] = y.astype(o_ref.dtype)
        return carry

    lax.fori_loop(0, q_ref.shape[0] // c, chunk, 0)


def _hgrn(proj, hf, lb_logits, norm_g, batch, seq):
    ns = seq // SEQ_TILE
    tok = lambda b, h, s: b * ns + s
    return pl.pallas_call(
        _hgrn_kernel,
        grid=(batch, HG_HEADS, ns),
        in_specs=[
            pl.BlockSpec((2, HG_DK), lambda b, h, s: (0, h)),
            pl.BlockSpec((SEQ_TILE, HG_DK), lambda b, h, s: (tok(b, h, s), COL_HQ + h)),
            pl.BlockSpec((SEQ_TILE, HG_DK), lambda b, h, s: (tok(b, h, s), h)),
            pl.BlockSpec((SEQ_TILE, HG_DK), lambda b, h, s: (tok(b, h, s), COL_HI + h)),
            pl.BlockSpec((SEQ_TILE, HG_DK), lambda b, h, s: (tok(b, h, s), COL_HG + h)),
            pl.BlockSpec((1, HG_DK), lambda b, h, s: (0, 0)),
        ],
        out_specs=pl.BlockSpec((SEQ_TILE, HG_DK), lambda b, h, s: (tok(b, h, s), h)),
        out_shape=jax.ShapeDtypeStruct((batch * seq, HG_WIDTH), BF16),
        scratch_shapes=[pltpu.VMEM((HG_DK, HG_DK), F32)],
        compiler_params=_cparams(("arbitrary", "arbitrary", "arbitrary")),
        name="hgrn2",
    )(lb_logits, proj, hf, proj, proj, norm_g)


def _ret_kernel(q_ref, k_ref, v_ref, g_ref, cos_ref, sin_ref, ng_ref, o_ref, r_ref):
    c = RET_CHUNK

    @pl.when(pl.program_id(2) == 0)
    def _():
        r_ref[...] = jnp.zeros_like(r_ref)

    hf = jnp.full((1, 1), pl.program_id(1), I32).astype(F32)
    lg = jnp.log1p(-jnp.exp2(-5.0 - hf))
    ti = lax.broadcasted_iota(I32, (c, c), 0)
    si = lax.broadcasted_iota(I32, (c, c), 1)
    rel = (ti - si).astype(F32)
    intra = jnp.where(ti >= si, jnp.exp(jnp.maximum(rel, 0.0) * lg), 0.0)
    idx = lax.broadcasted_iota(I32, (c, 1), 0).astype(F32)
    inter = jnp.exp((idx + 1.0) * lg)
    to_state = jnp.exp((c - 1.0 - idx) * lg)
    chunk_decay = jnp.exp(float(c) * lg)
    ng = ng_ref[...]
    half = RET_DK // 2

    def chunk(ci, carry):
        r0 = pl.multiple_of(ci * c, c)
        cos = cos_ref[pl.ds(r0, c), :]
        sin = sin_ref[pl.ds(r0, c), :]
        q = q_ref[pl.ds(r0, c), :].astype(F32)
        k = k_ref[pl.ds(r0, c), :].astype(F32)
        qr = (q * cos + pltpu.roll(q, half, 1) * sin) * (RET_DK ** -0.5)
        kr = k * cos + pltpu.roll(k, half, 1) * sin
        q16 = qr.astype(BF16)
        v16 = v_ref[pl.ds(r0, c), :]
        att = _dot_nt(q16, kr.astype(BF16)) * intra
        r = r_ref[...]
        o = _dot(att.astype(BF16), v16) + _dot(q16, r.astype(BF16)) * inter
        r_ref[...] = chunk_decay * r + _dot_tn((kr * to_state).astype(BF16), v16)
        y = _rms(o, ng) * _silu(g_ref[pl.ds(r0, c), :].astype(F32))
        o_ref[pl.ds(r0, c), :] = y.astype(o_ref.dtype)
        return carry

    lax.fori_loop(0, q_ref.shape[0] // c, chunk, 0)


def _retention(proj, cos2, sin2, norm_g, batch, seq):
    ns = seq // SEQ_TILE
    tok = lambda b, h, s: b * ns + s
    return pl.pallas_call(
        _ret_kernel,
        grid=(batch, RET_HEADS, ns),
        in_specs=[
            pl.BlockSpec((SEQ_TILE, RET_DK), lambda b, h, s: (tok(b, h, s), COL_RQ + h)),
            pl.BlockSpec((SEQ_TILE, RET_DK), lambda b, h, s: (tok(b, h, s), COL_RK + h)),
            pl.BlockSpec((SEQ_TILE, RET_DV), lambda b, h, s: (tok(b, h, s), COL_RV // 2 + h)),
            pl.BlockSpec((SEQ_TILE, RET_DV), lambda b, h, s: (tok(b, h, s), COL_RG // 2 + h)),
            pl.BlockSpec((SEQ_TILE, RET_DK), lambda b, h, s: (s, 0)),
            pl.BlockSpec((SEQ_TILE, RET_DK), lambda b, h, s: (s, 0)),
            pl.BlockSpec((1, RET_DV), lambda b, h, s: (0, 0)),
        ],
        out_specs=pl.BlockSpec((SEQ_TILE, RET_DV), lambda b, h, s: (tok(b, h, s), h)),
        out_shape=jax.ShapeDtypeStruct((batch * seq, RET_HEADS * RET_DV), BF16),
        scratch_shapes=[pltpu.VMEM((RET_DK, RET_DV), F32)],
        compiler_params=_cparams(("arbitrary", "arbitrary", "arbitrary")),
        name="retention",
    )(proj, proj, proj, proj, cos2, sin2, norm_g)


def _merge_kernel(x_ref, ya_ref, yb_ref, ga_ref, gb_ref, wa_ref, wb_ref, wo_ref, fg_ref,
                  wr_ref, br_ref, x1_ref, h2_ref, route_ref, cnt_ref, carry_ref):
    tm = x_ref.shape[0]

    @pl.when(pl.program_id(0) == 0)
    def _():
        carry_ref[...] = jnp.zeros_like(carry_ref)

    merged = (_sigmoid(ga_ref[...].astype(F32)) * _dot(ya_ref[...], wa_ref[...])
              + _sigmoid(gb_ref[...].astype(F32)) * _dot(yb_ref[...], wb_ref[...]))
    x1 = x_ref[...] + _dot(merged.astype(BF16), wo_ref[...])
    x1_ref[...] = x1
    h2 = _rms(x1, fg_ref[...])
    h2_ref[...] = h2

    hh, hm, _ = _split3(h2)
    wh, wm = wr_ref[0], wr_ref[1]
    logits = (_dot(hm, wh) + _dot(hh, wm)) + _dot(hh, wh) + br_ref[...]

    lane = lax.broadcasted_iota(I32, (tm, LANES), 1)
    neg = jnp.float32(-jnp.inf)
    big = jnp.int32(1 << 30)
    is_g = lane < N_GROUPS
    gl = jnp.where(is_g, logits, neg)
    gmax = jnp.max(gl, axis=-1, keepdims=True)
    g_idx = jnp.min(jnp.where(gl == gmax, lane, big), axis=-1, keepdims=True)
    g_w = 1.0 / jnp.sum(jnp.where(is_g, jnp.exp(gl - gmax), 0.0), axis=-1, keepdims=True)

    ex = lane - N_GROUPS
    in_grp = (ex >= g_idx * EXPERTS_PER_GROUP) & (ex < (g_idx + 1) * EXPERTS_PER_GROUP)
    el = jnp.where(in_grp, logits, neg)
    m1 = jnp.max(el, axis=-1, keepdims=True)
    e1 = jnp.min(jnp.where(el == m1, ex, big), axis=-1, keepdims=True)
    el2 = jnp.where(ex == e1, neg, el)
    m2 = jnp.max(el2, axis=-1, keepdims=True)
    e2 = jnp.min(jnp.where(el2 == m2, ex, big), axis=-1, keepdims=True)
    p2 = jnp.exp(m2 - m1)
    w1 = g_w / (1.0 + p2)
    w2 = g_w * p2 / (1.0 + p2)

    oh1 = ex == e1
    oh2 = ex == e2
    oh = jnp.where(oh1 | oh2, 1.0, 0.0)
    ri = lax.broadcasted_iota(I32, (tm, tm), 0)
    ci = lax.broadcasted_iota(I32, (tm, tm), 1)
    strict = jnp.where(ri > ci, 1.0, 0.0).astype(BF16)
    cum = _dot(strict, oh.astype(BF16)) + carry_ref[...]
    rank1 = jnp.sum(jnp.where(oh1, cum, 0.0), axis=-1, keepdims=True)
    rank2 = jnp.sum(jnp.where(oh2, cum, 0.0), axis=-1, keepdims=True)
    total = carry_ref[...] + jnp.sum(oh, axis=0, keepdims=True)
    carry_ref[...] = total
    cnt_ref[...] = total

    route = jnp.where(lane == 0, e1.astype(F32), 0.0)
    route = jnp.where(lane == 1, e2.astype(F32), route)
    route = jnp.where(lane == 2, w1, route)
    route = jnp.where(lane == 3, w2, route)
    route = jnp.where(lane == 4, rank1, route)
    route = jnp.where(lane == 5, rank2, route)
    route_ref[...] = route


def _merge(x2d, ya, yb, proj, wa, wb, wo, ffn_g, w_router, b_router):
    t = x2d.shape[0]
    const = lambda *shape: pl.BlockSpec(shape, lambda i: (0,) * len(shape))
    return pl.pallas_call(
        _merge_kernel,
        grid=(t // TOK_TILE,),
        in_specs=[
            pl.BlockSpec((TOK_TILE, D_MODEL), lambda i: (i, 0)),
            pl.BlockSpec((TOK_TILE, HG_WIDTH), lambda i: (i, 0)),
            pl.BlockSpec((TOK_TILE, D_MODEL), lambda i: (i, 0)),
            pl.BlockSpec((TOK_TILE, D_MODEL), lambda i: (i, COL_GA)),
            pl.BlockSpec((TOK_TILE, D_MODEL), lambda i: (i, COL_GB)),
            const(HG_WIDTH, D_MODEL),
            const(D_MODEL, D_MODEL),
            const(D_MODEL, D_MODEL),
            const(1, D_MODEL),
            const(2, D_MODEL, LANES),
            const(1, LANES),
        ],
        out_specs=[
            pl.BlockSpec((TOK_TILE, D_MODEL), lambda i: (i, 0)),
            pl.BlockSpec((TOK_TILE, D_MODEL), lambda i: (i, 0)),
            pl.BlockSpec((TOK_TILE, LANES), lambda i: (i, 0)),
            pl.BlockSpec((1, LANES), lambda i: (0, 0)),
        ],
        out_shape=[
            jax.ShapeDtypeStruct((t, D_MODEL), F32),
            jax.ShapeDtypeStruct((t, D_MODEL), F32),
            jax.ShapeDtypeStruct((t, LANES), F32),
            jax.ShapeDtypeStruct((1, LANES), F32),
        ],
        scratch_shapes=[pltpu.VMEM((1, LANES), F32)],
        compiler_params=_cparams(("arbitrary",)),
        name="merge_route",
    )(x2d, ya, yb, proj, proj, wa, wb, wo, ffn_g, w_router, b_router)


def _scatter_kernel(te_ref, nt_ref, dest_ref, h2_ref, xs_ref, zbuf, sem, zsem):
    n = dest_ref.shape[1]
    n_tiles = xs_ref.shape[0] // EXP_TILE

    @pl.when(pl.program_id(0) == 0)
    def _():
        zbuf[...] = jnp.zeros_like(zbuf)
        nt = nt_ref[0]

        def needs_zero(i):
            nxt = te_ref[jnp.minimum(i + 1, n_tiles - 1)]
            return (i >= nt - 1) | (nxt != te_ref[i])

        def zcopy(i):
            return pltpu.make_async_copy(zbuf, xs_ref.at[pl.ds(i * EXP_TILE, EXP_TILE)], zsem)

        def zstart(i, carry):
            @pl.when(needs_zero(i))
            def _():
                zcopy(i).start()
            return carry

        def zwait(i, carry):
            @pl.when(needs_zero(i))
            def _():
                zcopy(i).wait()
            return carry

        lax.fori_loop(0, n_tiles, zstart, 0)
        lax.fori_loop(0, n_tiles, zwait, 0)

    def copy(t, j):
        return pltpu.make_async_copy(
            h2_ref.at[pl.ds(t, 1)], xs_ref.at[pl.ds(dest_ref[j, t], 1)], sem)

    def start(t, carry):
        copy(t, 0).start()
        copy(t, 1).start()
        return carry

    def wait(t, carry):
        copy(t, 0).wait()
        copy(t, 1).wait()
        return carry

    lax.fori_loop(0, n, start, 0, unroll=8)
    lax.fori_loop(0, n, wait, 0, unroll=8)


def _scatter_rows(tile_expert, n_tiles_used, dest_t, h2, n_rows):
    t = h2.shape[0]
    grid_spec = pltpu.PrefetchScalarGridSpec(
        num_scalar_prefetch=2,
        grid=(t // SCAT_TILE,),
        in_specs=[
            pl.BlockSpec((2, SCAT_TILE), lambda i, te, nt: (0, i), memory_space=pltpu.SMEM),
            pl.BlockSpec((SCAT_TILE, D_MODEL), lambda i, te, nt: (i, 0)),
        ],
        out_specs=pl.BlockSpec(memory_space=pl.ANY),
        scratch_shapes=[pltpu.VMEM((EXP_TILE, D_MODEL), F32), pltpu.SemaphoreType.DMA(()),
                        pltpu.SemaphoreType.DMA(())],
    )
    return pl.pallas_call(
        _scatter_kernel,
        grid_spec=grid_spec,
        out_shape=jax.ShapeDtypeStruct((n_rows, D_MODEL), F32),
        compiler_params=_cparams(("arbitrary",)),
        name="scatter_rows",
    )(tile_expert, n_tiles_used, dest_t, h2)


def _expert_kernel(te_ref, nt_ref, xs_ref, wg_ref, wu_ref, wd_ref, ys_ref):
    del te_ref

    @pl.when(pl.program_id(0) < nt_ref[0])
    def _():
        x = xs_ref[...].astype(BF16)
        a = _dot(x, wg_ref[0])
        u = _dot(x, wu_ref[0])
        ys_ref[...] = _dot((_silu(a) * u).astype(BF16), wd_ref[0])

    @pl.when(pl.program_id(0) >= nt_ref[0])
    def _():
        ys_ref[...] = jnp.zeros_like(ys_ref)


def _experts(tile_expert, n_tiles_used, xs, wg, wu, wd):
    nr = xs.shape[0]
    grid_spec = pltpu.PrefetchScalarGridSpec(
        num_scalar_prefetch=2,
        grid=(nr // EXP_TILE,),
        in_specs=[
            pl.BlockSpec((EXP_TILE, D_MODEL), lambda i, te, nt: (i, 0)),
            pl.BlockSpec((1, D_MODEL, D_EXPERT), lambda i, te, nt: (te[i], 0, 0)),
            pl.BlockSpec((1, D_MODEL, D_EXPERT), lambda i, te, nt: (te[i], 0, 0)),
            pl.BlockSpec((1, D_EXPERT, D_MODEL), lambda i, te, nt: (te[i], 0, 0)),
        ],
        out_specs=pl.BlockSpec((EXP_TILE, D_MODEL), lambda i, te, nt: (i, 0)),
    )
    return pl.pallas_call(
        _expert_kernel,
        grid_spec=grid_spec,
        out_shape=jax.ShapeDtypeStruct((nr, D_MODEL), F32),
        compiler_params=_cparams(("arbitrary",)),
        name="experts",
    )(tile_expert, n_tiles_used, xs, wg, wu, wd)


def _final_kernel(dest_ref, x1_ref, route_ref, p_ref, ys_ref, pg_ref, wpg_ref, wpp_ref, fg_ref,
                  o_ref, ybuf, sem):
    tm = x1_ref.shape[0]

    def copy(t, j):
        return pltpu.make_async_copy(
            ys_ref.at[pl.ds(dest_ref[j, t], 1)], ybuf.at[j, pl.ds(t, 1)], sem)

    def start(t, carry):
        copy(t, 0).start()
        copy(t, 1).start()
        return carry

    def wait(t, carry):
        copy(t, 0).wait()
        copy(t, 1).wait()
        return carry

    lax.fori_loop(0, tm, start, 0, unroll=8)
    lax.fori_loop(0, tm, wait, 0, unroll=8)

    route = route_ref[...]
    w1 = route[:, 2:3]
    w2 = route[:, 3:4]
    x2 = x1_ref[...] + (w1 * ybuf[0] + w2 * ybuf[1])
    hp = _rms(x2, pg_ref[...]).astype(BF16)
    gate = _sigmoid(_dot(hp, wpg_ref[...]))
    x3 = x2 + gate * _dot(p_ref[...].astype(BF16), wpp_ref[...])
    o_ref[...] = _rms(x3, fg_ref[...])


def _final(dest_t, x1, route, p2d, ys, ple_g, wpg, wpp, final_g):
    t = x1.shape[0]
    const = lambda *shape: pl.BlockSpec(shape, lambda i: (0,) * len(shape))
    return pl.pallas_call(
        _final_kernel,
        grid=(t // TOK_TILE,),
        in_specs=[
            pl.BlockSpec((2, TOK_TILE), lambda i: (0, i), memory_space=pltpu.SMEM),
            pl.BlockSpec((TOK_TILE, D_MODEL), lambda i: (i, 0)),
            pl.BlockSpec((TOK_TILE, LANES), lambda i: (i, 0)),
            pl.BlockSpec((TOK_TILE, PLE_DIM), lambda i: (i, 0)),
            pl.BlockSpec(memory_space=pl.ANY),
            const(1, D_MODEL),
            const(D_MODEL, D_MODEL),
            const(PLE_DIM, D_MODEL),
            const(1, D_MODEL),
        ],
        out_specs=pl.BlockSpec((TOK_TILE, D_MODEL), lambda i: (i, 0)),
        out_shape=jax.ShapeDtypeStruct((t, D_MODEL), F32),
        scratch_shapes=[pltpu.VMEM((2, TOK_TILE, D_MODEL), F32), pltpu.SemaphoreType.DMA(())],
        compiler_params=_cparams(("arbitrary",)),
        name="combine_ple_final",
    )(dest_t, x1, route, p2d, ys, ple_g, wpg, wpp, final_g)


def _rotary_tables(seq):
    inv = ROPE_BASE ** (-jnp.arange(0, RET_DK, 2, dtype=F32) / RET_DK)
    ang = jnp.arange(seq, dtype=F32)[:, None] * inv[None, :]
    cos, sin = jnp.cos(ang), jnp.sin(ang)
    return jnp.concatenate([cos, cos], axis=1), jnp.concatenate([-sin, sin], axis=1)


def _layer(x2d, p2d, batch, seq, mix_norm, w_in, hg_lb_logits, hg_norm, ret_norm, w_branch_a,
           w_branch_b, w_out, ffn_norm, w_rg, b_rg, w_re, b_re, w_gate, w_up, w_down, ple_norm,
           w_ple_gate, w_ple_proj, out_gain):
    t = x2d.shape[0]
    row = lambda v: v.reshape(1, -1).astype(F32)

    proj, hf = _inproj(x2d, row(mix_norm), w_in.astype(BF16))
    ya = _hgrn(proj, hf, hg_lb_logits.astype(F32), row(hg_norm), batch, seq)
    cos2, sin2 = _rotary_tables(seq)
    yb = _retention(proj, cos2, sin2, row(ret_norm), batch, seq)

    n_r = N_GROUPS + N_EXPERTS
    w_router = jnp.zeros((D_MODEL, LANES), F32).at[:, :n_r].set(jnp.concatenate([w_rg, w_re], axis=1))
    wr_hi = w_router.astype(BF16)
    wr_mid = (w_router - wr_hi.astype(F32)).astype(BF16)
    b_router = jnp.zeros((1, LANES), F32).at[0, :n_r].set(jnp.concatenate([b_rg, b_re]))
    x1, h2, route, counts = _merge(
        x2d, ya, yb, proj, w_branch_a.astype(BF16), w_branch_b.astype(BF16), w_out.astype(BF16),
        row(ffn_norm), jnp.stack([wr_hi, wr_mid]), b_router)

    n_rows = 2 * t + N_EXPERTS * EXP_TILE
    cnt = counts[0, N_GROUPS:N_GROUPS + N_EXPERTS].astype(I32)
    tiles_per = (cnt + EXP_TILE - 1) // EXP_TILE
    tile_end = jnp.cumsum(tiles_per)
    offsets = (tile_end - tiles_per) * EXP_TILE
    eid = route[:, 0:2].astype(I32)
    dest_t = (jnp.take(offsets, eid) + route[:, 4:6].astype(I32)).T
    tile_ids = jnp.arange(n_rows // EXP_TILE, dtype=I32)
    tile_expert = jnp.minimum(jnp.sum((tile_end[None, :] <= tile_ids[:, None]).astype(I32), axis=1),
                              N_EXPERTS - 1)

    n_used = tile_end[-1:].astype(I32)
    xs = _scatter_rows(tile_expert, n_used, dest_t, h2, n_rows)
    ys = _experts(tile_expert, n_used, xs,
                  w_gate.astype(BF16), w_up.astype(BF16), w_down.astype(BF16))
    return _final(dest_t, x1, route, p2d, ys, row(ple_norm), w_ple_gate.astype(BF16),
                  w_ple_proj.astype(BF16), out_gain)


def kernel(x, p, mix_norm, w_in, hg_lb_logits, hg_norm, ret_norm, w_branch_a, w_branch_b, w_out,
           ffn_norm, w_router_group, b_router_group, w_router_expert, b_router_expert,
           w_expert_gate, w_expert_up, w_expert_down, ple_norm, w_ple_gate, w_ple_proj, final_norm):
    batch, seq, d = x.shape
    depth = p.shape[0]
    assert depth == 1, "the final rmsnorm is fused into the single layer"
    x2d = x.reshape(batch * seq, d)
    out = _layer(x2d, p[0].reshape(batch * seq, -1), batch, seq, mix_norm[0], w_in[0], hg_lb_logits,
                 hg_norm[0], ret_norm[0], w_branch_a[0], w_branch_b[0], w_out[0], ffn_norm[0],
                 w_router_group[0], b_router_group[0], w_router_expert[0], b_router_expert[0],
                 w_expert_gate[0], w_expert_up[0], w_expert_down[0], ple_norm[0], w_ple_gate[0],
                 w_ple_proj[0], final_norm.reshape(1, -1).astype(F32))
    return out.reshape(batch, seq, d)
```

```python
import functools

import jax
import jax.numpy as jnp
from jax import lax
from jax.experimental import pallas as pl
from jax.experimental.pallas import tpu as pltpu

F32 = jnp.float32
BF16 = jnp.bfloat16
I32 = jnp.int32
U32 = jnp.uint32

EPS = 1e-6
D_MODEL = 1024
PLE_DIM = 256
HG_HEADS = 4
HG_DK = 128
HG_WIDTH = HG_HEADS * HG_DK
RET_HEADS = 4
RET_DK = 128
RET_DV = 256
ROPE_BASE = 10000.0
IN_TOTAL = 7168
N_GROUPS = 4
EXPERTS_PER_GROUP = 8
N_EXPERTS = 32
D_EXPERT = 256

COL_HQ, COL_HF, COL_HI, COL_HG = 0, 4, 8, 12
COL_RQ, COL_RK = 16, 20
COL_RV, COL_RG = 24, 32
COL_GA, COL_GB = 5, 6

LANES = 128
VMEM_LIMIT = 56 * 1024 * 1024

HG_CHUNK = 64
HG_SUB = 8
HG_UNROLL = 4
RET_CHUNK = 128
RET_UNROLL = 4
SEQ_TILE = 1024
TOK_TILE = 512
EXP_TILE = 256
SCAT_TILE = 1024


def _cparams(sem):
    return pltpu.CompilerParams(dimension_semantics=sem, vmem_limit_bytes=VMEM_LIMIT)


def _rms(x, g):
    return x * lax.rsqrt(jnp.mean(x * x, axis=-1, keepdims=True) + EPS) * g


def _sigmoid(x):
    return 1.0 / (1.0 + jnp.exp(-x))


def _silu(x):
    return x * _sigmoid(x)


def _split3(x):
    hi = x.astype(BF16)
    r1 = x - hi.astype(F32)
    mid = r1.astype(BF16)
    lo = (r1 - mid.astype(F32)).astype(BF16)
    return hi, mid, lo


def _dot(a, b):
    return jnp.dot(a, b, preferred_element_type=F32)


def _dot_nt(a, b):
    return lax.dot_general(a, b, (((1,), (1,)), ((), ())), preferred_element_type=F32)


def _dot_tn(a, b):
    return lax.dot_general(a, b, (((0,), (0,)), ((), ())), preferred_element_type=F32)


def _inproj_kernel(x_ref, g_ref, w_ref, proj_ref, hf_ref):
    h = _rms(x_ref[...], g_ref[...]).astype(BF16)
    tn = HG_WIDTH
    for j in range(IN_TOTAL // tn):
        acc = _dot(h, w_ref[:, j * tn:(j + 1) * tn])
        proj_ref[:, j * tn:(j + 1) * tn] = acc.astype(BF16)
        if j * tn == COL_HF * LANES:
            hf_ref[...] = acc


def _inproj(x2d, gain, w_bf16):
    t = x2d.shape[0]
    return pl.pallas_call(
        _inproj_kernel,
        grid=(t // TOK_TILE,),
        in_specs=[
            pl.BlockSpec((TOK_TILE, D_MODEL), lambda i: (i, 0)),
            pl.BlockSpec((1, D_MODEL), lambda i: (0, 0)),
            pl.BlockSpec((D_MODEL, IN_TOTAL), lambda i: (0, 0), pipeline_mode=pl.Buffered(1)),
        ],
        out_specs=[
            pl.BlockSpec((TOK_TILE, IN_TOTAL), lambda i: (i, 0)),
            pl.BlockSpec((TOK_TILE, HG_WIDTH), lambda i: (i, 0)),
        ],
        out_shape=[
            jax.ShapeDtypeStruct((t, IN_TOTAL), BF16),
            jax.ShapeDtypeStruct((t, HG_WIDTH), F32),
        ],
        compiler_params=_cparams(("arbitrary",)),
        name="inproj",
    )(x2d, gain, w_bf16)


def _hgrn_kernel(lbl_ref, q_ref, f_ref, i_ref, g_ref, ng_ref, o_ref,
                 st_ref, b_s, k_s, v_s, oi_s, qe_s, kv_s, dec_s):
    c = HG_CHUNK
    nsub = c // HG_SUB
    n_chunks = q_ref.shape[0] // c

    @pl.when(pl.program_id(2) == 0)
    def _():
        st_ref[...] = jnp.zeros_like(st_ref)

    logits = lbl_ref[...]
    e = jnp.exp(logits - jnp.max(logits, axis=0, keepdims=True))
    lb = e[0:1] / jnp.sum(e, axis=0, keepdims=True)
    one_m_lb = jnp.sum(e[1:], axis=0, keepdims=True) / jnp.sum(e, axis=0, keepdims=True)
    ng = ng_ref[...]

    row = lax.broadcasted_iota(I32, (c, c), 0)
    col = lax.broadcasted_iota(I32, (c, c), 1)
    tri = jnp.where(row >= col, 1.0, 0.0).astype(BF16)
    row_k = lax.broadcasted_iota(I32, (c, HG_DK), 0)
    sub_row = lax.broadcasted_iota(I32, (HG_SUB, HG_DK), 0)
    masked = jnp.float32(-1e30)

    def bcast_row(ref, r, rows):
        return jnp.broadcast_to(ref[pl.ds(r, 1), :], (rows, HG_DK))

    def prep(ci, slot):
        r0 = pl.multiple_of(ci * c, c)
        z = f_ref[pl.ds(r0, c), :]
        ez = jnp.exp(-jnp.abs(z))
        rz = 1.0 / (1.0 + ez)
        pos = z >= 0.0
        logf = jnp.log(lb + one_m_lb * jnp.where(pos, rz, ez * rz))
        kk = one_m_lb * jnp.where(pos, ez * rz, rz)
        q = _silu(q_ref[pl.ds(r0, c), :].astype(F32))
        v = i_ref[pl.ds(r0, c), :].astype(F32)
        k_s[slot] = kk
        v_s[slot] = v
        return dict(ci=ci, r0=r0, slot=slot, kk=kk, q=q, v16=v.astype(BF16), parts=_split3(logf))

    def cumulate(s):
        hi, mid, lo = s.pop("parts")
        b = (_dot(tri, lo) + _dot(tri, mid)) + _dot(tri, hi)
        b_s[s["slot"]] = b
        qe_s[pl.ds(s["r0"], c), :] = (s["q"] * jnp.exp(b)).astype(BF16)
        s["b"] = b
        return s

    def off_diagonal(s):
        b, q, kk = s["b"], s["q"], s["kk"]
        bs_ref = b_s.at[s["slot"]]
        a_rows = [jnp.zeros((HG_SUB, c), F32)]
        for i in range(1, nsub):
            bi = bcast_row(bs_ref, i * HG_SUB - 1, c)
            qi = q[i * HG_SUB:(i + 1) * HG_SUB, :] * jnp.exp(b[i * HG_SUB:(i + 1) * HG_SUB, :] - bi[:HG_SUB])
            ki = kk * jnp.exp(jnp.where(row_k < i * HG_SUB, bi - b, masked))
            a_rows.append(_dot_nt(qi.astype(BF16), ki.astype(BF16)))
        s["a_off"] = jnp.concatenate(a_rows, axis=0)
        return s

    def apply_values(s):
        b, kk = s["b"], s["kk"]
        blast = b[c - 1:c, :]
        s["o"] = _dot(s.pop("a_off").astype(BF16), s["v16"])
        kd = kk * jnp.exp(blast - b)
        kv_s[s["ci"]] = _dot_tn(s["v16"], kd.astype(BF16))
        dec_s[s["ci"]] = jnp.broadcast_to(jnp.exp(blast), (HG_SUB, HG_DK))
        return s

    def diagonal(s):
        b, q = s["b"], s["q"]
        bs_ref, ks_ref, vs_ref = b_s.at[s["slot"]], k_s.at[s["slot"]], v_s.at[s["slot"]]
        d_blocks = []
        for i in range(nsub):
            sl = slice(i * HG_SUB, (i + 1) * HG_SUB)
            bt, qt = b[sl, :], q[sl, :]
            acc = jnp.zeros((HG_SUB, HG_DK), F32)
            for j in range(HG_SUB):
                r = i * HG_SUB + j
                arg = jnp.where(sub_row >= j, bt - bcast_row(bs_ref, r, HG_SUB), masked)
                g = jnp.exp(arg) * (qt * bcast_row(ks_ref, r, HG_SUB))
                acc = acc + jnp.sum(g, axis=-1, keepdims=True) * bcast_row(vs_ref, r, HG_SUB)
            d_blocks.append(acc)
        oi_s[pl.ds(s["r0"], c), :] = s["o"] + jnp.concatenate(d_blocks, axis=0)

    def local_group(gi, carry):
        states = [prep(gi * HG_UNROLL + slot, slot) for slot in range(HG_UNROLL)]
        for stage in (cumulate, off_diagonal, apply_values, diagonal):
            states = [stage(s) for s in states]
        return carry

    lax.fori_loop(0, n_chunks // HG_UNROLL, local_group, 0)

    def carried_group(gi, carry):
        st = st_ref[...]
        outs = []
        for j in range(HG_UNROLL):
            ci = gi * HG_UNROLL + j
            r0 = pl.multiple_of(ci * c, c)
            outs.append((r0, _dot_nt(qe_s[pl.ds(r0, c), :], st.astype(BF16))))
            st = st * dec_s[ci][0:1, :] + kv_s[ci]
        st_ref[...] = st
        for r0, os in outs:
            y = _rms(oi_s[pl.ds(r0, c), :] + os, ng) * _silu(g_ref[pl.ds(r0, c), :].astype(F32))
            o_ref[pl.ds(r0, c), :] = y.astype(o_ref.dtype)
        return carry

    lax.fori_loop(0, n_chunks // HG_UNROLL, carried_group, 0)


def _hgrn(proj, hf, lb_logits, norm_g, batch, seq):
    ns = seq // SEQ_TILE
    tok = lambda b, h, s: b * ns + s
    return pl.pallas_call(
        _hgrn_kernel,
        grid=(batch, HG_HEADS, ns),
        in_specs=[
            pl.BlockSpec((2, HG_DK), lambda b, h, s: (0, h)),
            pl.BlockSpec((SEQ_TILE, HG_DK), lambda b, h, s: (tok(b, h, s), COL_HQ + h)),
            pl.BlockSpec((SEQ_TILE, HG_DK), lambda b, h, s: (tok(b, h, s), h)),
            pl.BlockSpec((SEQ_TILE, HG_DK), lambda b, h, s: (tok(b, h, s), COL_HI + h)),
            pl.BlockSpec((SEQ_TILE, HG_DK), lambda b, h, s: (tok(b, h, s), COL_HG + h)),
            pl.BlockSpec((1, HG_DK), lambda b, h, s: (0, 0)),
        ],
        out_specs=pl.BlockSpec((SEQ_TILE, HG_DK), lambda b, h, s: (tok(b, h, s), h)),
        out_shape=jax.ShapeDtypeStruct((batch * seq, HG_WIDTH), BF16),
        scratch_shapes=[
            pltpu.VMEM((HG_DK, HG_DK), F32),
            pltpu.VMEM((HG_UNROLL, HG_CHUNK, HG_DK), F32),
            pltpu.VMEM((HG_UNROLL, HG_CHUNK, HG_DK), F32),
            pltpu.VMEM((HG_UNROLL, HG_CHUNK, HG_DK), F32),
            pltpu.VMEM((SEQ_TILE, HG_DK), F32),
            pltpu.VMEM((SEQ_TILE, HG_DK), BF16),
            pltpu.VMEM((SEQ_TILE // HG_CHUNK, HG_DK, HG_DK), F32),
            pltpu.VMEM((SEQ_TILE // HG_CHUNK, HG_SUB, HG_DK), F32),
        ],
        compiler_params=_cparams(("arbitrary", "arbitrary", "arbitrary")),
        name="hgrn2",
    )(lb_logits, proj, hf, proj, proj, norm_g)


def _ret_kernel(q_ref, k_ref, v_ref, g_ref, cos_ref, sin_ref, ng_ref, o_ref,
                r_ref, oi_s, q16_s, kv_s):
    c = RET_CHUNK

    @pl.when(pl.program_id(2) == 0)
    def _():
        r_ref[...] = jnp.zeros_like(r_ref)

    hf = jnp.full((1, 1), pl.program_id(1), I32).astype(F32)
    lg = jnp.log1p(-jnp.exp2(-5.0 - hf))
    ti = lax.broadcasted_iota(I32, (c, c), 0)
    si = lax.broadcasted_iota(I32, (c, c), 1)
    rel = (ti - si).astype(F32)
    intra = jnp.where(ti >= si, jnp.exp(jnp.maximum(rel, 0.0) * lg), 0.0)
    idx = lax.broadcasted_iota(I32, (c, 1), 0).astype(F32)
    inter = jnp.exp((idx + 1.0) * lg)
    to_state = jnp.exp((c - 1.0 - idx) * lg)
    chunk_decay = jnp.exp(float(c) * lg)
    ng = ng_ref[...]
    half = RET_DK // 2

    n_chunks = q_ref.shape[0] // c

    def prep(ci):
        r0 = pl.multiple_of(ci * c, c)
        cos = cos_ref[pl.ds(r0, c), :]
        sin = sin_ref[pl.ds(r0, c), :]
        q = q_ref[pl.ds(r0, c), :].astype(F32)
        k = k_ref[pl.ds(r0, c), :].astype(F32)
        qr = (q * cos + pltpu.roll(q, half, 1) * sin) * (RET_DK ** -0.5)
        kr = k * cos + pltpu.roll(k, half, 1) * sin
        q16 = qr.astype(BF16)
        q16_s[pl.ds(r0, c), :] = q16
        return dict(ci=ci, r0=r0, q16=q16, k16=kr.astype(BF16), kts=(kr * to_state).astype(BF16),
                    v16=v_ref[pl.ds(r0, c), :])

    def scores(s):
        s["att"] = (_dot_nt(s.pop("q16"), s.pop("k16")) * intra).astype(BF16)
        return s

    def apply_values(s):
        oi_s[pl.ds(s["r0"], c), :] = _dot(s["att"], s["v16"])
        kv_s[s["ci"]] = _dot_tn(s["kts"], s["v16"])
        return s

    def local_group(gi, carry):
        states = [prep(gi * RET_UNROLL + j) for j in range(RET_UNROLL)]
        for stage in (scores, apply_values):
            states = [stage(s) for s in states]
        return carry

    lax.fori_loop(0, n_chunks // RET_UNROLL, local_group, 0)

    def carried_group(gi, carry):
        r = r_ref[...]
        outs = []
        for j in range(RET_UNROLL):
            ci = gi * RET_UNROLL + j
            r0 = pl.multiple_of(ci * c, c)
            outs.append((r0, _dot(q16_s[pl.ds(r0, c), :], r.astype(BF16))))
            r = chunk_decay * r + kv_s[ci]
        r_ref[...] = r
        for r0, qr_state in outs:
            o = oi_s[pl.ds(r0, c), :] + qr_state * inter
            y = _rms(o, ng) * _silu(g_ref[pl.ds(r0, c), :].astype(F32))
            o_ref[pl.ds(r0, c), :] = y.astype(o_ref.dtype)
        return carry

    lax.fori_loop(0, n_chunks // RET_UNROLL, carried_group, 0)


def _retention(proj, cos2, sin2, norm_g, batch, seq):
    ns = seq // SEQ_TILE
    tok = lambda b, h, s: b * ns + s
    return pl.pallas_call(
        _ret_kernel,
        grid=(batch, RET_HEADS, ns),
        in_specs=[
            pl.BlockSpec((SEQ_TILE, RET_DK), lambda b, h, s: (tok(b, h, s), COL_RQ + h)),
            pl.BlockSpec((SEQ_TILE, RET_DK), lambda b, h, s: (tok(b, h, s), COL_RK + h)),
            pl.BlockSpec((SEQ_TILE, RET_DV), lambda b, h, s: (tok(b, h, s), COL_RV // 2 + h)),
            pl.BlockSpec((SEQ_TILE, RET_DV), lambda b, h, s: (tok(b, h, s), COL_RG // 2 + h)),
            pl.BlockSpec((SEQ_TILE, RET_DK), lambda b, h, s: (s, 0)),
            pl.BlockSpec((SEQ_TILE, RET_DK), lambda b, h, s: (s, 0)),
            pl.BlockSpec((1, RET_DV), lambda b, h, s: (0, 0)),
        ],
        out_specs=pl.BlockSpec((SEQ_TILE, RET_DV), lambda b, h, s: (tok(b, h, s), h)),
        out_shape=jax.ShapeDtypeStruct((batch * seq, RET_HEADS * RET_DV), BF16),
        scratch_shapes=[
            pltpu.VMEM((RET_DK, RET_DV), F32),
            pltpu.VMEM((SEQ_TILE, RET_DV), F32),
            pltpu.VMEM((SEQ_TILE, RET_DK), BF16),
            pltpu.VMEM((SEQ_TILE // RET_CHUNK, RET_DK, RET_DV), F32),
        ],
        compiler_params=_cparams(("arbitrary", "arbitrary", "arbitrary")),
        name="retention",
    )(proj, proj, proj, proj, cos2, sin2, norm_g)


def _merge_kernel(x_ref, ya_ref, yb_ref, ga_ref, gb_ref, wa_ref, wb_ref, wo_ref, fg_ref,
                  wr_ref, br_ref, x1_ref, h2_ref, route_ref, cnt_ref, carry_ref):
    tm = x_ref.shape[0]

    @pl.when(pl.program_id(0) == 0)
    def _():
        carry_ref[...] = jnp.zeros_like(carry_ref)

    merged = (_sigmoid(ga_ref[...].astype(F32)) * _dot(ya_ref[...], wa_ref[...])
              + _sigmoid(gb_ref[...].astype(F32)) * _dot(yb_ref[...], wb_ref[...]))
    x1 = x_ref[...] + _dot(merged.astype(BF16), wo_ref[...])
    x1_ref[...] = x1
    h2 = _rms(x1, fg_ref[...])
    h2_ref[...] = h2

    hh, hm, _ = _split3(h2)
    wh, wm = wr_ref[0], wr_ref[1]
    logits = (_dot(hm, wh) + _dot(hh, wm)) + _dot(hh, wh) + br_ref[...]

    lane = lax.broadcasted_iota(I32, (tm, LANES), 1)
    neg = jnp.float32(-jnp.inf)
    big = jnp.int32(1 << 30)
    is_g = lane < N_GROUPS
    gl = jnp.where(is_g, logits, neg)
    gmax = jnp.max(gl, axis=-1, keepdims=True)
    g_idx = jnp.min(jnp.where(gl == gmax, lane, big), axis=-1, keepdims=True)
    g_w = 1.0 / jnp.sum(jnp.where(is_g, jnp.exp(gl - gmax), 0.0), axis=-1, keepdims=True)

    ex = lane - N_GROUPS
    in_grp = (ex >= g_idx * EXPERTS_PER_GROUP) & (ex < (g_idx + 1) * EXPERTS_PER_GROUP)
    el = jnp.where(in_grp, logits, neg)
    m1 = jnp.max(el, axis=-1, keepdims=True)
    e1 = jnp.min(jnp.where(el == m1, ex, big), axis=-1, keepdims=True)
    el2 = jnp.where(ex == e1, neg, el)
    m2 = jnp.max(el2, axis=-1, keepdims=True)
    e2 = jnp.min(jnp.where(el2 == m2, ex, big), axis=-1, keepdims=True)
    p2 = jnp.exp(m2 - m1)
    w1 = g_w / (1.0 + p2)
    w2 = g_w * p2 / (1.0 + p2)

    oh1 = ex == e1
    oh2 = ex == e2
    oh = jnp.where(oh1 | oh2, 1.0, 0.0)
    ri = lax.broadcasted_iota(I32, (tm, tm), 0)
    ci = lax.broadcasted_iota(I32, (tm, tm), 1)
    strict = jnp.where(ri > ci, 1.0, 0.0).astype(BF16)
    cum = _dot(strict, oh.astype(BF16)) + carry_ref[...]
    rank1 = jnp.sum(jnp.where(oh1, cum, 0.0), axis=-1, keepdims=True)
    rank2 = jnp.sum(jnp.where(oh2, cum, 0.0), axis=-1, keepdims=True)
    total = carry_ref[...] + jnp.sum(oh, axis=0, keepdims=True)
    carry_ref[...] = total
    cnt_ref[...] = total

    route = jnp.where(lane == 0, e1.astype(F32), 0.0)
    route = jnp.where(lane == 1, e2.astype(F32), route)
    route = jnp.where(lane == 2, w1, route)
    route = jnp.where(lane == 3, w2, route)
    route = jnp.where(lane == 4, rank1, route)
    route = jnp.where(lane == 5, rank2, route)
    route_ref[...] = route


def _merge(x2d, ya, yb, proj, wa, wb, wo, ffn_g, w_router, b_router):
    t = x2d.shape[0]
    const = lambda *shape: pl.BlockSpec(shape, lambda i: (0,) * len(shape))
    return pl.pallas_call(
        _merge_kernel,
        grid=(t // TOK_TILE,),
        in_specs=[
            pl.BlockSpec((TOK_TILE, D_MODEL), lambda i: (i, 0)),
            pl.BlockSpec((TOK_TILE, HG_WIDTH), lambda i: (i, 0)),
            pl.BlockSpec((TOK_TILE, D_MODEL), lambda i: (i, 0)),
            pl.BlockSpec((TOK_TILE, D_MODEL), lambda i: (i, COL_GA)),
            pl.BlockSpec((TOK_TILE, D_MODEL), lambda i: (i, COL_GB)),
            const(HG_WIDTH, D_MODEL),
            const(D_MODEL, D_MODEL),
            const(D_MODEL, D_MODEL),
            const(1, D_MODEL),
            const(2, D_MODEL, LANES),
            const(1, LANES),
        ],
        out_specs=[
            pl.BlockSpec((TOK_TILE, D_MODEL), lambda i: (i, 0)),
            pl.BlockSpec((TOK_TILE, D_MODEL), lambda i: (i, 0)),
            pl.BlockSpec((TOK_TILE, LANES), lambda i: (i, 0)),
            pl.BlockSpec((1, LANES), lambda i: (0, 0)),
        ],
        out_shape=[
            jax.ShapeDtypeStruct((t, D_MODEL), F32),
            jax.ShapeDtypeStruct((t, D_MODEL), F32),
            jax.ShapeDtypeStruct((t, LANES), F32),
            jax.ShapeDtypeStruct((1, LANES), F32),
        ],
        scratch_shapes=[pltpu.VMEM((1, LANES), F32)],
        compiler_params=_cparams(("arbitrary",)),
        name="merge_route",
    )(x2d, ya, yb, proj, proj, wa, wb, wo, ffn_g, w_router, b_router)


def _scatter_kernel(te_ref, nt_ref, dest_ref, h2_ref, xs_ref, zbuf, sem, zsem):
    n = dest_ref.shape[1]
    n_tiles = xs_ref.shape[0] // EXP_TILE

    @pl.when(pl.program_id(0) == 0)
    def _():
        zbuf[...] = jnp.zeros_like(zbuf)
        nt = nt_ref[0]

        def needs_zero(i):
            nxt = te_ref[jnp.minimum(i + 1, n_tiles - 1)]
            return (i >= nt - 1) | (nxt != te_ref[i])

        def zcopy(i):
            return pltpu.make_async_copy(zbuf, xs_ref.at[pl.ds(i * EXP_TILE, EXP_TILE)], zsem)

        def zstart(i, carry):
            @pl.when(needs_zero(i))
            def _():
                zcopy(i).start()
            return carry

        def zwait(i, carry):
            @pl.when(needs_zero(i))
            def _():
                zcopy(i).wait()
            return carry

        lax.fori_loop(0, n_tiles, zstart, 0)
        lax.fori_loop(0, n_tiles, zwait, 0)

    def copy(t, j):
        return pltpu.make_async_copy(
            h2_ref.at[pl.ds(t, 1)], xs_ref.at[pl.ds(dest_ref[j, t], 1)], sem)

    def start(t, carry):
        copy(t, 0).start()
        copy(t, 1).start()
        return carry

    def wait(t, carry):
        copy(t, 0).wait()
        copy(t, 1).wait()
        return carry

    lax.fori_loop(0, n, start, 0, unroll=8)
    lax.fori_loop(0, n, wait, 0, unroll=8)


def _scatter_rows(tile_expert, n_tiles_used, dest_t, h2, n_rows):
    t = h2.shape[0]
    grid_spec = pltpu.PrefetchScalarGridSpec(
        num_scalar_prefetch=2,
        grid=(t // SCAT_TILE,),
        in_specs=[
            pl.BlockSpec((2, SCAT_TILE), lambda i, te, nt: (0, i), memory_space=pltpu.SMEM),
            pl.BlockSpec((SCAT_TILE, D_MODEL), lambda i, te, nt: (i, 0)),
        ],
        out_specs=pl.BlockSpec(memory_space=pl.ANY),
        scratch_shapes=[pltpu.VMEM((EXP_TILE, D_MODEL), F32), pltpu.SemaphoreType.DMA(()),
                        pltpu.SemaphoreType.DMA(())],
    )
    return pl.pallas_call(
        _scatter_kernel,
        grid_spec=grid_spec,
        out_shape=jax.ShapeDtypeStruct((n_rows, D_MODEL), F32),
        compiler_params=_cparams(("arbitrary",)),
        name="scatter_rows",
    )(tile_expert, n_tiles_used, dest_t, h2)


def _expert_kernel(te_ref, nt_ref, xs_ref, wg_ref, wu_ref, wd_ref, ys_ref):
    del te_ref

    @pl.when(pl.program_id(0) < nt_ref[0])
    def _():
        x = xs_ref[...].astype(BF16)
        a = _dot(x, wg_ref[0])
        u = _dot(x, wu_ref[0])
        ys_ref[...] = _dot((_silu(a) * u).astype(BF16), wd_ref[0])

    @pl.when(pl.program_id(0) >= nt_ref[0])
    def _():
        ys_ref[...] = jnp.zeros_like(ys_ref)


def _experts(tile_expert, n_tiles_used, xs, wg, wu, wd):
    nr = xs.shape[0]
    grid_spec = pltpu.PrefetchScalarGridSpec(
        num_scalar_prefetch=2,
        grid=(nr // EXP_TILE,),
        in_specs=[
            pl.BlockSpec((EXP_TILE, D_MODEL), lambda i, te, nt: (i, 0)),
            pl.BlockSpec((1, D_MODEL, D_EXPERT), lambda i, te, nt: (te[i], 0, 0)),
            pl.BlockSpec((1, D_MODEL, D_EXPERT), lambda i, te, nt: (te[i], 0, 0)),
            pl.BlockSpec((1, D_EXPERT, D_MODEL), lambda i, te, nt: (te[i], 0, 0)),
        ],
        out_specs=pl.BlockSpec((EXP_TILE, D_MODEL), lambda i, te, nt: (i, 0)),
    )
    return pl.pallas_call(
        _expert_kernel,
        grid_spec=grid_spec,
        out_shape=jax.ShapeDtypeStruct((nr, D_MODEL), F32),
        compiler_params=_cparams(("arbitrary",)),
        name="experts",
    )(tile_expert, n_tiles_used, xs, wg, wu, wd)


def _final_kernel(dest_ref, x1_ref, route_ref, p_ref, ys_ref, pg_ref, wpg_ref, wpp_ref, fg_ref,
                  o_ref, ybuf, sem):
    tm = x1_ref.shape[0]

    def copy(t, j):
        return pltpu.make_async_copy(
            ys_ref.at[pl.ds(dest_ref[j, t], 1)], ybuf.at[j, pl.ds(t, 1)], sem)

    def start(t, carry):
        copy(t, 0).start()
        copy(t, 1).start()
        return carry

    def wait(t, carry):
        copy(t, 0).wait()
        copy(t, 1).wait()
        return carry

    lax.fori_loop(0, tm, start, 0, unroll=8)
    lax.fori_loop(0, tm, wait, 0, unroll=8)

    route = route_ref[...]
    w1 = route[:, 2:3]
    w2 = route[:, 3:4]
    x2 = x1_ref[...] + (w1 * ybuf[0] + w2 * ybuf[1])
    hp = _rms(x2, pg_ref[...]).astype(BF16)
    gate = _sigmoid(_dot(hp, wpg_ref[...]))
    x3 = x2 + gate * _dot(p_ref[...].astype(BF16), wpp_ref[...])
    o_ref[...] = _rms(x3, fg_ref[...])


def _final(dest_t, x1, route, p2d, ys, ple_g, wpg, wpp, final_g):
    t = x1.shape[0]
    const = lambda *shape: pl.BlockSpec(shape, lambda i: (0,) * len(shape))
    return pl.pallas_call(
        _final_kernel,
        grid=(t // TOK_TILE,),
        in_specs=[
            pl.BlockSpec((2, TOK_TILE), lambda i: (0, i), memory_space=pltpu.SMEM),
            pl.BlockSpec((TOK_TILE, D_MODEL), lambda i: (i, 0)),
            pl.BlockSpec((TOK_TILE, LANES), lambda i: (i, 0)),
            pl.BlockSpec((TOK_TILE, PLE_DIM), lambda i: (i, 0)),
            pl.BlockSpec(memory_space=pl.ANY),
            const(1, D_MODEL),
            const(D_MODEL, D_MODEL),
            const(PLE_DIM, D_MODEL),
            const(1, D_MODEL),
        ],
        out_specs=pl.BlockSpec((TOK_TILE, D_MODEL), lambda i: (i, 0)),
        out_shape=jax.ShapeDtypeStruct((t, D_MODEL), F32),
        scratch_shapes=[pltpu.VMEM((2, TOK_TILE, D_MODEL), F32), pltpu.SemaphoreType.DMA(())],
        compiler_params=_cparams(("arbitrary",)),
        name="combine_ple_final",
    )(dest_t, x1, route, p2d, ys, ple_g, wpg, wpp, final_g)


def _rotary_tables(seq):
    inv = ROPE_BASE ** (-jnp.arange(0, RET_DK, 2, dtype=F32) / RET_DK)
    ang = jnp.arange(seq, dtype=F32)[:, None] * inv[None, :]
    cos, sin = jnp.cos(ang), jnp.sin(ang)
    return jnp.concatenate([cos, cos], axis=1), jnp.concatenate([-sin, sin], axis=1)


def _layer(x2d, p2d, batch, seq, mix_norm, w_in, hg_lb_logits, hg_norm, ret_norm, w_branch_a,
           w_branch_b, w_out, ffn_norm, w_rg, b_rg, w_re, b_re, w_gate, w_up, w_down, ple_norm,
           w_ple_gate, w_ple_proj, out_gain):
    t = x2d.shape[0]
    row = lambda v: v.reshape(1, -1).astype(F32)

    proj, hf = _inproj(x2d, row(mix_norm), w_in.astype(BF16))
    ya = _hgrn(proj, hf, hg_lb_logits.astype(F32), row(hg_norm), batch, seq)
    cos2, sin2 = _rotary_tables(seq)
    yb = _retention(proj, cos2, sin2, row(ret_norm), batch, seq)

    n_r = N_GROUPS + N_EXPERTS
    w_router = jnp.zeros((D_MODEL, LANES), F32).at[:, :n_r].set(jnp.concatenate([w_rg, w_re], axis=1))
    wr_hi = w_router.astype(BF16)
    wr_mid = (w_router - wr_hi.astype(F32)).astype(BF16)
    b_router = jnp.zeros((1, LANES), F32).at[0, :n_r].set(jnp.concatenate([b_rg, b_re]))
    x1, h2, route, counts = _merge(
        x2d, ya, yb, proj, w_branch_a.astype(BF16), w_branch_b.astype(BF16), w_out.astype(BF16),
        row(ffn_norm), jnp.stack([wr_hi, wr_mid]), b_router)

    n_rows = 2 * t + N_EXPERTS * EXP_TILE
    cnt = counts[0, N_GROUPS:N_GROUPS + N_EXPERTS].astype(I32)
    tiles_per = (cnt + EXP_TILE - 1) // EXP_TILE
    tile_end = jnp.cumsum(tiles_per)
    offsets = (tile_end - tiles_per) * EXP_TILE
    eid = route[:, 0:2].astype(I32)
    dest_t = (jnp.take(offsets, eid) + route[:, 4:6].astype(I32)).T
    tile_ids = jnp.arange(n_rows // EXP_TILE, dtype=I32)
    tile_expert = jnp.minimum(jnp.sum((tile_end[None, :] <= tile_ids[:, None]).astype(I32), axis=1),
                              N_EXPERTS - 1)

    n_used = tile_end[-1:].astype(I32)
    xs = _scatter_rows(tile_expert, n_used, dest_t, h2, n_rows)
    ys = _experts(tile_expert, n_used, xs,
                  w_gate.astype(BF16), w_up.astype(BF16), w_down.astype(BF16))
    return _final(dest_t, x1, route, p2d, ys, row(ple_norm), w_ple_gate.astype(BF16),
                  w_ple_proj.astype(BF16), out_gain)


def kernel(x, p, mix_norm, w_in, hg_lb_logits, hg_norm, ret_norm, w_branch_a, w_branch_b, w_out,
           ffn_norm, w_router_group, b_router_group, w_router_expert, b_router_expert,
           w_expert_gate, w_expert_up, w_expert_down, ple_norm, w_ple_gate, w_ple_proj, final_norm):
    batch, seq, d = x.shape
    depth = p.shape[0]
    assert depth == 1, "the final rmsnorm is fused into the single layer"
    x2d = x.reshape(batch * seq, d)
    out = _layer(x2d, p[0].reshape(batch * seq, -1), batch, seq, mix_norm[0], w_in[0], hg_lb_logits,
                 hg_norm[0], ret_norm[0], w_branch_a[0], w_branch_b[0], w_out[0], ffn_norm[0],
                 w_router_group[0], b_router_group[0], w_router_expert[0], b_router_expert[0],
                 w_expert_gate[0], w_expert_up[0], w_expert_down[0], ple_norm[0], w_ple_gate[0],
                 w_ple_proj[0], final_norm.reshape(1, -1).astype(F32))
    return out.reshape(batch, seq, d)
```

```python
import functools

import jax
import jax.numpy as jnp
from jax import lax
from jax.experimental import pallas as pl
from jax.experimental.pallas import tpu as pltpu

F32 = jnp.float32
BF16 = jnp.bfloat16
I32 = jnp.int32
U32 = jnp.uint32

EPS = 1e-6
D_MODEL = 1024
PLE_DIM = 256
HG_HEADS = 4
HG_DK = 128
HG_WIDTH = HG_HEADS * HG_DK
RET_HEADS = 4
RET_DK = 128
RET_DV = 256
ROPE_BASE = 10000.0
IN_TOTAL = 7168
N_GROUPS = 4
EXPERTS_PER_GROUP = 8
N_EXPERTS = 32
D_EXPERT = 256

COL_HQ, COL_HF, COL_HI, COL_HG = 0, 4, 8, 12
COL_RQ, COL_RK = 16, 20
COL_RV, COL_RG = 24, 32
COL_GA, COL_GB = 5, 6

LANES = 128
VMEM_LIMIT = 56 * 1024 * 1024

HG_CHUNK = 64
HG_SUB = 8
HG_UNROLL = 4
RET_CHUNK = 128
RET_UNROLL = 4
SEQ_TILE = 1024
TOK_TILE = 512
EXP_TILE = 256
ROW_CHUNK = 16
CHUNKS_PER_TILE = EXP_TILE // ROW_CHUNK
LOCAL_ROWS = 2 * TOK_TILE + N_EXPERTS * ROW_CHUNK
LOCAL_CHUNKS = LOCAL_ROWS // ROW_CHUNK
SORT_SLAB = 256


def _cparams(sem):
    return pltpu.CompilerParams(dimension_semantics=sem, vmem_limit_bytes=VMEM_LIMIT)


def _rms(x, g):
    return x * lax.rsqrt(jnp.mean(x * x, axis=-1, keepdims=True) + EPS) * g


def _sigmoid(x):
    return 1.0 / (1.0 + jnp.exp(-x))


def _silu(x):
    return x * _sigmoid(x)


def _split3(x):
    hi = x.astype(BF16)
    r1 = x - hi.astype(F32)
    mid = r1.astype(BF16)
    lo = (r1 - mid.astype(F32)).astype(BF16)
    return hi, mid, lo


def _dot(a, b):
    return jnp.dot(a, b, preferred_element_type=F32)


def _dot_nt(a, b):
    return lax.dot_general(a, b, (((1,), (1,)), ((), ())), preferred_element_type=F32)


def _dot_tn(a, b):
    return lax.dot_general(a, b, (((0,), (0,)), ((), ())), preferred_element_type=F32)


def _inproj_kernel(x_ref, g_ref, w_ref, proj_ref, hf_ref):
    h = _rms(x_ref[...], g_ref[...]).astype(BF16)
    tn = HG_WIDTH
    for j in range(IN_TOTAL // tn):
        acc = _dot(h, w_ref[:, j * tn:(j + 1) * tn])
        proj_ref[:, j * tn:(j + 1) * tn] = acc.astype(BF16)
        if j * tn == COL_HF * LANES:
            hf_ref[...] = acc


def _inproj(x2d, gain, w_bf16):
    t = x2d.shape[0]
    return pl.pallas_call(
        _inproj_kernel,
        grid=(t // TOK_TILE,),
        in_specs=[
            pl.BlockSpec((TOK_TILE, D_MODEL), lambda i: (i, 0)),
            pl.BlockSpec((1, D_MODEL), lambda i: (0, 0)),
            pl.BlockSpec((D_MODEL, IN_TOTAL), lambda i: (0, 0), pipeline_mode=pl.Buffered(1)),
        ],
        out_specs=[
            pl.BlockSpec((TOK_TILE, IN_TOTAL), lambda i: (i, 0)),
            pl.BlockSpec((TOK_TILE, HG_WIDTH), lambda i: (i, 0)),
        ],
        out_shape=[
            jax.ShapeDtypeStruct((t, IN_TOTAL), BF16),
            jax.ShapeDtypeStruct((t, HG_WIDTH), F32),
        ],
        compiler_params=_cparams(("arbitrary",)),
        name="inproj",
    )(x2d, gain, w_bf16)


def _hgrn_kernel(lbl_ref, q_ref, f_ref, i_ref, g_ref, ng_ref, o_ref,
                 st_ref, b_s, k_s, v_s, oi_s, qe_s, kv_s, dec_s):
    c = HG_CHUNK
    nsub = c // HG_SUB
    n_chunks = q_ref.shape[0] // c

    @pl.when(pl.program_id(2) == 0)
    def _():
        st_ref[...] = jnp.zeros_like(st_ref)

    logits = lbl_ref[...]
    e = jnp.exp(logits - jnp.max(logits, axis=0, keepdims=True))
    lb = e[0:1] / jnp.sum(e, axis=0, keepdims=True)
    one_m_lb = jnp.sum(e[1:], axis=0, keepdims=True) / jnp.sum(e, axis=0, keepdims=True)
    ng = ng_ref[...]

    row = lax.broadcasted_iota(I32, (c, c), 0)
    col = lax.broadcasted_iota(I32, (c, c), 1)
    tri = jnp.where(row >= col, 1.0, 0.0).astype(BF16)
    row_k = lax.broadcasted_iota(I32, (c, HG_DK), 0)
    sub_row = lax.broadcasted_iota(I32, (HG_SUB, HG_DK), 0)
    masked = jnp.float32(-1e30)

    def bcast_row(ref, r, rows):
        return jnp.broadcast_to(ref[pl.ds(r, 1), :], (rows, HG_DK))

    def prep(ci, slot):
        r0 = pl.multiple_of(ci * c, c)
        z = f_ref[pl.ds(r0, c), :]
        ez = jnp.exp(-jnp.abs(z))
        rz = 1.0 / (1.0 + ez)
        pos = z >= 0.0
        logf = jnp.log(lb + one_m_lb * jnp.where(pos, rz, ez * rz))
        kk = one_m_lb * jnp.where(pos, ez * rz, rz)
        q = _silu(q_ref[pl.ds(r0, c), :].astype(F32))
        v = i_ref[pl.ds(r0, c), :].astype(F32)
        k_s[slot] = kk
        v_s[slot] = v
        return dict(ci=ci, r0=r0, slot=slot, kk=kk, q=q, v16=v.astype(BF16), parts=_split3(logf))

    def cumulate(s):
        hi, mid, lo = s.pop("parts")
        b = (_dot(tri, lo) + _dot(tri, mid)) + _dot(tri, hi)
        b_s[s["slot"]] = b
        qe_s[pl.ds(s["r0"], c), :] = (s["q"] * jnp.exp(b)).astype(BF16)
        s["b"] = b
        return s

    def off_diagonal(s):
        b, q, kk = s["b"], s["q"], s["kk"]
        bs_ref = b_s.at[s["slot"]]
        a_rows = [jnp.zeros((HG_SUB, c), F32)]
        for i in range(1, nsub):
            bi = bcast_row(bs_ref, i * HG_SUB - 1, c)
            qi = q[i * HG_SUB:(i + 1) * HG_SUB, :] * jnp.exp(b[i * HG_SUB:(i + 1) * HG_SUB, :] - bi[:HG_SUB])
            ki = kk * jnp.exp(jnp.where(row_k < i * HG_SUB, bi - b, masked))
            a_rows.append(_dot_nt(qi.astype(BF16), ki.astype(BF16)))
        s["a_off"] = jnp.concatenate(a_rows, axis=0)
        return s

    def apply_values(s):
        b, kk = s["b"], s["kk"]
        blast = b[c - 1:c, :]
        s["o"] = _dot(s.pop("a_off").astype(BF16), s["v16"])
        kd = kk * jnp.exp(blast - b)
        kv_s[s["ci"]] = _dot_tn(s["v16"], kd.astype(BF16))
        dec_s[s["ci"]] = jnp.broadcast_to(jnp.exp(blast), (HG_SUB, HG_DK))
        return s

    def diagonal(s):
        b, q = s["b"], s["q"]
        bs_ref, ks_ref, vs_ref = b_s.at[s["slot"]], k_s.at[s["slot"]], v_s.at[s["slot"]]
        d_blocks = []
        for i in range(nsub):
            sl = slice(i * HG_SUB, (i + 1) * HG_SUB)
            bt, qt = b[sl, :], q[sl, :]
            acc = jnp.zeros((HG_SUB, HG_DK), F32)
            for j in range(HG_SUB):
                r = i * HG_SUB + j
                arg = jnp.where(sub_row >= j, bt - bcast_row(bs_ref, r, HG_SUB), masked)
                g = jnp.exp(arg) * (qt * bcast_row(ks_ref, r, HG_SUB))
                acc = acc + jnp.sum(g, axis=-1, keepdims=True) * bcast_row(vs_ref, r, HG_SUB)
            d_blocks.append(acc)
        oi_s[pl.ds(s["r0"], c), :] = s["o"] + jnp.concatenate(d_blocks, axis=0)

    def local_group(gi, carry):
        states = [prep(gi * HG_UNROLL + slot, slot) for slot in range(HG_UNROLL)]
        for stage in (cumulate, off_diagonal, apply_values, diagonal):
            states = [stage(s) for s in states]
        return carry

    lax.fori_loop(0, n_chunks // HG_UNROLL, local_group, 0)

    def carried_group(gi, carry):
        st = st_ref[...]
        outs = []
        for j in range(HG_UNROLL):
            ci = gi * HG_UNROLL + j
            r0 = pl.multiple_of(ci * c, c)
            outs.append((r0, _dot_nt(qe_s[pl.ds(r0, c), :], st.astype(BF16))))
            st = st * dec_s[ci][0:1, :] + kv_s[ci]
        st_ref[...] = st
        for r0, os in outs:
            y = _rms(oi_s[pl.ds(r0, c), :] + os, ng) * _silu(g_ref[pl.ds(r0, c), :].astype(F32))
            o_ref[pl.ds(r0, c), :] = y.astype(o_ref.dtype)
        return carry

    lax.fori_loop(0, n_chunks // HG_UNROLL, carried_group, 0)


def _hgrn(proj, hf, lb_logits, norm_g, batch, seq):
    ns = seq // SEQ_TILE
    tok = lambda b, h, s: b * ns + s
    return pl.pallas_call(
        _hgrn_kernel,
        grid=(batch, HG_HEADS, ns),
        in_specs=[
            pl.BlockSpec((2, HG_DK), lambda b, h, s: (0, h)),
            pl.BlockSpec((SEQ_TILE, HG_DK), lambda b, h, s: (tok(b, h, s), COL_HQ + h)),
            pl.BlockSpec((SEQ_TILE, HG_DK), lambda b, h, s: (tok(b, h, s), h)),
            pl.BlockSpec((SEQ_TILE, HG_DK), lambda b, h, s: (tok(b, h, s), COL_HI + h)),
            pl.BlockSpec((SEQ_TILE, HG_DK), lambda b, h, s: (tok(b, h, s), COL_HG + h)),
            pl.BlockSpec((1, HG_DK), lambda b, h, s: (0, 0)),
        ],
        out_specs=pl.BlockSpec((SEQ_TILE, HG_DK), lambda b, h, s: (tok(b, h, s), h)),
        out_shape=jax.ShapeDtypeStruct((batch * seq, HG_WIDTH), BF16),
        scratch_shapes=[
            pltpu.VMEM((HG_DK, HG_DK), F32),
            pltpu.VMEM((HG_UNROLL, HG_CHUNK, HG_DK), F32),
            pltpu.VMEM((HG_UNROLL, HG_CHUNK, HG_DK), F32),
            pltpu.VMEM((HG_UNROLL, HG_CHUNK, HG_DK), F32),
            pltpu.VMEM((SEQ_TILE, HG_DK), F32),
            pltpu.VMEM((SEQ_TILE, HG_DK), BF16),
            pltpu.VMEM((SEQ_TILE // HG_CHUNK, HG_DK, HG_DK), F32),
            pltpu.VMEM((SEQ_TILE // HG_CHUNK, HG_SUB, HG_DK), F32),
        ],
        compiler_params=_cparams(("arbitrary", "arbitrary", "arbitrary")),
        name="hgrn2",
    )(lb_logits, proj, hf, proj, proj, norm_g)


def _ret_kernel(q_ref, k_ref, v_ref, g_ref, cos_ref, sin_ref, ng_ref, o_ref,
                r_ref, oi_s, q16_s, kv_s):
    c = RET_CHUNK

    @pl.when(pl.program_id(2) == 0)
    def _():
        r_ref[...] = jnp.zeros_like(r_ref)

    hf = jnp.full((1, 1), pl.program_id(1), I32).astype(F32)
    lg = jnp.log1p(-jnp.exp2(-5.0 - hf))
    ti = lax.broadcasted_iota(I32, (c, c), 0)
    si = lax.broadcasted_iota(I32, (c, c), 1)
    rel = (ti - si).astype(F32)
    intra = jnp.where(ti >= si, jnp.exp(jnp.maximum(rel, 0.0) * lg), 0.0)
    idx = lax.broadcasted_iota(I32, (c, 1), 0).astype(F32)
    inter = jnp.exp((idx + 1.0) * lg)
    to_state = jnp.exp((c - 1.0 - idx) * lg)
    chunk_decay = jnp.exp(float(c) * lg)
    ng = ng_ref[...]
    half = RET_DK // 2

    n_chunks = q_ref.shape[0] // c

    def prep(ci):
        r0 = pl.multiple_of(ci * c, c)
        cos = cos_ref[pl.ds(r0, c), :]
        sin = sin_ref[pl.ds(r0, c), :]
        q = q_ref[pl.ds(r0, c), :].astype(F32)
        k = k_ref[pl.ds(r0, c), :].astype(F32)
        qr = (q * cos + pltpu.roll(q, half, 1) * sin) * (RET_DK ** -0.5)
        kr = k * cos + pltpu.roll(k, half, 1) * sin
        q16 = qr.astype(BF16)
        q16_s[pl.ds(r0, c), :] = q16
        return dict(ci=ci, r0=r0, q16=q16, k16=kr.astype(BF16), kts=(kr * to_state).astype(BF16),
                    v16=v_ref[pl.ds(r0, c), :])

    def scores(s):
        s["att"] = (_dot_nt(s.pop("q16"), s.pop("k16")) * intra).astype(BF16)
        return s

    def apply_values(s):
        oi_s[pl.ds(s["r0"], c), :] = _dot(s["att"], s["v16"])
        kv_s[s["ci"]] = _dot_tn(s["kts"], s["v16"])
        return s

    def local_group(gi, carry):
        states = [prep(gi * RET_UNROLL + j) for j in range(RET_UNROLL)]
        for stage in (scores, apply_values):
            states = [stage(s) for s in states]
        return carry

    lax.fori_loop(0, n_chunks // RET_UNROLL, local_group, 0)

    def carried_group(gi, carry):
        r = r_ref[...]
        outs = []
        for j in range(RET_UNROLL):
            ci = gi * RET_UNROLL + j
            r0 = pl.multiple_of(ci * c, c)
            outs.append((r0, _dot(q16_s[pl.ds(r0, c), :], r.astype(BF16))))
            r = chunk_decay * r + kv_s[ci]
        r_ref[...] = r
        for r0, qr_state in outs:
            o = oi_s[pl.ds(r0, c), :] + qr_state * inter
            y = _rms(o, ng) * _silu(g_ref[pl.ds(r0, c), :].astype(F32))
            o_ref[pl.ds(r0, c), :] = y.astype(o_ref.dtype)
        return carry

    lax.fori_loop(0, n_chunks // RET_UNROLL, carried_group, 0)


def _retention(proj, cos2, sin2, norm_g, batch, seq):
    ns = seq // SEQ_TILE
    tok = lambda b, h, s: b * ns + s
    return pl.pallas_call(
        _ret_kernel,
        grid=(batch, RET_HEADS, ns),
        in_specs=[
            pl.BlockSpec((SEQ_TILE, RET_DK), lambda b, h, s: (tok(b, h, s), COL_RQ + h)),
            pl.BlockSpec((SEQ_TILE, RET_DK), lambda b, h, s: (tok(b, h, s), COL_RK + h)),
            pl.BlockSpec((SEQ_TILE, RET_DV), lambda b, h, s: (tok(b, h, s), COL_RV // 2 + h)),
            pl.BlockSpec((SEQ_TILE, RET_DV), lambda b, h, s: (tok(b, h, s), COL_RG // 2 + h)),
            pl.BlockSpec((SEQ_TILE, RET_DK), lambda b, h, s: (s, 0)),
            pl.BlockSpec((SEQ_TILE, RET_DK), lambda b, h, s: (s, 0)),
            pl.BlockSpec((1, RET_DV), lambda b, h, s: (0, 0)),
        ],
        out_specs=pl.BlockSpec((SEQ_TILE, RET_DV), lambda b, h, s: (tok(b, h, s), h)),
        out_shape=jax.ShapeDtypeStruct((batch * seq, RET_HEADS * RET_DV), BF16),
        scratch_shapes=[
            pltpu.VMEM((RET_DK, RET_DV), F32),
            pltpu.VMEM((SEQ_TILE, RET_DV), F32),
            pltpu.VMEM((SEQ_TILE, RET_DK), BF16),
            pltpu.VMEM((SEQ_TILE // RET_CHUNK, RET_DK, RET_DV), F32),
        ],
        compiler_params=_cparams(("arbitrary", "arbitrary", "arbitrary")),
        name="retention",
    )(proj, proj, proj, proj, cos2, sin2, norm_g)


def _merge_kernel(x_ref, ya_ref, yb_ref, ga_ref, gb_ref, wa_ref, wb_ref, wo_ref, fg_ref,
                  wr_ref, br_ref, x1_ref, xs_ref, route_ref, cnt_ref):
    tm = x_ref.shape[0]

    merged = (_sigmoid(ga_ref[...].astype(F32)) * _dot(ya_ref[...], wa_ref[...])
              + _sigmoid(gb_ref[...].astype(F32)) * _dot(yb_ref[...], wb_ref[...]))
    x1 = x_ref[...] + _dot(merged.astype(BF16), wo_ref[...])
    x1_ref[...] = x1
    h2 = _rms(x1, fg_ref[...])

    hh, hm, _ = _split3(h2)
    wh, wm = wr_ref[0], wr_ref[1]
    logits = (_dot(hm, wh) + _dot(hh, wm)) + _dot(hh, wh) + br_ref[...]

    lane = lax.broadcasted_iota(I32, (tm, LANES), 1)
    neg = jnp.float32(-jnp.inf)
    big = jnp.int32(1 << 30)
    is_g = lane < N_GROUPS
    gl = jnp.where(is_g, logits, neg)
    gmax = jnp.max(gl, axis=-1, keepdims=True)
    g_idx = jnp.min(jnp.where(gl == gmax, lane, big), axis=-1, keepdims=True)
    g_w = 1.0 / jnp.sum(jnp.where(is_g, jnp.exp(gl - gmax), 0.0), axis=-1, keepdims=True)

    ex = lane - N_GROUPS
    in_grp = (ex >= g_idx * EXPERTS_PER_GROUP) & (ex < (g_idx + 1) * EXPERTS_PER_GROUP)
    el = jnp.where(in_grp, logits, neg)
    m1 = jnp.max(el, axis=-1, keepdims=True)
    e1 = jnp.min(jnp.where(el == m1, ex, big), axis=-1, keepdims=True)
    el2 = jnp.where(ex == e1, neg, el)
    m2 = jnp.max(el2, axis=-1, keepdims=True)
    e2 = jnp.min(jnp.where(el2 == m2, ex, big), axis=-1, keepdims=True)
    p2 = jnp.exp(m2 - m1)
    w1 = g_w / (1.0 + p2)
    w2 = g_w * p2 / (1.0 + p2)

    oh1 = ex == e1
    oh2 = ex == e2
    oh = jnp.where(oh1 | oh2, 1.0, 0.0)
    ri = lax.broadcasted_iota(I32, (tm, tm), 0)
    ci = lax.broadcasted_iota(I32, (tm, tm), 1)
    strict = jnp.where(ri > ci, 1.0, 0.0).astype(BF16)
    local_rank = _dot(strict, oh.astype(BF16))
    cnt = jnp.sum(oh, axis=0, keepdims=True)
    run_chunks = jnp.floor((cnt + (ROW_CHUNK - 1.0)) * (1.0 / ROW_CHUNK))
    ui = lax.broadcasted_iota(I32, (LANES, LANES), 0)
    uj = lax.broadcasted_iota(I32, (LANES, LANES), 1)
    before = jnp.where(ui < uj, 1.0, 0.0).astype(BF16)
    run_start = _dot(jnp.broadcast_to(run_chunks, (8, LANES)).astype(BF16), before)[0:1] * ROW_CHUNK
    slot = run_start + local_rank
    pos1 = jnp.sum(jnp.where(oh1, slot, 0.0), axis=-1, keepdims=True)
    pos2 = jnp.sum(jnp.where(oh2, slot, 0.0), axis=-1, keepdims=True)
    cnt_ref[0] = jnp.broadcast_to(cnt, (8, LANES))

    route = jnp.where(lane == 0, e1.astype(F32), 0.0)
    route = jnp.where(lane == 1, e2.astype(F32), route)
    route = jnp.where(lane == 2, w1, route)
    route = jnp.where(lane == 3, w2, route)
    route = jnp.where(lane == 4, pos1, route)
    route = jnp.where(lane == 5, pos2, route)
    route_ref[...] = route

    hi1 = jnp.floor(pos1 * (1.0 / 256.0))
    hi2 = jnp.floor(pos2 * (1.0 / 256.0))
    digits = jnp.where(lane == 0, hi1, 0.0)
    digits = jnp.where(lane == 1, pos1 - 256.0 * hi1, digits)
    digits = jnp.where(lane == 2, hi2, digits)
    digits = jnp.where(lane == 3, pos2 - 256.0 * hi2, digits)
    pick = jnp.where(lax.broadcasted_iota(I32, (8, LANES), 0) == lax.broadcasted_iota(I32, (8, LANES), 1),
                     1.0, 0.0).astype(BF16)
    rows = _dot_nt(pick, digits.astype(BF16))
    pos1_row = rows[0:1] * 256.0 + rows[1:2]
    pos2_row = rows[2:3] * 256.0 + rows[3:4]

    h2b = h2.astype(BF16)
    slab_row = lax.broadcasted_iota(I32, (SORT_SLAB, tm), 0).astype(F32)
    for r0 in range(0, LOCAL_ROWS, SORT_SLAB):
        sel = (slab_row == pos1_row - float(r0)) | (slab_row == pos2_row - float(r0))
        xs_ref[r0:r0 + SORT_SLAB, :] = _dot(jnp.where(sel, 1.0, 0.0).astype(BF16), h2b).astype(BF16)


def _merge(x2d, ya, yb, proj, wa, wb, wo, ffn_g, w_router, b_router):
    t = x2d.shape[0]
    const = lambda *shape: pl.BlockSpec(shape, lambda i: (0,) * len(shape))
    return pl.pallas_call(
        _merge_kernel,
        grid=(t // TOK_TILE,),
        in_specs=[
            pl.BlockSpec((TOK_TILE, D_MODEL), lambda i: (i, 0)),
            pl.BlockSpec((TOK_TILE, HG_WIDTH), lambda i: (i, 0)),
            pl.BlockSpec((TOK_TILE, D_MODEL), lambda i: (i, 0)),
            pl.BlockSpec((TOK_TILE, D_MODEL), lambda i: (i, COL_GA)),
            pl.BlockSpec((TOK_TILE, D_MODEL), lambda i: (i, COL_GB)),
            const(HG_WIDTH, D_MODEL),
            const(D_MODEL, D_MODEL),
            const(D_MODEL, D_MODEL),
            const(1, D_MODEL),
            const(2, D_MODEL, LANES),
            const(1, LANES),
        ],
        out_specs=[
            pl.BlockSpec((TOK_TILE, D_MODEL), lambda i: (i, 0)),
            pl.BlockSpec((LOCAL_ROWS, D_MODEL), lambda i: (i, 0)),
            pl.BlockSpec((TOK_TILE, LANES), lambda i: (i, 0)),
            pl.BlockSpec((1, 8, LANES), lambda i: (i, 0, 0)),
        ],
        out_shape=[
            jax.ShapeDtypeStruct((t, D_MODEL), F32),
            jax.ShapeDtypeStruct((t // TOK_TILE * LOCAL_ROWS, D_MODEL), BF16),
            jax.ShapeDtypeStruct((t, LANES), F32),
            jax.ShapeDtypeStruct((t // TOK_TILE, 8, LANES), F32),
        ],
        compiler_params=_cparams(("arbitrary",)),
        name="merge_route",
    )(x2d, ya, yb, proj, proj, wa, wb, wo, ffn_g, w_router, b_router)


def _expert_kernel(te_ref, nt_ref, src_ref, xs_ref, wg_ref, wu_ref, wd_ref, ys_ref, xbuf, sem):
    i = pl.program_id(0)
    nt = nt_ref[0]

    def gather(tile, slot):
        copies = []
        for c in range(CHUNKS_PER_TILE):
            row = pl.multiple_of(src_ref[tile * CHUNKS_PER_TILE + c] * ROW_CHUNK, ROW_CHUNK)
            copies.append(pltpu.make_async_copy(
                xs_ref.at[pl.ds(row, ROW_CHUNK)],
                xbuf.at[slot, pl.ds(c * ROW_CHUNK, ROW_CHUNK)], sem.at[slot]))
        return copies

    @pl.when((i == 0) & (nt > 0))
    def _():
        for cp in gather(0, 0):
            cp.start()

    @pl.when(i + 1 < nt)
    def _():
        for cp in gather(i + 1, (i + 1) % 2):
            cp.start()

    @pl.when(i < nt)
    def _():
        slot = i % 2
        for cp in gather(i, slot):
            cp.wait()
        x = xbuf[slot]
        a = _dot(x, wg_ref[0].astype(BF16))
        u = _dot(x, wu_ref[0].astype(BF16))
        ys_ref[...] = _dot((_silu(a) * u).astype(BF16), wd_ref[0].astype(BF16)).astype(ys_ref.dtype)

    @pl.when(i >= nt)
    def _():
        ys_ref[...] = jnp.zeros_like(ys_ref)


def _experts(tile_expert, n_tiles_used, src_chunk, xs, wg, wu, wd, n_tiles):
    grid_spec = pltpu.PrefetchScalarGridSpec(
        num_scalar_prefetch=3,
        grid=(n_tiles,),
        in_specs=[
            pl.BlockSpec(memory_space=pl.ANY),
            pl.BlockSpec((1, D_MODEL, D_EXPERT), lambda i, te, nt, src: (te[i], 0, 0)),
            pl.BlockSpec((1, D_MODEL, D_EXPERT), lambda i, te, nt, src: (te[i], 0, 0)),
            pl.BlockSpec((1, D_EXPERT, D_MODEL), lambda i, te, nt, src: (te[i], 0, 0)),
        ],
        out_specs=pl.BlockSpec((EXP_TILE, D_MODEL), lambda i, te, nt, src: (i, 0)),
        scratch_shapes=[pltpu.VMEM((2, EXP_TILE, D_MODEL), BF16), pltpu.SemaphoreType.DMA((2,))],
    )
    return pl.pallas_call(
        _expert_kernel,
        grid_spec=grid_spec,
        out_shape=jax.ShapeDtypeStruct((n_tiles * EXP_TILE, D_MODEL), BF16),
        compiler_params=_cparams(("arbitrary",)),
        name="experts",
    )(tile_expert, n_tiles_used, src_chunk, xs, wg, wu, wd)


def _final_kernel(src_ref, x1_ref, route_ref, p_ref, ys_ref, pg_ref, wpg_ref, wpp_ref, fg_ref,
                  o_ref, ybuf, sem):
    tm = x1_ref.shape[0]
    i = pl.program_id(0)
    nb = pl.num_programs(0)

    def gather(block, slot):
        copies = []
        for c in range(LOCAL_CHUNKS):
            row = pl.multiple_of(src_ref[block * LOCAL_CHUNKS + c] * ROW_CHUNK, ROW_CHUNK)
            copies.append(pltpu.make_async_copy(
                ys_ref.at[pl.ds(row, ROW_CHUNK)],
                ybuf.at[slot, pl.ds(c * ROW_CHUNK, ROW_CHUNK)], sem.at[slot]))
        return copies

    @pl.when(i == 0)
    def _():
        for cp in gather(0, 0):
            cp.start()

    @pl.when(i + 1 < nb)
    def _():
        for cp in gather(i + 1, (i + 1) % 2):
            cp.start()

    slot = i % 2
    for cp in gather(i, slot):
        cp.wait()

    route = route_ref[...]
    w1, w2 = route[:, 2:3], route[:, 3:4]
    pos1, pos2 = route[:, 4:5], route[:, 5:6]
    slab_col = lax.broadcasted_iota(I32, (tm, SORT_SLAB), 1).astype(F32)
    moe = jnp.zeros((tm, D_MODEL), F32)
    for k0 in range(0, LOCAL_ROWS, SORT_SLAB):
        sel = jnp.where(slab_col == pos1 - float(k0), w1,
                        jnp.where(slab_col == pos2 - float(k0), w2, 0.0)).astype(BF16)
        moe = moe + _dot(sel, ybuf[slot, k0:k0 + SORT_SLAB, :])
    x2 = x1_ref[...] + moe
    hp = _rms(x2, pg_ref[...]).astype(BF16)
    gate = _sigmoid(_dot(hp, wpg_ref[...]))
    x3 = x2 + gate * _dot(p_ref[...].astype(BF16), wpp_ref[...])
    o_ref[...] = _rms(x3, fg_ref[...])


def _final(src_chunk, x1, route, p2d, ys, ple_g, wpg, wpp, final_g):
    t = x1.shape[0]
    const = lambda *shape: pl.BlockSpec(shape, lambda i, src: (0,) * len(shape))
    grid_spec = pltpu.PrefetchScalarGridSpec(
        num_scalar_prefetch=1,
        grid=(t // TOK_TILE,),
        in_specs=[
            pl.BlockSpec((TOK_TILE, D_MODEL), lambda i, src: (i, 0)),
            pl.BlockSpec((TOK_TILE, LANES), lambda i, src: (i, 0)),
            pl.BlockSpec((TOK_TILE, PLE_DIM), lambda i, src: (i, 0)),
            pl.BlockSpec(memory_space=pl.ANY),
            const(1, D_MODEL),
            const(D_MODEL, D_MODEL),
            const(PLE_DIM, D_MODEL),
            const(1, D_MODEL),
        ],
        out_specs=pl.BlockSpec((TOK_TILE, D_MODEL), lambda i, src: (i, 0)),
        scratch_shapes=[pltpu.VMEM((2, LOCAL_ROWS, D_MODEL), BF16), pltpu.SemaphoreType.DMA((2,))],
    )
    return pl.pallas_call(
        _final_kernel,
        grid_spec=grid_spec,
        out_shape=jax.ShapeDtypeStruct((t, D_MODEL), F32),
        compiler_params=_cparams(("arbitrary",)),
        name="combine_ple_final",
    )(src_chunk, x1, route, p2d, ys, ple_g, wpg, wpp, final_g)


def _rotary_tables(seq):
    inv = ROPE_BASE ** (-jnp.arange(0, RET_DK, 2, dtype=F32) / RET_DK)
    ang = jnp.arange(seq, dtype=F32)[:, None] * inv[None, :]
    cos, sin = jnp.cos(ang), jnp.sin(ang)
    return jnp.concatenate([cos, cos], axis=1), jnp.concatenate([-sin, sin], axis=1)


def _layer(x2d, p2d, batch, seq, mix_norm, w_in, hg_lb_logits, hg_norm, ret_norm, w_branch_a,
           w_branch_b, w_out, ffn_norm, w_rg, b_rg, w_re, b_re, w_gate, w_up, w_down, ple_norm,
           w_ple_gate, w_ple_proj, out_gain):
    t = x2d.shape[0]
    row = lambda v: v.reshape(1, -1).astype(F32)

    proj, hf = _inproj(x2d, row(mix_norm), w_in.astype(BF16))
    ya = _hgrn(proj, hf, hg_lb_logits.astype(F32), row(hg_norm), batch, seq)
    cos2, sin2 = _rotary_tables(seq)
    yb = _retention(proj, cos2, sin2, row(ret_norm), batch, seq)

    n_r = N_GROUPS + N_EXPERTS
    w_router = jnp.zeros((D_MODEL, LANES), F32).at[:, :n_r].set(jnp.concatenate([w_rg, w_re], axis=1))
    wr_hi = w_router.astype(BF16)
    wr_mid = (w_router - wr_hi.astype(F32)).astype(BF16)
    b_router = jnp.zeros((1, LANES), F32).at[0, :n_r].set(jnp.concatenate([b_rg, b_re]))
    x1, xs, route, counts = _merge(
        x2d, ya, yb, proj, w_branch_a.astype(BF16), w_branch_b.astype(BF16), w_out.astype(BF16),
        row(ffn_norm), jnp.stack([wr_hi, wr_mid]), b_router)

    n_blocks = t // TOK_TILE
    cnt = counts[:, 0, N_GROUPS:N_GROUPS + N_EXPERTS].astype(I32)
    run_chunks = (cnt + ROW_CHUNK - 1) // ROW_CHUNK
    run_local = jnp.cumsum(run_chunks, axis=1) - run_chunks
    seg_chunks = jnp.sum(run_chunks, axis=0)
    tiles_per = (seg_chunks + CHUNKS_PER_TILE - 1) // CHUNKS_PER_TILE
    tile_end = jnp.cumsum(tiles_per)
    seg_start = (tile_end - tiles_per) * CHUNKS_PER_TILE
    run_global = seg_start[None, :] + jnp.cumsum(run_chunks, axis=0) - run_chunks
    max_chunks = (2 * t) // ROW_CHUNK + n_blocks * N_EXPERTS + N_EXPERTS * (CHUNKS_PER_TILE - 1)
    n_tiles = -(-max_chunks // CHUNKS_PER_TILE) + 1
    tile_ids = jnp.arange(n_tiles, dtype=I32)
    tile_expert = jnp.minimum(jnp.sum((tile_end[None, :] <= tile_ids[:, None]).astype(I32), axis=1),
                              N_EXPERTS - 1)
    n_used = tile_end[-1:].astype(I32)

    block_ids = jnp.arange(n_blocks, dtype=I32)
    zero_local = LOCAL_CHUNKS - 1
    zero_global = n_tiles * CHUNKS_PER_TILE - 1
    g = jnp.arange(n_tiles * CHUNKS_PER_TILE, dtype=I32)[:, None]
    e_g = tile_expert[g[:, 0] // CHUNKS_PER_TILE]
    rg, rc, rl = run_global.T[e_g], run_chunks.T[e_g], run_local.T[e_g]
    inside = (rg <= g) & (g < rg + rc)
    gather_src = jnp.sum(jnp.where(inside, block_ids[None, :] * LOCAL_CHUNKS + rl + (g - rg), 0), axis=1)
    gather_src = jnp.where(jnp.any(inside, axis=1), gather_src, zero_local).astype(I32)

    lc = jnp.arange(LOCAL_CHUNKS, dtype=I32)[None, :, None]
    inside = (run_local[:, None, :] <= lc) & (lc < (run_local + run_chunks)[:, None, :])
    back_src = jnp.sum(jnp.where(inside, run_global[:, None, :] + lc - run_local[:, None, :], 0), axis=2)
    back_src = jnp.where(jnp.any(inside, axis=2), back_src, zero_global).astype(I32).reshape(-1)

    ys = _experts(tile_expert, n_used, gather_src, xs, w_gate, w_up, w_down, n_tiles)
    return _final(back_src, x1, route, p2d, ys, row(ple_norm), w_ple_gate.astype(BF16),
                  w_ple_proj.astype(BF16), out_gain)


def kernel(x, p, mix_norm, w_in, hg_lb_logits, hg_norm, ret_norm, w_branch_a, w_branch_b, w_out,
           ffn_norm, w_router_group, b_router_group, w_router_expert, b_router_expert,
           w_expert_gate, w_expert_up, w_expert_down, ple_norm, w_ple_gate, w_ple_proj, final_norm):
    batch, seq, d = x.shape
    depth = p.shape[0]
    assert depth == 1, "the final rmsnorm is fused into the single layer"
    x2d = x.reshape(batch * seq, d)
    out = _layer(x2d, p[0].reshape(batch * seq, -1), batch, seq, mix_norm[0], w_in[0], hg_lb_logits,
                 hg_norm[0], ret_norm[0], w_branch_a[0], w_branch_b[0], w_out[0], ffn_norm[0],
                 w_router_group[0], b_router_group[0], w_router_expert[0], b_router_expert[0],
                 w_expert_gate[0], w_expert_up[0], w_expert_down[0], ple_norm[0], w_ple_gate[0],
                 w_ple_proj[0], final_norm.reshape(1, -1).astype(F32))
    return out.reshape(batch, seq, d)
```

```python
import functools

import jax
import jax.numpy as jnp
from jax import lax
from jax.experimental import pallas as pl
from jax.experimental.pallas import tpu as pltpu

F32 = jnp.float32
BF16 = jnp.bfloat16
I32 = jnp.int32
U32 = jnp.uint32

EPS = 1e-6
D_MODEL = 1024
PLE_DIM = 256
HG_HEADS = 4
HG_DK = 128
HG_WIDTH = HG_HEADS * HG_DK
RET_HEADS = 4
RET_DK = 128
RET_DV = 256
ROPE_BASE = 10000.0
IN_TOTAL = 7168
N_GROUPS = 4
EXPERTS_PER_GROUP = 8
N_EXPERTS = 32
D_EXPERT = 256

COL_HQ, COL_HF, COL_HI, COL_HG = 0, 4, 8, 12
COL_RQ, COL_RK = 16, 20
COL_RV, COL_RG = 24, 32
COL_GA, COL_GB = 5, 6

LANES = 128
VMEM_LIMIT = 56 * 1024 * 1024

HG_CHUNK = 64
HG_SUB = 8
HG_UNROLL = 4
RET_CHUNK = 128
RET_UNROLL = 4
SEQ_TILE = 1024
TOK_TILE = 512
EXP_TILE = 512
EXP_SUB = 256
ROW_CHUNK = 16
CHUNKS_PER_TILE = EXP_TILE // ROW_CHUNK
LOCAL_ROWS = 2 * TOK_TILE + N_EXPERTS * ROW_CHUNK
LOCAL_CHUNKS = LOCAL_ROWS // ROW_CHUNK
SORT_SLAB = 256


def _cparams(sem):
    return pltpu.CompilerParams(dimension_semantics=sem, vmem_limit_bytes=VMEM_LIMIT)


def _rms(x, g):
    return x * lax.rsqrt(jnp.mean(x * x, axis=-1, keepdims=True) + EPS) * g


def _sigmoid(x):
    return 1.0 / (1.0 + jnp.exp(-x))


def _silu(x):
    return x * _sigmoid(x)


def _split3(x):
    hi = x.astype(BF16)
    r1 = x - hi.astype(F32)
    mid = r1.astype(BF16)
    lo = (r1 - mid.astype(F32)).astype(BF16)
    return hi, mid, lo


def _dot(a, b):
    return jnp.dot(a, b, preferred_element_type=F32)


def _dot_nt(a, b):
    return lax.dot_general(a, b, (((1,), (1,)), ((), ())), preferred_element_type=F32)


def _dot_tn(a, b):
    return lax.dot_general(a, b, (((0,), (0,)), ((), ())), preferred_element_type=F32)


def _inproj_kernel(x_ref, g_ref, w_ref, proj_ref, hf_ref):
    h = _rms(x_ref[...], g_ref[...]).astype(BF16)
    tn = HG_WIDTH
    for j in range(IN_TOTAL // tn):
        acc = _dot(h, w_ref[:, j * tn:(j + 1) * tn])
        proj_ref[:, j * tn:(j + 1) * tn] = acc.astype(BF16)
        if j * tn == COL_HF * LANES:
            hf_ref[...] = acc


def _inproj(x2d, gain, w_bf16):
    t = x2d.shape[0]
    return pl.pallas_call(
        _inproj_kernel,
        grid=(t // TOK_TILE,),
        in_specs=[
            pl.BlockSpec((TOK_TILE, D_MODEL), lambda i: (i, 0)),
            pl.BlockSpec((1, D_MODEL), lambda i: (0, 0)),
            pl.BlockSpec((D_MODEL, IN_TOTAL), lambda i: (0, 0), pipeline_mode=pl.Buffered(1)),
        ],
        out_specs=[
            pl.BlockSpec((TOK_TILE, IN_TOTAL), lambda i: (i, 0)),
            pl.BlockSpec((TOK_TILE, HG_WIDTH), lambda i: (i, 0)),
        ],
        out_shape=[
            jax.ShapeDtypeStruct((t, IN_TOTAL), BF16),
            jax.ShapeDtypeStruct((t, HG_WIDTH), F32),
        ],
        compiler_params=_cparams(("arbitrary",)),
        name="inproj",
    )(x2d, gain, w_bf16)


def _hgrn_kernel(lbl_ref, q_ref, f_ref, i_ref, g_ref, ng_ref, o_ref,
                 st_ref, b_s, k_s, v_s, oi_s, qe_s, kv_s, dec_s):
    c = HG_CHUNK
    nsub = c // HG_SUB
    n_chunks = q_ref.shape[0] // c

    @pl.when(pl.program_id(2) == 0)
    def _():
        st_ref[...] = jnp.zeros_like(st_ref)

    logits = lbl_ref[...]
    e = jnp.exp(logits - jnp.max(logits, axis=0, keepdims=True))
    lb = e[0:1] / jnp.sum(e, axis=0, keepdims=True)
    one_m_lb = jnp.sum(e[1:], axis=0, keepdims=True) / jnp.sum(e, axis=0, keepdims=True)
    ng = ng_ref[...]

    row = lax.broadcasted_iota(I32, (c, c), 0)
    col = lax.broadcasted_iota(I32, (c, c), 1)
    tri = jnp.where(row >= col, 1.0, 0.0).astype(BF16)
    row_k = lax.broadcasted_iota(I32, (c, HG_DK), 0)
    sub_row = lax.broadcasted_iota(I32, (HG_SUB, HG_DK), 0)
    masked = jnp.float32(-1e30)

    def bcast_row(ref, r, rows):
        return jnp.broadcast_to(ref[pl.ds(r, 1), :], (rows, HG_DK))

    def prep(ci, slot):
        r0 = pl.multiple_of(ci * c, c)
        z = f_ref[pl.ds(r0, c), :]
        ez = jnp.exp(-jnp.abs(z))
        rz = 1.0 / (1.0 + ez)
        pos = z >= 0.0
        logf = jnp.log(lb + one_m_lb * jnp.where(pos, rz, ez * rz))
        kk = one_m_lb * jnp.where(pos, ez * rz, rz)
        q = _silu(q_ref[pl.ds(r0, c), :].astype(F32))
        v = i_ref[pl.ds(r0, c), :].astype(F32)
        k_s[slot] = kk
        v_s[slot] = v
        return dict(ci=ci, r0=r0, slot=slot, kk=kk, q=q, v16=v.astype(BF16), parts=_split3(logf))

    def cumulate(s):
        hi, mid, lo = s.pop("parts")
        b = (_dot(tri, lo) + _dot(tri, mid)) + _dot(tri, hi)
        b_s[s["slot"]] = b
        qe_s[pl.ds(s["r0"], c), :] = (s["q"] * jnp.exp(b)).astype(BF16)
        s["b"] = b
        return s

    def off_diagonal(s):
        b, q, kk = s["b"], s["q"], s["kk"]
        bs_ref = b_s.at[s["slot"]]
        a_rows = [jnp.zeros((HG_SUB, c), F32)]
        for i in range(1, nsub):
            bi = bcast_row(bs_ref, i * HG_SUB - 1, c)
            qi = q[i * HG_SUB:(i + 1) * HG_SUB, :] * jnp.exp(b[i * HG_SUB:(i + 1) * HG_SUB, :] - bi[:HG_SUB])
            ki = kk * jnp.exp(jnp.where(row_k < i * HG_SUB, bi - b, masked))
            a_rows.append(_dot_nt(qi.astype(BF16), ki.astype(BF16)))
        s["a_off"] = jnp.concatenate(a_rows, axis=0)
        return s

    def apply_values(s):
        b, kk = s["b"], s["kk"]
        blast = b[c - 1:c, :]
        s["o"] = _dot(s.pop("a_off").astype(BF16), s["v16"])
        kd = kk * jnp.exp(blast - b)
        kv_s[s["ci"]] = _dot_tn(s["v16"], kd.astype(BF16))
        dec_s[s["ci"]] = jnp.broadcast_to(jnp.exp(blast), (HG_SUB, HG_DK))
        return s

    def diagonal(s):
        b, q = s["b"], s["q"]
        bs_ref, ks_ref, vs_ref = b_s.at[s["slot"]], k_s.at[s["slot"]], v_s.at[s["slot"]]
        d_blocks = []
        for i in range(nsub):
            sl = slice(i * HG_SUB, (i + 1) * HG_SUB)
            bt, qt = b[sl, :], q[sl, :]
            acc = jnp.zeros((HG_SUB, HG_DK), F32)
            for j in range(HG_SUB):
                r = i * HG_SUB + j
                arg = jnp.where(sub_row >= j, bt - bcast_row(bs_ref, r, HG_SUB), masked)
                g = jnp.exp(arg) * (qt * bcast_row(ks_ref, r, HG_SUB))
                acc = acc + jnp.sum(g, axis=-1, keepdims=True) * bcast_row(vs_ref, r, HG_SUB)
            d_blocks.append(acc)
        oi_s[pl.ds(s["r0"], c), :] = s["o"] + jnp.concatenate(d_blocks, axis=0)

    def local_group(gi, carry):
        states = [prep(gi * HG_UNROLL + slot, slot) for slot in range(HG_UNROLL)]
        for stage in (cumulate, off_diagonal, apply_values, diagonal):
            states = [stage(s) for s in states]
        return carry

    lax.fori_loop(0, n_chunks // HG_UNROLL, local_group, 0)

    def carried_group(gi, carry):
        st = st_ref[...]
        outs = []
        for j in range(HG_UNROLL):
            ci = gi * HG_UNROLL + j
            r0 = pl.multiple_of(ci * c, c)
            outs.append((r0, _dot_nt(qe_s[pl.ds(r0, c), :], st.astype(BF16))))
            st = st * dec_s[ci][0:1, :] + kv_s[ci]
        st_ref[...] = st
        for r0, os in outs:
            y = _rms(oi_s[pl.ds(r0, c), :] + os, ng) * _silu(g_ref[pl.ds(r0, c), :].astype(F32))
            o_ref[pl.ds(r0, c), :] = y.astype(o_ref.dtype)
        return carry

    lax.fori_loop(0, n_chunks // HG_UNROLL, carried_group, 0)


def _hgrn(proj, hf, lb_logits, norm_g, batch, seq):
    ns = seq // SEQ_TILE
    tok = lambda b, h, s: b * ns + s
    return pl.pallas_call(
        _hgrn_kernel,
        grid=(batch, HG_HEADS, ns),
        in_specs=[
            pl.BlockSpec((2, HG_DK), lambda b, h, s: (0, h)),
            pl.BlockSpec((SEQ_TILE, HG_DK), lambda b, h, s: (tok(b, h, s), COL_HQ + h)),
            pl.BlockSpec((SEQ_TILE, HG_DK), lambda b, h, s: (tok(b, h, s), h)),
            pl.BlockSpec((SEQ_TILE, HG_DK), lambda b, h, s: (tok(b, h, s), COL_HI + h)),
            pl.BlockSpec((SEQ_TILE, HG_DK), lambda b, h, s: (tok(b, h, s), COL_HG + h)),
            pl.BlockSpec((1, HG_DK), lambda b, h, s: (0, 0)),
        ],
        out_specs=pl.BlockSpec((SEQ_TILE, HG_DK), lambda b, h, s: (tok(b, h, s), h)),
        out_shape=jax.ShapeDtypeStruct((batch * seq, HG_WIDTH), BF16),
        scratch_shapes=[
            pltpu.VMEM((HG_DK, HG_DK), F32),
            pltpu.VMEM((HG_UNROLL, HG_CHUNK, HG_DK), F32),
            pltpu.VMEM((HG_UNROLL, HG_CHUNK, HG_DK), F32),
            pltpu.VMEM((HG_UNROLL, HG_CHUNK, HG_DK), F32),
            pltpu.VMEM((SEQ_TILE, HG_DK), F32),
            pltpu.VMEM((SEQ_TILE, HG_DK), BF16),
            pltpu.VMEM((SEQ_TILE // HG_CHUNK, HG_DK, HG_DK), F32),
            pltpu.VMEM((SEQ_TILE // HG_CHUNK, HG_SUB, HG_DK), F32),
        ],
        compiler_params=_cparams(("arbitrary", "arbitrary", "arbitrary")),
        name="hgrn2",
    )(lb_logits, proj, hf, proj, proj, norm_g)


def _ret_kernel(q_ref, k_ref, v_ref, g_ref, cos_ref, sin_ref, ng_ref, o_ref,
                r_ref, oi_s, q16_s, kv_s):
    c = RET_CHUNK

    @pl.when(pl.program_id(2) == 0)
    def _():
        r_ref[...] = jnp.zeros_like(r_ref)

    hf = jnp.full((1, 1), pl.program_id(1), I32).astype(F32)
    lg = jnp.log1p(-jnp.exp2(-5.0 - hf))
    ti = lax.broadcasted_iota(I32, (c, c), 0)
    si = lax.broadcasted_iota(I32, (c, c), 1)
    rel = (ti - si).astype(F32)
    intra = jnp.where(ti >= si, jnp.exp(jnp.maximum(rel, 0.0) * lg), 0.0)
    idx = lax.broadcasted_iota(I32, (c, 1), 0).astype(F32)
    inter = jnp.exp((idx + 1.0) * lg)
    to_state = jnp.exp((c - 1.0 - idx) * lg)
    chunk_decay = jnp.exp(float(c) * lg)
    ng = ng_ref[...]
    half = RET_DK // 2

    n_chunks = q_ref.shape[0] // c

    def prep(ci):
        r0 = pl.multiple_of(ci * c, c)
        cos = cos_ref[pl.ds(r0, c), :]
        sin = sin_ref[pl.ds(r0, c), :]
        q = q_ref[pl.ds(r0, c), :].astype(F32)
        k = k_ref[pl.ds(r0, c), :].astype(F32)
        qr = (q * cos + pltpu.roll(q, half, 1) * sin) * (RET_DK ** -0.5)
        kr = k * cos + pltpu.roll(k, half, 1) * sin
        q16 = qr.astype(BF16)
        q16_s[pl.ds(r0, c), :] = q16
        return dict(ci=ci, r0=r0, q16=q16, k16=kr.astype(BF16), kts=(kr * to_state).astype(BF16),
                    v16=v_ref[pl.ds(r0, c), :])

    def scores(s):
        s["att"] = (_dot_nt(s.pop("q16"), s.pop("k16")) * intra).astype(BF16)
        return s

    def apply_values(s):
        oi_s[pl.ds(s["r0"], c), :] = _dot(s["att"], s["v16"])
        kv_s[s["ci"]] = _dot_tn(s["kts"], s["v16"])
        return s

    def local_group(gi, carry):
        states = [prep(gi * RET_UNROLL + j) for j in range(RET_UNROLL)]
        for stage in (scores, apply_values):
            states = [stage(s) for s in states]
        return carry

    lax.fori_loop(0, n_chunks // RET_UNROLL, local_group, 0)

    def carried_group(gi, carry):
        r = r_ref[...]
        outs = []
        for j in range(RET_UNROLL):
            ci = gi * RET_UNROLL + j
            r0 = pl.multiple_of(ci * c, c)
            outs.append((r0, _dot(q16_s[pl.ds(r0, c), :], r.astype(BF16))))
            r = chunk_decay * r + kv_s[ci]
        r_ref[...] = r
        for r0, qr_state in outs:
            o = oi_s[pl.ds(r0, c), :] + qr_state * inter
            y = _rms(o, ng) * _silu(g_ref[pl.ds(r0, c), :].astype(F32))
            o_ref[pl.ds(r0, c), :] = y.astype(o_ref.dtype)
        return carry

    lax.fori_loop(0, n_chunks // RET_UNROLL, carried_group, 0)


def _retention(proj, cos2, sin2, norm_g, batch, seq):
    ns = seq // SEQ_TILE
    tok = lambda b, h, s: b * ns + s
    return pl.pallas_call(
        _ret_kernel,
        grid=(batch, RET_HEADS, ns),
        in_specs=[
            pl.BlockSpec((SEQ_TILE, RET_DK), lambda b, h, s: (tok(b, h, s), COL_RQ + h)),
            pl.BlockSpec((SEQ_TILE, RET_DK), lambda b, h, s: (tok(b, h, s), COL_RK + h)),
            pl.BlockSpec((SEQ_TILE, RET_DV), lambda b, h, s: (tok(b, h, s), COL_RV // 2 + h)),
            pl.BlockSpec((SEQ_TILE, RET_DV), lambda b, h, s: (tok(b, h, s), COL_RG // 2 + h)),
            pl.BlockSpec((SEQ_TILE, RET_DK), lambda b, h, s: (s, 0)),
            pl.BlockSpec((SEQ_TILE, RET_DK), lambda b, h, s: (s, 0)),
            pl.BlockSpec((1, RET_DV), lambda b, h, s: (0, 0)),
        ],
        out_specs=pl.BlockSpec((SEQ_TILE, RET_DV), lambda b, h, s: (tok(b, h, s), h)),
        out_shape=jax.ShapeDtypeStruct((batch * seq, RET_HEADS * RET_DV), BF16),
        scratch_shapes=[
            pltpu.VMEM((RET_DK, RET_DV), F32),
            pltpu.VMEM((SEQ_TILE, RET_DV), F32),
            pltpu.VMEM((SEQ_TILE, RET_DK), BF16),
            pltpu.VMEM((SEQ_TILE // RET_CHUNK, RET_DK, RET_DV), F32),
        ],
        compiler_params=_cparams(("arbitrary", "arbitrary", "arbitrary")),
        name="retention",
    )(proj, proj, proj, proj, cos2, sin2, norm_g)


def _merge_kernel(x_ref, ya_ref, yb_ref, ga_ref, gb_ref, wa_ref, wb_ref, wo_ref, fg_ref,
                  wr_ref, br_ref, x1_ref, xs_ref, route_ref, cnt_ref):
    tm = x_ref.shape[0]

    merged = (_sigmoid(ga_ref[...].astype(F32)) * _dot(ya_ref[...], wa_ref[...])
              + _sigmoid(gb_ref[...].astype(F32)) * _dot(yb_ref[...], wb_ref[...]))
    x1 = x_ref[...] + _dot(merged.astype(BF16), wo_ref[...])
    x1_ref[...] = x1
    h2 = _rms(x1, fg_ref[...])

    hh, hm, _ = _split3(h2)
    wh, wm = wr_ref[0], wr_ref[1]
    logits = (_dot(hm, wh) + _dot(hh, wm)) + _dot(hh, wh) + br_ref[...]

    lane = lax.broadcasted_iota(I32, (tm, LANES), 1)
    neg = jnp.float32(-jnp.inf)
    big = jnp.int32(1 << 30)
    is_g = lane < N_GROUPS
    gl = jnp.where(is_g, logits, neg)
    gmax = jnp.max(gl, axis=-1, keepdims=True)
    g_idx = jnp.min(jnp.where(gl == gmax, lane, big), axis=-1, keepdims=True)
    g_w = 1.0 / jnp.sum(jnp.where(is_g, jnp.exp(gl - gmax), 0.0), axis=-1, keepdims=True)

    ex = lane - N_GROUPS
    in_grp = (ex >= g_idx * EXPERTS_PER_GROUP) & (ex < (g_idx + 1) * EXPERTS_PER_GROUP)
    el = jnp.where(in_grp, logits, neg)
    m1 = jnp.max(el, axis=-1, keepdims=True)
    e1 = jnp.min(jnp.where(el == m1, ex, big), axis=-1, keepdims=True)
    el2 = jnp.where(ex == e1, neg, el)
    m2 = jnp.max(el2, axis=-1, keepdims=True)
    e2 = jnp.min(jnp.where(el2 == m2, ex, big), axis=-1, keepdims=True)
    p2 = jnp.exp(m2 - m1)
    w1 = g_w / (1.0 + p2)
    w2 = g_w * p2 / (1.0 + p2)

    oh1 = ex == e1
    oh2 = ex == e2
    oh = jnp.where(oh1 | oh2, 1.0, 0.0)
    ri = lax.broadcasted_iota(I32, (tm, tm), 0)
    ci = lax.broadcasted_iota(I32, (tm, tm), 1)
    strict = jnp.where(ri > ci, 1.0, 0.0).astype(BF16)
    local_rank = _dot(strict, oh.astype(BF16))
    cnt = jnp.sum(oh, axis=0, keepdims=True)
    run_chunks = jnp.floor((cnt + (ROW_CHUNK - 1.0)) * (1.0 / ROW_CHUNK))
    ui = lax.broadcasted_iota(I32, (LANES, LANES), 0)
    uj = lax.broadcasted_iota(I32, (LANES, LANES), 1)
    before = jnp.where(ui < uj, 1.0, 0.0).astype(BF16)
    run_start = _dot(jnp.broadcast_to(run_chunks, (8, LANES)).astype(BF16), before)[0:1] * ROW_CHUNK
    slot = run_start + local_rank
    pos1 = jnp.sum(jnp.where(oh1, slot, 0.0), axis=-1, keepdims=True)
    pos2 = jnp.sum(jnp.where(oh2, slot, 0.0), axis=-1, keepdims=True)
    cnt_ref[0] = jnp.broadcast_to(cnt, (8, LANES))

    route = jnp.where(lane == 0, e1.astype(F32), 0.0)
    route = jnp.where(lane == 1, e2.astype(F32), route)
    route = jnp.where(lane == 2, w1, route)
    route = jnp.where(lane == 3, w2, route)
    route = jnp.where(lane == 4, pos1, route)
    route = jnp.where(lane == 5, pos2, route)
    route_ref[...] = route

    hi1 = jnp.floor(pos1 * (1.0 / 256.0))
    hi2 = jnp.floor(pos2 * (1.0 / 256.0))
    digits = jnp.where(lane == 0, hi1, 0.0)
    digits = jnp.where(lane == 1, pos1 - 256.0 * hi1, digits)
    digits = jnp.where(lane == 2, hi2, digits)
    digits = jnp.where(lane == 3, pos2 - 256.0 * hi2, digits)
    pick = jnp.where(lax.broadcasted_iota(I32, (8, LANES), 0) == lax.broadcasted_iota(I32, (8, LANES), 1),
                     1.0, 0.0).astype(BF16)
    rows = _dot_nt(pick, digits.astype(BF16))
    pos1_row = rows[0:1] * 256.0 + rows[1:2]
    pos2_row = rows[2:3] * 256.0 + rows[3:4]

    h2b = h2.astype(BF16)
    slab_row = lax.broadcasted_iota(I32, (SORT_SLAB, tm), 0).astype(F32)
    for r0 in range(0, LOCAL_ROWS, SORT_SLAB):
        sel = (slab_row == pos1_row - float(r0)) | (slab_row == pos2_row - float(r0))
        xs_ref[r0:r0 + SORT_SLAB, :] = _dot(jnp.where(sel, 1.0, 0.0).astype(BF16), h2b).astype(BF16)


def _merge(x2d, ya, yb, proj, wa, wb, wo, ffn_g, w_router, b_router):
    t = x2d.shape[0]
    const = lambda *shape: pl.BlockSpec(shape, lambda i: (0,) * len(shape))
    return pl.pallas_call(
        _merge_kernel,
        grid=(t // TOK_TILE,),
        in_specs=[
            pl.BlockSpec((TOK_TILE, D_MODEL), lambda i: (i, 0)),
            pl.BlockSpec((TOK_TILE, HG_WIDTH), lambda i: (i, 0)),
            pl.BlockSpec((TOK_TILE, D_MODEL), lambda i: (i, 0)),
            pl.BlockSpec((TOK_TILE, D_MODEL), lambda i: (i, COL_GA)),
            pl.BlockSpec((TOK_TILE, D_MODEL), lambda i: (i, COL_GB)),
            const(HG_WIDTH, D_MODEL),
            const(D_MODEL, D_MODEL),
            const(D_MODEL, D_MODEL),
            const(1, D_MODEL),
            const(2, D_MODEL, LANES),
            const(1, LANES),
        ],
        out_specs=[
            pl.BlockSpec((TOK_TILE, D_MODEL), lambda i: (i, 0)),
            pl.BlockSpec((LOCAL_ROWS, D_MODEL), lambda i: (i, 0)),
            pl.BlockSpec((TOK_TILE, LANES), lambda i: (i, 0)),
            pl.BlockSpec((1, 8, LANES), lambda i: (i, 0, 0)),
        ],
        out_shape=[
            jax.ShapeDtypeStruct((t, D_MODEL), F32),
            jax.ShapeDtypeStruct((t // TOK_TILE * LOCAL_ROWS, D_MODEL), BF16),
            jax.ShapeDtypeStruct((t, LANES), F32),
            jax.ShapeDtypeStruct((t // TOK_TILE, 8, LANES), F32),
        ],
        compiler_params=_cparams(("arbitrary",)),
        name="merge_route",
    )(x2d, ya, yb, proj, proj, wa, wb, wo, ffn_g, w_router, b_router)


def _expert_kernel(te_ref, nt_ref, src_ref, xs_ref, wg_ref, wu_ref, wd_ref, ys_ref,
                   xbuf, wg_s, wu_s, wd_s, sem):
    i = pl.program_id(0)
    nt = nt_ref[0]

    def gather(tile, slot):
        copies = []
        for c in range(CHUNKS_PER_TILE):
            row = pl.multiple_of(src_ref[tile * CHUNKS_PER_TILE + c] * ROW_CHUNK, ROW_CHUNK)
            copies.append(pltpu.make_async_copy(
                xs_ref.at[pl.ds(row, ROW_CHUNK)],
                xbuf.at[slot, pl.ds(c * ROW_CHUNK, ROW_CHUNK)], sem.at[slot]))
        return copies

    @pl.when(i == 0)
    def _():
        for cp in gather(0, 0):
            cp.start()

    @pl.when(i <= nt)
    def _():
        for cp in gather(i, i % 2):
            cp.wait()

    @pl.when((i < nt) & ((i == 0) | (te_ref[i] != te_ref[jnp.maximum(i - 1, 0)])))
    def _():
        wg_s[...] = wg_ref[0].astype(BF16)
        wu_s[...] = wu_ref[0].astype(BF16)
        wd_s[...] = wd_ref[0].astype(BF16)

    @pl.when(i < nt)
    def _():
        slot = i % 2
        wg, wu, wd = wg_s[...], wu_s[...], wd_s[...]
        subs = [pl.ds(r, EXP_SUB) for r in range(0, EXP_TILE, EXP_SUB)]
        xs = [xbuf[slot, s, :] for s in subs]
        gates = [(_dot(x, wg), _dot(x, wu)) for x in xs]
        for cp in gather(i + 1, (i + 1) % 2):
            cp.start()
        hidden = [(_silu(a) * u).astype(BF16) for a, u in gates]
        for s, h in zip(subs, hidden):
            ys_ref[s, :] = _dot(h, wd).astype(ys_ref.dtype)

    @pl.when(i >= nt)
    def _():
        ys_ref[...] = jnp.zeros_like(ys_ref)


def _experts(tile_expert, n_tiles_used, src_chunk, xs, wg, wu, wd, n_tiles):
    grid_spec = pltpu.PrefetchScalarGridSpec(
        num_scalar_prefetch=3,
        grid=(n_tiles,),
        in_specs=[
            pl.BlockSpec(memory_space=pl.ANY),
            pl.BlockSpec((1, D_MODEL, D_EXPERT), lambda i, te, nt, src: (te[i], 0, 0)),
            pl.BlockSpec((1, D_MODEL, D_EXPERT), lambda i, te, nt, src: (te[i], 0, 0)),
            pl.BlockSpec((1, D_EXPERT, D_MODEL), lambda i, te, nt, src: (te[i], 0, 0)),
        ],
        out_specs=pl.BlockSpec((EXP_TILE, D_MODEL), lambda i, te, nt, src: (i, 0)),
        scratch_shapes=[
            pltpu.VMEM((2, EXP_TILE, D_MODEL), BF16),
            pltpu.VMEM((D_MODEL, D_EXPERT), BF16),
            pltpu.VMEM((D_MODEL, D_EXPERT), BF16),
            pltpu.VMEM((D_EXPERT, D_MODEL), BF16),
            pltpu.SemaphoreType.DMA((2,)),
        ],
    )
    return pl.pallas_call(
        _expert_kernel,
        grid_spec=grid_spec,
        out_shape=jax.ShapeDtypeStruct((n_tiles * EXP_TILE, D_MODEL), BF16),
        compiler_params=_cparams(("arbitrary",)),
        name="experts",
    )(tile_expert, n_tiles_used, src_chunk, xs, wg, wu, wd)


def _final_kernel(src_ref, x1_ref, route_ref, p_ref, ys_ref, pg_ref, wpg_ref, wpp_ref, fg_ref,
                  o_ref, ybuf, sem):
    tm = x1_ref.shape[0]
    i = pl.program_id(0)
    nb = pl.num_programs(0)

    def gather(block, slot):
        copies = []
        for c in range(LOCAL_CHUNKS):
            row = pl.multiple_of(src_ref[block * LOCAL_CHUNKS + c] * ROW_CHUNK, ROW_CHUNK)
            copies.append(pltpu.make_async_copy(
                ys_ref.at[pl.ds(row, ROW_CHUNK)],
                ybuf.at[slot, pl.ds(c * ROW_CHUNK, ROW_CHUNK)], sem.at[slot]))
        return copies

    @pl.when(i == 0)
    def _():
        for cp in gather(0, 0):
            cp.start()

    @pl.when(i + 1 < nb)
    def _():
        for cp in gather(i + 1, (i + 1) % 2):
            cp.start()

    slot = i % 2
    for cp in gather(i, slot):
        cp.wait()

    route = route_ref[...]
    w1, w2 = route[:, 2:3], route[:, 3:4]
    pos1, pos2 = route[:, 4:5], route[:, 5:6]
    slab_col = lax.broadcasted_iota(I32, (tm, SORT_SLAB), 1).astype(F32)
    moe = jnp.zeros((tm, D_MODEL), F32)
    for k0 in range(0, LOCAL_ROWS, SORT_SLAB):
        sel = jnp.where(slab_col == pos1 - float(k0), w1,
                        jnp.where(slab_col == pos2 - float(k0), w2, 0.0)).astype(BF16)
        moe = moe + _dot(sel, ybuf[slot, k0:k0 + SORT_SLAB, :])
    x2 = x1_ref[...] + moe
    hp = _rms(x2, pg_ref[...]).astype(BF16)
    gate = _sigmoid(_dot(hp, wpg_ref[...]))
    x3 = x2 + gate * _dot(p_ref[...].astype(BF16), wpp_ref[...])
    o_ref[...] = _rms(x3, fg_ref[...])


def _final(src_chunk, x1, route, p2d, ys, ple_g, wpg, wpp, final_g):
    t = x1.shape[0]
    const = lambda *shape: pl.BlockSpec(shape, lambda i, src: (0,) * len(shape))
    grid_spec = pltpu.PrefetchScalarGridSpec(
        num_scalar_prefetch=1,
        grid=(t // TOK_TILE,),
        in_specs=[
            pl.BlockSpec((TOK_TILE, D_MODEL), lambda i, src: (i, 0)),
            pl.BlockSpec((TOK_TILE, LANES), lambda i, src: (i, 0)),
            pl.BlockSpec((TOK_TILE, PLE_DIM), lambda i, src: (i, 0)),
            pl.BlockSpec(memory_space=pl.ANY),
            const(1, D_MODEL),
            const(D_MODEL, D_MODEL),
            const(PLE_DIM, D_MODEL),
            const(1, D_MODEL),
        ],
        out_specs=pl.BlockSpec((TOK_TILE, D_MODEL), lambda i, src: (i, 0)),
        scratch_shapes=[pltpu.VMEM((2, LOCAL_ROWS, D_MODEL), BF16), pltpu.SemaphoreType.DMA((2,))],
    )
    return pl.pallas_call(
        _final_kernel,
        grid_spec=grid_spec,
        out_shape=jax.ShapeDtypeStruct((t, D_MODEL), F32),
        compiler_params=_cparams(("arbitrary",)),
        name="combine_ple_final",
    )(src_chunk, x1, route, p2d, ys, ple_g, wpg, wpp, final_g)


def _rotary_tables(seq):
    inv = ROPE_BASE ** (-jnp.arange(0, RET_DK, 2, dtype=F32) / RET_DK)
    ang = jnp.arange(seq, dtype=F32)[:, None] * inv[None, :]
    cos, sin = jnp.cos(ang), jnp.sin(ang)
    return jnp.concatenate([cos, cos], axis=1), jnp.concatenate([-sin, sin], axis=1)


def _layer(x2d, p2d, batch, seq, mix_norm, w_in, hg_lb_logits, hg_norm, ret_norm, w_branch_a,
           w_branch_b, w_out, ffn_norm, w_rg, b_rg, w_re, b_re, w_gate, w_up, w_down, ple_norm,
           w_ple_gate, w_ple_proj, out_gain):
    t = x2d.shape[0]
    row = lambda v: v.reshape(1, -1).astype(F32)

    proj, hf = _inproj(x2d, row(mix_norm), w_in.astype(BF16))
    ya = _hgrn(proj, hf, hg_lb_logits.astype(F32), row(hg_norm), batch, seq)
    cos2, sin2 = _rotary_tables(seq)
    yb = _retention(proj, cos2, sin2, row(ret_norm), batch, seq)

    n_r = N_GROUPS + N_EXPERTS
    w_router = jnp.zeros((D_MODEL, LANES), F32).at[:, :n_r].set(jnp.concatenate([w_rg, w_re], axis=1))
    wr_hi = w_router.astype(BF16)
    wr_mid = (w_router - wr_hi.astype(F32)).astype(BF16)
    b_router = jnp.zeros((1, LANES), F32).at[0, :n_r].set(jnp.concatenate([b_rg, b_re]))
    x1, xs, route, counts = _merge(
        x2d, ya, yb, proj, w_branch_a.astype(BF16), w_branch_b.astype(BF16), w_out.astype(BF16),
        row(ffn_norm), jnp.stack([wr_hi, wr_mid]), b_router)

    n_blocks = t // TOK_TILE
    cnt = counts[:, 0, N_GROUPS:N_GROUPS + N_EXPERTS].astype(I32)
    run_chunks = (cnt + ROW_CHUNK - 1) // ROW_CHUNK
    run_local = jnp.cumsum(run_chunks, axis=1) - run_chunks
    seg_chunks = jnp.sum(run_chunks, axis=0)
    tiles_per = (seg_chunks + CHUNKS_PER_TILE - 1) // CHUNKS_PER_TILE
    tile_end = jnp.cumsum(tiles_per)
    seg_start = (tile_end - tiles_per) * CHUNKS_PER_TILE
    run_global = seg_start[None, :] + jnp.cumsum(run_chunks, axis=0) - run_chunks
    max_chunks = (2 * t) // ROW_CHUNK + n_blocks * N_EXPERTS + N_EXPERTS * (CHUNKS_PER_TILE - 1)
    n_tiles = -(-max_chunks // CHUNKS_PER_TILE) + 1
    tile_ids = jnp.arange(n_tiles, dtype=I32)
    tile_expert = jnp.minimum(jnp.sum((tile_end[None, :] <= tile_ids[:, None]).astype(I32), axis=1),
                              N_EXPERTS - 1)
    n_used = tile_end[-1:].astype(I32)

    block_ids = jnp.arange(n_blocks, dtype=I32)
    zero_local = LOCAL_CHUNKS - 1
    zero_global = n_tiles * CHUNKS_PER_TILE - 1
    g = jnp.arange(n_tiles * CHUNKS_PER_TILE, dtype=I32)[:, None]
    e_g = jnp.repeat(tile_expert, CHUNKS_PER_TILE)
    pick_e = (e_g[:, None] == jnp.arange(N_EXPERTS, dtype=I32)[None, :]).astype(I32)
    rg, rc, rl = (jnp.sum(pick_e[:, :, None] * tab.T[None, :, :], axis=1)
                  for tab in (run_global, run_chunks, run_local))
    inside = (rg <= g) & (g < rg + rc)
    gather_src = jnp.sum(jnp.where(inside, block_ids[None, :] * LOCAL_CHUNKS + rl + (g - rg), 0), axis=1)
    gather_src = jnp.where(jnp.any(inside, axis=1), gather_src, zero_local).astype(I32)

    lc = jnp.arange(LOCAL_CHUNKS, dtype=I32)[None, :, None]
    inside = (run_local[:, None, :] <= lc) & (lc < (run_local + run_chunks)[:, None, :])
    back_src = jnp.sum(jnp.where(inside, run_global[:, None, :] + lc - run_local[:, None, :], 0), axis=2)
    back_src = jnp.where(jnp.any(inside, axis=2), back_src, zero_global).astype(I32).reshape(-1)

    ys = _experts(tile_expert, n_used, gather_src, xs, w_gate, w_up, w_down, n_tiles)
    return _final(back_src, x1, route, p2d, ys, row(ple_norm), w_ple_gate.astype(BF16),
                  w_ple_proj.astype(BF16), out_gain)


def kernel(x, p, mix_norm, w_in, hg_lb_logits, hg_norm, ret_norm, w_branch_a, w_branch_b, w_out,
           ffn_norm, w_router_group, b_router_group, w_router_expert, b_router_expert,
           w_expert_gate, w_expert_up, w_expert_down, ple_norm, w_ple_gate, w_ple_proj, final_norm):
    batch, seq, d = x.shape
    depth = p.shape[0]
    assert depth == 1, "the final rmsnorm is fused into the single layer"
    x2d = x.reshape(batch * seq, d)
    out = _layer(x2d, p[0].reshape(batch * seq, -1), batch, seq, mix_norm[0], w_in[0], hg_lb_logits,
                 hg_norm[0], ret_norm[0], w_branch_a[0], w_branch_b[0], w_out[0], ffn_norm[0],
                 w_router_group[0], b_router_group[0], w_router_expert[0], b_router_expert[0],
                 w_expert_gate[0], w_expert_up[0], w_expert_down[0], ple_norm[0], w_ple_gate[0],
                 w_ple_proj[0], final_norm.reshape(1, -1).astype(F32))
    return out.reshape(batch, seq, d)
```

```python
import functools

import jax
import jax.numpy as jnp
from jax import lax
from jax.experimental import pallas as pl
from jax.experimental.pallas import tpu as pltpu

F32 = jnp.float32
BF16 = jnp.bfloat16
I32 = jnp.int32
U32 = jnp.uint32

EPS = 1e-6
D_MODEL = 1024
PLE_DIM = 256
HG_HEADS = 4
HG_DK = 128
HG_WIDTH = HG_HEADS * HG_DK
RET_HEADS = 4
RET_DK = 128
RET_DV = 256
ROPE_BASE = 10000.0
IN_TOTAL = 7168
N_GROUPS = 4
EXPERTS_PER_GROUP = 8
N_EXPERTS = 32
D_EXPERT = 256

COL_HQ, COL_HF, COL_HI, COL_HG = 0, 4, 8, 12
COL_RQ, COL_RK = 16, 20
COL_RV, COL_RG = 24, 32
COL_GA, COL_GB = 5, 6

LANES = 128
VMEM_LIMIT = 56 * 1024 * 1024

HG_CHUNK = 64
HG_SUB = 8
HG_UNROLL = 4
HG_STATE_UNROLL = 8
HG_NORM_ROWS = 256
RET_CHUNK = 128
RET_UNROLL = 4
SEQ_TILE = 1024
TOK_TILE = 512
EXP_TILE = 512
EXP_SUB = 512
EXP_AHEAD = 2
EXP_SLOTS = EXP_AHEAD + 1
ROW_CHUNK = 16
CHUNKS_PER_TILE = EXP_TILE // ROW_CHUNK
LOCAL_ROWS = 2 * TOK_TILE + N_EXPERTS * ROW_CHUNK
LOCAL_CHUNKS = LOCAL_ROWS // ROW_CHUNK
SORT_SLAB = 256


def _cparams(sem):
    return pltpu.CompilerParams(dimension_semantics=sem, vmem_limit_bytes=VMEM_LIMIT)


def _rms(x, g):
    return x * lax.rsqrt(jnp.mean(x * x, axis=-1, keepdims=True) + EPS) * g


def _sigmoid(x):
    return 1.0 / (1.0 + jnp.exp(-x))


def _silu(x):
    return x * _sigmoid(x)


def _split3(x):
    hi = x.astype(BF16)
    r1 = x - hi.astype(F32)
    mid = r1.astype(BF16)
    lo = (r1 - mid.astype(F32)).astype(BF16)
    return hi, mid, lo


def _dot(a, b):
    return jnp.dot(a, b, preferred_element_type=F32)


def _dot_nt(a, b):
    return lax.dot_general(a, b, (((1,), (1,)), ((), ())), preferred_element_type=F32)


def _dot_tn(a, b):
    return lax.dot_general(a, b, (((0,), (0,)), ((), ())), preferred_element_type=F32)


def _inproj_kernel(x_ref, g_ref, w_ref, proj_ref, hf_ref):
    h = _rms(x_ref[...], g_ref[...]).astype(BF16)
    tn = HG_WIDTH
    for j in range(IN_TOTAL // tn):
        acc = _dot(h, w_ref[:, j * tn:(j + 1) * tn])
        proj_ref[:, j * tn:(j + 1) * tn] = acc.astype(BF16)
        if j * tn == COL_HF * LANES:
            hf_ref[...] = acc


def _inproj(x2d, gain, w_bf16):
    t = x2d.shape[0]
    return pl.pallas_call(
        _inproj_kernel,
        grid=(t // TOK_TILE,),
        in_specs=[
            pl.BlockSpec((TOK_TILE, D_MODEL), lambda i: (i, 0)),
            pl.BlockSpec((1, D_MODEL), lambda i: (0, 0)),
            pl.BlockSpec((D_MODEL, IN_TOTAL), lambda i: (0, 0), pipeline_mode=pl.Buffered(1)),
        ],
        out_specs=[
            pl.BlockSpec((TOK_TILE, IN_TOTAL), lambda i: (i, 0)),
            pl.BlockSpec((TOK_TILE, HG_WIDTH), lambda i: (i, 0)),
        ],
        out_shape=[
            jax.ShapeDtypeStruct((t, IN_TOTAL), BF16),
            jax.ShapeDtypeStruct((t, HG_WIDTH), F32),
        ],
        compiler_params=_cparams(("arbitrary",)),
        name="inproj",
    )(x2d, gain, w_bf16)


def _hgrn_kernel(lbl_ref, q_ref, f_ref, i_ref, g_ref, ng_ref, o_ref,
                 st_ref, b_s, k_s, v_s, oi_s, qe_s, kv_s, dec_s):
    c = HG_CHUNK
    nsub = c // HG_SUB
    n_chunks = q_ref.shape[0] // c

    @pl.when(pl.program_id(2) == 0)
    def _():
        st_ref[...] = jnp.zeros_like(st_ref)

    logits = lbl_ref[...]
    e = jnp.exp(logits - jnp.max(logits, axis=0, keepdims=True))
    lb = e[0:1] / jnp.sum(e, axis=0, keepdims=True)
    one_m_lb = jnp.sum(e[1:], axis=0, keepdims=True) / jnp.sum(e, axis=0, keepdims=True)
    ng = ng_ref[...]

    row = lax.broadcasted_iota(I32, (c, c), 0)
    col = lax.broadcasted_iota(I32, (c, c), 1)
    tri = jnp.where(row >= col, 1.0, 0.0).astype(BF16)
    row_k = lax.broadcasted_iota(I32, (c, HG_DK), 0)
    sub_row = lax.broadcasted_iota(I32, (HG_SUB, HG_DK), 0)
    masked = jnp.float32(-1e30)

    def bcast_row(ref, r, rows):
        return jnp.broadcast_to(ref[pl.ds(r, 1), :], (rows, HG_DK))

    def prep(ci, slot):
        r0 = pl.multiple_of(ci * c, c)
        z = f_ref[pl.ds(r0, c), :]
        ez = jnp.exp(-jnp.abs(z))
        rz = 1.0 / (1.0 + ez)
        pos = z >= 0.0
        logf = jnp.log2(lb + one_m_lb * jnp.where(pos, rz, ez * rz))
        kk = one_m_lb * jnp.where(pos, ez * rz, rz)
        q = _silu(q_ref[pl.ds(r0, c), :].astype(F32))
        v = i_ref[pl.ds(r0, c), :].astype(F32)
        k_s[slot] = kk
        v_s[slot] = v
        return dict(ci=ci, r0=r0, slot=slot, kk=kk, q=q, v16=v.astype(BF16), parts=_split3(logf))

    def cumulate(s):
        hi, mid, lo = s.pop("parts")
        b = (_dot(tri, lo) + _dot(tri, mid)) + _dot(tri, hi)
        b_s[s["slot"]] = b
        qe_s[pl.ds(s["r0"], c), :] = (s["q"] * jnp.exp2(b)).astype(BF16)
        s["b"] = b
        return s

    def off_diagonal(s):
        b, q, kk = s["b"], s["q"], s["kk"]
        bs_ref = b_s.at[s["slot"]]
        a_rows = [jnp.zeros((HG_SUB, c), F32)]
        for i in range(1, nsub):
            bi = bcast_row(bs_ref, i * HG_SUB - 1, c)
            qi = q[i * HG_SUB:(i + 1) * HG_SUB, :] * jnp.exp2(b[i * HG_SUB:(i + 1) * HG_SUB, :] - bi[:HG_SUB])
            ki = kk * jnp.exp2(jnp.where(row_k < i * HG_SUB, bi - b, masked))
            a_rows.append(_dot_nt(qi.astype(BF16), ki.astype(BF16)))
        s["a_off"] = jnp.concatenate(a_rows, axis=0)
        return s

    def apply_values(s):
        b, kk = s["b"], s["kk"]
        blast = b[c - 1:c, :]
        s["o"] = _dot(s.pop("a_off").astype(BF16), s["v16"])
        kd = kk * jnp.exp2(blast - b)
        kv_s[s["ci"]] = _dot_tn(s["v16"], kd.astype(BF16))
        dec_s[s["ci"]] = jnp.broadcast_to(jnp.exp2(blast), (HG_SUB, HG_DK))
        return s

    def diagonal(s):
        b, q = s["b"], s["q"]
        bs_ref, ks_ref, vs_ref = b_s.at[s["slot"]], k_s.at[s["slot"]], v_s.at[s["slot"]]
        d_blocks = []
        for i in range(nsub):
            sl = slice(i * HG_SUB, (i + 1) * HG_SUB)
            bt, qt = b[sl, :], q[sl, :]
            acc = jnp.zeros((HG_SUB, HG_DK), F32)
            for j in range(HG_SUB):
                r = i * HG_SUB + j
                arg = jnp.where(sub_row >= j, bt - bcast_row(bs_ref, r, HG_SUB), masked)
                g = jnp.exp2(arg) * (qt * bcast_row(ks_ref, r, HG_SUB))
                acc = acc + jnp.sum(g, axis=-1, keepdims=True) * bcast_row(vs_ref, r, HG_SUB)
            d_blocks.append(acc)
        oi_s[pl.ds(s["r0"], c), :] = s["o"] + jnp.concatenate(d_blocks, axis=0)

    def local_group(gi, carry):
        states = [prep(gi * HG_UNROLL + slot, slot) for slot in range(HG_UNROLL)]
        for stage in (cumulate, off_diagonal, apply_values, diagonal):
            states = [stage(s) for s in states]
        return carry

    lax.fori_loop(0, n_chunks // HG_UNROLL, local_group, 0)

    def carried_group(gi, carry):
        st = st_ref[...]
        outs = []
        for j in range(HG_STATE_UNROLL):
            ci = gi * HG_STATE_UNROLL + j
            r0 = pl.multiple_of(ci * c, c)
            outs.append((r0, _dot_nt(qe_s[pl.ds(r0, c), :], st.astype(BF16))))
            st = st * dec_s[ci][0:1, :] + kv_s[ci]
        st_ref[...] = st
        for r0, os in outs:
            oi_s[pl.ds(r0, c), :] += os
        return carry

    lax.fori_loop(0, n_chunks // HG_STATE_UNROLL, carried_group, 0)

    def finish(ri, carry):
        r0 = pl.multiple_of(ri * HG_NORM_ROWS, HG_NORM_ROWS)
        y = _rms(oi_s[pl.ds(r0, HG_NORM_ROWS), :], ng) * _silu(g_ref[pl.ds(r0, HG_NORM_ROWS), :].astype(F32))
        o_ref[pl.ds(r0, HG_NORM_ROWS), :] = y.astype(o_ref.dtype)
        return carry

    lax.fori_loop(0, q_ref.shape[0] // HG_NORM_ROWS, finish, 0)


def _hgrn(proj, hf, lb_logits, norm_g, batch, seq):
    ns = seq // SEQ_TILE
    tok = lambda b, h, s: b * ns + s
    return pl.pallas_call(
        _hgrn_kernel,
        grid=(batch, HG_HEADS, ns),
        in_specs=[
            pl.BlockSpec((2, HG_DK), lambda b, h, s: (0, h)),
            pl.BlockSpec((SEQ_TILE, HG_DK), lambda b, h, s: (tok(b, h, s), COL_HQ + h)),
            pl.BlockSpec((SEQ_TILE, HG_DK), lambda b, h, s: (tok(b, h, s), h)),
            pl.BlockSpec((SEQ_TILE, HG_DK), lambda b, h, s: (tok(b, h, s), COL_HI + h)),
            pl.BlockSpec((SEQ_TILE, HG_DK), lambda b, h, s: (tok(b, h, s), COL_HG + h)),
            pl.BlockSpec((1, HG_DK), lambda b, h, s: (0, 0)),
        ],
        out_specs=pl.BlockSpec((SEQ_TILE, HG_DK), lambda b, h, s: (tok(b, h, s), h)),
        out_shape=jax.ShapeDtypeStruct((batch * seq, HG_WIDTH), BF16),
        scratch_shapes=[
            pltpu.VMEM((HG_DK, HG_DK), F32),
            pltpu.VMEM((HG_UNROLL, HG_CHUNK, HG_DK), F32),
            pltpu.VMEM((HG_UNROLL, HG_CHUNK, HG_DK), F32),
            pltpu.VMEM((HG_UNROLL, HG_CHUNK, HG_DK), F32),
            pltpu.VMEM((SEQ_TILE, HG_DK), F32),
            pltpu.VMEM((SEQ_TILE, HG_DK), BF16),
            pltpu.VMEM((SEQ_TILE // HG_CHUNK, HG_DK, HG_DK), F32),
            pltpu.VMEM((SEQ_TILE // HG_CHUNK, HG_SUB, HG_DK), F32),
        ],
        compiler_params=_cparams(("arbitrary", "arbitrary", "arbitrary")),
        name="hgrn2",
    )(lb_logits, proj, hf, proj, proj, norm_g)


def _ret_kernel(q_ref, k_ref, v_ref, g_ref, cos_ref, sin_ref, ng_ref, o_ref,
                r_ref, oi_s, q16_s, kv_s):
    c = RET_CHUNK

    @pl.when(pl.program_id(2) == 0)
    def _():
        r_ref[...] = jnp.zeros_like(r_ref)

    hf = jnp.full((1, 1), pl.program_id(1), I32).astype(F32)
    lg = jnp.log1p(-jnp.exp2(-5.0 - hf))
    ti = lax.broadcasted_iota(I32, (c, c), 0)
    si = lax.broadcasted_iota(I32, (c, c), 1)
    rel = (ti - si).astype(F32)
    intra = jnp.where(ti >= si, jnp.exp(jnp.maximum(rel, 0.0) * lg), 0.0)
    idx = lax.broadcasted_iota(I32, (c, 1), 0).astype(F32)
    inter = jnp.exp((idx + 1.0) * lg)
    to_state = jnp.exp((c - 1.0 - idx) * lg)
    chunk_decay = jnp.exp(float(c) * lg)
    ng = ng_ref[...]
    half = RET_DK // 2

    n_chunks = q_ref.shape[0] // c

    def prep(ci):
        r0 = pl.multiple_of(ci * c, c)
        cos = cos_ref[pl.ds(r0, c), :]
        sin = sin_ref[pl.ds(r0, c), :]
        q = q_ref[pl.ds(r0, c), :].astype(F32)
        k = k_ref[pl.ds(r0, c), :].astype(F32)
        qr = (q * cos + pltpu.roll(q, half, 1) * sin) * (RET_DK ** -0.5)
        kr = k * cos + pltpu.roll(k, half, 1) * sin
        q16 = qr.astype(BF16)
        q16_s[pl.ds(r0, c), :] = q16
        return dict(ci=ci, r0=r0, q16=q16, k16=kr.astype(BF16), kts=(kr * to_state).astype(BF16),
                    v16=v_ref[pl.ds(r0, c), :])

    def scores(s):
        s["att"] = (_dot_nt(s.pop("q16"), s.pop("k16")) * intra).astype(BF16)
        return s

    def apply_values(s):
        oi_s[pl.ds(s["r0"], c), :] = _dot(s["att"], s["v16"])
        kv_s[s["ci"]] = _dot_tn(s["kts"], s["v16"])
        return s

    def local_group(gi, carry):
        states = [prep(gi * RET_UNROLL + j) for j in range(RET_UNROLL)]
        for stage in (scores, apply_values):
            states = [stage(s) for s in states]
        return carry

    lax.fori_loop(0, n_chunks // RET_UNROLL, local_group, 0)

    def carried_group(gi, carry):
        r = r_ref[...]
        outs = []
        for j in range(RET_UNROLL):
            ci = gi * RET_UNROLL + j
            r0 = pl.multiple_of(ci * c, c)
            outs.append((r0, _dot(q16_s[pl.ds(r0, c), :], r.astype(BF16))))
            r = chunk_decay * r + kv_s[ci]
        r_ref[...] = r
        for r0, qr_state in outs:
            o = oi_s[pl.ds(r0, c), :] + qr_state * inter
            y = _rms(o, ng) * _silu(g_ref[pl.ds(r0, c), :].astype(F32))
            o_ref[pl.ds(r0, c), :] = y.astype(o_ref.dtype)
        return carry

    lax.fori_loop(0, n_chunks // RET_UNROLL, carried_group, 0)


def _retention(proj, cos2, sin2, norm_g, batch, seq):
    ns = seq // SEQ_TILE
    tok = lambda b, h, s: b * ns + s
    return pl.pallas_call(
        _ret_kernel,
        grid=(batch, RET_HEADS, ns),
        in_specs=[
            pl.BlockSpec((SEQ_TILE, RET_DK), lambda b, h, s: (tok(b, h, s), COL_RQ + h)),
            pl.BlockSpec((SEQ_TILE, RET_DK), lambda b, h, s: (tok(b, h, s), COL_RK + h)),
            pl.BlockSpec((SEQ_TILE, RET_DV), lambda b, h, s: (tok(b, h, s), COL_RV // 2 + h)),
            pl.BlockSpec((SEQ_TILE, RET_DV), lambda b, h, s: (tok(b, h, s), COL_RG // 2 + h)),
            pl.BlockSpec((SEQ_TILE, RET_DK), lambda b, h, s: (s, 0)),
            pl.BlockSpec((SEQ_TILE, RET_DK), lambda b, h, s: (s, 0)),
            pl.BlockSpec((1, RET_DV), lambda b, h, s: (0, 0)),
        ],
        out_specs=pl.BlockSpec((SEQ_TILE, RET_DV), lambda b, h, s: (tok(b, h, s), h)),
        out_shape=jax.ShapeDtypeStruct((batch * seq, RET_HEADS * RET_DV), BF16),
        scratch_shapes=[
            pltpu.VMEM((RET_DK, RET_DV), F32),
            pltpu.VMEM((SEQ_TILE, RET_DV), F32),
            pltpu.VMEM((SEQ_TILE, RET_DK), BF16),
            pltpu.VMEM((SEQ_TILE // RET_CHUNK, RET_DK, RET_DV), F32),
        ],
        compiler_params=_cparams(("arbitrary", "arbitrary", "arbitrary")),
        name="retention",
    )(proj, proj, proj, proj, cos2, sin2, norm_g)


def _merge_kernel(x_ref, ya_ref, yb_ref, ga_ref, gb_ref, wa_ref, wb_ref, wo_ref, fg_ref,
                  wr_ref, br_ref, x1_ref, xs_ref, route_ref, cnt_ref):
    tm = x_ref.shape[0]

    merged = (_sigmoid(ga_ref[...].astype(F32)) * _dot(ya_ref[...], wa_ref[...])
              + _sigmoid(gb_ref[...].astype(F32)) * _dot(yb_ref[...], wb_ref[...]))
    x1 = x_ref[...] + _dot(merged.astype(BF16), wo_ref[...])
    x1_ref[...] = x1
    h2 = _rms(x1, fg_ref[...])

    hh, hm, _ = _split3(h2)
    wh, wm = wr_ref[0], wr_ref[1]
    logits = (_dot(hm, wh) + _dot(hh, wm)) + _dot(hh, wh) + br_ref[...]

    lane = lax.broadcasted_iota(I32, (tm, LANES), 1)
    neg = jnp.float32(-jnp.inf)
    big = jnp.int32(1 << 30)
    is_g = lane < N_GROUPS
    gl = jnp.where(is_g, logits, neg)
    gmax = jnp.max(gl, axis=-1, keepdims=True)
    g_idx = jnp.min(jnp.where(gl == gmax, lane, big), axis=-1, keepdims=True)
    g_w = 1.0 / jnp.sum(jnp.where(is_g, jnp.exp(gl - gmax), 0.0), axis=-1, keepdims=True)

    ex = lane - N_GROUPS
    in_grp = (ex >= g_idx * EXPERTS_PER_GROUP) & (ex < (g_idx + 1) * EXPERTS_PER_GROUP)
    el = jnp.where(in_grp, logits, neg)
    m1 = jnp.max(el, axis=-1, keepdims=True)
    e1 = jnp.min(jnp.where(el == m1, ex, big), axis=-1, keepdims=True)
    el2 = jnp.where(ex == e1, neg, el)
    m2 = jnp.max(el2, axis=-1, keepdims=True)
    e2 = jnp.min(jnp.where(el2 == m2, ex, big), axis=-1, keepdims=True)
    p2 = jnp.exp(m2 - m1)
    w1 = g_w / (1.0 + p2)
    w2 = g_w * p2 / (1.0 + p2)

    oh1 = ex == e1
    oh2 = ex == e2
    oh = jnp.where(oh1 | oh2, 1.0, 0.0)
    ri = lax.broadcasted_iota(I32, (tm, tm), 0)
    ci = lax.broadcasted_iota(I32, (tm, tm), 1)
    strict = jnp.where(ri > ci, 1.0, 0.0).astype(BF16)
    local_rank = _dot(strict, oh.astype(BF16))
    cnt = jnp.sum(oh, axis=0, keepdims=True)
    run_chunks = jnp.floor((cnt + (ROW_CHUNK - 1.0)) * (1.0 / ROW_CHUNK))
    ui = lax.broadcasted_iota(I32, (LANES, LANES), 0)
    uj = lax.broadcasted_iota(I32, (LANES, LANES), 1)
    before = jnp.where(ui < uj, 1.0, 0.0).astype(BF16)
    run_start = _dot(jnp.broadcast_to(run_chunks, (8, LANES)).astype(BF16), before)[0:1] * ROW_CHUNK
    slot = run_start + local_rank
    pos1 = jnp.sum(jnp.where(oh1, slot, 0.0), axis=-1, keepdims=True)
    pos2 = jnp.sum(jnp.where(oh2, slot, 0.0), axis=-1, keepdims=True)
    cnt_ref[0] = jnp.broadcast_to(cnt, (8, LANES))

    route = jnp.where(lane == 0, e1.astype(F32), 0.0)
    route = jnp.where(lane == 1, e2.astype(F32), route)
    route = jnp.where(lane == 2, w1, route)
    route = jnp.where(lane == 3, w2, route)
    route = jnp.where(lane == 4, pos1, route)
    route = jnp.where(lane == 5, pos2, route)
    route_ref[...] = route

    hi1 = jnp.floor(pos1 * (1.0 / 256.0))
    hi2 = jnp.floor(pos2 * (1.0 / 256.0))
    digits = jnp.where(lane == 0, hi1, 0.0)
    digits = jnp.where(lane == 1, pos1 - 256.0 * hi1, digits)
    digits = jnp.where(lane == 2, hi2, digits)
    digits = jnp.where(lane == 3, pos2 - 256.0 * hi2, digits)
    pick = jnp.where(lax.broadcasted_iota(I32, (8, LANES), 0) == lax.broadcasted_iota(I32, (8, LANES), 1),
                     1.0, 0.0).astype(BF16)
    rows = _dot_nt(pick, digits.astype(BF16))
    pos1_row = rows[0:1] * 256.0 + rows[1:2]
    pos2_row = rows[2:3] * 256.0 + rows[3:4]

    h2b = h2.astype(BF16)
    slab_row = lax.broadcasted_iota(I32, (SORT_SLAB, tm), 0).astype(F32)
    for r0 in range(0, LOCAL_ROWS, SORT_SLAB):
        sel = (slab_row == pos1_row - float(r0)) | (slab_row == pos2_row - float(r0))
        xs_ref[r0:r0 + SORT_SLAB, :] = _dot(jnp.where(sel, 1.0, 0.0).astype(BF16), h2b).astype(BF16)


def _merge(x2d, ya, yb, proj, wa, wb, wo, ffn_g, w_router, b_router):
    t = x2d.shape[0]
    const = lambda *shape: pl.BlockSpec(shape, lambda i: (0,) * len(shape))
    return pl.pallas_call(
        _merge_kernel,
        grid=(t // TOK_TILE,),
        in_specs=[
            pl.BlockSpec((TOK_TILE, D_MODEL), lambda i: (i, 0)),
            pl.BlockSpec((TOK_TILE, HG_WIDTH), lambda i: (i, 0)),
            pl.BlockSpec((TOK_TILE, D_MODEL), lambda i: (i, 0)),
            pl.BlockSpec((TOK_TILE, D_MODEL), lambda i: (i, COL_GA)),
            pl.BlockSpec((TOK_TILE, D_MODEL), lambda i: (i, COL_GB)),
            const(HG_WIDTH, D_MODEL),
            const(D_MODEL, D_MODEL),
            const(D_MODEL, D_MODEL),
            const(1, D_MODEL),
            const(2, D_MODEL, LANES),
            const(1, LANES),
        ],
        out_specs=[
            pl.BlockSpec((TOK_TILE, D_MODEL), lambda i: (i, 0)),
            pl.BlockSpec((LOCAL_ROWS, D_MODEL), lambda i: (i, 0)),
            pl.BlockSpec((TOK_TILE, LANES), lambda i: (i, 0)),
            pl.BlockSpec((1, 8, LANES), lambda i: (i, 0, 0)),
        ],
        out_shape=[
            jax.ShapeDtypeStruct((t, D_MODEL), F32),
            jax.ShapeDtypeStruct((t // TOK_TILE * LOCAL_ROWS, D_MODEL), BF16),
            jax.ShapeDtypeStruct((t, LANES), F32),
            jax.ShapeDtypeStruct((t // TOK_TILE, 8, LANES), F32),
        ],
        compiler_params=_cparams(("arbitrary",)),
        name="merge_route",
    )(x2d, ya, yb, proj, proj, wa, wb, wo, ffn_g, w_router, b_router)


def _expert_kernel(te_ref, nt_ref, src_ref, xs_ref, wg_ref, wu_ref, wd_ref, ys_ref,
                   xbuf, wg_s, wu_s, wd_s, sem):
    i = pl.program_id(0)
    nt = nt_ref[0]

    def gather(tile, slot):
        copies = []
        for c in range(CHUNKS_PER_TILE):
            row = pl.multiple_of(src_ref[tile * CHUNKS_PER_TILE + c] * ROW_CHUNK, ROW_CHUNK)
            copies.append(pltpu.make_async_copy(
                xs_ref.at[pl.ds(row, ROW_CHUNK)],
                xbuf.at[slot, pl.ds(c * ROW_CHUNK, ROW_CHUNK)], sem.at[slot]))
        return copies

    @pl.when(i == 0)
    def _():
        for k in range(EXP_AHEAD):
            for cp in gather(k, k):
                cp.start()

    @pl.when(i < nt + EXP_AHEAD)
    def _():
        for cp in gather(i, i % EXP_SLOTS):
            cp.wait()

    @pl.when((i < nt) & ((i == 0) | (te_ref[i] != te_ref[jnp.maximum(i - 1, 0)])))
    def _():
        wg_s[...] = wg_ref[0].astype(BF16)
        wu_s[...] = wu_ref[0].astype(BF16)
        wd_s[...] = wd_ref[0].astype(BF16)

    @pl.when(i < nt)
    def _():
        slot = i % EXP_SLOTS
        wg, wu, wd = wg_s[...], wu_s[...], wd_s[...]
        subs = [pl.ds(r, EXP_SUB) for r in range(0, EXP_TILE, EXP_SUB)]
        xs = [xbuf[slot, s, :] for s in subs]
        gates = [(_dot(x, wg), _dot(x, wu)) for x in xs]
        for cp in gather(i + EXP_AHEAD, (i + EXP_AHEAD) % EXP_SLOTS):
            cp.start()
        hidden = [(_silu(a) * u).astype(BF16) for a, u in gates]
        for s, h in zip(subs, hidden):
            ys_ref[s, :] = _dot(h, wd).astype(ys_ref.dtype)

    @pl.when(i >= nt)
    def _():
        ys_ref[...] = jnp.zeros_like(ys_ref)


def _experts(tile_expert, n_tiles_used, src_chunk, xs, wg, wu, wd, n_tiles):
    grid_spec = pltpu.PrefetchScalarGridSpec(
        num_scalar_prefetch=3,
        grid=(n_tiles,),
        in_specs=[
            pl.BlockSpec(memory_space=pl.ANY),
            pl.BlockSpec((1, D_MODEL, D_EXPERT), lambda i, te, nt, src: (te[i], 0, 0)),
            pl.BlockSpec((1, D_MODEL, D_EXPERT), lambda i, te, nt, src: (te[i], 0, 0)),
            pl.BlockSpec((1, D_EXPERT, D_MODEL), lambda i, te, nt, src: (te[i], 0, 0)),
        ],
        out_specs=pl.BlockSpec((EXP_TILE, D_MODEL), lambda i, te, nt, src: (i, 0)),
        scratch_shapes=[
            pltpu.VMEM((EXP_SLOTS, EXP_TILE, D_MODEL), BF16),
            pltpu.VMEM((D_MODEL, D_EXPERT), BF16),
            pltpu.VMEM((D_MODEL, D_EXPERT), BF16),
            pltpu.VMEM((D_EXPERT, D_MODEL), BF16),
            pltpu.SemaphoreType.DMA((EXP_SLOTS,)),
        ],
    )
    return pl.pallas_call(
        _expert_kernel,
        grid_spec=grid_spec,
        out_shape=jax.ShapeDtypeStruct((n_tiles * EXP_TILE, D_MODEL), BF16),
        compiler_params=_cparams(("arbitrary",)),
        name="experts",
    )(tile_expert, n_tiles_used, src_chunk, xs, wg, wu, wd)


def _final_kernel(src_ref, x1_ref, route_ref, p_ref, ys_ref, pg_ref, wpg_ref, wpp_ref, fg_ref,
                  o_ref, ybuf, sem):
    tm = x1_ref.shape[0]
    i = pl.program_id(0)
    nb = pl.num_programs(0)

    def gather(block, slot):
        copies = []
        for c in range(LOCAL_CHUNKS):
            row = pl.multiple_of(src_ref[block * LOCAL_CHUNKS + c] * ROW_CHUNK, ROW_CHUNK)
            copies.append(pltpu.make_async_copy(
                ys_ref.at[pl.ds(row, ROW_CHUNK)],
                ybuf.at[slot, pl.ds(c * ROW_CHUNK, ROW_CHUNK)], sem.at[slot]))
        return copies

    @pl.when(i == 0)
    def _():
        for cp in gather(0, 0):
            cp.start()

    @pl.when(i + 1 < nb)
    def _():
        for cp in gather(i + 1, (i + 1) % 2):
            cp.start()

    slot = i % 2
    for cp in gather(i, slot):
        cp.wait()

    route = route_ref[...]
    w1, w2 = route[:, 2:3], route[:, 3:4]
    pos1, pos2 = route[:, 4:5], route[:, 5:6]
    slab_col = lax.broadcasted_iota(I32, (tm, SORT_SLAB), 1).astype(F32)
    moe = jnp.zeros((tm, D_MODEL), F32)
    for k0 in range(0, LOCAL_ROWS, SORT_SLAB):
        sel = jnp.where(slab_col == pos1 - float(k0), w1,
                        jnp.where(slab_col == pos2 - float(k0), w2, 0.0)).astype(BF16)
        moe = moe + _dot(sel, ybuf[slot, k0:k0 + SORT_SLAB, :])
    x2 = x1_ref[...] + moe
    hp = _rms(x2, pg_ref[...]).astype(BF16)
    gate = _sigmoid(_dot(hp, wpg_ref[...]))
    x3 = x2 + gate * _dot(p_ref[...].astype(BF16), wpp_ref[...])
    o_ref[...] = _rms(x3, fg_ref[...])


def _final(src_chunk, x1, route, p2d, ys, ple_g, wpg, wpp, final_g):
    t = x1.shape[0]
    const = lambda *shape: pl.BlockSpec(shape, lambda i, src: (0,) * len(shape))
    grid_spec = pltpu.PrefetchScalarGridSpec(
        num_scalar_prefetch=1,
        grid=(t // TOK_TILE,),
        in_specs=[
            pl.BlockSpec((TOK_TILE, D_MODEL), lambda i, src: (i, 0)),
            pl.BlockSpec((TOK_TILE, LANES), lambda i, src: (i, 0)),
            pl.BlockSpec((TOK_TILE, PLE_DIM), lambda i, src: (i, 0)),
            pl.BlockSpec(memory_space=pl.ANY),
            const(1, D_MODEL),
            const(D_MODEL, D_MODEL),
            const(PLE_DIM, D_MODEL),
            const(1, D_MODEL),
        ],
        out_specs=pl.BlockSpec((TOK_TILE, D_MODEL), lambda i, src: (i, 0)),
        scratch_shapes=[pltpu.VMEM((2, LOCAL_ROWS, D_MODEL), BF16), pltpu.SemaphoreType.DMA((2,))],
    )
    return pl.pallas_call(
        _final_kernel,
        grid_spec=grid_spec,
        out_shape=jax.ShapeDtypeStruct((t, D_MODEL), F32),
        compiler_params=_cparams(("arbitrary",)),
        name="combine_ple_final",
    )(src_chunk, x1, route, p2d, ys, ple_g, wpg, wpp, final_g)


def _rotary_tables(seq):
    inv = ROPE_BASE ** (-jnp.arange(0, RET_DK, 2, dtype=F32) / RET_DK)
    ang = jnp.arange(seq, dtype=F32)[:, None] * inv[None, :]
    cos, sin = jnp.cos(ang), jnp.sin(ang)
    return jnp.concatenate([cos, cos], axis=1), jnp.concatenate([-sin, sin], axis=1)


def _layer(x2d, p2d, batch, seq, mix_norm, w_in, hg_lb_logits, hg_norm, ret_norm, w_branch_a,
           w_branch_b, w_out, ffn_norm, w_rg, b_rg, w_re, b_re, w_gate, w_up, w_down, ple_norm,
           w_ple_gate, w_ple_proj, out_gain):
    t = x2d.shape[0]
    row = lambda v: v.reshape(1, -1).astype(F32)

    proj, hf = _inproj(x2d, row(mix_norm), w_in.astype(BF16))
    ya = _hgrn(proj, hf, hg_lb_logits.astype(F32), row(hg_norm), batch, seq)
    cos2, sin2 = _rotary_tables(seq)
    yb = _retention(proj, cos2, sin2, row(ret_norm), batch, seq)

    n_r = N_GROUPS + N_EXPERTS
    w_router = jnp.zeros((D_MODEL, LANES), F32).at[:, :n_r].set(jnp.concatenate([w_rg, w_re], axis=1))
    wr_hi = w_router.astype(BF16)
    wr_mid = (w_router - wr_hi.astype(F32)).astype(BF16)
    b_router = jnp.zeros((1, LANES), F32).at[0, :n_r].set(jnp.concatenate([b_rg, b_re]))
    x1, xs, route, counts = _merge(
        x2d, ya, yb, proj, w_branch_a.astype(BF16), w_branch_b.astype(BF16), w_out.astype(BF16),
        row(ffn_norm), jnp.stack([wr_hi, wr_mid]), b_router)

    n_blocks = t // TOK_TILE
    cnt = counts[:, 0, N_GROUPS:N_GROUPS + N_EXPERTS].astype(I32)
    run_chunks = (cnt + ROW_CHUNK - 1) // ROW_CHUNK
    run_local = jnp.cumsum(run_chunks, axis=1) - run_chunks
    seg_chunks = jnp.sum(run_chunks, axis=0)
    tiles_per = (seg_chunks + CHUNKS_PER_TILE - 1) // CHUNKS_PER_TILE
    tile_end = jnp.cumsum(tiles_per)
    seg_start = (tile_end - tiles_per) * CHUNKS_PER_TILE
    run_global = seg_start[None, :] + jnp.cumsum(run_chunks, axis=0) - run_chunks
    max_chunks = (2 * t) // ROW_CHUNK + n_blocks * N_EXPERTS + N_EXPERTS * (CHUNKS_PER_TILE - 1)
    n_tiles = -(-max_chunks // CHUNKS_PER_TILE) + EXP_AHEAD
    tile_ids = jnp.arange(n_tiles, dtype=I32)
    tile_expert = jnp.minimum(jnp.sum((tile_end[None, :] <= tile_ids[:, None]).astype(I32), axis=1),
                              N_EXPERTS - 1)
    n_used = tile_end[-1:].astype(I32)

    block_ids = jnp.arange(n_blocks, dtype=I32)
    zero_local = LOCAL_CHUNKS - 1
    zero_global = n_tiles * CHUNKS_PER_TILE - 1
    g = jnp.arange(n_tiles * CHUNKS_PER_TILE, dtype=I32)[:, None]
    e_g = jnp.repeat(tile_expert, CHUNKS_PER_TILE)
    pick_e = (e_g[:, None] == jnp.arange(N_EXPERTS, dtype=I32)[None, :]).astype(I32)
    rg, rc, rl = (jnp.sum(pick_e[:, :, None] * tab.T[None, :, :], axis=1)
                  for tab in (run_global, run_chunks, run_local))
    inside = (rg <= g) & (g < rg + rc)
    gather_src = jnp.sum(jnp.where(inside, block_ids[None, :] * LOCAL_CHUNKS + rl + (g - rg), 0), axis=1)
    gather_src = jnp.where(jnp.any(inside, axis=1), gather_src, zero_local).astype(I32)

    lc = jnp.arange(LOCAL_CHUNKS, dtype=I32)[None, :, None]
    inside = (run_local[:, None, :] <= lc) & (lc < (run_local + run_chunks)[:, None, :])
    back_src = jnp.sum(jnp.where(inside, run_global[:, None, :] + lc - run_local[:, None, :], 0), axis=2)
    back_src = jnp.where(jnp.any(inside, axis=2), back_src, zero_global).astype(I32).reshape(-1)

    ys = _experts(tile_expert, n_used, gather_src, xs, w_gate, w_up, w_down, n_tiles)
    return _final(back_src, x1, route, p2d, ys, row(ple_norm), w_ple_gate.astype(BF16),
                  w_ple_proj.astype(BF16), out_gain)


def kernel(x, p, mix_norm, w_in, hg_lb_logits, hg_norm, ret_norm, w_branch_a, w_branch_b, w_out,
           ffn_norm, w_router_group, b_router_group, w_router_expert, b_router_expert,
           w_expert_gate, w_expert_up, w_expert_down, ple_norm, w_ple_gate, w_ple_proj, final_norm):
    batch, seq, d = x.shape
    depth = p.shape[0]
    assert depth == 1, "the final rmsnorm is fused into the single layer"
    x2d = x.reshape(batch * seq, d)
    out = _layer(x2d, p[0].reshape(batch * seq, -1), batch, seq, mix_norm[0], w_in[0], hg_lb_logits,
                 hg_norm[0], ret_norm[0], w_branch_a[0], w_branch_b[0], w_out[0], ffn_norm[0],
                 w_router_group[0], b_router_group[0], w_router_expert[0], b_router_expert[0],
                 w_expert_gate[0], w_expert_up[0], w_expert_down[0], ple_norm[0], w_ple_gate[0],
                 w_ple_proj[0], final_norm.reshape(1, -1).astype(F32))
    return out.reshape(batch, seq, d)
```

```python
import functools

import jax
import jax.numpy as jnp
from jax import lax
from jax.experimental import pallas as pl
from jax.experimental.pallas import tpu as pltpu

F32 = jnp.float32
BF16 = jnp.bfloat16
I32 = jnp.int32
U32 = jnp.uint32

EPS = 1e-6
D_MODEL = 1024
PLE_DIM = 256
HG_HEADS = 4
HG_DK = 128
HG_WIDTH = HG_HEADS * HG_DK
RET_HEADS = 4
RET_DK = 128
RET_DV = 256
ROPE_BASE = 10000.0
IN_TOTAL = 7168
N_GROUPS = 4
EXPERTS_PER_GROUP = 8
N_EXPERTS = 32
D_EXPERT = 256

COL_HQ, COL_HF, COL_HI, COL_HG = 0, 4, 8, 12
COL_RQ, COL_RK = 16, 20
COL_RV, COL_RG = 24, 32
COL_GA, COL_GB = 5, 6

LANES = 128
VMEM_LIMIT = 56 * 1024 * 1024

HG_CHUNK = 64
HG_SUB = 8
HG_UNROLL = 4
HG_STATE_UNROLL = 8
HG_NORM_ROWS = 256
RET_CHUNK = 128
RET_UNROLL = 4
SEQ_TILE = 1024
TOK_TILE = 512
EXP_TILE = 512
EXP_SUB = 512
EXP_AHEAD = 2
EXP_SLOTS = EXP_AHEAD + 1
ROW_CHUNK = 16
CHUNKS_PER_TILE = EXP_TILE // ROW_CHUNK
LOCAL_ROWS = 2 * TOK_TILE + N_EXPERTS * ROW_CHUNK
LOCAL_CHUNKS = LOCAL_ROWS // ROW_CHUNK
SORT_SLAB = 256


def _cparams(sem):
    return pltpu.CompilerParams(dimension_semantics=sem, vmem_limit_bytes=VMEM_LIMIT)


def _rms(x, g):
    return x * lax.rsqrt(jnp.mean(x * x, axis=-1, keepdims=True) + EPS) * g


def _sigmoid(x):
    return 1.0 / (1.0 + jnp.exp(-x))


def _silu(x):
    return x * _sigmoid(x)


def _split3(x):
    hi = x.astype(BF16)
    r1 = x - hi.astype(F32)
    mid = r1.astype(BF16)
    lo = (r1 - mid.astype(F32)).astype(BF16)
    return hi, mid, lo


def _dot(a, b):
    return jnp.dot(a, b, preferred_element_type=F32)


def _dot_nt(a, b):
    return lax.dot_general(a, b, (((1,), (1,)), ((), ())), preferred_element_type=F32)


def _dot_tn(a, b):
    return lax.dot_general(a, b, (((0,), (0,)), ((), ())), preferred_element_type=F32)


def _inproj_kernel(x_ref, g_ref, w_ref, wg_ref, wu_ref, wd_ref, proj_ref, hf_ref,
                   wg16_ref, wu16_ref, wd16_ref):
    wg16_ref[...] = wg_ref[...].astype(BF16)
    wu16_ref[...] = wu_ref[...].astype(BF16)
    wd16_ref[...] = wd_ref[...].astype(BF16)

    h = _rms(x_ref[...], g_ref[...]).astype(BF16)
    tn = HG_WIDTH
    for j in range(IN_TOTAL // tn):
        acc = _dot(h, w_ref[:, j * tn:(j + 1) * tn])
        proj_ref[:, j * tn:(j + 1) * tn] = acc.astype(BF16)
        if j * tn == COL_HF * LANES:
            hf_ref[...] = acc


def _inproj(x2d, gain, w_bf16, w_gate, w_up, w_down):
    t = x2d.shape[0]
    steps = t // TOK_TILE
    assert N_EXPERTS % steps == 0, "expert weights are converted in equal shares per grid step"
    share = N_EXPERTS // steps
    up_spec = pl.BlockSpec((share, D_MODEL, D_EXPERT), lambda i: (i, 0, 0))
    down_spec = pl.BlockSpec((share, D_EXPERT, D_MODEL), lambda i: (i, 0, 0))
    return pl.pallas_call(
        _inproj_kernel,
        grid=(steps,),
        in_specs=[
            pl.BlockSpec((TOK_TILE, D_MODEL), lambda i: (i, 0)),
            pl.BlockSpec((1, D_MODEL), lambda i: (0, 0)),
            pl.BlockSpec((D_MODEL, IN_TOTAL), lambda i: (0, 0), pipeline_mode=pl.Buffered(1)),
            up_spec, up_spec, down_spec,
        ],
        out_specs=[
            pl.BlockSpec((TOK_TILE, IN_TOTAL), lambda i: (i, 0)),
            pl.BlockSpec((TOK_TILE, HG_WIDTH), lambda i: (i, 0)),
            up_spec, up_spec, down_spec,
        ],
        out_shape=[
            jax.ShapeDtypeStruct((t, IN_TOTAL), BF16),
            jax.ShapeDtypeStruct((t, HG_WIDTH), F32),
            jax.ShapeDtypeStruct(w_gate.shape, BF16),
            jax.ShapeDtypeStruct(w_up.shape, BF16),
            jax.ShapeDtypeStruct(w_down.shape, BF16),
        ],
        compiler_params=_cparams(("arbitrary",)),
        name="inproj",
    )(x2d, gain, w_bf16, w_gate, w_up, w_down)


def _hgrn_kernel(lbl_ref, q_ref, f_ref, i_ref, g_ref, ng_ref, o_ref,
                 st_ref, b_s, k_s, v_s, oi_s, qe_s, kv_s, dec_s):
    c = HG_CHUNK
    nsub = c // HG_SUB
    n_chunks = q_ref.shape[0] // c

    @pl.when(pl.program_id(2) == 0)
    def _():
        st_ref[...] = jnp.zeros_like(st_ref)

    logits = lbl_ref[...]
    e = jnp.exp(logits - jnp.max(logits, axis=0, keepdims=True))
    lb = e[0:1] / jnp.sum(e, axis=0, keepdims=True)
    one_m_lb = jnp.sum(e[1:], axis=0, keepdims=True) / jnp.sum(e, axis=0, keepdims=True)
    ng = ng_ref[...]

    row = lax.broadcasted_iota(I32, (c, c), 0)
    col = lax.broadcasted_iota(I32, (c, c), 1)
    tri = jnp.where(row >= col, 1.0, 0.0).astype(BF16)
    row_k = lax.broadcasted_iota(I32, (c, HG_DK), 0)
    sub_row = lax.broadcasted_iota(I32, (HG_SUB, HG_DK), 0)
    masked = jnp.float32(-1e30)

    def bcast_row(ref, r, rows):
        return jnp.broadcast_to(ref[pl.ds(r, 1), :], (rows, HG_DK))

    def prep(ci, slot):
        r0 = pl.multiple_of(ci * c, c)
        z = f_ref[pl.ds(r0, c), :]
        ez = jnp.exp(-jnp.abs(z))
        rz = 1.0 / (1.0 + ez)
        pos = z >= 0.0
        logf = jnp.log2(lb + one_m_lb * jnp.where(pos, rz, ez * rz))
        kk = one_m_lb * jnp.where(pos, ez * rz, rz)
        q = _silu(q_ref[pl.ds(r0, c), :].astype(F32))
        v = i_ref[pl.ds(r0, c), :].astype(F32)
        k_s[slot] = kk
        v_s[slot] = v
        return dict(ci=ci, r0=r0, slot=slot, kk=kk, q=q, v16=v.astype(BF16), parts=_split3(logf))

    def cumulate(s):
        hi, mid, lo = s.pop("parts")
        b = (_dot(tri, lo) + _dot(tri, mid)) + _dot(tri, hi)
        b_s[s["slot"]] = b
        qe_s[pl.ds(s["r0"], c), :] = (s["q"] * jnp.exp2(b)).astype(BF16)
        s["b"] = b
        return s

    def off_diagonal(s):
        b, q, kk = s["b"], s["q"], s["kk"]
        bs_ref = b_s.at[s["slot"]]
        a_rows = [jnp.zeros((HG_SUB, c), F32)]
        for i in range(1, nsub):
            bi = bcast_row(bs_ref, i * HG_SUB - 1, c)
            qi = q[i * HG_SUB:(i + 1) * HG_SUB, :] * jnp.exp2(b[i * HG_SUB:(i + 1) * HG_SUB, :] - bi[:HG_SUB])
            ki = kk * jnp.exp2(jnp.where(row_k < i * HG_SUB, bi - b, masked))
            a_rows.append(_dot_nt(qi.astype(BF16), ki.astype(BF16)))
        s["a_off"] = jnp.concatenate(a_rows, axis=0)
        return s

    def apply_values(s):
        b, kk = s["b"], s["kk"]
        blast = b[c - 1:c, :]
        s["o"] = _dot(s.pop("a_off").astype(BF16), s["v16"])
        kd = kk * jnp.exp2(blast - b)
        kv_s[s["ci"]] = _dot_tn(s["v16"], kd.astype(BF16))
        dec_s[s["ci"]] = jnp.broadcast_to(jnp.exp2(blast), (HG_SUB, HG_DK))
        return s

    def diagonal(s):
        b, q = s["b"], s["q"]
        bs_ref, ks_ref, vs_ref = b_s.at[s["slot"]], k_s.at[s["slot"]], v_s.at[s["slot"]]
        d_blocks = []
        for i in range(nsub):
            sl = slice(i * HG_SUB, (i + 1) * HG_SUB)
            bt, qt = b[sl, :], q[sl, :]
            acc = jnp.zeros((HG_SUB, HG_DK), F32)
            for j in range(HG_SUB):
                r = i * HG_SUB + j
                arg = jnp.where(sub_row >= j, bt - bcast_row(bs_ref, r, HG_SUB), masked)
                g = jnp.exp2(arg) * (qt * bcast_row(ks_ref, r, HG_SUB))
                acc = acc + jnp.sum(g, axis=-1, keepdims=True) * bcast_row(vs_ref, r, HG_SUB)
            d_blocks.append(acc)
        oi_s[pl.ds(s["r0"], c), :] = s["o"] + jnp.concatenate(d_blocks, axis=0)

    def local_group(gi, carry):
        states = [prep(gi * HG_UNROLL + slot, slot) for slot in range(HG_UNROLL)]
        for stage in (cumulate, off_diagonal, apply_values, diagonal):
            states = [stage(s) for s in states]
        return carry

    lax.fori_loop(0, n_chunks // HG_UNROLL, local_group, 0)

    def carried_group(gi, carry):
        st = st_ref[...]
        outs = []
        for j in range(HG_STATE_UNROLL):
            ci = gi * HG_STATE_UNROLL + j
            r0 = pl.multiple_of(ci * c, c)
            outs.append((r0, _dot_nt(qe_s[pl.ds(r0, c), :], st.astype(BF16))))
            st = st * dec_s[ci][0:1, :] + kv_s[ci]
        st_ref[...] = st
        for r0, os in outs:
            oi_s[pl.ds(r0, c), :] += os
        return carry

    lax.fori_loop(0, n_chunks // HG_STATE_UNROLL, carried_group, 0)

    def finish(ri, carry):
        r0 = pl.multiple_of(ri * HG_NORM_ROWS, HG_NORM_ROWS)
        y = _rms(oi_s[pl.ds(r0, HG_NORM_ROWS), :], ng) * _silu(g_ref[pl.ds(r0, HG_NORM_ROWS), :].astype(F32))
        o_ref[pl.ds(r0, HG_NORM_ROWS), :] = y.astype(o_ref.dtype)
        return carry

    lax.fori_loop(0, q_ref.shape[0] // HG_NORM_ROWS, finish, 0)


def _hgrn(proj, hf, lb_logits, norm_g, batch, seq):
    ns = seq // SEQ_TILE
    tok = lambda b, h, s: b * ns + s
    return pl.pallas_call(
        _hgrn_kernel,
        grid=(batch, HG_HEADS, ns),
        in_specs=[
            pl.BlockSpec((2, HG_DK), lambda b, h, s: (0, h)),
            pl.BlockSpec((SEQ_TILE, HG_DK), lambda b, h, s: (tok(b, h, s), COL_HQ + h)),
            pl.BlockSpec((SEQ_TILE, HG_DK), lambda b, h, s: (tok(b, h, s), h)),
            pl.BlockSpec((SEQ_TILE, HG_DK), lambda b, h, s: (tok(b, h, s), COL_HI + h)),
            pl.BlockSpec((SEQ_TILE, HG_DK), lambda b, h, s: (tok(b, h, s), COL_HG + h)),
            pl.BlockSpec((1, HG_DK), lambda b, h, s: (0, 0)),
        ],
        out_specs=pl.BlockSpec((SEQ_TILE, HG_DK), lambda b, h, s: (tok(b, h, s), h)),
        out_shape=jax.ShapeDtypeStruct((batch * seq, HG_WIDTH), BF16),
        scratch_shapes=[
            pltpu.VMEM((HG_DK, HG_DK), F32),
            pltpu.VMEM((HG_UNROLL, HG_CHUNK, HG_DK), F32),
            pltpu.VMEM((HG_UNROLL, HG_CHUNK, HG_DK), F32),
            pltpu.VMEM((HG_UNROLL, HG_CHUNK, HG_DK), F32),
            pltpu.VMEM((SEQ_TILE, HG_DK), F32),
            pltpu.VMEM((SEQ_TILE, HG_DK), BF16),
            pltpu.VMEM((SEQ_TILE // HG_CHUNK, HG_DK, HG_DK), F32),
            pltpu.VMEM((SEQ_TILE // HG_CHUNK, HG_SUB, HG_DK), F32),
        ],
        compiler_params=_cparams(("arbitrary", "arbitrary", "arbitrary")),
        name="hgrn2",
    )(lb_logits, proj, hf, proj, proj, norm_g)


def _ret_kernel(q_ref, k_ref, v_ref, g_ref, cos_ref, sin_ref, ng_ref, o_ref,
                r_ref, oi_s, q16_s, kv_s):
    c = RET_CHUNK

    @pl.when(pl.program_id(2) == 0)
    def _():
        r_ref[...] = jnp.zeros_like(r_ref)

    hf = jnp.full((1, 1), pl.program_id(1), I32).astype(F32)
    lg = jnp.log1p(-jnp.exp2(-5.0 - hf))
    ti = lax.broadcasted_iota(I32, (c, c), 0)
    si = lax.broadcasted_iota(I32, (c, c), 1)
    rel = (ti - si).astype(F32)
    intra = jnp.where(ti >= si, jnp.exp(jnp.maximum(rel, 0.0) * lg), 0.0)
    idx = lax.broadcasted_iota(I32, (c, 1), 0).astype(F32)
    inter = jnp.exp((idx + 1.0) * lg)
    to_state = jnp.exp((c - 1.0 - idx) * lg)
    chunk_decay = jnp.exp(float(c) * lg)
    ng = ng_ref[...]
    half = RET_DK // 2

    n_chunks = q_ref.shape[0] // c

    def prep(ci):
        r0 = pl.multiple_of(ci * c, c)
        cos = cos_ref[pl.ds(r0, c), :]
        sin = sin_ref[pl.ds(r0, c), :]
        q = q_ref[pl.ds(r0, c), :].astype(F32)
        k = k_ref[pl.ds(r0, c), :].astype(F32)
        qr = (q * cos + pltpu.roll(q, half, 1) * sin) * (RET_DK ** -0.5)
        kr = k * cos + pltpu.roll(k, half, 1) * sin
        q16 = qr.astype(BF16)
        q16_s[pl.ds(r0, c), :] = q16
        return dict(ci=ci, r0=r0, q16=q16, k16=kr.astype(BF16), kts=(kr * to_state).astype(BF16),
                    v16=v_ref[pl.ds(r0, c), :])

    def scores(s):
        s["att"] = (_dot_nt(s.pop("q16"), s.pop("k16")) * intra).astype(BF16)
        return s

    def apply_values(s):
        oi_s[pl.ds(s["r0"], c), :] = _dot(s["att"], s["v16"])
        kv_s[s["ci"]] = _dot_tn(s["kts"], s["v16"])
        return s

    def local_group(gi, carry):
        states = [prep(gi * RET_UNROLL + j) for j in range(RET_UNROLL)]
        for stage in (scores, apply_values):
            states = [stage(s) for s in states]
        return carry

    lax.fori_loop(0, n_chunks // RET_UNROLL, local_group, 0)

    def carried_group(gi, carry):
        r = r_ref[...]
        outs = []
        for j in range(RET_UNROLL):
            ci = gi * RET_UNROLL + j
            r0 = pl.multiple_of(ci * c, c)
            outs.append((r0, _dot(q16_s[pl.ds(r0, c), :], r.astype(BF16))))
            r = chunk_decay * r + kv_s[ci]
        r_ref[...] = r
        for r0, qr_state in outs:
            o = oi_s[pl.ds(r0, c), :] + qr_state * inter
            y = _rms(o, ng) * _silu(g_ref[pl.ds(r0, c), :].astype(F32))
            o_ref[pl.ds(r0, c), :] = y.astype(o_ref.dtype)
        return carry

    lax.fori_loop(0, n_chunks // RET_UNROLL, carried_group, 0)


def _retention(proj, cos2, sin2, norm_g, batch, seq):
    ns = seq // SEQ_TILE
    tok = lambda b, h, s: b * ns + s
    return pl.pallas_call(
        _ret_kernel,
        grid=(batch, RET_HEADS, ns),
        in_specs=[
            pl.BlockSpec((SEQ_TILE, RET_DK), lambda b, h, s: (tok(b, h, s), COL_RQ + h)),
            pl.BlockSpec((SEQ_TILE, RET_DK), lambda b, h, s: (tok(b, h, s), COL_RK + h)),
            pl.BlockSpec((SEQ_TILE, RET_DV), lambda b, h, s: (tok(b, h, s), COL_RV // 2 + h)),
            pl.BlockSpec((SEQ_TILE, RET_DV), lambda b, h, s: (tok(b, h, s), COL_RG // 2 + h)),
            pl.BlockSpec((SEQ_TILE, RET_DK), lambda b, h, s: (s, 0)),
            pl.BlockSpec((SEQ_TILE, RET_DK), lambda b, h, s: (s, 0)),
            pl.BlockSpec((1, RET_DV), lambda b, h, s: (0, 0)),
        ],
        out_specs=pl.BlockSpec((SEQ_TILE, RET_DV), lambda b, h, s: (tok(b, h, s), h)),
        out_shape=jax.ShapeDtypeStruct((batch * seq, RET_HEADS * RET_DV), BF16),
        scratch_shapes=[
            pltpu.VMEM((RET_DK, RET_DV), F32),
            pltpu.VMEM((SEQ_TILE, RET_DV), F32),
            pltpu.VMEM((SEQ_TILE, RET_DK), BF16),
            pltpu.VMEM((SEQ_TILE // RET_CHUNK, RET_DK, RET_DV), F32),
        ],
        compiler_params=_cparams(("arbitrary", "arbitrary", "arbitrary")),
        name="retention",
    )(proj, proj, proj, proj, cos2, sin2, norm_g)


def _merge_kernel(x_ref, ya_ref, yb_ref, ga_ref, gb_ref, wa_ref, wb_ref, wo_ref, fg_ref,
                  wr_ref, br_ref, x1_ref, xs_ref, route_ref, cnt_ref, h2b_s, rows_s):
    tm = x_ref.shape[0]

    @pl.when(pl.program_id(0) == 0)
    def _():
        h2b_s[...] = jnp.zeros_like(h2b_s)
        rows_s[...] = jnp.full(rows_s.shape, -1.0, F32)

    def sort_previous(lo, hi):
        slab_row = lax.broadcasted_iota(I32, (SORT_SLAB, tm), 0).astype(F32)
        for r0 in range(lo, hi, SORT_SLAB):
            sel = ((slab_row == rows_s[0:1, :] - float(r0)) | (slab_row == rows_s[1:2, :] - float(r0)))
            xs_ref[r0:r0 + SORT_SLAB, :] = _dot(jnp.where(sel, 1.0, 0.0).astype(BF16),
                                                h2b_s[...]).astype(BF16)

    merged = (_sigmoid(ga_ref[...].astype(F32)) * _dot(ya_ref[...], wa_ref[...])
              + _sigmoid(gb_ref[...].astype(F32)) * _dot(yb_ref[...], wb_ref[...]))
    x1 = x_ref[...] + _dot(merged.astype(BF16), wo_ref[...])
    x1_ref[...] = x1
    h2 = _rms(x1, fg_ref[...])

    h2b = h2.astype(BF16)
    logits = _dot(h2b, wr_ref[...]) + br_ref[...]
    sort_previous(0, LOCAL_ROWS)

    lane = lax.broadcasted_iota(I32, (tm, LANES), 1)
    neg = jnp.float32(-jnp.inf)
    big = jnp.int32(1 << 30)
    is_g = lane < N_GROUPS
    gl = jnp.where(is_g, logits, neg)
    gmax = jnp.max(gl, axis=-1, keepdims=True)
    g_idx = jnp.min(jnp.where(gl == gmax, lane, big), axis=-1, keepdims=True)
    g_w = 1.0 / jnp.sum(jnp.where(is_g, jnp.exp(gl - gmax), 0.0), axis=-1, keepdims=True)

    ex = lane - N_GROUPS
    in_grp = (ex >= g_idx * EXPERTS_PER_GROUP) & (ex < (g_idx + 1) * EXPERTS_PER_GROUP)
    el = jnp.where(in_grp, logits, neg)
    m1 = jnp.max(el, axis=-1, keepdims=True)
    e1 = jnp.min(jnp.where(el == m1, ex, big), axis=-1, keepdims=True)
    el2 = jnp.where(ex == e1, neg, el)
    m2 = jnp.max(el2, axis=-1, keepdims=True)
    e2 = jnp.min(jnp.where(el2 == m2, ex, big), axis=-1, keepdims=True)
    p2 = jnp.exp(m2 - m1)
    w1 = g_w / (1.0 + p2)
    w2 = g_w * p2 / (1.0 + p2)

    oh1 = ex == e1
    oh2 = ex == e2
    oh = jnp.where(oh1 | oh2, 1.0, 0.0)
    ri = lax.broadcasted_iota(I32, (tm, tm), 0)
    ci = lax.broadcasted_iota(I32, (tm, tm), 1)
    strict = jnp.where(ri > ci, 1.0, 0.0).astype(BF16)
    local_rank = _dot(strict, oh.astype(BF16))
    cnt = jnp.sum(oh, axis=0, keepdims=True)
    run_chunks = jnp.floor((cnt + (ROW_CHUNK - 1.0)) * (1.0 / ROW_CHUNK))
    ui = lax.broadcasted_iota(I32, (LANES, LANES), 0)
    uj = lax.broadcasted_iota(I32, (LANES, LANES), 1)
    before = jnp.where(ui < uj, 1.0, 0.0).astype(BF16)
    run_start = _dot(jnp.broadcast_to(run_chunks, (8, LANES)).astype(BF16), before)[0:1] * ROW_CHUNK
    slot = run_start + local_rank
    pos1 = jnp.sum(jnp.where(oh1, slot, 0.0), axis=-1, keepdims=True)
    pos2 = jnp.sum(jnp.where(oh2, slot, 0.0), axis=-1, keepdims=True)
    cnt_ref[0] = jnp.broadcast_to(cnt, (8, LANES))

    route = jnp.where(lane == 0, e1.astype(F32), 0.0)
    route = jnp.where(lane == 1, e2.astype(F32), route)
    route = jnp.where(lane == 2, w1, route)
    route = jnp.where(lane == 3, w2, route)
    route = jnp.where(lane == 4, pos1, route)
    route = jnp.where(lane == 5, pos2, route)
    route_ref[...] = route

    hi1 = jnp.floor(pos1 * (1.0 / 256.0))
    hi2 = jnp.floor(pos2 * (1.0 / 256.0))
    digits = jnp.where(lane == 0, hi1, 0.0)
    digits = jnp.where(lane == 1, pos1 - 256.0 * hi1, digits)
    digits = jnp.where(lane == 2, hi2, digits)
    digits = jnp.where(lane == 3, pos2 - 256.0 * hi2, digits)
    pick = jnp.where(lax.broadcasted_iota(I32, (8, LANES), 0) == lax.broadcasted_iota(I32, (8, LANES), 1),
                     1.0, 0.0).astype(BF16)
    rows = _dot_nt(pick, digits.astype(BF16))

    h2b_s[...] = h2b
    rows_s[0:1, :] = rows[0:1] * 256.0 + rows[1:2]
    rows_s[1:2, :] = rows[2:3] * 256.0 + rows[3:4]


def _merge(x2d, ya, yb, proj, wa, wb, wo, ffn_g, w_router, b_router):
    t = x2d.shape[0]
    nb = t // TOK_TILE
    const = lambda *shape: pl.BlockSpec(shape, lambda i: (0,) * len(shape))
    cur = lambda i: jnp.minimum(i, nb - 1)
    prev = lambda i: jnp.maximum(i - 1, 0)
    return pl.pallas_call(
        _merge_kernel,
        grid=(nb + 1,),
        in_specs=[
            pl.BlockSpec((TOK_TILE, D_MODEL), lambda i: (cur(i), 0)),
            pl.BlockSpec((TOK_TILE, HG_WIDTH), lambda i: (cur(i), 0)),
            pl.BlockSpec((TOK_TILE, D_MODEL), lambda i: (cur(i), 0)),
            pl.BlockSpec((TOK_TILE, D_MODEL), lambda i: (cur(i), COL_GA)),
            pl.BlockSpec((TOK_TILE, D_MODEL), lambda i: (cur(i), COL_GB)),
            const(HG_WIDTH, D_MODEL),
            const(D_MODEL, D_MODEL),
            const(D_MODEL, D_MODEL),
            const(1, D_MODEL),
            const(D_MODEL, LANES),
            const(1, LANES),
        ],
        out_specs=[
            pl.BlockSpec((TOK_TILE, D_MODEL), lambda i: (cur(i), 0)),
            pl.BlockSpec((LOCAL_ROWS, D_MODEL), lambda i: (prev(i), 0)),
            pl.BlockSpec((TOK_TILE, LANES), lambda i: (cur(i), 0)),
            pl.BlockSpec((1, 8, LANES), lambda i: (cur(i), 0, 0)),
        ],
        out_shape=[
            jax.ShapeDtypeStruct((t, D_MODEL), F32),
            jax.ShapeDtypeStruct((t // TOK_TILE * LOCAL_ROWS, D_MODEL), BF16),
            jax.ShapeDtypeStruct((t, LANES), F32),
            jax.ShapeDtypeStruct((t // TOK_TILE, 8, LANES), F32),
        ],
        scratch_shapes=[pltpu.VMEM((TOK_TILE, D_MODEL), BF16), pltpu.VMEM((8, TOK_TILE), F32)],
        compiler_params=_cparams(("arbitrary",)),
        name="merge_route",
    )(x2d, ya, yb, proj, proj, wa, wb, wo, ffn_g, w_router, b_router)


def _expert_kernel(te_ref, nt_ref, src_ref, xs_ref, wg_ref, wu_ref, wd_ref, ys_ref, xbuf, sem):
    i = pl.program_id(0)
    nt = nt_ref[0]

    def gather(tile, slot):
        copies = []
        for c in range(CHUNKS_PER_TILE):
            row = pl.multiple_of(src_ref[tile * CHUNKS_PER_TILE + c] * ROW_CHUNK, ROW_CHUNK)
            copies.append(pltpu.make_async_copy(
                xs_ref.at[pl.ds(row, ROW_CHUNK)],
                xbuf.at[slot, pl.ds(c * ROW_CHUNK, ROW_CHUNK)], sem.at[slot]))
        return copies

    @pl.when(i == 0)
    def _():
        for k in range(EXP_AHEAD):
            for cp in gather(k, k):
                cp.start()

    @pl.when(i < nt + EXP_AHEAD)
    def _():
        for cp in gather(i, i % EXP_SLOTS):
            cp.wait()

    @pl.when(i < nt)
    def _():
        slot = i % EXP_SLOTS
        wg, wu, wd = wg_ref[0], wu_ref[0], wd_ref[0]
        subs = [pl.ds(r, EXP_SUB) for r in range(0, EXP_TILE, EXP_SUB)]
        xs = [xbuf[slot, s, :] for s in subs]
        gates = [(_dot(x, wg), _dot(x, wu)) for x in xs]
        for cp in gather(i + EXP_AHEAD, (i + EXP_AHEAD) % EXP_SLOTS):
            cp.start()
        hidden = [(_silu(a) * u).astype(BF16) for a, u in gates]
        for s, h in zip(subs, hidden):
            ys_ref[s, :] = _dot(h, wd).astype(ys_ref.dtype)

    @pl.when(i >= nt)
    def _():
        ys_ref[...] = jnp.zeros_like(ys_ref)


def _experts(tile_expert, n_tiles_used, src_chunk, xs, wg, wu, wd, n_tiles):
    grid_spec = pltpu.PrefetchScalarGridSpec(
        num_scalar_prefetch=3,
        grid=(n_tiles,),
        in_specs=[
            pl.BlockSpec(memory_space=pl.ANY),
            pl.BlockSpec((1, D_MODEL, D_EXPERT), lambda i, te, nt, src: (te[i], 0, 0)),
            pl.BlockSpec((1, D_MODEL, D_EXPERT), lambda i, te, nt, src: (te[i], 0, 0)),
            pl.BlockSpec((1, D_EXPERT, D_MODEL), lambda i, te, nt, src: (te[i], 0, 0)),
        ],
        out_specs=pl.BlockSpec((EXP_TILE, D_MODEL), lambda i, te, nt, src: (i, 0)),
        scratch_shapes=[
            pltpu.VMEM((EXP_SLOTS, EXP_TILE, D_MODEL), BF16),
            pltpu.SemaphoreType.DMA((EXP_SLOTS,)),
        ],
    )
    return pl.pallas_call(
        _expert_kernel,
        grid_spec=grid_spec,
        out_shape=jax.ShapeDtypeStruct((n_tiles * EXP_TILE, D_MODEL), BF16),
        compiler_params=_cparams(("arbitrary",)),
        name="experts",
    )(tile_expert, n_tiles_used, src_chunk, xs, wg, wu, wd)


def _final_kernel(src_ref, x1_ref, route_ref, p_ref, ys_ref, pg_ref, wpg_ref, wpp_ref, fg_ref,
                  o_ref, ybuf, sem):
    tm = x1_ref.shape[0]
    i = pl.program_id(0)
    nb = pl.num_programs(0)

    def gather(block, slot):
        copies = []
        for c in range(LOCAL_CHUNKS):
            row = pl.multiple_of(src_ref[block * LOCAL_CHUNKS + c] * ROW_CHUNK, ROW_CHUNK)
            copies.append(pltpu.make_async_copy(
                ys_ref.at[pl.ds(row, ROW_CHUNK)],
                ybuf.at[slot, pl.ds(c * ROW_CHUNK, ROW_CHUNK)], sem.at[slot]))
        return copies

    @pl.when(i == 0)
    def _():
        for cp in gather(0, 0):
            cp.start()

    slot = i % 2
    nxt = jnp.where(i + 1 < nb, i + 1, 0)
    for cp in gather(i, slot):
        cp.wait()
    for cp in gather(nxt, 1 - slot):
        cp.start()

    route = route_ref[...]
    w1, w2 = route[:, 2:3], route[:, 3:4]
    pos1, pos2 = route[:, 4:5], route[:, 5:6]
    slab_col = lax.broadcasted_iota(I32, (tm, SORT_SLAB), 1).astype(F32)
    moe = jnp.zeros((tm, D_MODEL), F32)
    for k0 in range(0, LOCAL_ROWS, SORT_SLAB):
        sel = jnp.where(slab_col == pos1 - float(k0), w1,
                        jnp.where(slab_col == pos2 - float(k0), w2, 0.0)).astype(BF16)
        moe = moe + _dot(sel, ybuf[slot, k0:k0 + SORT_SLAB, :])
    x2 = x1_ref[...] + moe
    hp = _rms(x2, pg_ref[...]).astype(BF16)
    gate = _sigmoid(_dot(hp, wpg_ref[...]))
    x3 = x2 + gate * _dot(p_ref[...].astype(BF16), wpp_ref[...])
    o_ref[...] = _rms(x3, fg_ref[...])

    @pl.when(i == nb - 1)
    def _():
        for cp in gather(0, 1 - slot):
            cp.wait()


def _final(src_chunk, x1, route, p2d, ys, ple_g, wpg, wpp, final_g):
    t = x1.shape[0]
    const = lambda *shape: pl.BlockSpec(shape, lambda i, src: (0,) * len(shape))
    grid_spec = pltpu.PrefetchScalarGridSpec(
        num_scalar_prefetch=1,
        grid=(t // TOK_TILE,),
        in_specs=[
            pl.BlockSpec((TOK_TILE, D_MODEL), lambda i, src: (i, 0)),
            pl.BlockSpec((TOK_TILE, LANES), lambda i, src: (i, 0)),
            pl.BlockSpec((TOK_TILE, PLE_DIM), lambda i, src: (i, 0)),
            pl.BlockSpec(memory_space=pl.ANY),
            const(1, D_MODEL),
            const(D_MODEL, D_MODEL),
            const(PLE_DIM, D_MODEL),
            const(1, D_MODEL),
        ],
        out_specs=pl.BlockSpec((TOK_TILE, D_MODEL), lambda i, src: (i, 0)),
        scratch_shapes=[pltpu.VMEM((2, LOCAL_ROWS, D_MODEL), BF16), pltpu.SemaphoreType.DMA((2,))],
    )
    return pl.pallas_call(
        _final_kernel,
        grid_spec=grid_spec,
        out_shape=jax.ShapeDtypeStruct((t, D_MODEL), F32),
        compiler_params=_cparams(("arbitrary",)),
        name="combine_ple_final",
    )(src_chunk, x1, route, p2d, ys, ple_g, wpg, wpp, final_g)


def _rotary_tables(seq):
    inv = ROPE_BASE ** (-jnp.arange(0, RET_DK, 2, dtype=F32) / RET_DK)
    ang = jnp.arange(seq, dtype=F32)[:, None] * inv[None, :]
    cos, sin = jnp.cos(ang), jnp.sin(ang)
    return jnp.concatenate([cos, cos], axis=1), jnp.concatenate([-sin, sin], axis=1)


def _layer(x2d, p2d, batch, seq, mix_norm, w_in, hg_lb_logits, hg_norm, ret_norm, w_branch_a,
           w_branch_b, w_out, ffn_norm, w_rg, b_rg, w_re, b_re, w_gate, w_up, w_down, ple_norm,
           w_ple_gate, w_ple_proj, out_gain):
    t = x2d.shape[0]
    row = lambda v: v.reshape(1, -1).astype(F32)

    proj, hf, wg16, wu16, wd16 = _inproj(x2d, row(mix_norm), w_in.astype(BF16), w_gate, w_up, w_down)
    ya = _hgrn(proj, hf, hg_lb_logits.astype(F32), row(hg_norm), batch, seq)
    cos2, sin2 = _rotary_tables(seq)
    yb = _retention(proj, cos2, sin2, row(ret_norm), batch, seq)

    n_r = N_GROUPS + N_EXPERTS
    w_router = jnp.pad(jnp.concatenate([w_rg, w_re], axis=1).astype(BF16), ((0, 0), (0, LANES - n_r)))
    b_router = jnp.zeros((1, LANES), F32).at[0, :n_r].set(jnp.concatenate([b_rg, b_re]))
    x1, xs, route, counts = _merge(
        x2d, ya, yb, proj, w_branch_a.astype(BF16), w_branch_b.astype(BF16), w_out.astype(BF16),
        row(ffn_norm), w_router, b_router)

    n_blocks = t // TOK_TILE
    cnt = counts[:, 0, N_GROUPS:N_GROUPS + N_EXPERTS].astype(I32)
    run_chunks = (cnt + ROW_CHUNK - 1) // ROW_CHUNK
    run_local = jnp.cumsum(run_chunks, axis=1) - run_chunks
    seg_chunks = jnp.sum(run_chunks, axis=0)
    tiles_per = (seg_chunks + CHUNKS_PER_TILE - 1) // CHUNKS_PER_TILE
    tile_end = jnp.cumsum(tiles_per)
    seg_start = (tile_end - tiles_per) * CHUNKS_PER_TILE
    run_global = seg_start[None, :] + jnp.cumsum(run_chunks, axis=0) - run_chunks
    max_chunks = (2 * t) // ROW_CHUNK + n_blocks * N_EXPERTS + N_EXPERTS * (CHUNKS_PER_TILE - 1)
    n_tiles = -(-max_chunks // CHUNKS_PER_TILE) + EXP_AHEAD
    tile_ids = jnp.arange(n_tiles, dtype=I32)
    tile_expert = jnp.minimum(jnp.sum((tile_end[None, :] <= tile_ids[:, None]).astype(I32), axis=1),
                              N_EXPERTS - 1)
    n_used = tile_end[-1:].astype(I32)

    block_ids = jnp.arange(n_blocks, dtype=I32)
    zero_local = LOCAL_CHUNKS - 1
    zero_global = n_tiles * CHUNKS_PER_TILE - 1
    g = jnp.arange(n_tiles * CHUNKS_PER_TILE, dtype=I32)[:, None]
    e_g = jnp.repeat(tile_expert, CHUNKS_PER_TILE)
    pick_e = (e_g[:, None] == jnp.arange(N_EXPERTS, dtype=I32)[None, :]).astype(I32)
    rg, rc, rl = (jnp.sum(pick_e[:, :, None] * tab.T[None, :, :], axis=1)
                  for tab in (run_global, run_chunks, run_local))
    inside = (rg <= g) & (g < rg + rc)
    gather_src = jnp.sum(jnp.where(inside, block_ids[None, :] * LOCAL_CHUNKS + rl + (g - rg), 0), axis=1)
    gather_src = jnp.where(jnp.any(inside, axis=1), gather_src, zero_local).astype(I32)

    lc = jnp.arange(LOCAL_CHUNKS, dtype=I32)[None, :, None]
    inside = (run_local[:, None, :] <= lc) & (lc < (run_local + run_chunks)[:, None, :])
    back_src = jnp.sum(jnp.where(inside, run_global[:, None, :] + lc - run_local[:, None, :], 0), axis=2)
    back_src = jnp.where(jnp.any(inside, axis=2), back_src, zero_global).astype(I32).reshape(-1)

    ys = _experts(tile_expert, n_used, gather_src, xs, wg16, wu16, wd16, n_tiles)
    return _final(back_src, x1, route, p2d, ys, row(ple_norm), w_ple_gate.astype(BF16),
                  w_ple_proj.astype(BF16), out_gain)


def kernel(x, p, mix_norm, w_in, hg_lb_logits, hg_norm, ret_norm, w_branch_a, w_branch_b, w_out,
           ffn_norm, w_router_group, b_router_group, w_router_expert, b_router_expert,
           w_expert_gate, w_expert_up, w_expert_down, ple_norm, w_ple_gate, w_ple_proj, final_norm):
    batch, seq, d = x.shape
    depth = p.shape[0]
    assert depth == 1, "the final rmsnorm is fused into the single layer"
    x2d = x.reshape(batch * seq, d)
    out = _layer(x2d, p[0].reshape(batch * seq, -1), batch, seq, mix_norm[0], w_in[0], hg_lb_logits,
                 hg_norm[0], ret_norm[0], w_branch_a[0], w_branch_b[0], w_out[0], ffn_norm[0],
                 w_router_group[0], b_router_group[0], w_router_expert[0], b_router_expert[0],
                 w_expert_gate[0], w_expert_up[0], w_expert_down[0], ple_norm[0], w_ple_gate[0],
                 w_ple_proj[0], final_norm.reshape(1, -1).astype(F32))
    return out.reshape(batch, seq, d)
```

```python
import functools

import jax
import jax.numpy as jnp
from jax import lax
from jax.experimental import pallas as pl
from jax.experimental.pallas import tpu as pltpu

F32 = jnp.float32
BF16 = jnp.bfloat16
I32 = jnp.int32
U32 = jnp.uint32

EPS = 1e-6
D_MODEL = 1024
PLE_DIM = 256
HG_HEADS = 4
HG_DK = 128
HG_WIDTH = HG_HEADS * HG_DK
RET_HEADS = 4
RET_DK = 128
RET_DV = 256
ROPE_BASE = 10000.0
ROPE_SPLIT = 64
IN_TOTAL = 7168
N_GROUPS = 4
EXPERTS_PER_GROUP = 8
N_EXPERTS = 32
D_EXPERT = 256

COL_HQ, COL_HF, COL_HI, COL_HG = 0, 4, 8, 12
COL_RQ, COL_RK = 16, 20
COL_RV, COL_RG = 24, 32
COL_GA, COL_GB = 5, 6

LANES = 128
VMEM_LIMIT = 56 * 1024 * 1024

HG_CHUNK = 64
HG_SUB = 8
HG_UNROLL = 4
HG_STATE_UNROLL = 8
HG_NORM_ROWS = 256
RET_CHUNK = 128
RET_UNROLL = 4
SEQ_TILE = 1024
TOK_TILE = 512
EXP_TILE = 512
EXP_SUB = 512
EXP_AHEAD = 2
EXP_SLOTS = EXP_AHEAD + 1
ROW_CHUNK = 16
CHUNKS_PER_TILE = EXP_TILE // ROW_CHUNK
LOCAL_ROWS = 2 * TOK_TILE + N_EXPERTS * ROW_CHUNK
LOCAL_CHUNKS = LOCAL_ROWS // ROW_CHUNK
SORT_SLAB = 256


def _cparams(sem):
    return pltpu.CompilerParams(dimension_semantics=sem, vmem_limit_bytes=VMEM_LIMIT)


def _rms(x, g):
    return x * lax.rsqrt(jnp.mean(x * x, axis=-1, keepdims=True) + EPS) * g


def _sigmoid(x):
    return 1.0 / (1.0 + jnp.exp(-x))


def _silu(x):
    return x * _sigmoid(x)


def _split3(x):
    hi = x.astype(BF16)
    r1 = x - hi.astype(F32)
    mid = r1.astype(BF16)
    lo = (r1 - mid.astype(F32)).astype(BF16)
    return hi, mid, lo


def _dot(a, b):
    return jnp.dot(a, b, preferred_element_type=F32)


def _dot_nt(a, b):
    return lax.dot_general(a, b, (((1,), (1,)), ((), ())), preferred_element_type=F32)


def _dot_tn(a, b):
    return lax.dot_general(a, b, (((0,), (0,)), ((), ())), preferred_element_type=F32)


def _inproj_kernel(x_ref, g_ref, w_ref, wg_ref, wu_ref, wd_ref, proj_ref, hf_ref,
                   wg16_ref, wu16_ref, wd16_ref):
    wg16_ref[...] = wg_ref[...].astype(BF16)
    wu16_ref[...] = wu_ref[...].astype(BF16)
    wd16_ref[...] = wd_ref[...].astype(BF16)

    h = _rms(x_ref[...], g_ref[...]).astype(BF16)
    tn = HG_WIDTH
    for j in range(IN_TOTAL // tn):
        acc = _dot(h, w_ref[:, j * tn:(j + 1) * tn])
        proj_ref[:, j * tn:(j + 1) * tn] = acc.astype(BF16)
        if j * tn == COL_HF * LANES:
            hf_ref[...] = acc


def _inproj(x2d, gain, w_bf16, w_gate, w_up, w_down):
    t = x2d.shape[0]
    steps = t // TOK_TILE
    assert N_EXPERTS % steps == 0, "expert weights are converted in equal shares per grid step"
    share = N_EXPERTS // steps
    up_spec = pl.BlockSpec((share, D_MODEL, D_EXPERT), lambda i: (i, 0, 0))
    down_spec = pl.BlockSpec((share, D_EXPERT, D_MODEL), lambda i: (i, 0, 0))
    return pl.pallas_call(
        _inproj_kernel,
        grid=(steps,),
        in_specs=[
            pl.BlockSpec((TOK_TILE, D_MODEL), lambda i: (i, 0)),
            pl.BlockSpec((1, D_MODEL), lambda i: (0, 0)),
            pl.BlockSpec((D_MODEL, IN_TOTAL), lambda i: (0, 0), pipeline_mode=pl.Buffered(1)),
            up_spec, up_spec, down_spec,
        ],
        out_specs=[
            pl.BlockSpec((TOK_TILE, IN_TOTAL), lambda i: (i, 0)),
            pl.BlockSpec((TOK_TILE, HG_WIDTH), lambda i: (i, 0)),
            up_spec, up_spec, down_spec,
        ],
        out_shape=[
            jax.ShapeDtypeStruct((t, IN_TOTAL), BF16),
            jax.ShapeDtypeStruct((t, HG_WIDTH), F32),
            jax.ShapeDtypeStruct(w_gate.shape, BF16),
            jax.ShapeDtypeStruct(w_up.shape, BF16),
            jax.ShapeDtypeStruct(w_down.shape, BF16),
        ],
        compiler_params=_cparams(("arbitrary",)),
        name="inproj",
    )(x2d, gain, w_bf16, w_gate, w_up, w_down)


def _hgrn_kernel(lbl_ref, q_ref, f_ref, i_ref, g_ref, ng_ref, o_ref,
                 st_ref, b_s, k_s, v_s, oi_s, qe_s, kv_s, dec_s):
    c = HG_CHUNK
    nsub = c // HG_SUB
    n_chunks = q_ref.shape[0] // c

    @pl.when(pl.program_id(2) == 0)
    def _():
        st_ref[...] = jnp.zeros_like(st_ref)

    logits = lbl_ref[...]
    e = jnp.exp(logits - jnp.max(logits, axis=0, keepdims=True))
    lb = e[0:1] / jnp.sum(e, axis=0, keepdims=True)
    one_m_lb = jnp.sum(e[1:], axis=0, keepdims=True) / jnp.sum(e, axis=0, keepdims=True)
    ng = ng_ref[...]

    row = lax.broadcasted_iota(I32, (c, c), 0)
    col = lax.broadcasted_iota(I32, (c, c), 1)
    tri = jnp.where(row >= col, 1.0, 0.0).astype(BF16)
    row_k = lax.broadcasted_iota(I32, (c, HG_DK), 0)
    sub_row = lax.broadcasted_iota(I32, (HG_SUB, HG_DK), 0)
    masked = jnp.float32(-1e30)

    def bcast_row(ref, r, rows):
        return jnp.broadcast_to(ref[pl.ds(r, 1), :], (rows, HG_DK))

    def prep(ci, slot):
        r0 = pl.multiple_of(ci * c, c)
        z = f_ref[pl.ds(r0, c), :]
        ez = jnp.exp(-jnp.abs(z))
        rz = 1.0 / (1.0 + ez)
        pos = z >= 0.0
        logf = jnp.log2(lb + one_m_lb * jnp.where(pos, rz, ez * rz))
        kk = one_m_lb * jnp.where(pos, ez * rz, rz)
        q = _silu(q_ref[pl.ds(r0, c), :].astype(F32))
        v = i_ref[pl.ds(r0, c), :].astype(F32)
        k_s[slot] = kk
        v_s[slot] = v
        return dict(ci=ci, r0=r0, slot=slot, kk=kk, q=q, v16=v.astype(BF16), parts=_split3(logf))

    def cumulate(s):
        hi, mid, lo = s.pop("parts")
        b = (_dot(tri, lo) + _dot(tri, mid)) + _dot(tri, hi)
        b_s[s["slot"]] = b
        qe_s[pl.ds(s["r0"], c), :] = (s["q"] * jnp.exp2(b)).astype(BF16)
        s["b"] = b
        return s

    def off_diagonal(s):
        b, q, kk = s["b"], s["q"], s["kk"]
        bs_ref = b_s.at[s["slot"]]
        a_rows = [jnp.zeros((HG_SUB, c), F32)]
        for i in range(1, nsub):
            bi = bcast_row(bs_ref, i * HG_SUB - 1, c)
            qi = q[i * HG_SUB:(i + 1) * HG_SUB, :] * jnp.exp2(b[i * HG_SUB:(i + 1) * HG_SUB, :] - bi[:HG_SUB])
            ki = kk * jnp.exp2(jnp.where(row_k < i * HG_SUB, bi - b, masked))
            a_rows.append(_dot_nt(qi.astype(BF16), ki.astype(BF16)))
        s["a_off"] = jnp.concatenate(a_rows, axis=0)
        return s

    def apply_values(s):
        b, kk = s["b"], s["kk"]
        blast = b[c - 1:c, :]
        s["o"] = _dot(s.pop("a_off").astype(BF16), s["v16"])
        kd = kk * jnp.exp2(blast - b)
        kv_s[s["ci"]] = _dot_tn(s["v16"], kd.astype(BF16))
        dec_s[s["ci"]] = jnp.broadcast_to(jnp.exp2(blast), (HG_SUB, HG_DK))
        return s

    def diagonal(s):
        b, q = s["b"], s["q"]
        bs_ref, ks_ref, vs_ref = b_s.at[s["slot"]], k_s.at[s["slot"]], v_s.at[s["slot"]]
        d_blocks = []
        for i in range(nsub):
            sl = slice(i * HG_SUB, (i + 1) * HG_SUB)
            bt, qt = b[sl, :], q[sl, :]
            acc = jnp.zeros((HG_SUB, HG_DK), F32)
            for j in range(HG_SUB):
                r = i * HG_SUB + j
                arg = jnp.where(sub_row >= j, bt - bcast_row(bs_ref, r, HG_SUB), masked)
                g = jnp.exp2(arg) * (qt * bcast_row(ks_ref, r, HG_SUB))
                acc = acc + jnp.sum(g, axis=-1, keepdims=True) * bcast_row(vs_ref, r, HG_SUB)
            d_blocks.append(acc)
        oi_s[pl.ds(s["r0"], c), :] = s["o"] + jnp.concatenate(d_blocks, axis=0)

    def local_group(gi, carry):
        states = [prep(gi * HG_UNROLL + slot, slot) for slot in range(HG_UNROLL)]
        for stage in (cumulate, off_diagonal, apply_values, diagonal):
            states = [stage(s) for s in states]
        return carry

    lax.fori_loop(0, n_chunks // HG_UNROLL, local_group, 0)

    def carried_group(gi, carry):
        st = st_ref[...]
        outs = []
        for j in range(HG_STATE_UNROLL):
            ci = gi * HG_STATE_UNROLL + j
            r0 = pl.multiple_of(ci * c, c)
            outs.append((r0, _dot_nt(qe_s[pl.ds(r0, c), :], st.astype(BF16))))
            st = st * dec_s[ci][0:1, :] + kv_s[ci]
        st_ref[...] = st
        for r0, os in outs:
            oi_s[pl.ds(r0, c), :] += os
        return carry

    lax.fori_loop(0, n_chunks // HG_STATE_UNROLL, carried_group, 0)

    def finish(ri, carry):
        r0 = pl.multiple_of(ri * HG_NORM_ROWS, HG_NORM_ROWS)
        y = _rms(oi_s[pl.ds(r0, HG_NORM_ROWS), :], ng) * _silu(g_ref[pl.ds(r0, HG_NORM_ROWS), :].astype(F32))
        o_ref[pl.ds(r0, HG_NORM_ROWS), :] = y.astype(o_ref.dtype)
        return carry

    lax.fori_loop(0, q_ref.shape[0] // HG_NORM_ROWS, finish, 0)


def _hgrn(proj, hf, lb_logits, norm_g, batch, seq):
    ns = seq // SEQ_TILE
    tok = lambda b, h, s: b * ns + s
    return pl.pallas_call(
        _hgrn_kernel,
        grid=(batch, HG_HEADS, ns),
        in_specs=[
            pl.BlockSpec((2, HG_DK), lambda b, h, s: (0, h)),
            pl.BlockSpec((SEQ_TILE, HG_DK), lambda b, h, s: (tok(b, h, s), COL_HQ + h)),
            pl.BlockSpec((SEQ_TILE, HG_DK), lambda b, h, s: (tok(b, h, s), h)),
            pl.BlockSpec((SEQ_TILE, HG_DK), lambda b, h, s: (tok(b, h, s), COL_HI + h)),
            pl.BlockSpec((SEQ_TILE, HG_DK), lambda b, h, s: (tok(b, h, s), COL_HG + h)),
            pl.BlockSpec((1, HG_DK), lambda b, h, s: (0, 0)),
        ],
        out_specs=pl.BlockSpec((SEQ_TILE, HG_DK), lambda b, h, s: (tok(b, h, s), h)),
        out_shape=jax.ShapeDtypeStruct((batch * seq, HG_WIDTH), BF16),
        scratch_shapes=[
            pltpu.VMEM((HG_DK, HG_DK), F32),
            pltpu.VMEM((HG_UNROLL, HG_CHUNK, HG_DK), F32),
            pltpu.VMEM((HG_UNROLL, HG_CHUNK, HG_DK), F32),
            pltpu.VMEM((HG_UNROLL, HG_CHUNK, HG_DK), F32),
            pltpu.VMEM((SEQ_TILE, HG_DK), F32),
            pltpu.VMEM((SEQ_TILE, HG_DK), BF16),
            pltpu.VMEM((SEQ_TILE // HG_CHUNK, HG_DK, HG_DK), F32),
            pltpu.VMEM((SEQ_TILE // HG_CHUNK, HG_SUB, HG_DK), F32),
        ],
        compiler_params=_cparams(("arbitrary", "arbitrary", "arbitrary")),
        name="hgrn2",
    )(lb_logits, proj, hf, proj, proj, norm_g)


def _ret_kernel(q_ref, k_ref, v_ref, g_ref, cos_ref, sin_ref, ng_ref, o_ref,
                r_ref, oi_s, q16_s, kv_s):
    c = RET_CHUNK

    @pl.when(pl.program_id(2) == 0)
    def _():
        r_ref[...] = jnp.zeros_like(r_ref)

    hf = jnp.full((1, 1), pl.program_id(1), I32).astype(F32)
    lg = jnp.log1p(-jnp.exp2(-5.0 - hf))
    ti = lax.broadcasted_iota(I32, (c, c), 0)
    si = lax.broadcasted_iota(I32, (c, c), 1)
    rel = (ti - si).astype(F32)
    intra = jnp.where(ti >= si, jnp.exp(jnp.maximum(rel, 0.0) * lg), 0.0)
    idx = lax.broadcasted_iota(I32, (c, 1), 0).astype(F32)
    inter = jnp.exp((idx + 1.0) * lg)
    to_state = jnp.exp((c - 1.0 - idx) * lg)
    chunk_decay = jnp.exp(float(c) * lg)
    ng = ng_ref[...]
    half = RET_DK // 2

    n_chunks = q_ref.shape[0] // c

    def prep(ci):
        r0 = pl.multiple_of(ci * c, c)
        cos = cos_ref[pl.ds(r0, c), :]
        sin = sin_ref[pl.ds(r0, c), :]
        q = q_ref[pl.ds(r0, c), :].astype(F32)
        k = k_ref[pl.ds(r0, c), :].astype(F32)
        qr = (q * cos + pltpu.roll(q, half, 1) * sin) * (RET_DK ** -0.5)
        kr = k * cos + pltpu.roll(k, half, 1) * sin
        q16 = qr.astype(BF16)
        q16_s[pl.ds(r0, c), :] = q16
        return dict(ci=ci, r0=r0, q16=q16, k16=kr.astype(BF16), kts=(kr * to_state).astype(BF16),
                    v16=v_ref[pl.ds(r0, c), :])

    def scores(s):
        s["att"] = (_dot_nt(s.pop("q16"), s.pop("k16")) * intra).astype(BF16)
        return s

    def apply_values(s):
        oi_s[pl.ds(s["r0"], c), :] = _dot(s["att"], s["v16"])
        kv_s[s["ci"]] = _dot_tn(s["kts"], s["v16"])
        return s

    def local_group(gi, carry):
        states = [prep(gi * RET_UNROLL + j) for j in range(RET_UNROLL)]
        for stage in (scores, apply_values):
            states = [stage(s) for s in states]
        return carry

    lax.fori_loop(0, n_chunks // RET_UNROLL, local_group, 0)

    def carried_group(gi, carry):
        r = r_ref[...]
        outs = []
        for j in range(RET_UNROLL):
            ci = gi * RET_UNROLL + j
            r0 = pl.multiple_of(ci * c, c)
            outs.append((r0, _dot(q16_s[pl.ds(r0, c), :], r.astype(BF16))))
            r = chunk_decay * r + kv_s[ci]
        r_ref[...] = r
        for r0, qr_state in outs:
            o = oi_s[pl.ds(r0, c), :] + qr_state * inter
            y = _rms(o, ng) * _silu(g_ref[pl.ds(r0, c), :].astype(F32))
            o_ref[pl.ds(r0, c), :] = y.astype(o_ref.dtype)
        return carry

    lax.fori_loop(0, n_chunks // RET_UNROLL, carried_group, 0)


def _retention(proj, cos2, sin2, norm_g, batch, seq):
    ns = seq // SEQ_TILE
    tok = lambda b, h, s: b * ns + s
    return pl.pallas_call(
        _ret_kernel,
        grid=(batch, RET_HEADS, ns),
        in_specs=[
            pl.BlockSpec((SEQ_TILE, RET_DK), lambda b, h, s: (tok(b, h, s), COL_RQ + h)),
            pl.BlockSpec((SEQ_TILE, RET_DK), lambda b, h, s: (tok(b, h, s), COL_RK + h)),
            pl.BlockSpec((SEQ_TILE, RET_DV), lambda b, h, s: (tok(b, h, s), COL_RV // 2 + h)),
            pl.BlockSpec((SEQ_TILE, RET_DV), lambda b, h, s: (tok(b, h, s), COL_RG // 2 + h)),
            pl.BlockSpec((SEQ_TILE, RET_DK), lambda b, h, s: (s, 0)),
            pl.BlockSpec((SEQ_TILE, RET_DK), lambda b, h, s: (s, 0)),
            pl.BlockSpec((1, RET_DV), lambda b, h, s: (0, 0)),
        ],
        out_specs=pl.BlockSpec((SEQ_TILE, RET_DV), lambda b, h, s: (tok(b, h, s), h)),
        out_shape=jax.ShapeDtypeStruct((batch * seq, RET_HEADS * RET_DV), BF16),
        scratch_shapes=[
            pltpu.VMEM((RET_DK, RET_DV), F32),
            pltpu.VMEM((SEQ_TILE, RET_DV), F32),
            pltpu.VMEM((SEQ_TILE, RET_DK), BF16),
            pltpu.VMEM((SEQ_TILE // RET_CHUNK, RET_DK, RET_DV), F32),
        ],
        compiler_params=_cparams(("arbitrary", "arbitrary", "arbitrary")),
        name="retention",
    )(proj, proj, proj, proj, cos2, sin2, norm_g)


def _merge_kernel(x_ref, ya_ref, yb_ref, ga_ref, gb_ref, wa_ref, wb_ref, wo_ref, fg_ref,
                  wr_ref, br_ref, x1_ref, xs_ref, route_ref, cnt_ref, h2b_s, rows_s):
    tm = x_ref.shape[0]

    @pl.when(pl.program_id(0) == 0)
    def _():
        h2b_s[...] = jnp.zeros_like(h2b_s)
        rows_s[...] = jnp.full(rows_s.shape, -1.0, F32)

    def sort_previous(lo, hi):
        slab_row = lax.broadcasted_iota(I32, (SORT_SLAB, tm), 0).astype(F32)
        for r0 in range(lo, hi, SORT_SLAB):
            sel = ((slab_row == rows_s[0:1, :] - float(r0)) | (slab_row == rows_s[1:2, :] - float(r0)))
            xs_ref[r0:r0 + SORT_SLAB, :] = _dot(jnp.where(sel, 1.0, 0.0).astype(BF16),
                                                h2b_s[...]).astype(BF16)

    merged = (_sigmoid(ga_ref[...].astype(F32)) * _dot(ya_ref[...], wa_ref[...])
              + _sigmoid(gb_ref[...].astype(F32)) * _dot(yb_ref[...], wb_ref[...]))
    x1 = x_ref[...] + _dot(merged.astype(BF16), wo_ref[...])
    x1_ref[...] = x1
    h2 = _rms(x1, fg_ref[...])

    h2b = h2.astype(BF16)
    logits = _dot(h2b, wr_ref[...]) + br_ref[...]
    sort_previous(0, LOCAL_ROWS)

    lane = lax.broadcasted_iota(I32, (tm, LANES), 1)
    neg = jnp.float32(-jnp.inf)
    big = jnp.int32(1 << 30)
    is_g = lane < N_GROUPS
    gl = jnp.where(is_g, logits, neg)
    gmax = jnp.max(gl, axis=-1, keepdims=True)
    g_idx = jnp.min(jnp.where(gl == gmax, lane, big), axis=-1, keepdims=True)
    g_w = 1.0 / jnp.sum(jnp.where(is_g, jnp.exp(gl - gmax), 0.0), axis=-1, keepdims=True)

    ex = lane - N_GROUPS
    in_grp = (ex >= g_idx * EXPERTS_PER_GROUP) & (ex < (g_idx + 1) * EXPERTS_PER_GROUP)
    el = jnp.where(in_grp, logits, neg)
    m1 = jnp.max(el, axis=-1, keepdims=True)
    e1 = jnp.min(jnp.where(el == m1, ex, big), axis=-1, keepdims=True)
    el2 = jnp.where(ex == e1, neg, el)
    m2 = jnp.max(el2, axis=-1, keepdims=True)
    e2 = jnp.min(jnp.where(el2 == m2, ex, big), axis=-1, keepdims=True)
    p2 = jnp.exp(m2 - m1)
    w1 = g_w / (1.0 + p2)
    w2 = g_w * p2 / (1.0 + p2)

    oh1 = ex == e1
    oh2 = ex == e2
    oh = jnp.where(oh1 | oh2, 1.0, 0.0)
    ri = lax.broadcasted_iota(I32, (tm, tm), 0)
    ci = lax.broadcasted_iota(I32, (tm, tm), 1)
    strict = jnp.where(ri > ci, 1.0, 0.0).astype(BF16)
    local_rank = _dot(strict, oh.astype(BF16))
    cnt = jnp.sum(oh, axis=0, keepdims=True)
    run_chunks = jnp.floor((cnt + (ROW_CHUNK - 1.0)) * (1.0 / ROW_CHUNK))
    ui = lax.broadcasted_iota(I32, (LANES, LANES), 0)
    uj = lax.broadcasted_iota(I32, (LANES, LANES), 1)
    before = jnp.where(ui < uj, 1.0, 0.0).astype(BF16)
    run_start = _dot(jnp.broadcast_to(run_chunks, (8, LANES)).astype(BF16), before)[0:1] * ROW_CHUNK
    slot = run_start + local_rank
    pos1 = jnp.sum(jnp.where(oh1, slot, 0.0), axis=-1, keepdims=True)
    pos2 = jnp.sum(jnp.where(oh2, slot, 0.0), axis=-1, keepdims=True)
    cnt_ref[0] = jnp.broadcast_to(cnt, (8, LANES))

    route = jnp.where(lane == 0, e1.astype(F32), 0.0)
    route = jnp.where(lane == 1, e2.astype(F32), route)
    route = jnp.where(lane == 2, w1, route)
    route = jnp.where(lane == 3, w2, route)
    route = jnp.where(lane == 4, pos1, route)
    route = jnp.where(lane == 5, pos2, route)
    route_ref[...] = route

    hi1 = jnp.floor(pos1 * (1.0 / 256.0))
    hi2 = jnp.floor(pos2 * (1.0 / 256.0))
    digits = jnp.where(lane == 0, hi1, 0.0)
    digits = jnp.where(lane == 1, pos1 - 256.0 * hi1, digits)
    digits = jnp.where(lane == 2, hi2, digits)
    digits = jnp.where(lane == 3, pos2 - 256.0 * hi2, digits)
    pick = jnp.where(lax.broadcasted_iota(I32, (8, LANES), 0) == lax.broadcasted_iota(I32, (8, LANES), 1),
                     1.0, 0.0).astype(BF16)
    rows = _dot_nt(pick, digits.astype(BF16))

    h2b_s[...] = h2b
    rows_s[0:1, :] = rows[0:1] * 256.0 + rows[1:2]
    rows_s[1:2, :] = rows[2:3] * 256.0 + rows[3:4]


def _merge(x2d, ya, yb, proj, wa, wb, wo, ffn_g, w_router, b_router):
    t = x2d.shape[0]
    nb = t // TOK_TILE
    const = lambda *shape: pl.BlockSpec(shape, lambda i: (0,) * len(shape))
    cur = lambda i: jnp.minimum(i, nb - 1)
    prev = lambda i: jnp.maximum(i - 1, 0)
    return pl.pallas_call(
        _merge_kernel,
        grid=(nb + 1,),
        in_specs=[
            pl.BlockSpec((TOK_TILE, D_MODEL), lambda i: (cur(i), 0)),
            pl.BlockSpec((TOK_TILE, HG_WIDTH), lambda i: (cur(i), 0)),
            pl.BlockSpec((TOK_TILE, D_MODEL), lambda i: (cur(i), 0)),
            pl.BlockSpec((TOK_TILE, D_MODEL), lambda i: (cur(i), COL_GA)),
            pl.BlockSpec((TOK_TILE, D_MODEL), lambda i: (cur(i), COL_GB)),
            const(HG_WIDTH, D_MODEL),
            const(D_MODEL, D_MODEL),
            const(D_MODEL, D_MODEL),
            const(1, D_MODEL),
            const(D_MODEL, LANES),
            const(1, LANES),
        ],
        out_specs=[
            pl.BlockSpec((TOK_TILE, D_MODEL), lambda i: (cur(i), 0)),
            pl.BlockSpec((LOCAL_ROWS, D_MODEL), lambda i: (prev(i), 0)),
            pl.BlockSpec((TOK_TILE, LANES), lambda i: (cur(i), 0)),
            pl.BlockSpec((1, 8, LANES), lambda i: (cur(i), 0, 0)),
        ],
        out_shape=[
            jax.ShapeDtypeStruct((t, D_MODEL), F32),
            jax.ShapeDtypeStruct((t // TOK_TILE * LOCAL_ROWS, D_MODEL), BF16),
            jax.ShapeDtypeStruct((t, LANES), F32),
            jax.ShapeDtypeStruct((t // TOK_TILE, 8, LANES), F32),
        ],
        scratch_shapes=[pltpu.VMEM((TOK_TILE, D_MODEL), BF16), pltpu.VMEM((8, TOK_TILE), F32)],
        compiler_params=_cparams(("arbitrary",)),
        name="merge_route",
    )(x2d, ya, yb, proj, proj, wa, wb, wo, ffn_g, w_router, b_router)


def _expert_kernel(te_ref, nt_ref, src_ref, xs_ref, wg_ref, wu_ref, wd_ref, ys_ref, xbuf, sem):
    i = pl.program_id(0)
    nt = nt_ref[0]

    def gather(tile, slot):
        copies = []
        for c in range(CHUNKS_PER_TILE):
            row = pl.multiple_of(src_ref[tile * CHUNKS_PER_TILE + c] * ROW_CHUNK, ROW_CHUNK)
            copies.append(pltpu.make_async_copy(
                xs_ref.at[pl.ds(row, ROW_CHUNK)],
                xbuf.at[slot, pl.ds(c * ROW_CHUNK, ROW_CHUNK)], sem.at[slot]))
        return copies

    @pl.when(i == 0)
    def _():
        for k in range(EXP_AHEAD):
            for cp in gather(k, k):
                cp.start()

    @pl.when(i < nt + EXP_AHEAD)
    def _():
        for cp in gather(i, i % EXP_SLOTS):
            cp.wait()

    @pl.when(i < nt)
    def _():
        slot = i % EXP_SLOTS
        wg, wu, wd = wg_ref[0], wu_ref[0], wd_ref[0]
        subs = [pl.ds(r, EXP_SUB) for r in range(0, EXP_TILE, EXP_SUB)]
        xs = [xbuf[slot, s, :] for s in subs]
        gates = [(_dot(x, wg), _dot(x, wu)) for x in xs]
        for c, cp in enumerate(gather(i + EXP_AHEAD, (i + EXP_AHEAD) % EXP_SLOTS)):
            cp.start(priority=c % 2)
        hidden = [(_silu(a) * u).astype(BF16) for a, u in gates]
        for s, h in zip(subs, hidden):
            ys_ref[s, :] = _dot(h, wd).astype(ys_ref.dtype)

    @pl.when(i >= nt)
    def _():
        ys_ref[...] = jnp.zeros_like(ys_ref)


def _experts(tile_expert, n_tiles_used, src_chunk, xs, wg, wu, wd, n_tiles):
    grid_spec = pltpu.PrefetchScalarGridSpec(
        num_scalar_prefetch=3,
        grid=(n_tiles,),
        in_specs=[
            pl.BlockSpec(memory_space=pl.ANY),
            pl.BlockSpec((1, D_MODEL, D_EXPERT), lambda i, te, nt, src: (te[i], 0, 0)),
            pl.BlockSpec((1, D_MODEL, D_EXPERT), lambda i, te, nt, src: (te[i], 0, 0)),
            pl.BlockSpec((1, D_EXPERT, D_MODEL), lambda i, te, nt, src: (te[i], 0, 0)),
        ],
        out_specs=pl.BlockSpec((EXP_TILE, D_MODEL), lambda i, te, nt, src: (i, 0)),
        scratch_shapes=[
            pltpu.VMEM((EXP_SLOTS, EXP_TILE, D_MODEL), BF16),
            pltpu.SemaphoreType.DMA((EXP_SLOTS,)),
        ],
    )
    return pl.pallas_call(
        _expert_kernel,
        grid_spec=grid_spec,
        out_shape=jax.ShapeDtypeStruct((n_tiles * EXP_TILE, D_MODEL), BF16),
        compiler_params=_cparams(("arbitrary",)),
        name="experts",
    )(tile_expert, n_tiles_used, src_chunk, xs, wg, wu, wd)


def _final_kernel(src_ref, x1_ref, route_ref, p_ref, ys_ref, pg_ref, wpg_ref, wpp_ref, fg_ref,
                  o_ref, ybuf, sem):
    tm = x1_ref.shape[0]
    i = pl.program_id(0)
    nb = pl.num_programs(0)

    def gather(block, slot):
        copies = []
        for c in range(LOCAL_CHUNKS):
            row = pl.multiple_of(src_ref[block * LOCAL_CHUNKS + c] * ROW_CHUNK, ROW_CHUNK)
            copies.append(pltpu.make_async_copy(
                ys_ref.at[pl.ds(row, ROW_CHUNK)],
                ybuf.at[slot, pl.ds(c * ROW_CHUNK, ROW_CHUNK)], sem.at[slot]))
        return copies

    @pl.when(i == 0)
    def _():
        for cp in gather(0, 0):
            cp.start()

    slot = i % 2
    nxt = jnp.where(i + 1 < nb, i + 1, 0)
    for cp in gather(i, slot):
        cp.wait()
    for c, cp in enumerate(gather(nxt, 1 - slot)):
        cp.start(priority=c % 2)

    route = route_ref[...]
    w1, w2 = route[:, 2:3], route[:, 3:4]
    pos1, pos2 = route[:, 4:5], route[:, 5:6]
    slab_col = lax.broadcasted_iota(I32, (tm, SORT_SLAB), 1).astype(F32)
    moe = jnp.zeros((tm, D_MODEL), F32)
    for k0 in range(0, LOCAL_ROWS, SORT_SLAB):
        sel = jnp.where(slab_col == pos1 - float(k0), w1,
                        jnp.where(slab_col == pos2 - float(k0), w2, 0.0)).astype(BF16)
        moe = moe + _dot(sel, ybuf[slot, k0:k0 + SORT_SLAB, :])
    x2 = x1_ref[...] + moe
    hp = _rms(x2, pg_ref[...]).astype(BF16)
    gate = _sigmoid(_dot(hp, wpg_ref[...]))
    x3 = x2 + gate * _dot(p_ref[...].astype(BF16), wpp_ref[...])
    o_ref[...] = _rms(x3, fg_ref[...])

    @pl.when(i == nb - 1)
    def _():
        for cp in gather(0, 1 - slot):
            cp.wait()


def _final(src_chunk, x1, route, p2d, ys, ple_g, wpg, wpp, final_g):
    t = x1.shape[0]
    const = lambda *shape: pl.BlockSpec(shape, lambda i, src: (0,) * len(shape))
    grid_spec = pltpu.PrefetchScalarGridSpec(
        num_scalar_prefetch=1,
        grid=(t // TOK_TILE,),
        in_specs=[
            pl.BlockSpec((TOK_TILE, D_MODEL), lambda i, src: (i, 0)),
            pl.BlockSpec((TOK_TILE, LANES), lambda i, src: (i, 0)),
            pl.BlockSpec((TOK_TILE, PLE_DIM), lambda i, src: (i, 0)),
            pl.BlockSpec(memory_space=pl.ANY),
            const(1, D_MODEL),
            const(D_MODEL, D_MODEL),
            const(PLE_DIM, D_MODEL),
            const(1, D_MODEL),
        ],
        out_specs=pl.BlockSpec((TOK_TILE, D_MODEL), lambda i, src: (i, 0)),
        scratch_shapes=[pltpu.VMEM((2, LOCAL_ROWS, D_MODEL), BF16), pltpu.SemaphoreType.DMA((2,))],
    )
    return pl.pallas_call(
        _final_kernel,
        grid_spec=grid_spec,
        out_shape=jax.ShapeDtypeStruct((t, D_MODEL), F32),
        compiler_params=_cparams(("arbitrary",)),
        name="combine_ple_final",
    )(src_chunk, x1, route, p2d, ys, ple_g, wpg, wpp, final_g)


def _rotary_tables(seq):
    inv = ROPE_BASE ** (-jnp.arange(0, RET_DK, 2, dtype=F32) / RET_DK)
    hi = (jnp.arange(seq // ROPE_SPLIT, dtype=F32) * ROPE_SPLIT)[:, None] * inv[None, :]
    lo = jnp.arange(ROPE_SPLIT, dtype=F32)[:, None] * inv[None, :]
    ch, sh, cl, sl = jnp.cos(hi)[:, None, :], jnp.sin(hi)[:, None, :], jnp.cos(lo)[None], jnp.sin(lo)[None]
    cos = (ch * cl - sh * sl).reshape(seq, RET_DK // 2)
    sin = (sh * cl + ch * sl).reshape(seq, RET_DK // 2)
    return jnp.concatenate([cos, cos], axis=1), jnp.concatenate([-sin, sin], axis=1)


def _layer(x2d, p2d, batch, seq, mix_norm, w_in, hg_lb_logits, hg_norm, ret_norm, w_branch_a,
           w_branch_b, w_out, ffn_norm, w_rg, b_rg, w_re, b_re, w_gate, w_up, w_down, ple_norm,
           w_ple_gate, w_ple_proj, out_gain):
    t = x2d.shape[0]
    row = lambda v: v.reshape(1, -1).astype(F32)

    proj, hf, wg16, wu16, wd16 = _inproj(x2d, row(mix_norm), w_in.astype(BF16), w_gate, w_up, w_down)
    ya = _hgrn(proj, hf, hg_lb_logits.astype(F32), row(hg_norm), batch, seq)
    cos2, sin2 = _rotary_tables(seq)
    yb = _retention(proj, cos2, sin2, row(ret_norm), batch, seq)

    n_r = N_GROUPS + N_EXPERTS
    w_router = jnp.pad(jnp.concatenate([w_rg, w_re], axis=1).astype(BF16), ((0, 0), (0, LANES - n_r)))
    b_router = jnp.zeros((1, LANES), F32).at[0, :n_r].set(jnp.concatenate([b_rg, b_re]))
    x1, xs, route, counts = _merge(
        x2d, ya, yb, proj, w_branch_a.astype(BF16), w_branch_b.astype(BF16), w_out.astype(BF16),
        row(ffn_norm), w_router, b_router)

    n_blocks = t // TOK_TILE
    cnt = counts[:, 0, N_GROUPS:N_GROUPS + N_EXPERTS].astype(I32)
    run_chunks = (cnt + ROW_CHUNK - 1) // ROW_CHUNK
    run_local = jnp.cumsum(run_chunks, axis=1) - run_chunks
    seg_chunks = jnp.sum(run_chunks, axis=0)
    tiles_per = (seg_chunks + CHUNKS_PER_TILE - 1) // CHUNKS_PER_TILE
    tile_end = jnp.cumsum(tiles_per)
    seg_start = (tile_end - tiles_per) * CHUNKS_PER_TILE
    run_global = seg_start[None, :] + jnp.cumsum(run_chunks, axis=0) - run_chunks
    max_chunks = (2 * t) // ROW_CHUNK + n_blocks * N_EXPERTS + N_EXPERTS * (CHUNKS_PER_TILE - 1)
    n_tiles = -(-max_chunks // CHUNKS_PER_TILE) + EXP_AHEAD
    tile_ids = jnp.arange(n_tiles, dtype=I32)
    tile_expert = jnp.minimum(jnp.sum((tile_end[None, :] <= tile_ids[:, None]).astype(I32), axis=1),
                              N_EXPERTS - 1)
    n_used = tile_end[-1:].astype(I32)

    block_ids = jnp.arange(n_blocks, dtype=I32)
    zero_local = LOCAL_CHUNKS - 1
    zero_global = n_tiles * CHUNKS_PER_TILE - 1
    g = jnp.arange(n_tiles * CHUNKS_PER_TILE, dtype=I32)[:, None]
    e_g = jnp.repeat(tile_expert, CHUNKS_PER_TILE)
    pick_e = (e_g[:, None] == jnp.arange(N_EXPERTS, dtype=I32)[None, :]).astype(I32)
    rg, rc, rl = (jnp.sum(pick_e[:, :, None] * tab.T[None, :, :], axis=1)
                  for tab in (run_global, run_chunks, run_local))
    inside = (rg <= g) & (g < rg + rc)
    gather_src = jnp.sum(jnp.where(inside, block_ids[None, :] * LOCAL_CHUNKS + rl + (g - rg), 0), axis=1)
    gather_src = jnp.where(jnp.any(inside, axis=1), gather_src, zero_local).astype(I32)

    lc = jnp.arange(LOCAL_CHUNKS, dtype=I32)[None, :, None]
    inside = (run_local[:, None, :] <= lc) & (lc < (run_local + run_chunks)[:, None, :])
    back_src = jnp.sum(jnp.where(inside, run_global[:, None, :] + lc - run_local[:, None, :], 0), axis=2)
    back_src = jnp.where(jnp.any(inside, axis=2), back_src, zero_global).astype(I32).reshape(-1)

    ys = _experts(tile_expert, n_used, gather_src, xs, wg16, wu16, wd16, n_tiles)
    return _final(back_src, x1, route, p2d, ys, row(ple_norm), w_ple_gate.astype(BF16),
                  w_ple_proj.astype(BF16), out_gain)


def kernel(x, p, mix_norm, w_in, hg_lb_logits, hg_norm, ret_norm, w_branch_a, w_branch_b, w_out,
           ffn_norm, w_router_group, b_router_group, w_router_expert, b_router_expert,
           w_expert_gate, w_expert_up, w_expert_down, ple_norm, w_ple_gate, w_ple_proj, final_norm):
    batch, seq, d = x.shape
    depth = p.shape[0]
    assert depth == 1, "the final rmsnorm is fused into the single layer"
    x2d = x.reshape(batch * seq, d)
    out = _layer(x2d, p[0].reshape(batch * seq, -1), batch, seq, mix_norm[0], w_in[0], hg_lb_logits,
                 hg_norm[0], ret_norm[0], w_branch_a[0], w_branch_b[0], w_out[0], ffn_norm[0],
                 w_router_group[0], b_router_group[0], w_router_expert[0], b_router_expert[0],
                 w_expert_gate[0], w_expert_up[0], w_expert_down[0], ple_norm[0], w_ple_gate[0],
                 w_ple_proj[0], final_norm.reshape(1, -1).astype(F32))
    return out.reshape(batch, seq, d)
```

```python
import functools

import jax
import jax.numpy as jnp
from jax import lax
from jax.experimental import pallas as pl
from jax.experimental.pallas import tpu as pltpu

F32 = jnp.float32
BF16 = jnp.bfloat16
I32 = jnp.int32
U32 = jnp.uint32

EPS = 1e-6
D_MODEL = 1024
PLE_DIM = 256
HG_HEADS = 4
HG_DK = 128
HG_WIDTH = HG_HEADS * HG_DK
RET_HEADS = 4
RET_DK = 128
RET_DV = 256
ROPE_BASE = 10000.0
ROPE_SPLIT = 64
IN_TOTAL = 7168
N_GROUPS = 4
EXPERTS_PER_GROUP = 8
N_EXPERTS = 32
D_EXPERT = 256

COL_HQ, COL_HF, COL_HI, COL_HG = 0, 4, 8, 12
COL_RQ, COL_RK = 16, 20
COL_RV, COL_RG = 24, 32
COL_GA, COL_GB = 5, 6

LANES = 128
VMEM_LIMIT = 56 * 1024 * 1024

HG_CHUNK = 64
HG_SUB = 8
HG_UNROLL = 4
HG_STATE_UNROLL = 8
HG_NORM_ROWS = 256
RET_CHUNK = 128
RET_UNROLL = 4
SEQ_TILE = 1024
TOK_TILE = 512
EXP_TILE = 512
EXP_SUB = 512
EXP_AHEAD = 2
EXP_SLOTS = EXP_AHEAD + 1
ROW_CHUNK = 16
CHUNKS_PER_TILE = EXP_TILE // ROW_CHUNK
LOCAL_ROWS = 2 * TOK_TILE + N_EXPERTS * ROW_CHUNK
LOCAL_CHUNKS = LOCAL_ROWS // ROW_CHUNK
SORT_SLAB = 256
FIN_AHEAD = 2
FIN_SLOTS = FIN_AHEAD + 1


def _cparams(sem):
    return pltpu.CompilerParams(dimension_semantics=sem, vmem_limit_bytes=VMEM_LIMIT)


def _rms(x, g):
    return x * lax.rsqrt(jnp.mean(x * x, axis=-1, keepdims=True) + EPS) * g


def _sigmoid(x):
    return 1.0 / (1.0 + jnp.exp(-x))


def _silu(x):
    return x * _sigmoid(x)


def _split3(x):
    hi = x.astype(BF16)
    r1 = x - hi.astype(F32)
    mid = r1.astype(BF16)
    lo = (r1 - mid.astype(F32)).astype(BF16)
    return hi, mid, lo


def _dot(a, b):
    return jnp.dot(a, b, preferred_element_type=F32)


def _dot_nt(a, b):
    return lax.dot_general(a, b, (((1,), (1,)), ((), ())), preferred_element_type=F32)


def _dot_tn(a, b):
    return lax.dot_general(a, b, (((0,), (0,)), ((), ())), preferred_element_type=F32)


def _inproj_kernel(x_ref, g_ref, w_ref, wg_ref, wu_ref, wd_ref, proj_ref, hf_ref,
                   wg16_ref, wu16_ref, wd16_ref):
    wg16_ref[...] = wg_ref[...].astype(BF16)
    wu16_ref[...] = wu_ref[...].astype(BF16)
    wd16_ref[...] = wd_ref[...].astype(BF16)

    h = _rms(x_ref[...], g_ref[...]).astype(BF16)
    tn = HG_WIDTH
    for j in range(IN_TOTAL // tn):
        acc = _dot(h, w_ref[:, j * tn:(j + 1) * tn])
        proj_ref[:, j * tn:(j + 1) * tn] = acc.astype(BF16)
        if j * tn == COL_HF * LANES:
            hf_ref[...] = acc


def _inproj(x2d, gain, w_bf16, w_gate, w_up, w_down):
    t = x2d.shape[0]
    steps = t // TOK_TILE
    assert N_EXPERTS % steps == 0, "expert weights are converted in equal shares per grid step"
    share = N_EXPERTS // steps
    up_spec = pl.BlockSpec((share, D_MODEL, D_EXPERT), lambda i: (i, 0, 0))
    down_spec = pl.BlockSpec((share, D_EXPERT, D_MODEL), lambda i: (i, 0, 0))
    return pl.pallas_call(
        _inproj_kernel,
        grid=(steps,),
        in_specs=[
            pl.BlockSpec((TOK_TILE, D_MODEL), lambda i: (i, 0)),
            pl.BlockSpec((1, D_MODEL), lambda i: (0, 0)),
            pl.BlockSpec((D_MODEL, IN_TOTAL), lambda i: (0, 0), pipeline_mode=pl.Buffered(1)),
            up_spec, up_spec, down_spec,
        ],
        out_specs=[
            pl.BlockSpec((TOK_TILE, IN_TOTAL), lambda i: (i, 0)),
            pl.BlockSpec((TOK_TILE, HG_WIDTH), lambda i: (i, 0)),
            up_spec, up_spec, down_spec,
        ],
        out_shape=[
            jax.ShapeDtypeStruct((t, IN_TOTAL), BF16),
            jax.ShapeDtypeStruct((t, HG_WIDTH), F32),
            jax.ShapeDtypeStruct(w_gate.shape, BF16),
            jax.ShapeDtypeStruct(w_up.shape, BF16),
            jax.ShapeDtypeStruct(w_down.shape, BF16),
        ],
        compiler_params=_cparams(("arbitrary",)),
        name="inproj",
    )(x2d, gain, w_bf16, w_gate, w_up, w_down)


def _hgrn_kernel(lbl_ref, q_ref, f_ref, i_ref, g_ref, ng_ref, o_ref,
                 st_ref, b_s, k_s, v_s, oi_s, qe_s, kv_s, dec_s):
    c = HG_CHUNK
    nsub = c // HG_SUB
    n_chunks = q_ref.shape[0] // c

    @pl.when(pl.program_id(2) == 0)
    def _():
        st_ref[...] = jnp.zeros_like(st_ref)

    logits = lbl_ref[...]
    e = jnp.exp(logits - jnp.max(logits, axis=0, keepdims=True))
    lb = e[0:1] / jnp.sum(e, axis=0, keepdims=True)
    one_m_lb = jnp.sum(e[1:], axis=0, keepdims=True) / jnp.sum(e, axis=0, keepdims=True)
    ng = ng_ref[...]

    row = lax.broadcasted_iota(I32, (c, c), 0)
    col = lax.broadcasted_iota(I32, (c, c), 1)
    tri = jnp.where(row >= col, 1.0, 0.0).astype(BF16)
    row_k = lax.broadcasted_iota(I32, (c, HG_DK), 0)
    sub_row = lax.broadcasted_iota(I32, (HG_SUB, HG_DK), 0)
    masked = jnp.float32(-1e30)

    def bcast_row(ref, r, rows):
        return jnp.broadcast_to(ref[pl.ds(r, 1), :], (rows, HG_DK))

    def prep(ci, slot):
        r0 = pl.multiple_of(ci * c, c)
        z = f_ref[pl.ds(r0, c), :]
        ez = jnp.exp(-jnp.abs(z))
        rz = 1.0 / (1.0 + ez)
        pos = z >= 0.0
        logf = jnp.log2(lb + one_m_lb * jnp.where(pos, rz, ez * rz))
        kk = one_m_lb * jnp.where(pos, ez * rz, rz)
        q = _silu(q_ref[pl.ds(r0, c), :].astype(F32))
        v = i_ref[pl.ds(r0, c), :].astype(F32)
        k_s[slot] = kk
        v_s[slot] = v
        return dict(ci=ci, r0=r0, slot=slot, kk=kk, q=q, v16=v.astype(BF16), parts=_split3(logf))

    def cumulate(s):
        hi, mid, lo = s.pop("parts")
        b = (_dot(tri, lo) + _dot(tri, mid)) + _dot(tri, hi)
        b_s[s["slot"]] = b
        qe_s[pl.ds(s["r0"], c), :] = (s["q"] * jnp.exp2(b)).astype(BF16)
        s["b"] = b
        return s

    def off_diagonal(s):
        b, q, kk = s["b"], s["q"], s["kk"]
        bs_ref = b_s.at[s["slot"]]
        a_rows = [jnp.zeros((HG_SUB, c), F32)]
        for i in range(1, nsub):
            bi = bcast_row(bs_ref, i * HG_SUB - 1, c)
            qi = q[i * HG_SUB:(i + 1) * HG_SUB, :] * jnp.exp2(b[i * HG_SUB:(i + 1) * HG_SUB, :] - bi[:HG_SUB])
            ki = kk * jnp.exp2(jnp.where(row_k < i * HG_SUB, bi - b, masked))
            a_rows.append(_dot_nt(qi.astype(BF16), ki.astype(BF16)))
        s["a_off"] = jnp.concatenate(a_rows, axis=0)
        return s

    def apply_values(s):
        b, kk = s["b"], s["kk"]
        blast = b[c - 1:c, :]
        s["o"] = _dot(s.pop("a_off").astype(BF16), s["v16"])
        kd = kk * jnp.exp2(blast - b)
        kv_s[s["ci"]] = _dot_tn(s["v16"], kd.astype(BF16))
        dec_s[s["ci"]] = jnp.broadcast_to(jnp.exp2(blast), (HG_SUB, HG_DK))
        return s

    def diagonal(s):
        b, q = s["b"], s["q"]
        bs_ref, ks_ref, vs_ref = b_s.at[s["slot"]], k_s.at[s["slot"]], v_s.at[s["slot"]]
        d_blocks = []
        for i in range(nsub):
            sl = slice(i * HG_SUB, (i + 1) * HG_SUB)
            bt, qt = b[sl, :], q[sl, :]
            acc = jnp.zeros((HG_SUB, HG_DK), F32)
            for j in range(HG_SUB):
                r = i * HG_SUB + j
                arg = jnp.where(sub_row >= j, bt - bcast_row(bs_ref, r, HG_SUB), masked)
                g = jnp.exp2(arg) * (qt * bcast_row(ks_ref, r, HG_SUB))
                acc = acc + jnp.sum(g, axis=-1, keepdims=True) * bcast_row(vs_ref, r, HG_SUB)
            d_blocks.append(acc)
        oi_s[pl.ds(s["r0"], c), :] = s["o"] + jnp.concatenate(d_blocks, axis=0)

    def local_group(gi, carry):
        states = [prep(gi * HG_UNROLL + slot, slot) for slot in range(HG_UNROLL)]
        for stage in (cumulate, off_diagonal, apply_values, diagonal):
            states = [stage(s) for s in states]
        return carry

    lax.fori_loop(0, n_chunks // HG_UNROLL, local_group, 0)

    def carried_group(gi, carry):
        st = st_ref[...]
        outs = []
        for j in range(HG_STATE_UNROLL):
            ci = gi * HG_STATE_UNROLL + j
            r0 = pl.multiple_of(ci * c, c)
            outs.append((r0, _dot_nt(qe_s[pl.ds(r0, c), :], st.astype(BF16))))
            st = st * dec_s[ci][0:1, :] + kv_s[ci]
        st_ref[...] = st
        for r0, os in outs:
            oi_s[pl.ds(r0, c), :] += os
        return carry

    lax.fori_loop(0, n_chunks // HG_STATE_UNROLL, carried_group, 0)

    def finish(ri, carry):
        r0 = pl.multiple_of(ri * HG_NORM_ROWS, HG_NORM_ROWS)
        y = _rms(oi_s[pl.ds(r0, HG_NORM_ROWS), :], ng) * _silu(g_ref[pl.ds(r0, HG_NORM_ROWS), :].astype(F32))
        o_ref[pl.ds(r0, HG_NORM_ROWS), :] = y.astype(o_ref.dtype)
        return carry

    lax.fori_loop(0, q_ref.shape[0] // HG_NORM_ROWS, finish, 0)


def _hgrn(proj, hf, lb_logits, norm_g, batch, seq):
    ns = seq // SEQ_TILE
    tok = lambda b, h, s: b * ns + s
    return pl.pallas_call(
        _hgrn_kernel,
        grid=(batch, HG_HEADS, ns),
        in_specs=[
            pl.BlockSpec((2, HG_DK), lambda b, h, s: (0, h)),
            pl.BlockSpec((SEQ_TILE, HG_DK), lambda b, h, s: (tok(b, h, s), COL_HQ + h)),
            pl.BlockSpec((SEQ_TILE, HG_DK), lambda b, h, s: (tok(b, h, s), h)),
            pl.BlockSpec((SEQ_TILE, HG_DK), lambda b, h, s: (tok(b, h, s), COL_HI + h)),
            pl.BlockSpec((SEQ_TILE, HG_DK), lambda b, h, s: (tok(b, h, s), COL_HG + h)),
            pl.BlockSpec((1, HG_DK), lambda b, h, s: (0, 0)),
        ],
        out_specs=pl.BlockSpec((SEQ_TILE, HG_DK), lambda b, h, s: (tok(b, h, s), h)),
        out_shape=jax.ShapeDtypeStruct((batch * seq, HG_WIDTH), BF16),
        scratch_shapes=[
            pltpu.VMEM((HG_DK, HG_DK), F32),
            pltpu.VMEM((HG_UNROLL, HG_CHUNK, HG_DK), F32),
            pltpu.VMEM((HG_UNROLL, HG_CHUNK, HG_DK), F32),
            pltpu.VMEM((HG_UNROLL, HG_CHUNK, HG_DK), F32),
            pltpu.VMEM((SEQ_TILE, HG_DK), F32),
            pltpu.VMEM((SEQ_TILE, HG_DK), BF16),
            pltpu.VMEM((SEQ_TILE // HG_CHUNK, HG_DK, HG_DK), F32),
            pltpu.VMEM((SEQ_TILE // HG_CHUNK, HG_SUB, HG_DK), F32),
        ],
        compiler_params=_cparams(("arbitrary", "arbitrary", "arbitrary")),
        name="hgrn2",
    )(lb_logits, proj, hf, proj, proj, norm_g)


def _ret_kernel(q_ref, k_ref, v_ref, g_ref, cos_ref, sin_ref, ng_ref, o_ref,
                r_ref, oi_s, q16_s, kv_s):
    c = RET_CHUNK

    @pl.when(pl.program_id(2) == 0)
    def _():
        r_ref[...] = jnp.zeros_like(r_ref)

    hf = jnp.full((1, 1), pl.program_id(1), I32).astype(F32)
    lg = jnp.log1p(-jnp.exp2(-5.0 - hf))
    ti = lax.broadcasted_iota(I32, (c, c), 0)
    si = lax.broadcasted_iota(I32, (c, c), 1)
    rel = (ti - si).astype(F32)
    intra = jnp.where(ti >= si, jnp.exp(jnp.maximum(rel, 0.0) * lg), 0.0)
    idx = lax.broadcasted_iota(I32, (c, 1), 0).astype(F32)
    inter = jnp.exp((idx + 1.0) * lg)
    to_state = jnp.exp((c - 1.0 - idx) * lg)
    chunk_decay = jnp.exp(float(c) * lg)
    ng = ng_ref[...]
    half = RET_DK // 2

    n_chunks = q_ref.shape[0] // c

    def prep(ci):
        r0 = pl.multiple_of(ci * c, c)
        cos = cos_ref[pl.ds(r0, c), :]
        sin = sin_ref[pl.ds(r0, c), :]
        q = q_ref[pl.ds(r0, c), :].astype(F32)
        k = k_ref[pl.ds(r0, c), :].astype(F32)
        qr = (q * cos + pltpu.roll(q, half, 1) * sin) * (RET_DK ** -0.5)
        kr = k * cos + pltpu.roll(k, half, 1) * sin
        q16 = qr.astype(BF16)
        q16_s[pl.ds(r0, c), :] = q16
        return dict(ci=ci, r0=r0, q16=q16, k16=kr.astype(BF16), kts=(kr * to_state).astype(BF16),
                    v16=v_ref[pl.ds(r0, c), :])

    def scores(s):
        s["att"] = (_dot_nt(s.pop("q16"), s.pop("k16")) * intra).astype(BF16)
        return s

    def apply_values(s):
        oi_s[pl.ds(s["r0"], c), :] = _dot(s["att"], s["v16"])
        kv_s[s["ci"]] = _dot_tn(s["kts"], s["v16"])
        return s

    def local_group(gi, carry):
        states = [prep(gi * RET_UNROLL + j) for j in range(RET_UNROLL)]
        for stage in (scores, apply_values):
            states = [stage(s) for s in states]
        return carry

    lax.fori_loop(0, n_chunks // RET_UNROLL, local_group, 0)

    def carried_group(gi, carry):
        r = r_ref[...]
        outs = []
        for j in range(RET_UNROLL):
            ci = gi * RET_UNROLL + j
            r0 = pl.multiple_of(ci * c, c)
            outs.append((r0, _dot(q16_s[pl.ds(r0, c), :], r.astype(BF16))))
            r = chunk_decay * r + kv_s[ci]
        r_ref[...] = r
        for r0, qr_state in outs:
            o = oi_s[pl.ds(r0, c), :] + qr_state * inter
            y = _rms(o, ng) * _silu(g_ref[pl.ds(r0, c), :].astype(F32))
            o_ref[pl.ds(r0, c), :] = y.astype(o_ref.dtype)
        return carry

    lax.fori_loop(0, n_chunks // RET_UNROLL, carried_group, 0)


def _retention(proj, cos2, sin2, norm_g, batch, seq):
    ns = seq // SEQ_TILE
    tok = lambda b, h, s: b * ns + s
    return pl.pallas_call(
        _ret_kernel,
        grid=(batch, RET_HEADS, ns),
        in_specs=[
            pl.BlockSpec((SEQ_TILE, RET_DK), lambda b, h, s: (tok(b, h, s), COL_RQ + h)),
            pl.BlockSpec((SEQ_TILE, RET_DK), lambda b, h, s: (tok(b, h, s), COL_RK + h)),
            pl.BlockSpec((SEQ_TILE, RET_DV), lambda b, h, s: (tok(b, h, s), COL_RV // 2 + h)),
            pl.BlockSpec((SEQ_TILE, RET_DV), lambda b, h, s: (tok(b, h, s), COL_RG // 2 + h)),
            pl.BlockSpec((SEQ_TILE, RET_DK), lambda b, h, s: (s, 0)),
            pl.BlockSpec((SEQ_TILE, RET_DK), lambda b, h, s: (s, 0)),
            pl.BlockSpec((1, RET_DV), lambda b, h, s: (0, 0)),
        ],
        out_specs=pl.BlockSpec((SEQ_TILE, RET_DV), lambda b, h, s: (tok(b, h, s), h)),
        out_shape=jax.ShapeDtypeStruct((batch * seq, RET_HEADS * RET_DV), BF16),
        scratch_shapes=[
            pltpu.VMEM((RET_DK, RET_DV), F32),
            pltpu.VMEM((SEQ_TILE, RET_DV), F32),
            pltpu.VMEM((SEQ_TILE, RET_DK), BF16),
            pltpu.VMEM((SEQ_TILE // RET_CHUNK, RET_DK, RET_DV), F32),
        ],
        compiler_params=_cparams(("arbitrary", "arbitrary", "arbitrary")),
        name="retention",
    )(proj, proj, proj, proj, cos2, sin2, norm_g)


def _merge_kernel(x_ref, ya_ref, yb_ref, ga_ref, gb_ref, wa_ref, wb_ref, wo_ref, fg_ref,
                  wr_ref, br_ref, x1_ref, xs_ref, route_ref, cnt_ref, h2b_s, rows_s):
    tm = x_ref.shape[0]

    @pl.when(pl.program_id(0) == 0)
    def _():
        h2b_s[...] = jnp.zeros_like(h2b_s)
        rows_s[...] = jnp.full(rows_s.shape, -1.0, F32)

    def sort_previous(lo, hi):
        slab_row = lax.broadcasted_iota(I32, (SORT_SLAB, tm), 0).astype(F32)
        for r0 in range(lo, hi, SORT_SLAB):
            sel = ((slab_row == rows_s[0:1, :] - float(r0)) | (slab_row == rows_s[1:2, :] - float(r0)))
            xs_ref[r0:r0 + SORT_SLAB, :] = _dot(jnp.where(sel, 1.0, 0.0).astype(BF16),
                                                h2b_s[...]).astype(BF16)

    merged = (_sigmoid(ga_ref[...].astype(F32)) * _dot(ya_ref[...], wa_ref[...])
              + _sigmoid(gb_ref[...].astype(F32)) * _dot(yb_ref[...], wb_ref[...]))
    x1 = x_ref[...] + _dot(merged.astype(BF16), wo_ref[...])
    x1_ref[...] = x1
    h2 = _rms(x1, fg_ref[...])

    h2b = h2.astype(BF16)
    logits = _dot(h2b, wr_ref[...]) + br_ref[...]
    sort_previous(0, LOCAL_ROWS)

    lane = lax.broadcasted_iota(I32, (tm, LANES), 1)
    neg = jnp.float32(-jnp.inf)
    big = jnp.int32(1 << 30)
    is_g = lane < N_GROUPS
    gl = jnp.where(is_g, logits, neg)
    gmax = jnp.max(gl, axis=-1, keepdims=True)
    g_idx = jnp.min(jnp.where(gl == gmax, lane, big), axis=-1, keepdims=True)
    g_w = 1.0 / jnp.sum(jnp.where(is_g, jnp.exp(gl - gmax), 0.0), axis=-1, keepdims=True)

    ex = lane - N_GROUPS
    in_grp = (ex >= g_idx * EXPERTS_PER_GROUP) & (ex < (g_idx + 1) * EXPERTS_PER_GROUP)
    el = jnp.where(in_grp, logits, neg)
    m1 = jnp.max(el, axis=-1, keepdims=True)
    e1 = jnp.min(jnp.where(el == m1, ex, big), axis=-1, keepdims=True)
    el2 = jnp.where(ex == e1, neg, el)
    m2 = jnp.max(el2, axis=-1, keepdims=True)
    e2 = jnp.min(jnp.where(el2 == m2, ex, big), axis=-1, keepdims=True)
    p2 = jnp.exp(m2 - m1)
    w1 = g_w / (1.0 + p2)
    w2 = g_w * p2 / (1.0 + p2)

    oh1 = ex == e1
    oh2 = ex == e2
    oh = jnp.where(oh1 | oh2, 1.0, 0.0)
    ri = lax.broadcasted_iota(I32, (tm, tm), 0)
    ci = lax.broadcasted_iota(I32, (tm, tm), 1)
    strict = jnp.where(ri > ci, 1.0, 0.0).astype(BF16)
    local_rank = _dot(strict, oh.astype(BF16))
    cnt = jnp.sum(oh, axis=0, keepdims=True)
    run_chunks = jnp.floor((cnt + (ROW_CHUNK - 1.0)) * (1.0 / ROW_CHUNK))
    ui = lax.broadcasted_iota(I32, (LANES, LANES), 0)
    uj = lax.broadcasted_iota(I32, (LANES, LANES), 1)
    before = jnp.where(ui < uj, 1.0, 0.0).astype(BF16)
    run_start = _dot(jnp.broadcast_to(run_chunks, (8, LANES)).astype(BF16), before)[0:1] * ROW_CHUNK
    slot = run_start + local_rank
    pos1 = jnp.sum(jnp.where(oh1, slot, 0.0), axis=-1, keepdims=True)
    pos2 = jnp.sum(jnp.where(oh2, slot, 0.0), axis=-1, keepdims=True)
    cnt_ref[0] = jnp.broadcast_to(cnt, (8, LANES))

    route = jnp.where(lane == 0, e1.astype(F32), 0.0)
    route = jnp.where(lane == 1, e2.astype(F32), route)
    route = jnp.where(lane == 2, w1, route)
    route = jnp.where(lane == 3, w2, route)
    route = jnp.where(lane == 4, pos1, route)
    route = jnp.where(lane == 5, pos2, route)
    route_ref[...] = route

    hi1 = jnp.floor(pos1 * (1.0 / 256.0))
    hi2 = jnp.floor(pos2 * (1.0 / 256.0))
    digits = jnp.where(lane == 0, hi1, 0.0)
    digits = jnp.where(lane == 1, pos1 - 256.0 * hi1, digits)
    digits = jnp.where(lane == 2, hi2, digits)
    digits = jnp.where(lane == 3, pos2 - 256.0 * hi2, digits)
    pick = jnp.where(lax.broadcasted_iota(I32, (8, LANES), 0) == lax.broadcasted_iota(I32, (8, LANES), 1),
                     1.0, 0.0).astype(BF16)
    rows = _dot_nt(pick, digits.astype(BF16))

    h2b_s[...] = h2b
    rows_s[0:1, :] = rows[0:1] * 256.0 + rows[1:2]
    rows_s[1:2, :] = rows[2:3] * 256.0 + rows[3:4]


def _merge(x2d, ya, yb, proj, wa, wb, wo, ffn_g, w_router, b_router):
    t = x2d.shape[0]
    nb = t // TOK_TILE
    const = lambda *shape: pl.BlockSpec(shape, lambda i: (0,) * len(shape))
    cur = lambda i: jnp.minimum(i, nb - 1)
    prev = lambda i: jnp.maximum(i - 1, 0)
    return pl.pallas_call(
        _merge_kernel,
        grid=(nb + 1,),
        in_specs=[
            pl.BlockSpec((TOK_TILE, D_MODEL), lambda i: (cur(i), 0)),
            pl.BlockSpec((TOK_TILE, HG_WIDTH), lambda i: (cur(i), 0)),
            pl.BlockSpec((TOK_TILE, D_MODEL), lambda i: (cur(i), 0)),
            pl.BlockSpec((TOK_TILE, D_MODEL), lambda i: (cur(i), COL_GA)),
            pl.BlockSpec((TOK_TILE, D_MODEL), lambda i: (cur(i), COL_GB)),
            const(HG_WIDTH, D_MODEL),
            const(D_MODEL, D_MODEL),
            const(D_MODEL, D_MODEL),
            const(1, D_MODEL),
            const(D_MODEL, LANES),
            const(1, LANES),
        ],
        out_specs=[
            pl.BlockSpec((TOK_TILE, D_MODEL), lambda i: (cur(i), 0)),
            pl.BlockSpec((LOCAL_ROWS, D_MODEL), lambda i: (prev(i), 0)),
            pl.BlockSpec((TOK_TILE, LANES), lambda i: (cur(i), 0)),
            pl.BlockSpec((1, 8, LANES), lambda i: (cur(i), 0, 0)),
        ],
        out_shape=[
            jax.ShapeDtypeStruct((t, D_MODEL), F32),
            jax.ShapeDtypeStruct((t // TOK_TILE * LOCAL_ROWS, D_MODEL), BF16),
            jax.ShapeDtypeStruct((t, LANES), F32),
            jax.ShapeDtypeStruct((t // TOK_TILE, 8, LANES), F32),
        ],
        scratch_shapes=[pltpu.VMEM((TOK_TILE, D_MODEL), BF16), pltpu.VMEM((8, TOK_TILE), F32)],
        compiler_params=_cparams(("arbitrary",)),
        name="merge_route",
    )(x2d, ya, yb, proj, proj, wa, wb, wo, ffn_g, w_router, b_router)


def _expert_kernel(te_ref, nt_ref, src_ref, xs_ref, wg_ref, wu_ref, wd_ref, ys_ref, xbuf, sem):
    i = pl.program_id(0)
    nt = nt_ref[0]

    def gather(tile, slot):
        copies = []
        for c in range(CHUNKS_PER_TILE):
            row = pl.multiple_of(src_ref[tile * CHUNKS_PER_TILE + c] * ROW_CHUNK, ROW_CHUNK)
            copies.append(pltpu.make_async_copy(
                xs_ref.at[pl.ds(row, ROW_CHUNK)],
                xbuf.at[slot, pl.ds(c * ROW_CHUNK, ROW_CHUNK)], sem.at[slot]))
        return copies

    @pl.when(i == 0)
    def _():
        for k in range(EXP_AHEAD):
            for cp in gather(k, k):
                cp.start()

    @pl.when(i < nt + EXP_AHEAD)
    def _():
        for cp in gather(i, i % EXP_SLOTS):
            cp.wait()

    @pl.when(i < nt)
    def _():
        slot = i % EXP_SLOTS
        wg, wu, wd = wg_ref[0], wu_ref[0], wd_ref[0]
        subs = [pl.ds(r, EXP_SUB) for r in range(0, EXP_TILE, EXP_SUB)]
        xs = [xbuf[slot, s, :] for s in subs]
        gates = [(_dot(x, wg), _dot(x, wu)) for x in xs]
        for c, cp in enumerate(gather(i + EXP_AHEAD, (i + EXP_AHEAD) % EXP_SLOTS)):
            cp.start(priority=c % 2)
        hidden = [(_silu(a) * u).astype(BF16) for a, u in gates]
        for s, h in zip(subs, hidden):
            ys_ref[s, :] = _dot(h, wd).astype(ys_ref.dtype)

    @pl.when(i >= nt)
    def _():
        ys_ref[...] = jnp.zeros_like(ys_ref)


def _experts(tile_expert, n_tiles_used, src_chunk, xs, wg, wu, wd, n_tiles):
    grid_spec = pltpu.PrefetchScalarGridSpec(
        num_scalar_prefetch=3,
        grid=(n_tiles,),
        in_specs=[
            pl.BlockSpec(memory_space=pl.ANY),
            pl.BlockSpec((1, D_MODEL, D_EXPERT), lambda i, te, nt, src: (te[i], 0, 0)),
            pl.BlockSpec((1, D_MODEL, D_EXPERT), lambda i, te, nt, src: (te[i], 0, 0)),
            pl.BlockSpec((1, D_EXPERT, D_MODEL), lambda i, te, nt, src: (te[i], 0, 0)),
        ],
        out_specs=pl.BlockSpec((EXP_TILE, D_MODEL), lambda i, te, nt, src: (i, 0)),
        scratch_shapes=[
            pltpu.VMEM((EXP_SLOTS, EXP_TILE, D_MODEL), BF16),
            pltpu.SemaphoreType.DMA((EXP_SLOTS,)),
        ],
    )
    return pl.pallas_call(
        _expert_kernel,
        grid_spec=grid_spec,
        out_shape=jax.ShapeDtypeStruct((n_tiles * EXP_TILE, D_MODEL), BF16),
        compiler_params=_cparams(("arbitrary",)),
        name="experts",
    )(tile_expert, n_tiles_used, src_chunk, xs, wg, wu, wd)


def _final_kernel(src_ref, x1_ref, route_ref, p_ref, ys_ref, pg_ref, wpg_ref, wpp_ref, fg_ref,
                  o_ref, ybuf, sem):
    tm = x1_ref.shape[0]
    i = pl.program_id(0)
    nb = pl.num_programs(0)

    def gather(block, slot):
        copies = []
        for c in range(LOCAL_CHUNKS):
            row = pl.multiple_of(src_ref[block * LOCAL_CHUNKS + c] * ROW_CHUNK, ROW_CHUNK)
            copies.append(pltpu.make_async_copy(
                ys_ref.at[pl.ds(row, ROW_CHUNK)],
                ybuf.at[slot, pl.ds(c * ROW_CHUNK, ROW_CHUNK)], sem.at[slot]))
        return copies

    @pl.when(i == 0)
    def _():
        for k in range(FIN_AHEAD):
            for cp in gather(k % nb, k):
                cp.start()

    slot = i % FIN_SLOTS
    for cp in gather(i, slot):
        cp.wait()

    ple = _dot(p_ref[...].astype(BF16), wpp_ref[...])

    route = route_ref[...]
    w1, w2 = route[:, 2:3], route[:, 3:4]
    pos1, pos2 = route[:, 4:5], route[:, 5:6]
    slab_col = lax.broadcasted_iota(I32, (tm, SORT_SLAB), 1).astype(F32)
    moe = jnp.zeros((tm, D_MODEL), F32)
    for k0 in range(0, LOCAL_ROWS, SORT_SLAB):
        sel = jnp.where(slab_col == pos1 - float(k0), w1,
                        jnp.where(slab_col == pos2 - float(k0), w2, 0.0)).astype(BF16)
        moe = moe + _dot(sel, ybuf[slot, k0:k0 + SORT_SLAB, :])
    for c, cp in enumerate(gather((i + FIN_AHEAD) % nb, (i + FIN_AHEAD) % FIN_SLOTS)):
        cp.start(priority=c % 2)
    x2 = x1_ref[...] + moe
    hp = _rms(x2, pg_ref[...]).astype(BF16)
    gate = _sigmoid(_dot(hp, wpg_ref[...]))
    x3 = x2 + gate * ple
    o_ref[...] = _rms(x3, fg_ref[...])

    @pl.when(i == nb - 1)
    def _():
        for k in range(1, FIN_AHEAD + 1):
            for cp in gather(0, (i + k) % FIN_SLOTS):
                cp.wait()


def _final(src_chunk, x1, route, p2d, ys, ple_g, wpg, wpp, final_g):
    t = x1.shape[0]
    const = lambda *shape: pl.BlockSpec(shape, lambda i, src: (0,) * len(shape))
    grid_spec = pltpu.PrefetchScalarGridSpec(
        num_scalar_prefetch=1,
        grid=(t // TOK_TILE,),
        in_specs=[
            pl.BlockSpec((TOK_TILE, D_MODEL), lambda i, src: (i, 0)),
            pl.BlockSpec((TOK_TILE, LANES), lambda i, src: (i, 0)),
            pl.BlockSpec((TOK_TILE, PLE_DIM), lambda i, src: (i, 0)),
            pl.BlockSpec(memory_space=pl.ANY),
            const(1, D_MODEL),
            const(D_MODEL, D_MODEL),
            const(PLE_DIM, D_MODEL),
            const(1, D_MODEL),
        ],
        out_specs=pl.BlockSpec((TOK_TILE, D_MODEL), lambda i, src: (i, 0)),
        scratch_shapes=[pltpu.VMEM((FIN_SLOTS, LOCAL_ROWS, D_MODEL), BF16),
                        pltpu.SemaphoreType.DMA((FIN_SLOTS,))],
    )
    return pl.pallas_call(
        _final_kernel,
        grid_spec=grid_spec,
        out_shape=jax.ShapeDtypeStruct((t, D_MODEL), F32),
        compiler_params=_cparams(("arbitrary",)),
        name="combine_ple_final",
    )(src_chunk, x1, route, p2d, ys, ple_g, wpg, wpp, final_g)


def _rotary_tables(seq):
    inv = ROPE_BASE ** (-jnp.arange(0, RET_DK, 2, dtype=F32) / RET_DK)
    hi = (jnp.arange(seq // ROPE_SPLIT, dtype=F32) * ROPE_SPLIT)[:, None] * inv[None, :]
    lo = jnp.arange(ROPE_SPLIT, dtype=F32)[:, None] * inv[None, :]
    ch, sh, cl, sl = jnp.cos(hi)[:, None, :], jnp.sin(hi)[:, None, :], jnp.cos(lo)[None], jnp.sin(lo)[None]
    cos = (ch * cl - sh * sl).reshape(seq, RET_DK // 2)
    sin = (sh * cl + ch * sl).reshape(seq, RET_DK // 2)
    return jnp.concatenate([cos, cos], axis=1), jnp.concatenate([-sin, sin], axis=1)


def _layer(x2d, p2d, batch, seq, mix_norm, w_in, hg_lb_logits, hg_norm, ret_norm, w_branch_a,
           w_branch_b, w_out, ffn_norm, w_rg, b_rg, w_re, b_re, w_gate, w_up, w_down, ple_norm,
           w_ple_gate, w_ple_proj, out_gain):
    t = x2d.shape[0]
    row = lambda v: v.reshape(1, -1).astype(F32)

    proj, hf, wg16, wu16, wd16 = _inproj(x2d, row(mix_norm), w_in.astype(BF16), w_gate, w_up, w_down)
    ya = _hgrn(proj, hf, hg_lb_logits.astype(F32), row(hg_norm), batch, seq)
    cos2, sin2 = _rotary_tables(seq)
    yb = _retention(proj, cos2, sin2, row(ret_norm), batch, seq)

    n_r = N_GROUPS + N_EXPERTS
    w_router = jnp.pad(jnp.concatenate([w_rg, w_re], axis=1).astype(BF16), ((0, 0), (0, LANES - n_r)))
    b_router = jnp.zeros((1, LANES), F32).at[0, :n_r].set(jnp.concatenate([b_rg, b_re]))
    x1, xs, route, counts = _merge(
        x2d, ya, yb, proj, w_branch_a.astype(BF16), w_branch_b.astype(BF16), w_out.astype(BF16),
        row(ffn_norm), w_router, b_router)

    n_blocks = t // TOK_TILE
    cnt = counts[:, 0, N_GROUPS:N_GROUPS + N_EXPERTS].astype(I32)
    run_chunks = (cnt + ROW_CHUNK - 1) // ROW_CHUNK
    earlier_e = jnp.tril(jnp.ones((N_EXPERTS, N_EXPERTS), I32), -1)
    earlier_b = jnp.tril(jnp.ones((n_blocks, n_blocks), I32), -1)
    run_local = jnp.sum(run_chunks[:, None, :] * earlier_e[None], axis=2)
    seg_chunks = jnp.sum(run_chunks, axis=0)
    tiles_per = (seg_chunks + CHUNKS_PER_TILE - 1) // CHUNKS_PER_TILE
    seg_start = jnp.sum(tiles_per[None, :] * earlier_e, axis=1) * CHUNKS_PER_TILE
    tile_end = seg_start // CHUNKS_PER_TILE + tiles_per
    run_global = seg_start[None, :] + jnp.sum(run_chunks.T[:, None, :] * earlier_b[None], axis=2).T
    max_chunks = (2 * t) // ROW_CHUNK + n_blocks * N_EXPERTS + N_EXPERTS * (CHUNKS_PER_TILE - 1)
    n_tiles = -(-max_chunks // CHUNKS_PER_TILE) + EXP_AHEAD
    tile_ids = jnp.arange(n_tiles, dtype=I32)
    tile_expert = jnp.minimum(jnp.sum((tile_end[None, :] <= tile_ids[:, None]).astype(I32), axis=1),
                              N_EXPERTS - 1)
    n_used = tile_end[-1:].astype(I32)

    block_ids = jnp.arange(n_blocks, dtype=I32)
    zero_local = LOCAL_CHUNKS - 1
    zero_global = n_tiles * CHUNKS_PER_TILE - 1
    g = jnp.arange(n_tiles * CHUNKS_PER_TILE, dtype=I32)[:, None]
    e_g = jnp.repeat(tile_expert, CHUNKS_PER_TILE)
    pick_e = (e_g[:, None] == jnp.arange(N_EXPERTS, dtype=I32)[None, :]).astype(I32)
    rg, rc, rl = (jnp.sum(pick_e[:, :, None] * tab.T[None, :, :], axis=1)
                  for tab in (run_global, run_chunks, run_local))
    inside = (rg <= g) & (g < rg + rc)
    gather_src = jnp.sum(jnp.where(inside, block_ids[None, :] * LOCAL_CHUNKS + rl + (g - rg), 0), axis=1)
    gather_src = jnp.where(jnp.any(inside, axis=1), gather_src, zero_local).astype(I32)

    lc = jnp.arange(LOCAL_CHUNKS, dtype=I32)[None, :, None]
    inside = (run_local[:, None, :] <= lc) & (lc < (run_local + run_chunks)[:, None, :])
    back_src = jnp.sum(jnp.where(inside, run_global[:, None, :] + lc - run_local[:, None, :], 0), axis=2)
    back_src = jnp.where(jnp.any(inside, axis=2), back_src, zero_global).astype(I32).reshape(-1)

    ys = _experts(tile_expert, n_used, gather_src, xs, wg16, wu16, wd16, n_tiles)
    return _final(back_src, x1, route, p2d, ys, row(ple_norm), w_ple_gate.astype(BF16),
                  w_ple_proj.astype(BF16), out_gain)


def kernel(x, p, mix_norm, w_in, hg_lb_logits, hg_norm, ret_norm, w_branch_a, w_branch_b, w_out,
           ffn_norm, w_router_group, b_router_group, w_router_expert, b_router_expert,
           w_expert_gate, w_expert_up, w_expert_down, ple_norm, w_ple_gate, w_ple_proj, final_norm):
    batch, seq, d = x.shape
    depth = p.shape[0]
    assert depth == 1, "the final rmsnorm is fused into the single layer"
    x2d = x.reshape(batch * seq, d)
    out = _layer(x2d, p[0].reshape(batch * seq, -1), batch, seq, mix_norm[0], w_in[0], hg_lb_logits,
                 hg_norm[0], ret_norm[0], w_branch_a[0], w_branch_b[0], w_out[0], ffn_norm[0],
                 w_router_group[0], b_router_group[0], w_router_expert[0], b_router_expert[0],
                 w_expert_gate[0], w_expert_up[0], w_expert_down[0], ple_norm[0], w_ple_gate[0],
                 w_ple_proj[0], final_norm.reshape(1, -1).astype(F32))
    return out.reshape(batch, seq, d)
```

```python
import functools

import jax
import jax.numpy as jnp
from jax import lax
from jax.experimental import pallas as pl
from jax.experimental.pallas import tpu as pltpu

F32 = jnp.float32
BF16 = jnp.bfloat16
I32 = jnp.int32
U32 = jnp.uint32

EPS = 1e-6
D_MODEL = 1024
PLE_DIM = 256
HG_HEADS = 4
HG_DK = 128
HG_WIDTH = HG_HEADS * HG_DK
RET_HEADS = 4
RET_DK = 128
RET_DV = 256
ROPE_BASE = 10000.0
ROPE_SPLIT = 64
IN_TOTAL = 7168
N_GROUPS = 4
EXPERTS_PER_GROUP = 8
N_EXPERTS = 32
D_EXPERT = 256

COL_HQ, COL_HF, COL_HI, COL_HG = 0, 4, 8, 12
COL_RQ, COL_RK = 16, 20
COL_RV, COL_RG = 24, 32
COL_GA, COL_GB = 5, 6

LANES = 128
VMEM_LIMIT = 56 * 1024 * 1024

HG_CHUNK = 64
HG_SUB = 8
HG_UNROLL = 4
HG_STATE_UNROLL = 8
HG_NORM_ROWS = 256
RET_CHUNK = 128
RET_STATE_UNROLL = 4
SEQ_TILE = 1024
TOK_TILE = 512
EXP_TILE = 512
EXP_SUB = 512
EXP_AHEAD = 2
EXP_SLOTS = EXP_AHEAD + 1
ROW_CHUNK = 16
CHUNKS_PER_TILE = EXP_TILE // ROW_CHUNK
LOCAL_ROWS = 2 * TOK_TILE + N_EXPERTS * ROW_CHUNK
LOCAL_CHUNKS = LOCAL_ROWS // ROW_CHUNK
SORT_SLAB = 256
FIN_AHEAD = 2
FIN_SLOTS = FIN_AHEAD + 1


def _cparams(sem):
    return pltpu.CompilerParams(dimension_semantics=sem, vmem_limit_bytes=VMEM_LIMIT)


def _rms(x, g):
    return x * lax.rsqrt(jnp.mean(x * x, axis=-1, keepdims=True) + EPS) * g


def _sigmoid(x):
    return 1.0 / (1.0 + jnp.exp(-x))


def _silu(x):
    return x * _sigmoid(x)


def _split3(x):
    hi = x.astype(BF16)
    r1 = x - hi.astype(F32)
    mid = r1.astype(BF16)
    lo = (r1 - mid.astype(F32)).astype(BF16)
    return hi, mid, lo


def _dot(a, b):
    return jnp.dot(a, b, preferred_element_type=F32)


def _dot_nt(a, b):
    return lax.dot_general(a, b, (((1,), (1,)), ((), ())), preferred_element_type=F32)


def _dot_tn(a, b):
    return lax.dot_general(a, b, (((0,), (0,)), ((), ())), preferred_element_type=F32)


def _inproj_kernel(x_ref, g_ref, w_ref, wg_ref, wu_ref, wd_ref, proj_ref, hf_ref,
                   wg16_ref, wu16_ref, wd16_ref):
    wg16_ref[...] = wg_ref[...].astype(BF16)
    wu16_ref[...] = wu_ref[...].astype(BF16)
    wd16_ref[...] = wd_ref[...].astype(BF16)

    h = _rms(x_ref[...], g_ref[...]).astype(BF16)
    tn = HG_WIDTH
    for j in range(IN_TOTAL // tn):
        acc = _dot(h, w_ref[:, j * tn:(j + 1) * tn])
        proj_ref[:, j * tn:(j + 1) * tn] = acc.astype(BF16)
        if j * tn == COL_HF * LANES:
            hf_ref[...] = acc


def _inproj(x2d, gain, w_bf16, w_gate, w_up, w_down):
    t = x2d.shape[0]
    steps = t // TOK_TILE
    assert N_EXPERTS % steps == 0, "expert weights are converted in equal shares per grid step"
    share = N_EXPERTS // steps
    up_spec = pl.BlockSpec((share, D_MODEL, D_EXPERT), lambda i: (i, 0, 0))
    down_spec = pl.BlockSpec((share, D_EXPERT, D_MODEL), lambda i: (i, 0, 0))
    return pl.pallas_call(
        _inproj_kernel,
        grid=(steps,),
        in_specs=[
            pl.BlockSpec((TOK_TILE, D_MODEL), lambda i: (i, 0)),
            pl.BlockSpec((1, D_MODEL), lambda i: (0, 0)),
            pl.BlockSpec((D_MODEL, IN_TOTAL), lambda i: (0, 0), pipeline_mode=pl.Buffered(1)),
            up_spec, up_spec, down_spec,
        ],
        out_specs=[
            pl.BlockSpec((TOK_TILE, IN_TOTAL), lambda i: (i, 0)),
            pl.BlockSpec((TOK_TILE, HG_WIDTH), lambda i: (i, 0)),
            up_spec, up_spec, down_spec,
        ],
        out_shape=[
            jax.ShapeDtypeStruct((t, IN_TOTAL), BF16),
            jax.ShapeDtypeStruct((t, HG_WIDTH), F32),
            jax.ShapeDtypeStruct(w_gate.shape, BF16),
            jax.ShapeDtypeStruct(w_up.shape, BF16),
            jax.ShapeDtypeStruct(w_down.shape, BF16),
        ],
        compiler_params=_cparams(("arbitrary",)),
        name="inproj",
    )(x2d, gain, w_bf16, w_gate, w_up, w_down)


def _hgrn_stages(lbl_ref, q_ref, f_ref, i_ref, g_ref, ng_ref, o_ref,
                 st_ref, b_s, k_s, v_s, oi_s, qe_s, kv_s, dec_s):
    c = HG_CHUNK
    nsub = c // HG_SUB

    logits = lbl_ref[...]
    e = jnp.exp(logits - jnp.max(logits, axis=0, keepdims=True))
    lb = e[0:1] / jnp.sum(e, axis=0, keepdims=True)
    one_m_lb = jnp.sum(e[1:], axis=0, keepdims=True) / jnp.sum(e, axis=0, keepdims=True)
    ng = ng_ref[...]

    row = lax.broadcasted_iota(I32, (c, c), 0)
    col = lax.broadcasted_iota(I32, (c, c), 1)
    tri = jnp.where(row >= col, 1.0, 0.0).astype(BF16)
    row_k = lax.broadcasted_iota(I32, (c, HG_DK), 0)
    sub_row = lax.broadcasted_iota(I32, (HG_SUB, HG_DK), 0)
    masked = jnp.float32(-1e30)

    def bcast_row(ref, r, rows):
        return jnp.broadcast_to(ref[pl.ds(r, 1), :], (rows, HG_DK))

    def prep(ci, slot):
        r0 = pl.multiple_of(ci * c, c)
        z = f_ref[pl.ds(r0, c), :]
        ez = jnp.exp(-jnp.abs(z))
        rz = 1.0 / (1.0 + ez)
        pos = z >= 0.0
        logf = jnp.log2(lb + one_m_lb * jnp.where(pos, rz, ez * rz))
        kk = one_m_lb * jnp.where(pos, ez * rz, rz)
        q = _silu(q_ref[pl.ds(r0, c), :].astype(F32))
        v = i_ref[pl.ds(r0, c), :].astype(F32)
        k_s[slot] = kk
        v_s[slot] = v
        return dict(ci=ci, r0=r0, slot=slot, kk=kk, q=q, v16=v.astype(BF16), parts=_split3(logf))

    def cumulate(s):
        hi, mid, lo = s.pop("parts")
        b = (_dot(tri, lo) + _dot(tri, mid)) + _dot(tri, hi)
        b_s[s["slot"]] = b
        qe_s[pl.ds(s["r0"], c), :] = (s["q"] * jnp.exp2(b)).astype(BF16)
        s["b"] = b
        return s

    def off_diagonal(s):
        b, q, kk = s["b"], s["q"], s["kk"]
        bs_ref = b_s.at[s["slot"]]
        a_rows = [jnp.zeros((HG_SUB, c), F32)]
        for i in range(1, nsub):
            bi = bcast_row(bs_ref, i * HG_SUB - 1, c)
            qi = q[i * HG_SUB:(i + 1) * HG_SUB, :] * jnp.exp2(b[i * HG_SUB:(i + 1) * HG_SUB, :] - bi[:HG_SUB])
            ki = kk * jnp.exp2(jnp.where(row_k < i * HG_SUB, bi - b, masked))
            a_rows.append(_dot_nt(qi.astype(BF16), ki.astype(BF16)))
        s["a_off"] = jnp.concatenate(a_rows, axis=0)
        return s

    def apply_values(s):
        b, kk = s["b"], s["kk"]
        blast = b[c - 1:c, :]
        s["o"] = _dot(s.pop("a_off").astype(BF16), s["v16"])
        kd = kk * jnp.exp2(blast - b)
        kv_s[s["ci"]] = _dot_tn(s["v16"], kd.astype(BF16))
        dec_s[s["ci"]] = jnp.broadcast_to(jnp.exp2(blast), (HG_SUB, HG_DK))
        return s

    def diagonal(s):
        b, q = s["b"], s["q"]
        bs_ref, ks_ref, vs_ref = b_s.at[s["slot"]], k_s.at[s["slot"]], v_s.at[s["slot"]]
        d_blocks = []
        for i in range(nsub):
            sl = slice(i * HG_SUB, (i + 1) * HG_SUB)
            bt, qt = b[sl, :], q[sl, :]
            acc = jnp.zeros((HG_SUB, HG_DK), F32)
            for j in range(HG_SUB):
                r = i * HG_SUB + j
                arg = jnp.where(sub_row >= j, bt - bcast_row(bs_ref, r, HG_SUB), masked)
                g = jnp.exp2(arg) * (qt * bcast_row(ks_ref, r, HG_SUB))
                acc = acc + jnp.sum(g, axis=-1, keepdims=True) * bcast_row(vs_ref, r, HG_SUB)
            d_blocks.append(acc)
        oi_s[pl.ds(s["r0"], c), :] = s["o"] + jnp.concatenate(d_blocks, axis=0)

    def carried_group(gi):
        st = st_ref[...]
        outs = []
        for j in range(HG_STATE_UNROLL):
            ci = gi * HG_STATE_UNROLL + j
            r0 = pl.multiple_of(ci * c, c)
            outs.append((r0, _dot_nt(qe_s[pl.ds(r0, c), :], st.astype(BF16))))
            st = st * dec_s[ci][0:1, :] + kv_s[ci]
        st_ref[...] = st
        for r0, os in outs:
            oi_s[pl.ds(r0, c), :] += os

    def finish(ri):
        r0 = pl.multiple_of(ri * HG_NORM_ROWS, HG_NORM_ROWS)
        y = _rms(oi_s[pl.ds(r0, HG_NORM_ROWS), :], ng) * _silu(g_ref[pl.ds(r0, HG_NORM_ROWS), :].astype(F32))
        o_ref[pl.ds(r0, HG_NORM_ROWS), :] = y.astype(o_ref.dtype)

    return prep, (cumulate, off_diagonal, apply_values, diagonal), carried_group, finish


HG_SCRATCH = [
    pltpu.VMEM((HG_DK, HG_DK), F32),
    pltpu.VMEM((HG_UNROLL, HG_CHUNK, HG_DK), F32),
    pltpu.VMEM((HG_UNROLL, HG_CHUNK, HG_DK), F32),
    pltpu.VMEM((HG_UNROLL, HG_CHUNK, HG_DK), F32),
    pltpu.VMEM((SEQ_TILE, HG_DK), F32),
    pltpu.VMEM((SEQ_TILE, HG_DK), BF16),
    pltpu.VMEM((SEQ_TILE // HG_CHUNK, HG_DK, HG_DK), F32),
    pltpu.VMEM((SEQ_TILE // HG_CHUNK, HG_SUB, HG_DK), F32),
]


def _ret_stages(q_ref, k_ref, v_ref, g_ref, cos_ref, sin_ref, ng_ref, o_ref,
                r_ref, oi_s, q16_s, kv_s):
    c = RET_CHUNK

    hf = jnp.full((1, 1), pl.program_id(1), I32).astype(F32)
    lg = jnp.log1p(-jnp.exp2(-5.0 - hf))
    ti = lax.broadcasted_iota(I32, (c, c), 0)
    si = lax.broadcasted_iota(I32, (c, c), 1)
    rel = (ti - si).astype(F32)
    intra = jnp.where(ti >= si, jnp.exp(jnp.maximum(rel, 0.0) * lg), 0.0)
    idx = lax.broadcasted_iota(I32, (c, 1), 0).astype(F32)
    inter = jnp.exp((idx + 1.0) * lg)
    to_state = jnp.exp((c - 1.0 - idx) * lg)
    chunk_decay = jnp.exp(float(c) * lg)
    ng = ng_ref[...]
    half = RET_DK // 2

    def prep(ci):
        r0 = pl.multiple_of(ci * c, c)
        cos = cos_ref[pl.ds(r0, c), :]
        sin = sin_ref[pl.ds(r0, c), :]
        q = q_ref[pl.ds(r0, c), :].astype(F32)
        k = k_ref[pl.ds(r0, c), :].astype(F32)
        qr = (q * cos + pltpu.roll(q, half, 1) * sin) * (RET_DK ** -0.5)
        kr = k * cos + pltpu.roll(k, half, 1) * sin
        q16 = qr.astype(BF16)
        q16_s[pl.ds(r0, c), :] = q16
        return dict(ci=ci, r0=r0, q16=q16, k16=kr.astype(BF16), kts=(kr * to_state).astype(BF16),
                    v16=v_ref[pl.ds(r0, c), :])

    def scores(s):
        s["att"] = (_dot_nt(s.pop("q16"), s.pop("k16")) * intra).astype(BF16)
        return s

    def apply_values(s):
        oi_s[pl.ds(s["r0"], c), :] = _dot(s["att"], s["v16"])
        kv_s[s["ci"]] = _dot_tn(s["kts"], s["v16"])
        return s

    def carried_group(gi):
        r = r_ref[...]
        outs = []
        for j in range(RET_STATE_UNROLL):
            ci = gi * RET_STATE_UNROLL + j
            r0 = pl.multiple_of(ci * c, c)
            outs.append((r0, _dot(q16_s[pl.ds(r0, c), :], r.astype(BF16))))
            r = chunk_decay * r + kv_s[ci]
        r_ref[...] = r
        for r0, qr_state in outs:
            o = oi_s[pl.ds(r0, c), :] + qr_state * inter
            y = _rms(o, ng) * _silu(g_ref[pl.ds(r0, c), :].astype(F32))
            o_ref[pl.ds(r0, c), :] = y.astype(o_ref.dtype)

    return prep, (scores, apply_values), carried_group


RET_SCRATCH = [
    pltpu.VMEM((RET_DK, RET_DV), F32),
    pltpu.VMEM((SEQ_TILE, RET_DV), F32),
    pltpu.VMEM((SEQ_TILE, RET_DK), BF16),
    pltpu.VMEM((SEQ_TILE // RET_CHUNK, RET_DK, RET_DV), F32),
]


def _mixers_kernel(*refs):
    hg_refs = refs[:6] + refs[13:14] + refs[15:15 + len(HG_SCRATCH)]
    ret_refs = refs[6:13] + refs[14:15] + refs[15 + len(HG_SCRATCH):]
    st_ref, r_ref = hg_refs[7], ret_refs[8]
    n_tokens = refs[1].shape[0]

    @pl.when(pl.program_id(2) == 0)
    def _():
        st_ref[...] = jnp.zeros_like(st_ref)
        r_ref[...] = jnp.zeros_like(r_ref)

    hg_prep, hg_local, hg_carried, hg_finish = _hgrn_stages(*hg_refs)
    ret_prep, ret_local, ret_carried = _ret_stages(*ret_refs)
    group_tokens = HG_UNROLL * HG_CHUNK
    ret_unroll = group_tokens // RET_CHUNK

    def local_group(gi, carry):
        hs = [hg_prep(gi * HG_UNROLL + slot, slot) for slot in range(HG_UNROLL)]
        rs = [ret_prep(gi * ret_unroll + j) for j in range(ret_unroll)]
        cumulate, off_diagonal, apply_values, diagonal = hg_local
        scores, ret_apply = ret_local
        hs = [cumulate(s) for s in hs]
        rs = [scores(s) for s in rs]
        hs = [off_diagonal(s) for s in hs]
        rs = [ret_apply(s) for s in rs]
        hs = [apply_values(s) for s in hs]
        for s in hs:
            diagonal(s)
        return carry

    lax.fori_loop(0, n_tokens // group_tokens, local_group, 0)

    state_tokens = HG_STATE_UNROLL * HG_CHUNK
    assert state_tokens == RET_STATE_UNROLL * RET_CHUNK

    def carried_group(gi, carry):
        hg_carried(gi)
        ret_carried(gi)
        return carry

    lax.fori_loop(0, n_tokens // state_tokens, carried_group, 0)

    def finish(ri, carry):
        hg_finish(ri)
        return carry

    lax.fori_loop(0, n_tokens // HG_NORM_ROWS, finish, 0)


def _mixers(proj, hf, lb_logits, hg_norm, cos2, sin2, ret_norm, batch, seq):
    ns = seq // SEQ_TILE
    tok = lambda b, h, s: b * ns + s
    hg_col = lambda base: pl.BlockSpec((SEQ_TILE, HG_DK), lambda b, h, s: (tok(b, h, s), base + h))
    return pl.pallas_call(
        _mixers_kernel,
        grid=(batch, HG_HEADS, ns),
        in_specs=[
            pl.BlockSpec((2, HG_DK), lambda b, h, s: (0, h)),
            hg_col(COL_HQ), hg_col(0), hg_col(COL_HI), hg_col(COL_HG),
            pl.BlockSpec((1, HG_DK), lambda b, h, s: (0, 0)),
            hg_col(COL_RQ), hg_col(COL_RK),
            pl.BlockSpec((SEQ_TILE, RET_DV), lambda b, h, s: (tok(b, h, s), COL_RV // 2 + h)),
            pl.BlockSpec((SEQ_TILE, RET_DV), lambda b, h, s: (tok(b, h, s), COL_RG // 2 + h)),
            pl.BlockSpec((SEQ_TILE, RET_DK), lambda b, h, s: (s, 0)),
            pl.BlockSpec((SEQ_TILE, RET_DK), lambda b, h, s: (s, 0)),
            pl.BlockSpec((1, RET_DV), lambda b, h, s: (0, 0)),
        ],
        out_specs=[
            pl.BlockSpec((SEQ_TILE, HG_DK), lambda b, h, s: (tok(b, h, s), h)),
            pl.BlockSpec((SEQ_TILE, RET_DV), lambda b, h, s: (tok(b, h, s), h)),
        ],
        out_shape=[
            jax.ShapeDtypeStruct((batch * seq, HG_WIDTH), BF16),
            jax.ShapeDtypeStruct((batch * seq, RET_HEADS * RET_DV), BF16),
        ],
        scratch_shapes=HG_SCRATCH + RET_SCRATCH,
        compiler_params=_cparams(("arbitrary", "arbitrary", "arbitrary")),
        name="mixers",
    )(lb_logits, proj, hf, proj, proj, hg_norm, proj, proj, proj, proj, cos2, sin2, ret_norm)


def _merge_kernel(x_ref, ya_ref, yb_ref, ga_ref, gb_ref, wa_ref, wb_ref, wo_ref, fg_ref,
                  wr_ref, br_ref, x1_ref, xs_ref, route_ref, cnt_ref, h2b_s, rows_s):
    tm = x_ref.shape[0]

    @pl.when(pl.program_id(0) == 0)
    def _():
        h2b_s[...] = jnp.zeros_like(h2b_s)
        rows_s[...] = jnp.full(rows_s.shape, -1.0, F32)

    def sort_previous(lo, hi):
        slab_row = lax.broadcasted_iota(I32, (SORT_SLAB, tm), 0).astype(F32)
        for r0 in range(lo, hi, SORT_SLAB):
            sel = ((slab_row == rows_s[0:1, :] - float(r0)) | (slab_row == rows_s[1:2, :] - float(r0)))
            xs_ref[r0:r0 + SORT_SLAB, :] = _dot(jnp.where(sel, 1.0, 0.0).astype(BF16),
                                                h2b_s[...]).astype(BF16)

    merged = (_sigmoid(ga_ref[...].astype(F32)) * _dot(ya_ref[...], wa_ref[...])
              + _sigmoid(gb_ref[...].astype(F32)) * _dot(yb_ref[...], wb_ref[...]))
    x1 = x_ref[...] + _dot(merged.astype(BF16), wo_ref[...])
    x1_ref[...] = x1
    h2 = _rms(x1, fg_ref[...])

    h2b = h2.astype(BF16)
    logits = _dot(h2b, wr_ref[...]) + br_ref[...]
    sort_previous(0, LOCAL_ROWS)

    lane = lax.broadcasted_iota(I32, (tm, LANES), 1)
    neg = jnp.float32(-jnp.inf)
    big = jnp.int32(1 << 30)
    is_g = lane < N_GROUPS
    gl = jnp.where(is_g, logits, neg)
    gmax = jnp.max(gl, axis=-1, keepdims=True)
    g_idx = jnp.min(jnp.where(gl == gmax, lane, big), axis=-1, keepdims=True)
    g_w = 1.0 / jnp.sum(jnp.where(is_g, jnp.exp(gl - gmax), 0.0), axis=-1, keepdims=True)

    ex = lane - N_GROUPS
    in_grp = (ex >= g_idx * EXPERTS_PER_GROUP) & (ex < (g_idx + 1) * EXPERTS_PER_GROUP)
    el = jnp.where(in_grp, logits, neg)
    m1 = jnp.max(el, axis=-1, keepdims=True)
    e1 = jnp.min(jnp.where(el == m1, ex, big), axis=-1, keepdims=True)
    el2 = jnp.where(ex == e1, neg, el)
    m2 = jnp.max(el2, axis=-1, keepdims=True)
    e2 = jnp.min(jnp.where(el2 == m2, ex, big), axis=-1, keepdims=True)
    p2 = jnp.exp(m2 - m1)
    w1 = g_w / (1.0 + p2)
    w2 = g_w * p2 / (1.0 + p2)

    oh1 = ex == e1
    oh2 = ex == e2
    oh = jnp.where(oh1 | oh2, 1.0, 0.0)
    ri = lax.broadcasted_iota(I32, (tm, tm), 0)
    ci = lax.broadcasted_iota(I32, (tm, tm), 1)
    strict = jnp.where(ri > ci, 1.0, 0.0).astype(BF16)
    local_rank = _dot(strict, oh.astype(BF16))
    cnt = jnp.sum(oh, axis=0, keepdims=True)
    run_chunks = jnp.floor((cnt + (ROW_CHUNK - 1.0)) * (1.0 / ROW_CHUNK))
    ui = lax.broadcasted_iota(I32, (LANES, LANES), 0)
    uj = lax.broadcasted_iota(I32, (LANES, LANES), 1)
    before = jnp.where(ui < uj, 1.0, 0.0).astype(BF16)
    run_start = _dot(jnp.broadcast_to(run_chunks, (8, LANES)).astype(BF16), before)[0:1] * ROW_CHUNK
    slot = run_start + local_rank
    pos1 = jnp.sum(jnp.where(oh1, slot, 0.0), axis=-1, keepdims=True)
    pos2 = jnp.sum(jnp.where(oh2, slot, 0.0), axis=-1, keepdims=True)
    cnt_ref[0] = jnp.broadcast_to(cnt, (8, LANES))

    route = jnp.where(lane == 0, e1.astype(F32), 0.0)
    route = jnp.where(lane == 1, e2.astype(F32), route)
    route = jnp.where(lane == 2, w1, route)
    route = jnp.where(lane == 3, w2, route)
    route = jnp.where(lane == 4, pos1, route)
    route = jnp.where(lane == 5, pos2, route)
    route_ref[...] = route

    hi1 = jnp.floor(pos1 * (1.0 / 256.0))
    hi2 = jnp.floor(pos2 * (1.0 / 256.0))
    digits = jnp.where(lane == 0, hi1, 0.0)
    digits = jnp.where(lane == 1, pos1 - 256.0 * hi1, digits)
    digits = jnp.where(lane == 2, hi2, digits)
    digits = jnp.where(lane == 3, pos2 - 256.0 * hi2, digits)
    pick = jnp.where(lax.broadcasted_iota(I32, (8, LANES), 0) == lax.broadcasted_iota(I32, (8, LANES), 1),
                     1.0, 0.0).astype(BF16)
    rows = _dot_nt(pick, digits.astype(BF16))

    h2b_s[...] = h2b
    rows_s[0:1, :] = rows[0:1] * 256.0 + rows[1:2]
    rows_s[1:2, :] = rows[2:3] * 256.0 + rows[3:4]


def _merge(x2d, ya, yb, proj, wa, wb, wo, ffn_g, w_router, b_router):
    t = x2d.shape[0]
    nb = t // TOK_TILE
    const = lambda *shape: pl.BlockSpec(shape, lambda i: (0,) * len(shape))
    cur = lambda i: jnp.minimum(i, nb - 1)
    prev = lambda i: jnp.maximum(i - 1, 0)
    return pl.pallas_call(
        _merge_kernel,
        grid=(nb + 1,),
        in_specs=[
            pl.BlockSpec((TOK_TILE, D_MODEL), lambda i: (cur(i), 0)),
            pl.BlockSpec((TOK_TILE, HG_WIDTH), lambda i: (cur(i), 0)),
            pl.BlockSpec((TOK_TILE, D_MODEL), lambda i: (cur(i), 0)),
            pl.BlockSpec((TOK_TILE, D_MODEL), lambda i: (cur(i), COL_GA)),
            pl.BlockSpec((TOK_TILE, D_MODEL), lambda i: (cur(i), COL_GB)),
            const(HG_WIDTH, D_MODEL),
            const(D_MODEL, D_MODEL),
            const(D_MODEL, D_MODEL),
            const(1, D_MODEL),
            const(D_MODEL, LANES),
            const(1, LANES),
        ],
        out_specs=[
            pl.BlockSpec((TOK_TILE, D_MODEL), lambda i: (cur(i), 0)),
            pl.BlockSpec((LOCAL_ROWS, D_MODEL), lambda i: (prev(i), 0)),
            pl.BlockSpec((TOK_TILE, LANES), lambda i: (cur(i), 0)),
            pl.BlockSpec((1, 8, LANES), lambda i: (cur(i), 0, 0)),
        ],
        out_shape=[
            jax.ShapeDtypeStruct((t, D_MODEL), F32),
            jax.ShapeDtypeStruct((t // TOK_TILE * LOCAL_ROWS, D_MODEL), BF16),
            jax.ShapeDtypeStruct((t, LANES), F32),
            jax.ShapeDtypeStruct((t // TOK_TILE, 8, LANES), F32),
        ],
        scratch_shapes=[pltpu.VMEM((TOK_TILE, D_MODEL), BF16), pltpu.VMEM((8, TOK_TILE), F32)],
        compiler_params=_cparams(("arbitrary",)),
        name="merge_route",
    )(x2d, ya, yb, proj, proj, wa, wb, wo, ffn_g, w_router, b_router)


def _expert_kernel(te_ref, nt_ref, src_ref, xs_ref, wg_ref, wu_ref, wd_ref, ys_ref, xbuf, sem):
    i = pl.program_id(0)
    nt = nt_ref[0]

    def gather(tile, slot):
        copies = []
        for c in range(CHUNKS_PER_TILE):
            row = pl.multiple_of(src_ref[tile * CHUNKS_PER_TILE + c] * ROW_CHUNK, ROW_CHUNK)
            copies.append(pltpu.make_async_copy(
                xs_ref.at[pl.ds(row, ROW_CHUNK)],
                xbuf.at[slot, pl.ds(c * ROW_CHUNK, ROW_CHUNK)], sem.at[slot]))
        return copies

    @pl.when(i == 0)
    def _():
        for k in range(EXP_AHEAD):
            for cp in gather(k, k):
                cp.start()

    @pl.when(i < nt + EXP_AHEAD)
    def _():
        for cp in gather(i, i % EXP_SLOTS):
            cp.wait()

    @pl.when(i < nt)
    def _():
        slot = i % EXP_SLOTS
        wg, wu, wd = wg_ref[0], wu_ref[0], wd_ref[0]
        subs = [pl.ds(r, EXP_SUB) for r in range(0, EXP_TILE, EXP_SUB)]
        xs = [xbuf[slot, s, :] for s in subs]
        gates = [(_dot(x, wg), _dot(x, wu)) for x in xs]
        for c, cp in enumerate(gather(i + EXP_AHEAD, (i + EXP_AHEAD) % EXP_SLOTS)):
            cp.start(priority=c % 2)
        hidden = [(_silu(a) * u).astype(BF16) for a, u in gates]
        for s, h in zip(subs, hidden):
            ys_ref[s, :] = _dot(h, wd).astype(ys_ref.dtype)

    @pl.when(i >= nt)
    def _():
        ys_ref[...] = jnp.zeros_like(ys_ref)


def _experts(tile_expert, n_tiles_used, src_chunk, xs, wg, wu, wd, n_tiles):
    grid_spec = pltpu.PrefetchScalarGridSpec(
        num_scalar_prefetch=3,
        grid=(n_tiles,),
        in_specs=[
            pl.BlockSpec(memory_space=pl.ANY),
            pl.BlockSpec((1, D_MODEL, D_EXPERT), lambda i, te, nt, src: (te[i], 0, 0)),
            pl.BlockSpec((1, D_MODEL, D_EXPERT), lambda i, te, nt, src: (te[i], 0, 0)),
            pl.BlockSpec((1, D_EXPERT, D_MODEL), lambda i, te, nt, src: (te[i], 0, 0)),
        ],
        out_specs=pl.BlockSpec((EXP_TILE, D_MODEL), lambda i, te, nt, src: (i, 0)),
        scratch_shapes=[
            pltpu.VMEM((EXP_SLOTS, EXP_TILE, D_MODEL), BF16),
            pltpu.SemaphoreType.DMA((EXP_SLOTS,)),
        ],
    )
    return pl.pallas_call(
        _expert_kernel,
        grid_spec=grid_spec,
        out_shape=jax.ShapeDtypeStruct((n_tiles * EXP_TILE, D_MODEL), BF16),
        compiler_params=_cparams(("arbitrary",)),
        name="experts",
    )(tile_expert, n_tiles_used, src_chunk, xs, wg, wu, wd)


def _final_kernel(src_ref, x1_ref, route_ref, p_ref, ys_ref, pg_ref, wpg_ref, wpp_ref, fg_ref,
                  o_ref, ybuf, sem):
    tm = x1_ref.shape[0]
    i = pl.program_id(0)
    nb = pl.num_programs(0)

    def gather(block, slot):
        copies = []
        for c in range(LOCAL_CHUNKS):
            row = pl.multiple_of(src_ref[block * LOCAL_CHUNKS + c] * ROW_CHUNK, ROW_CHUNK)
            copies.append(pltpu.make_async_copy(
                ys_ref.at[pl.ds(row, ROW_CHUNK)],
                ybuf.at[slot, pl.ds(c * ROW_CHUNK, ROW_CHUNK)], sem.at[slot]))
        return copies

    @pl.when(i == 0)
    def _():
        for k in range(FIN_AHEAD):
            for cp in gather(k % nb, k):
                cp.start()

    slot = i % FIN_SLOTS
    for cp in gather(i, slot):
        cp.wait()

    ple = _dot(p_ref[...].astype(BF16), wpp_ref[...])

    route = route_ref[...]
    w1, w2 = route[:, 2:3], route[:, 3:4]
    pos1, pos2 = route[:, 4:5], route[:, 5:6]
    slab_col = lax.broadcasted_iota(I32, (tm, SORT_SLAB), 1).astype(F32)
    moe = jnp.zeros((tm, D_MODEL), F32)
    for k0 in range(0, LOCAL_ROWS, SORT_SLAB):
        sel = jnp.where(slab_col == pos1 - float(k0), w1,
                        jnp.where(slab_col == pos2 - float(k0), w2, 0.0)).astype(BF16)
        moe = moe + _dot(sel, ybuf[slot, k0:k0 + SORT_SLAB, :])
    for c, cp in enumerate(gather((i + FIN_AHEAD) % nb, (i + FIN_AHEAD) % FIN_SLOTS)):
        cp.start(priority=c % 2)
    x2 = x1_ref[...] + moe
    hp = _rms(x2, pg_ref[...]).astype(BF16)
    gate = _sigmoid(_dot(hp, wpg_ref[...]))
    x3 = x2 + gate * ple
    o_ref[...] = _rms(x3, fg_ref[...])

    @pl.when(i == nb - 1)
    def _():
        for k in range(1, FIN_AHEAD + 1):
            for cp in gather(0, (i + k) % FIN_SLOTS):
                cp.wait()


def _final(src_chunk, x1, route, p2d, ys, ple_g, wpg, wpp, final_g):
    t = x1.shape[0]
    const = lambda *shape: pl.BlockSpec(shape, lambda i, src: (0,) * len(shape))
    grid_spec = pltpu.PrefetchScalarGridSpec(
        num_scalar_prefetch=1,
        grid=(t // TOK_TILE,),
        in_specs=[
            pl.BlockSpec((TOK_TILE, D_MODEL), lambda i, src: (i, 0)),
            pl.BlockSpec((TOK_TILE, LANES), lambda i, src: (i, 0)),
            pl.BlockSpec((TOK_TILE, PLE_DIM), lambda i, src: (i, 0)),
            pl.BlockSpec(memory_space=pl.ANY),
            const(1, D_MODEL),
            const(D_MODEL, D_MODEL),
            const(PLE_DIM, D_MODEL),
            const(1, D_MODEL),
        ],
        out_specs=pl.BlockSpec((TOK_TILE, D_MODEL), lambda i, src: (i, 0)),
        scratch_shapes=[pltpu.VMEM((FIN_SLOTS, LOCAL_ROWS, D_MODEL), BF16),
                        pltpu.SemaphoreType.DMA((FIN_SLOTS,))],
    )
    return pl.pallas_call(
        _final_kernel,
        grid_spec=grid_spec,
        out_shape=jax.ShapeDtypeStruct((t, D_MODEL), F32),
        compiler_params=_cparams(("arbitrary",)),
        name="combine_ple_final",
    )(src_chunk, x1, route, p2d, ys, ple_g, wpg, wpp, final_g)


def _rotary_tables(seq):
    inv = ROPE_BASE ** (-jnp.arange(0, RET_DK, 2, dtype=F32) / RET_DK)
    inv = jnp.concatenate([inv, inv])
    sign = jnp.where(jnp.arange(RET_DK) < RET_DK // 2, -1.0, 1.0).astype(F32)
    hi = (jnp.arange(seq // ROPE_SPLIT, dtype=F32) * ROPE_SPLIT)[:, None] * inv[None, :]
    lo = jnp.arange(ROPE_SPLIT, dtype=F32)[:, None] * inv[None, :]
    ch, sh = jnp.cos(hi)[:, None, :], jnp.sin(hi)[:, None, :]
    cl, sl = jnp.cos(lo)[None], jnp.sin(lo)[None]
    cos = (ch * cl - sh * sl).reshape(seq, RET_DK)
    sin = ((sh * cl + ch * sl) * sign).reshape(seq, RET_DK)
    return cos, sin


def _layer(x2d, p2d, batch, seq, mix_norm, w_in, hg_lb_logits, hg_norm, ret_norm, w_branch_a,
           w_branch_b, w_out, ffn_norm, w_rg, b_rg, w_re, b_re, w_gate, w_up, w_down, ple_norm,
           w_ple_gate, w_ple_proj, out_gain):
    t = x2d.shape[0]
    row = lambda v: v.reshape(1, -1).astype(F32)

    proj, hf, wg16, wu16, wd16 = _inproj(x2d, row(mix_norm), w_in.astype(BF16), w_gate, w_up, w_down)
    cos2, sin2 = _rotary_tables(seq)
    ya, yb = _mixers(proj, hf, hg_lb_logits.astype(F32), row(hg_norm), cos2, sin2, row(ret_norm), batch, seq)

    n_r = N_GROUPS + N_EXPERTS
    w_router = jnp.pad(jnp.concatenate([w_rg, w_re], axis=1).astype(BF16), ((0, 0), (0, LANES - n_r)))
    b_router = jnp.zeros((1, LANES), F32).at[0, :n_r].set(jnp.concatenate([b_rg, b_re]))
    x1, xs, route, counts = _merge(
        x2d, ya, yb, proj, w_branch_a.astype(BF16), w_branch_b.astype(BF16), w_out.astype(BF16),
        row(ffn_norm), w_router, b_router)

    n_blocks = t // TOK_TILE
    cnt = counts[:, 0, N_GROUPS:N_GROUPS + N_EXPERTS].astype(I32)
    run_chunks = (cnt + ROW_CHUNK - 1) // ROW_CHUNK
    earlier_e = jnp.tril(jnp.ones((N_EXPERTS, N_EXPERTS), I32), -1)
    earlier_b = jnp.tril(jnp.ones((n_blocks, n_blocks), I32), -1)
    run_local = jnp.sum(run_chunks[:, None, :] * earlier_e[None], axis=2)
    seg_chunks = jnp.sum(run_chunks, axis=0)
    tiles_per = (seg_chunks + CHUNKS_PER_TILE - 1) // CHUNKS_PER_TILE
    seg_start = jnp.sum(tiles_per[None, :] * earlier_e, axis=1) * CHUNKS_PER_TILE
    tile_end = seg_start // CHUNKS_PER_TILE + tiles_per
    run_global = seg_start[None, :] + jnp.sum(run_chunks.T[:, None, :] * earlier_b[None], axis=2).T
    max_chunks = (2 * t) // ROW_CHUNK + n_blocks * N_EXPERTS + N_EXPERTS * (CHUNKS_PER_TILE - 1)
    n_tiles = -(-max_chunks // CHUNKS_PER_TILE) + EXP_AHEAD
    tile_ids = jnp.arange(n_tiles, dtype=I32)
    tile_expert = jnp.minimum(jnp.sum((tile_end[None, :] <= tile_ids[:, None]).astype(I32), axis=1),
                              N_EXPERTS - 1)
    n_used = tile_end[-1:].astype(I32)

    block_ids = jnp.arange(n_blocks, dtype=I32)
    zero_local = LOCAL_CHUNKS - 1
    zero_global = n_tiles * CHUNKS_PER_TILE - 1
    g = jnp.arange(n_tiles * CHUNKS_PER_TILE, dtype=I32)[:, None]
    e_g = jnp.repeat(tile_expert, CHUNKS_PER_TILE)
    pick_e = (e_g[:, None] == jnp.arange(N_EXPERTS, dtype=I32)[None, :]).astype(I32)
    rg, rc, rl = (jnp.sum(pick_e[:, :, None] * tab.T[None, :, :], axis=1)
                  for tab in (run_global, run_chunks, run_local))
    inside = (rg <= g) & (g < rg + rc)
    gather_src = jnp.sum(jnp.where(inside, block_ids[None, :] * LOCAL_CHUNKS + rl + (g - rg), 0), axis=1)
    gather_src = jnp.where(jnp.any(inside, axis=1), gather_src, zero_local).astype(I32)

    lc = jnp.arange(LOCAL_CHUNKS, dtype=I32)[None, :, None]
    inside = (run_local[:, None, :] <= lc) & (lc < (run_local + run_chunks)[:, None, :])
    back_src = jnp.sum(jnp.where(inside, run_global[:, None, :] + lc - run_local[:, None, :], 0), axis=2)
    back_src = jnp.where(jnp.any(inside, axis=2), back_src, zero_global).astype(I32).reshape(-1)

    ys = _experts(tile_expert, n_used, gather_src, xs, wg16, wu16, wd16, n_tiles)
    return _final(back_src, x1, route, p2d, ys, row(ple_norm), w_ple_gate.astype(BF16),
                  w_ple_proj.astype(BF16), out_gain)


def kernel(x, p, mix_norm, w_in, hg_lb_logits, hg_norm, ret_norm, w_branch_a, w_branch_b, w_out,
           ffn_norm, w_router_group, b_router_group, w_router_expert, b_router_expert,
           w_expert_gate, w_expert_up, w_expert_down, ple_norm, w_ple_gate, w_ple_proj, final_norm):
    batch, seq, d = x.shape
    depth = p.shape[0]
    assert depth == 1, "the final rmsnorm is fused into the single layer"
    x2d = x.reshape(batch * seq, d)
    out = _layer(x2d, p[0].reshape(batch * seq, -1), batch, seq, mix_norm[0], w_in[0], hg_lb_logits,
                 hg_norm[0], ret_norm[0], w_branch_a[0], w_branch_b[0], w_out[0], ffn_norm[0],
                 w_router_group[0], b_router_group[0], w_router_expert[0], b_router_expert[0],
                 w_expert_gate[0], w_expert_up[0], w_expert_down[0], ple_norm[0], w_ple_gate[0],
                 w_ple_proj[0], final_norm.reshape(1, -1).astype(F32))
    return out.reshape(batch, seq, d)
```

```python
import jax
import jax.numpy as jnp
from jax import lax
from jax.experimental import pallas as pl
from jax.experimental.pallas import tpu as pltpu

F32 = jnp.float32
BF16 = jnp.bfloat16
I32 = jnp.int32

EPS = 1e-6
D_MODEL = 1024
PLE_DIM = 256
HG_HEADS = 4
HG_DK = 128
HG_WIDTH = HG_HEADS * HG_DK
RET_HEADS = 4
RET_DK = 128
RET_DV = 256
ROPE_BASE = 10000.0
ROPE_SPLIT = 64
IN_TOTAL = 7168
N_GROUPS = 4
EXPERTS_PER_GROUP = 8
N_EXPERTS = 32
D_EXPERT = 256

COL_HQ, COL_HF, COL_HI, COL_HG = 0, 4, 8, 12
COL_RQ, COL_RK = 16, 20
COL_RV, COL_RG = 24, 32
COL_GA, COL_GB = 5, 6

LANES = 128
VMEM_LIMIT = 56 * 1024 * 1024

HG_CHUNK = 64
HG_SUB = 8
HG_UNROLL = 16
HG_STATE_UNROLL = 16
HG_NORM_ROWS = 1024
RET_CHUNK = 128
RET_STATE_UNROLL = 8
SEQ_TILE = 1024
TOK_TILE = 512
EXP_TILE = 512
EXP_SUB = 512
EXP_AHEAD = 3
EXP_SLOTS = EXP_AHEAD + 1
ROW_CHUNK = 16
CHUNKS_PER_TILE = EXP_TILE // ROW_CHUNK
LOCAL_ROWS = 2 * TOK_TILE + N_EXPERTS * ROW_CHUNK
LOCAL_CHUNKS = LOCAL_ROWS // ROW_CHUNK
SORT_SLAB = 256
FIN_AHEAD = 2
FIN_SLOTS = FIN_AHEAD + 1


def _cparams(sem):
    return pltpu.CompilerParams(dimension_semantics=sem, vmem_limit_bytes=VMEM_LIMIT)


def _rms(x, g):
    return x * lax.rsqrt(jnp.mean(x * x, axis=-1, keepdims=True) + EPS) * g


def _sigmoid(x):
    return 1.0 / (1.0 + jnp.exp(-x))


def _silu(x):
    return x * _sigmoid(x)


def _split3(x):
    hi = x.astype(BF16)
    r1 = x - hi.astype(F32)
    mid = r1.astype(BF16)
    lo = (r1 - mid.astype(F32)).astype(BF16)
    return hi, mid, lo


def _dot(a, b):
    return jnp.dot(a, b, preferred_element_type=F32)


def _dot_nt(a, b):
    return lax.dot_general(a, b, (((1,), (1,)), ((), ())), preferred_element_type=F32)


def _dot_tn(a, b):
    return lax.dot_general(a, b, (((0,), (0,)), ((), ())), preferred_element_type=F32)


def _inproj_kernel(x_ref, g_ref, w_ref, wg_ref, wu_ref, wd_ref, proj_ref, hf_ref,
                   wg16_ref, wu16_ref, wd16_ref):
    wg16_ref[...] = wg_ref[...].astype(BF16)
    wu16_ref[...] = wu_ref[...].astype(BF16)
    wd16_ref[...] = wd_ref[...].astype(BF16)

    h = _rms(x_ref[...], g_ref[...]).astype(BF16)
    tn = HG_WIDTH
    for j in range(IN_TOTAL // tn):
        acc = _dot(h, w_ref[:, j * tn:(j + 1) * tn])
        proj_ref[:, j * tn:(j + 1) * tn] = acc.astype(BF16)
        if j * tn == COL_HF * LANES:
            hf_ref[...] = acc


def _inproj(x2d, gain, w_bf16, w_gate, w_up, w_down):
    t = x2d.shape[0]
    steps = t // TOK_TILE
    assert N_EXPERTS % steps == 0, "expert weights are converted in equal shares per grid step"
    share = N_EXPERTS // steps
    up_spec = pl.BlockSpec((share, D_MODEL, D_EXPERT), lambda i: (i, 0, 0))
    down_spec = pl.BlockSpec((share, D_EXPERT, D_MODEL), lambda i: (i, 0, 0))
    return pl.pallas_call(
        _inproj_kernel,
        grid=(steps,),
        in_specs=[
            pl.BlockSpec((TOK_TILE, D_MODEL), lambda i: (i, 0)),
            pl.BlockSpec((1, D_MODEL), lambda i: (0, 0)),
            pl.BlockSpec((D_MODEL, IN_TOTAL), lambda i: (0, 0), pipeline_mode=pl.Buffered(1)),
            up_spec, up_spec, down_spec,
        ],
        out_specs=[
            pl.BlockSpec((TOK_TILE, IN_TOTAL), lambda i: (i, 0)),
            pl.BlockSpec((TOK_TILE, HG_WIDTH), lambda i: (i, 0)),
            up_spec, up_spec, down_spec,
        ],
        out_shape=[
            jax.ShapeDtypeStruct((t, IN_TOTAL), BF16),
            jax.ShapeDtypeStruct((t, HG_WIDTH), F32),
            jax.ShapeDtypeStruct(w_gate.shape, BF16),
            jax.ShapeDtypeStruct(w_up.shape, BF16),
            jax.ShapeDtypeStruct(w_down.shape, BF16),
        ],
        compiler_params=_cparams(("arbitrary",)),
        name="inproj",
    )(x2d, gain, w_bf16, w_gate, w_up, w_down)


def _hgrn_stages(lbl_ref, q_ref, f_ref, i_ref, g_ref, ng_ref, o_ref,
                 st_ref, b_s, k_s, v_s, oi_s, qe_s, kv_s, dec_s):
    c = HG_CHUNK
    nsub = c // HG_SUB

    logits = lbl_ref[...]
    e = jnp.exp(logits - jnp.max(logits, axis=0, keepdims=True))
    lb = e[0:1] / jnp.sum(e, axis=0, keepdims=True)
    one_m_lb = jnp.sum(e[1:], axis=0, keepdims=True) / jnp.sum(e, axis=0, keepdims=True)
    ng = ng_ref[...]

    row = lax.broadcasted_iota(I32, (c, c), 0)
    col = lax.broadcasted_iota(I32, (c, c), 1)
    tri = jnp.where(row >= col, 1.0, 0.0).astype(BF16)
    row_k = lax.broadcasted_iota(I32, (c, HG_DK), 0)
    sub_row = lax.broadcasted_iota(I32, (HG_SUB, HG_DK), 0)
    masked = jnp.float32(-1e30)

    def bcast_row(ref, r, rows):
        return jnp.broadcast_to(ref[pl.ds(r, 1), :], (rows, HG_DK))

    def prep(ci, slot):
        r0 = pl.multiple_of(ci * c, c)
        z = f_ref[pl.ds(r0, c), :]
        ez = jnp.exp(-jnp.abs(z))
        rz = 1.0 / (1.0 + ez)
        pos = z >= 0.0
        logf = jnp.log2(lb + one_m_lb * jnp.where(pos, rz, ez * rz))
        kk = one_m_lb * jnp.where(pos, ez * rz, rz)
        q = _silu(q_ref[pl.ds(r0, c), :].astype(F32))
        v = i_ref[pl.ds(r0, c), :].astype(F32)
        k_s[slot] = kk
        v_s[slot] = v
        return dict(ci=ci, r0=r0, slot=slot, kk=kk, q=q, v16=v.astype(BF16), parts=_split3(logf))

    def cumulate(s):
        hi, mid, lo = s.pop("parts")
        b = (_dot(tri, lo) + _dot(tri, mid)) + _dot(tri, hi)
        b_s[s["slot"]] = b
        qe_s[pl.ds(s["r0"], c), :] = (s["q"] * jnp.exp2(b)).astype(BF16)
        s["b"] = b
        return s

    def off_diagonal(s):
        b, q, kk = s["b"], s["q"], s["kk"]
        bs_ref = b_s.at[s["slot"]]
        a_rows = [jnp.zeros((HG_SUB, c), F32)]
        for i in range(1, nsub):
            bi = bcast_row(bs_ref, i * HG_SUB - 1, c)
            qi = q[i * HG_SUB:(i + 1) * HG_SUB, :] * jnp.exp2(b[i * HG_SUB:(i + 1) * HG_SUB, :] - bi[:HG_SUB])
            ki = kk * jnp.exp2(jnp.where(row_k < i * HG_SUB, bi - b, masked))
            a_rows.append(_dot_nt(qi.astype(BF16), ki.astype(BF16)))
        s["a_off"] = jnp.concatenate(a_rows, axis=0)
        return s

    def apply_values(s):
        b, kk = s["b"], s["kk"]
        blast = b[c - 1:c, :]
        s["o"] = _dot(s.pop("a_off").astype(BF16), s["v16"])
        kd = kk * jnp.exp2(blast - b)
        kv_s[s["ci"]] = _dot_tn(s["v16"], kd.astype(BF16))
        dec_s[s["ci"]] = jnp.broadcast_to(jnp.exp2(blast), (HG_SUB, HG_DK))
        return s

    def diagonal(s):
        b, q = s["b"], s["q"]
        bs_ref, ks_ref, vs_ref = b_s.at[s["slot"]], k_s.at[s["slot"]], v_s.at[s["slot"]]
        d_blocks = []
        for i in range(nsub):
            sl = slice(i * HG_SUB, (i + 1) * HG_SUB)
            bt, qt = b[sl, :], q[sl, :]
            acc = jnp.zeros((HG_SUB, HG_DK), F32)
            for j in range(HG_SUB):
                r = i * HG_SUB + j
                arg = jnp.where(sub_row >= j, bt - bcast_row(bs_ref, r, HG_SUB), masked)
                g = jnp.exp2(arg) * (qt * bcast_row(ks_ref, r, HG_SUB))
                acc = acc + jnp.sum(g, axis=-1, keepdims=True) * bcast_row(vs_ref, r, HG_SUB)
            d_blocks.append(acc)
        oi_s[pl.ds(s["r0"], c), :] = s["o"] + jnp.concatenate(d_blocks, axis=0)

    def carried_group(gi):
        st = st_ref[...]
        outs = []
        for j in range(HG_STATE_UNROLL):
            ci = gi * HG_STATE_UNROLL + j
            r0 = pl.multiple_of(ci * c, c)
            outs.append((r0, _dot_nt(qe_s[pl.ds(r0, c), :], st.astype(BF16))))
            st = st * dec_s[ci][0:1, :] + kv_s[ci]
        st_ref[...] = st
        for r0, os in outs:
            oi_s[pl.ds(r0, c), :] += os

    def finish(ri):
        r0 = pl.multiple_of(ri * HG_NORM_ROWS, HG_NORM_ROWS)
        y = _rms(oi_s[pl.ds(r0, HG_NORM_ROWS), :], ng) * _silu(g_ref[pl.ds(r0, HG_NORM_ROWS), :].astype(F32))
        o_ref[pl.ds(r0, HG_NORM_ROWS), :] = y.astype(o_ref.dtype)

    return prep, (cumulate, off_diagonal, apply_values, diagonal), carried_group, finish


HG_SCRATCH = [
    pltpu.VMEM((HG_DK, HG_DK), F32),
    pltpu.VMEM((HG_UNROLL, HG_CHUNK, HG_DK), F32),
    pltpu.VMEM((HG_UNROLL, HG_CHUNK, HG_DK), F32),
    pltpu.VMEM((HG_UNROLL, HG_CHUNK, HG_DK), F32),
    pltpu.VMEM((SEQ_TILE, HG_DK), F32),
    pltpu.VMEM((SEQ_TILE, HG_DK), BF16),
    pltpu.VMEM((SEQ_TILE // HG_CHUNK, HG_DK, HG_DK), F32),
    pltpu.VMEM((SEQ_TILE // HG_CHUNK, HG_SUB, HG_DK), F32),
]


def _ret_stages(q_ref, k_ref, v_ref, g_ref, cos_ref, sin_ref, ng_ref, o_ref,
                r_ref, oi_s, q16_s, kv_s):
    c = RET_CHUNK

    hf = jnp.full((1, 1), pl.program_id(1), I32).astype(F32)
    lg = jnp.log1p(-jnp.exp2(-5.0 - hf))
    ti = lax.broadcasted_iota(I32, (c, c), 0)
    si = lax.broadcasted_iota(I32, (c, c), 1)
    rel = (ti - si).astype(F32)
    intra = jnp.where(ti >= si, jnp.exp(jnp.maximum(rel, 0.0) * lg), 0.0)
    idx = lax.broadcasted_iota(I32, (c, 1), 0).astype(F32)
    inter = jnp.exp((idx + 1.0) * lg)
    to_state = jnp.exp((c - 1.0 - idx) * lg)
    chunk_decay = jnp.exp(float(c) * lg)
    ng = ng_ref[...]
    half = RET_DK // 2

    def prep(ci):
        r0 = pl.multiple_of(ci * c, c)
        cos = cos_ref[pl.ds(r0, c), :]
        sin = sin_ref[pl.ds(r0, c), :]
        q = q_ref[pl.ds(r0, c), :].astype(F32)
        k = k_ref[pl.ds(r0, c), :].astype(F32)
        qr = (q * cos + pltpu.roll(q, half, 1) * sin) * (RET_DK ** -0.5)
        kr = k * cos + pltpu.roll(k, half, 1) * sin
        q16 = qr.astype(BF16)
        q16_s[pl.ds(r0, c), :] = q16
        return dict(ci=ci, r0=r0, q16=q16, k16=kr.astype(BF16), kts=(kr * to_state).astype(BF16),
                    v16=v_ref[pl.ds(r0, c), :])

    def scores(s):
        s["att"] = (_dot_nt(s.pop("q16"), s.pop("k16")) * intra).astype(BF16)
        return s

    def apply_values(s):
        oi_s[pl.ds(s["r0"], c), :] = _dot(s["att"], s["v16"])
        kv_s[s["ci"]] = _dot_tn(s["kts"], s["v16"])
        return s

    def carried_group(gi):
        r = r_ref[...]
        outs = []
        for j in range(RET_STATE_UNROLL):
            ci = gi * RET_STATE_UNROLL + j
            r0 = pl.multiple_of(ci * c, c)
            outs.append((r0, _dot(q16_s[pl.ds(r0, c), :], r.astype(BF16))))
            r = chunk_decay * r + kv_s[ci]
        r_ref[...] = r
        for r0, qr_state in outs:
            o = oi_s[pl.ds(r0, c), :] + qr_state * inter
            y = _rms(o, ng) * _silu(g_ref[pl.ds(r0, c), :].astype(F32))
            o_ref[pl.ds(r0, c), :] = y.astype(o_ref.dtype)

    return prep, (scores, apply_values), carried_group


RET_SCRATCH = [
    pltpu.VMEM((RET_DK, RET_DV), F32),
    pltpu.VMEM((SEQ_TILE, RET_DV), F32),
    pltpu.VMEM((SEQ_TILE, RET_DK), BF16),
    pltpu.VMEM((SEQ_TILE // RET_CHUNK, RET_DK, RET_DV), F32),
]


def _mixers_kernel(*refs):
    hg_refs = refs[:6] + refs[13:14] + refs[15:15 + len(HG_SCRATCH)]
    ret_refs = refs[6:13] + refs[14:15] + refs[15 + len(HG_SCRATCH):]
    st_ref, r_ref = hg_refs[7], ret_refs[8]
    n_tokens = refs[1].shape[0]

    @pl.when(pl.program_id(2) == 0)
    def _():
        st_ref[...] = jnp.zeros_like(st_ref)
        r_ref[...] = jnp.zeros_like(r_ref)

    hg_prep, hg_local, hg_carried, hg_finish = _hgrn_stages(*hg_refs)
    ret_prep, ret_local, ret_carried = _ret_stages(*ret_refs)
    group_tokens = HG_UNROLL * HG_CHUNK
    ret_unroll = group_tokens // RET_CHUNK

    def local_group(gi, carry):
        hs = [hg_prep(gi * HG_UNROLL + slot, slot) for slot in range(HG_UNROLL)]
        rs = [ret_prep(gi * ret_unroll + j) for j in range(ret_unroll)]
        cumulate, off_diagonal, apply_values, diagonal = hg_local
        scores, ret_apply = ret_local
        hs = [cumulate(s) for s in hs]
        rs = [scores(s) for s in rs]
        hs = [off_diagonal(s) for s in hs]
        rs = [ret_apply(s) for s in rs]
        hs = [apply_values(s) for s in hs]
        for s in hs:
            diagonal(s)
        return carry

    lax.fori_loop(0, n_tokens // group_tokens, local_group, 0)

    state_tokens = HG_STATE_UNROLL * HG_CHUNK
    assert state_tokens == RET_STATE_UNROLL * RET_CHUNK

    def carried_group(gi, carry):
        hg_carried(gi)
        ret_carried(gi)
        return carry

    lax.fori_loop(0, n_tokens // state_tokens, carried_group, 0)

    def finish(ri, carry):
        hg_finish(ri)
        return carry

    lax.fori_loop(0, n_tokens // HG_NORM_ROWS, finish, 0)


def _mixers(proj, hf, lb_logits, hg_norm, cos2, sin2, ret_norm, batch, seq):
    ns = seq // SEQ_TILE
    tok = lambda b, h, s: b * ns + s
    hg_col = lambda base: pl.BlockSpec((SEQ_TILE, HG_DK), lambda b, h, s: (tok(b, h, s), base + h))
    return pl.pallas_call(
        _mixers_kernel,
        grid=(batch, HG_HEADS, ns),
        in_specs=[
            pl.BlockSpec((2, HG_DK), lambda b, h, s: (0, h)),
            hg_col(COL_HQ), hg_col(0), hg_col(COL_HI), hg_col(COL_HG),
            pl.BlockSpec((1, HG_DK), lambda b, h, s: (0, 0)),
            hg_col(COL_RQ), hg_col(COL_RK),
            pl.BlockSpec((SEQ_TILE, RET_DV), lambda b, h, s: (tok(b, h, s), COL_RV // 2 + h)),
            pl.BlockSpec((SEQ_TILE, RET_DV), lambda b, h, s: (tok(b, h, s), COL_RG // 2 + h)),
            pl.BlockSpec((SEQ_TILE, RET_DK), lambda b, h, s: (s, 0)),
            pl.BlockSpec((SEQ_TILE, RET_DK), lambda b, h, s: (s, 0)),
            pl.BlockSpec((1, RET_DV), lambda b, h, s: (0, 0)),
        ],
        out_specs=[
            pl.BlockSpec((SEQ_TILE, HG_DK), lambda b, h, s: (tok(b, h, s), h)),
            pl.BlockSpec((SEQ_TILE, RET_DV), lambda b, h, s: (tok(b, h, s), h)),
        ],
        out_shape=[
            jax.ShapeDtypeStruct((batch * seq, HG_WIDTH), BF16),
            jax.ShapeDtypeStruct((batch * seq, RET_HEADS * RET_DV), BF16),
        ],
        scratch_shapes=HG_SCRATCH + RET_SCRATCH,
        compiler_params=_cparams(("arbitrary", "arbitrary", "arbitrary")),
        name="mixers",
    )(lb_logits, proj, hf, proj, proj, hg_norm, proj, proj, proj, proj, cos2, sin2, ret_norm)


def _merge_kernel(x_ref, ya_ref, yb_ref, ga_ref, gb_ref, wa_ref, wb_ref, wo_ref, fg_ref,
                  wr_ref, br_ref, x1_ref, xs_ref, route_ref, cnt_ref, h2b_s, rows_s):
    tm = x_ref.shape[0]

    @pl.when(pl.program_id(0) == 0)
    def _():
        h2b_s[...] = jnp.zeros_like(h2b_s)
        rows_s[...] = jnp.full(rows_s.shape, -1.0, F32)

    def sort_previous(lo, hi):
        slab_row = lax.broadcasted_iota(I32, (SORT_SLAB, tm), 0).astype(F32)
        for r0 in range(lo, hi, SORT_SLAB):
            sel = ((slab_row == rows_s[0:1, :] - float(r0)) | (slab_row == rows_s[1:2, :] - float(r0)))
            xs_ref[r0:r0 + SORT_SLAB, :] = _dot(jnp.where(sel, 1.0, 0.0).astype(BF16),
                                                h2b_s[...]).astype(BF16)

    merged = (_sigmoid(ga_ref[...].astype(F32)) * _dot(ya_ref[...], wa_ref[...])
              + _sigmoid(gb_ref[...].astype(F32)) * _dot(yb_ref[...], wb_ref[...]))
    x1 = x_ref[...] + _dot(merged.astype(BF16), wo_ref[...])
    x1_ref[...] = x1
    h2 = _rms(x1, fg_ref[...])

    h2b = h2.astype(BF16)
    logits = _dot(h2b, wr_ref[...]) + br_ref[...]
    sort_previous(0, LOCAL_ROWS)

    lane = lax.broadcasted_iota(I32, (tm, LANES), 1)
    neg = jnp.float32(-jnp.inf)
    big = jnp.int32(1 << 30)
    is_g = lane < N_GROUPS
    gl = jnp.where(is_g, logits, neg)
    gmax = jnp.max(gl, axis=-1, keepdims=True)
    g_idx = jnp.min(jnp.where(gl == gmax, lane, big), axis=-1, keepdims=True)
    g_w = 1.0 / jnp.sum(jnp.where(is_g, jnp.exp(gl - gmax), 0.0), axis=-1, keepdims=True)

    ex = lane - N_GROUPS
    in_grp = (ex >= g_idx * EXPERTS_PER_GROUP) & (ex < (g_idx + 1) * EXPERTS_PER_GROUP)
    el = jnp.where(in_grp, logits, neg)
    m1 = jnp.max(el, axis=-1, keepdims=True)
    e1 = jnp.min(jnp.where(el == m1, ex, big), axis=-1, keepdims=True)
    el2 = jnp.where(ex == e1, neg, el)
    m2 = jnp.max(el2, axis=-1, keepdims=True)
    e2 = jnp.min(jnp.where(el2 == m2, ex, big), axis=-1, keepdims=True)
    p2 = jnp.exp(m2 - m1)
    w1 = g_w / (1.0 + p2)
    w2 = g_w * p2 / (1.0 + p2)

    oh1 = ex == e1
    oh2 = ex == e2
    oh = jnp.where(oh1 | oh2, 1.0, 0.0)
    ri = lax.broadcasted_iota(I32, (tm, tm), 0)
    ci = lax.broadcasted_iota(I32, (tm, tm), 1)
    strict = jnp.where(ri > ci, 1.0, 0.0).astype(BF16)
    local_rank = _dot(strict, oh.astype(BF16))
    cnt = jnp.sum(oh, axis=0, keepdims=True)
    run_chunks = jnp.floor((cnt + (ROW_CHUNK - 1.0)) * (1.0 / ROW_CHUNK))
    ui = lax.broadcasted_iota(I32, (LANES, LANES), 0)
    uj = lax.broadcasted_iota(I32, (LANES, LANES), 1)
    before = jnp.where(ui < uj, 1.0, 0.0).astype(BF16)
    run_start = _dot(jnp.broadcast_to(run_chunks, (8, LANES)).astype(BF16), before)[0:1] * ROW_CHUNK
    slot = run_start + local_rank
    pos1 = jnp.sum(jnp.where(oh1, slot, 0.0), axis=-1, keepdims=True)
    pos2 = jnp.sum(jnp.where(oh2, slot, 0.0), axis=-1, keepdims=True)
    cnt_ref[0] = jnp.broadcast_to(cnt, (8, LANES))

    route = jnp.where(lane == 0, e1.astype(F32), 0.0)
    route = jnp.where(lane == 1, e2.astype(F32), route)
    route = jnp.where(lane == 2, w1, route)
    route = jnp.where(lane == 3, w2, route)
    route = jnp.where(lane == 4, pos1, route)
    route = jnp.where(lane == 5, pos2, route)
    route_ref[...] = route

    hi1 = jnp.floor(pos1 * (1.0 / 256.0))
    hi2 = jnp.floor(pos2 * (1.0 / 256.0))
    digits = jnp.where(lane == 0, hi1, 0.0)
    digits = jnp.where(lane == 1, pos1 - 256.0 * hi1, digits)
    digits = jnp.where(lane == 2, hi2, digits)
    digits = jnp.where(lane == 3, pos2 - 256.0 * hi2, digits)
    pick = jnp.where(lax.broadcasted_iota(I32, (8, LANES), 0) == lax.broadcasted_iota(I32, (8, LANES), 1),
                     1.0, 0.0).astype(BF16)
    rows = _dot_nt(pick, digits.astype(BF16))

    h2b_s[...] = h2b
    rows_s[0:1, :] = rows[0:1] * 256.0 + rows[1:2]
    rows_s[1:2, :] = rows[2:3] * 256.0 + rows[3:4]


def _merge(x2d, ya, yb, proj, wa, wb, wo, ffn_g, w_router, b_router):
    t = x2d.shape[0]
    nb = t // TOK_TILE
    const = lambda *shape: pl.BlockSpec(shape, lambda i: (0,) * len(shape))
    cur = lambda i: jnp.minimum(i, nb - 1)
    prev = lambda i: jnp.maximum(i - 1, 0)
    return pl.pallas_call(
        _merge_kernel,
        grid=(nb + 1,),
        in_specs=[
            pl.BlockSpec((TOK_TILE, D_MODEL), lambda i: (cur(i), 0)),
            pl.BlockSpec((TOK_TILE, HG_WIDTH), lambda i: (cur(i), 0)),
            pl.BlockSpec((TOK_TILE, D_MODEL), lambda i: (cur(i), 0)),
            pl.BlockSpec((TOK_TILE, D_MODEL), lambda i: (cur(i), COL_GA)),
            pl.BlockSpec((TOK_TILE, D_MODEL), lambda i: (cur(i), COL_GB)),
            const(HG_WIDTH, D_MODEL),
            const(D_MODEL, D_MODEL),
            const(D_MODEL, D_MODEL),
            const(1, D_MODEL),
            const(D_MODEL, LANES),
            const(1, LANES),
        ],
        out_specs=[
            pl.BlockSpec((TOK_TILE, D_MODEL), lambda i: (cur(i), 0)),
            pl.BlockSpec((LOCAL_ROWS, D_MODEL), lambda i: (prev(i), 0)),
            pl.BlockSpec((TOK_TILE, LANES), lambda i: (cur(i), 0)),
            pl.BlockSpec((1, 8, LANES), lambda i: (cur(i), 0, 0)),
        ],
        out_shape=[
            jax.ShapeDtypeStruct((t, D_MODEL), F32),
            jax.ShapeDtypeStruct((t // TOK_TILE * LOCAL_ROWS, D_MODEL), BF16),
            jax.ShapeDtypeStruct((t, LANES), F32),
            jax.ShapeDtypeStruct((t // TOK_TILE, 8, LANES), F32),
        ],
        scratch_shapes=[pltpu.VMEM((TOK_TILE, D_MODEL), BF16), pltpu.VMEM((8, TOK_TILE), F32)],
        compiler_params=_cparams(("arbitrary",)),
        name="merge_route",
    )(x2d, ya, yb, proj, proj, wa, wb, wo, ffn_g, w_router, b_router)


def _expert_kernel(te_ref, nt_ref, src_ref, xs_ref, wg_ref, wu_ref, wd_ref, ys_ref, xbuf, sem):
    i = pl.program_id(0)
    nt = nt_ref[0]

    def gather(tile, slot):
        copies = []
        for c in range(CHUNKS_PER_TILE):
            row = pl.multiple_of(src_ref[tile * CHUNKS_PER_TILE + c] * ROW_CHUNK, ROW_CHUNK)
            copies.append(pltpu.make_async_copy(
                xs_ref.at[pl.ds(row, ROW_CHUNK)],
                xbuf.at[slot, pl.ds(c * ROW_CHUNK, ROW_CHUNK)], sem.at[slot]))
        return copies

    @pl.when(i == 0)
    def _():
        for k in range(EXP_AHEAD):
            for cp in gather(k, k):
                cp.start()

    @pl.when(i < nt + EXP_AHEAD)
    def _():
        for cp in gather(i, i % EXP_SLOTS):
            cp.wait()

    @pl.when(i < nt)
    def _():
        slot = i % EXP_SLOTS
        wg, wu, wd = wg_ref[0], wu_ref[0], wd_ref[0]
        subs = [pl.ds(r, EXP_SUB) for r in range(0, EXP_TILE, EXP_SUB)]
        xs = [xbuf[slot, s, :] for s in subs]
        gates = [(_dot(x, wg), _dot(x, wu)) for x in xs]
        for c, cp in enumerate(gather(i + EXP_AHEAD, (i + EXP_AHEAD) % EXP_SLOTS)):
            cp.start(priority=c % 2)
        hidden = [(_silu(a) * u).astype(BF16) for a, u in gates]
        for s, h in zip(subs, hidden):
            ys_ref[s, :] = _dot(h, wd).astype(ys_ref.dtype)

    @pl.when(i >= nt)
    def _():
        ys_ref[...] = jnp.zeros_like(ys_ref)


def _experts(tile_expert, n_tiles_used, src_chunk, xs, wg, wu, wd, n_tiles):
    grid_spec = pltpu.PrefetchScalarGridSpec(
        num_scalar_prefetch=3,
        grid=(n_tiles,),
        in_specs=[
            pl.BlockSpec(memory_space=pl.ANY),
            pl.BlockSpec((1, D_MODEL, D_EXPERT), lambda i, te, nt, src: (te[i], 0, 0)),
            pl.BlockSpec((1, D_MODEL, D_EXPERT), lambda i, te, nt, src: (te[i], 0, 0)),
            pl.BlockSpec((1, D_EXPERT, D_MODEL), lambda i, te, nt, src: (te[i], 0, 0)),
        ],
        out_specs=pl.BlockSpec((EXP_TILE, D_MODEL), lambda i, te, nt, src: (i, 0)),
        scratch_shapes=[
            pltpu.VMEM((EXP_SLOTS, EXP_TILE, D_MODEL), BF16),
            pltpu.SemaphoreType.DMA((EXP_SLOTS,)),
        ],
    )
    return pl.pallas_call(
        _expert_kernel,
        grid_spec=grid_spec,
        out_shape=jax.ShapeDtypeStruct((n_tiles * EXP_TILE, D_MODEL), BF16),
        compiler_params=_cparams(("arbitrary",)),
        name="experts",
    )(tile_expert, n_tiles_used, src_chunk, xs, wg, wu, wd)


def _final_kernel(src_ref, x1_ref, route_ref, p_ref, ys_ref, pg_ref, wpg_ref, wpp_ref, fg_ref,
                  o_ref, ybuf, sem):
    tm = x1_ref.shape[0]
    i = pl.program_id(0)
    nb = pl.num_programs(0)

    def gather(block, slot):
        copies = []
        for c in range(LOCAL_CHUNKS):
            row = pl.multiple_of(src_ref[block * LOCAL_CHUNKS + c] * ROW_CHUNK, ROW_CHUNK)
            copies.append(pltpu.make_async_copy(
                ys_ref.at[pl.ds(row, ROW_CHUNK)],
                ybuf.at[slot, pl.ds(c * ROW_CHUNK, ROW_CHUNK)], sem.at[slot]))
        return copies

    @pl.when(i == 0)
    def _():
        for k in range(FIN_AHEAD):
            for cp in gather(k % nb, k):
                cp.start()

    slot = i % FIN_SLOTS
    for cp in gather(i, slot):
        cp.wait()

    ple = _dot(p_ref[...].astype(BF16), wpp_ref[...])

    route = route_ref[...]
    w1, w2 = route[:, 2:3], route[:, 3:4]
    pos1, pos2 = route[:, 4:5], route[:, 5:6]
    slab_col = lax.broadcasted_iota(I32, (tm, SORT_SLAB), 1).astype(F32)
    moe = jnp.zeros((tm, D_MODEL), F32)
    for k0 in range(0, LOCAL_ROWS, SORT_SLAB):
        sel = jnp.where(slab_col == pos1 - float(k0), w1,
                        jnp.where(slab_col == pos2 - float(k0), w2, 0.0)).astype(BF16)
        moe = moe + _dot(sel, ybuf[slot, k0:k0 + SORT_SLAB, :])
    for c, cp in enumerate(gather((i + FIN_AHEAD) % nb, (i + FIN_AHEAD) % FIN_SLOTS)):
        cp.start(priority=c % 2)
    x2 = x1_ref[...] + moe
    hp = _rms(x2, pg_ref[...]).astype(BF16)
    gate = _sigmoid(_dot(hp, wpg_ref[...]))
    x3 = x2 + gate * ple
    o_ref[...] = _rms(x3, fg_ref[...])

    @pl.when(i == nb - 1)
    def _():
        for k in range(1, FIN_AHEAD + 1):
            for cp in gather(0, (i + k) % FIN_SLOTS):
                cp.wait()


def _final(src_chunk, x1, route, p2d, ys, ple_g, wpg, wpp, final_g):
    t = x1.shape[0]
    const = lambda *shape: pl.BlockSpec(shape, lambda i, src: (0,) * len(shape))
    grid_spec = pltpu.PrefetchScalarGridSpec(
        num_scalar_prefetch=1,
        grid=(t // TOK_TILE,),
        in_specs=[
            pl.BlockSpec((TOK_TILE, D_MODEL), lambda i, src: (i, 0)),
            pl.BlockSpec((TOK_TILE, LANES), lambda i, src: (i, 0)),
            pl.BlockSpec((TOK_TILE, PLE_DIM), lambda i, src: (i, 0)),
            pl.BlockSpec(memory_space=pl.ANY),
            const(1, D_MODEL),
            const(D_MODEL, D_MODEL),
            const(PLE_DIM, D_MODEL),
            const(1, D_MODEL),
        ],
        out_specs=pl.BlockSpec((TOK_TILE, D_MODEL), lambda i, src: (i, 0)),
        scratch_shapes=[pltpu.VMEM((FIN_SLOTS, LOCAL_ROWS, D_MODEL), BF16),
                        pltpu.SemaphoreType.DMA((FIN_SLOTS,))],
    )
    return pl.pallas_call(
        _final_kernel,
        grid_spec=grid_spec,
        out_shape=jax.ShapeDtypeStruct((t, D_MODEL), F32),
        compiler_params=_cparams(("arbitrary",)),
        name="combine_ple_final",
    )(src_chunk, x1, route, p2d, ys, ple_g, wpg, wpp, final_g)


def _rotary_tables(seq):
    inv = ROPE_BASE ** (-jnp.arange(0, RET_DK, 2, dtype=F32) / RET_DK)
    inv = jnp.concatenate([inv, inv])
    sign = jnp.where(jnp.arange(RET_DK) < RET_DK // 2, -1.0, 1.0).astype(F32)
    hi = (jnp.arange(seq // ROPE_SPLIT, dtype=F32) * ROPE_SPLIT)[:, None] * inv[None, :]
    lo = jnp.arange(ROPE_SPLIT, dtype=F32)[:, None] * inv[None, :]
    ch, sh = jnp.cos(hi)[:, None, :], jnp.sin(hi)[:, None, :]
    cl, sl = jnp.cos(lo)[None], jnp.sin(lo)[None]
    cos = (ch * cl - sh * sl).reshape(seq, RET_DK)
    sin = ((sh * cl + ch * sl) * sign).reshape(seq, RET_DK)
    return cos, sin


def _layer(x2d, p2d, batch, seq, mix_norm, w_in, hg_lb_logits, hg_norm, ret_norm, w_branch_a,
           w_branch_b, w_out, ffn_norm, w_rg, b_rg, w_re, b_re, w_gate, w_up, w_down, ple_norm,
           w_ple_gate, w_ple_proj, out_gain):
    t = x2d.shape[0]
    row = lambda v: v.reshape(1, -1).astype(F32)

    proj, hf, wg16, wu16, wd16 = _inproj(x2d, row(mix_norm), w_in.astype(BF16), w_gate, w_up, w_down)
    cos2, sin2 = _rotary_tables(seq)
    ya, yb = _mixers(proj, hf, hg_lb_logits.astype(F32), row(hg_norm), cos2, sin2, row(ret_norm), batch, seq)

    n_r = N_GROUPS + N_EXPERTS
    w_router = jnp.pad(jnp.concatenate([w_rg, w_re], axis=1).astype(BF16), ((0, 0), (0, LANES - n_r)))
    b_router = jnp.zeros((1, LANES), F32).at[0, :n_r].set(jnp.concatenate([b_rg, b_re]))
    x1, xs, route, counts = _merge(
        x2d, ya, yb, proj, w_branch_a.astype(BF16), w_branch_b.astype(BF16), w_out.astype(BF16),
        row(ffn_norm), w_router, b_router)

    n_blocks = t // TOK_TILE
    cnt = counts[:, 0, N_GROUPS:N_GROUPS + N_EXPERTS].astype(I32)
    run_chunks = (cnt + ROW_CHUNK - 1) // ROW_CHUNK
    earlier_e = jnp.tril(jnp.ones((N_EXPERTS, N_EXPERTS), I32), -1)
    earlier_b = jnp.tril(jnp.ones((n_blocks, n_blocks), I32), -1)
    run_local = jnp.sum(run_chunks[:, None, :] * earlier_e[None], axis=2)
    seg_chunks = jnp.sum(run_chunks, axis=0)
    tiles_per = (seg_chunks + CHUNKS_PER_TILE - 1) // CHUNKS_PER_TILE
    seg_start = jnp.sum(tiles_per[None, :] * earlier_e, axis=1) * CHUNKS_PER_TILE
    tile_end = seg_start // CHUNKS_PER_TILE + tiles_per
    run_global = seg_start[None, :] + jnp.sum(run_chunks.T[:, None, :] * earlier_b[None], axis=2).T
    max_chunks = (2 * t) // ROW_CHUNK + n_blocks * N_EXPERTS + N_EXPERTS * (CHUNKS_PER_TILE - 1)
    n_tiles = -(-max_chunks // CHUNKS_PER_TILE) + EXP_AHEAD
    tile_ids = jnp.arange(n_tiles, dtype=I32)
    tile_expert = jnp.minimum(jnp.sum((tile_end[None, :] <= tile_ids[:, None]).astype(I32), axis=1),
                              N_EXPERTS - 1)
    n_used = tile_end[-1:].astype(I32)

    block_ids = jnp.arange(n_blocks, dtype=I32)
    zero_local = LOCAL_CHUNKS - 1
    zero_global = n_tiles * CHUNKS_PER_TILE - 1
    g = jnp.arange(n_tiles * CHUNKS_PER_TILE, dtype=I32)[:, None]
    e_g = jnp.repeat(tile_expert, CHUNKS_PER_TILE)
    pick_e = (e_g[:, None] == jnp.arange(N_EXPERTS, dtype=I32)[None, :]).astype(I32)
    rg, rc, rl = (jnp.sum(pick_e[:, :, None] * tab.T[None, :, :], axis=1)
                  for tab in (run_global, run_chunks, run_local))
    inside = (rg <= g) & (g < rg + rc)
    gather_src = jnp.sum(jnp.where(inside, block_ids[None, :] * LOCAL_CHUNKS + rl + (g - rg), 0), axis=1)
    gather_src = jnp.where(jnp.any(inside, axis=1), gather_src, zero_local).astype(I32)

    lc = jnp.arange(LOCAL_CHUNKS, dtype=I32)[None, :, None]
    inside = (run_local[:, None, :] <= lc) & (lc < (run_local + run_chunks)[:, None, :])
    back_src = jnp.sum(jnp.where(inside, run_global[:, None, :] + lc - run_local[:, None, :], 0), axis=2)
    back_src = jnp.where(jnp.any(inside, axis=2), back_src, zero_global).astype(I32).reshape(-1)

    ys = _experts(tile_expert, n_used, gather_src, xs, wg16, wu16, wd16, n_tiles)
    return _final(back_src, x1, route, p2d, ys, row(ple_norm), w_ple_gate.astype(BF16),
                  w_ple_proj.astype(BF16), out_gain)


def kernel(x, p, mix_norm, w_in, hg_lb_logits, hg_norm, ret_norm, w_branch_a, w_branch_b, w_out,
           ffn_norm, w_router_group, b_router_group, w_router_expert, b_router_expert,
           w_expert_gate, w_expert_up, w_expert_down, ple_norm, w_ple_gate, w_ple_proj, final_norm):
    batch, seq, d = x.shape
    depth = p.shape[0]
    assert depth == 1, "the final rmsnorm is fused into the single layer"
    x2d = x.reshape(batch * seq, d)
    out = _layer(x2d, p[0].reshape(batch * seq, -1), batch, seq, mix_norm[0], w_in[0], hg_lb_logits,
                 hg_norm[0], ret_norm[0], w_branch_a[0], w_branch_b[0], w_out[0], ffn_norm[0],
                 w_router_group[0], b_router_group[0], w_router_expert[0], b_router_expert[0],
                 w_expert_gate[0], w_expert_up[0], w_expert_down[0], ple_norm[0], w_ple_gate[0],
                 w_ple_proj[0], final_norm.reshape(1, -1).astype(F32))
    return out.reshape(batch, seq, d)
```

```python
import jax
import jax.numpy as jnp
from jax import lax
from jax.experimental import pallas as pl
from jax.experimental.pallas import tpu as pltpu

F32 = jnp.float32
BF16 = jnp.bfloat16
I32 = jnp.int32

EPS = 1e-6
D_MODEL = 1024
PLE_DIM = 256
HG_HEADS = 4
HG_DK = 128
HG_WIDTH = HG_HEADS * HG_DK
RET_HEADS = 4
RET_DK = 128
RET_DV = 256
ROPE_BASE = 10000.0
ROPE_SPLIT = 64
IN_TOTAL = 7168
N_GROUPS = 4
EXPERTS_PER_GROUP = 8
N_EXPERTS = 32
D_EXPERT = 256

COL_HQ, COL_HF, COL_HI, COL_HG = 0, 4, 8, 12
COL_RQ, COL_RK = 16, 20
COL_RV, COL_RG = 24, 32
COL_GA, COL_GB = 5, 6

LANES = 128
VMEM_LIMIT = 56 * 1024 * 1024

HG_CHUNK = 64
HG_SUB = 8
HG_UNROLL = 16
HG_STATE_UNROLL = 16
HG_NORM_ROWS = 1024
RET_CHUNK = 128
RET_STATE_UNROLL = 8
SEQ_TILE = 1024
TOK_TILE = 512
EXP_TILE = 512
EXP_SUB = 512
EXP_AHEAD = 4
EXP_SLOTS = EXP_AHEAD + 1
ROW_CHUNK = 16
CHUNKS_PER_TILE = EXP_TILE // ROW_CHUNK
LOCAL_ROWS = 2 * TOK_TILE + N_EXPERTS * ROW_CHUNK
LOCAL_CHUNKS = LOCAL_ROWS // ROW_CHUNK
SORT_SLAB = 256
FIN_AHEAD = 3
FIN_SLOTS = FIN_AHEAD + 1


def _cparams(sem):
    return pltpu.CompilerParams(dimension_semantics=sem, vmem_limit_bytes=VMEM_LIMIT)


def _rms(x, g):
    return x * lax.rsqrt(jnp.mean(x * x, axis=-1, keepdims=True) + EPS) * g


def _sigmoid(x):
    return 1.0 / (1.0 + jnp.exp(-x))


def _silu(x):
    return x * _sigmoid(x)


def _split3(x):
    hi = x.astype(BF16)
    r1 = x - hi.astype(F32)
    mid = r1.astype(BF16)
    lo = (r1 - mid.astype(F32)).astype(BF16)
    return hi, mid, lo


def _dot(a, b):
    return jnp.dot(a, b, preferred_element_type=F32)


def _dot_nt(a, b):
    return lax.dot_general(a, b, (((1,), (1,)), ((), ())), preferred_element_type=F32)


def _dot_tn(a, b):
    return lax.dot_general(a, b, (((0,), (0,)), ((), ())), preferred_element_type=F32)


def _inproj_kernel(x_ref, g_ref, w_ref, wg_ref, wu_ref, wd_ref, proj_ref, hf_ref,
                   wg16_ref, wu16_ref, wd16_ref):
    wg16_ref[...] = wg_ref[...].astype(BF16)
    wu16_ref[...] = wu_ref[...].astype(BF16)
    wd16_ref[...] = wd_ref[...].astype(BF16)

    h = _rms(x_ref[...], g_ref[...]).astype(BF16)
    tn = HG_WIDTH
    for j in range(IN_TOTAL // tn):
        acc = _dot(h, w_ref[:, j * tn:(j + 1) * tn])
        proj_ref[:, j * tn:(j + 1) * tn] = acc.astype(BF16)
        if j * tn == COL_HF * LANES:
            hf_ref[...] = acc


def _inproj(x2d, gain, w_bf16, w_gate, w_up, w_down):
    t = x2d.shape[0]
    steps = t // TOK_TILE
    assert N_EXPERTS % steps == 0, "expert weights are converted in equal shares per grid step"
    share = N_EXPERTS // steps
    up_spec = pl.BlockSpec((share, D_MODEL, D_EXPERT), lambda i: (i, 0, 0))
    down_spec = pl.BlockSpec((share, D_EXPERT, D_MODEL), lambda i: (i, 0, 0))
    return pl.pallas_call(
        _inproj_kernel,
        grid=(steps,),
        in_specs=[
            pl.BlockSpec((TOK_TILE, D_MODEL), lambda i: (i, 0)),
            pl.BlockSpec((1, D_MODEL), lambda i: (0, 0)),
            pl.BlockSpec((D_MODEL, IN_TOTAL), lambda i: (0, 0), pipeline_mode=pl.Buffered(1)),
            up_spec, up_spec, down_spec,
        ],
        out_specs=[
            pl.BlockSpec((TOK_TILE, IN_TOTAL), lambda i: (i, 0)),
            pl.BlockSpec((TOK_TILE, HG_WIDTH), lambda i: (i, 0)),
            up_spec, up_spec, down_spec,
        ],
        out_shape=[
            jax.ShapeDtypeStruct((t, IN_TOTAL), BF16),
            jax.ShapeDtypeStruct((t, HG_WIDTH), F32),
            jax.ShapeDtypeStruct(w_gate.shape, BF16),
            jax.ShapeDtypeStruct(w_up.shape, BF16),
            jax.ShapeDtypeStruct(w_down.shape, BF16),
        ],
        compiler_params=_cparams(("arbitrary",)),
        name="inproj",
    )(x2d, gain, w_bf16, w_gate, w_up, w_down)


def _hgrn_stages(lbl_ref, q_ref, f_ref, i_ref, g_ref, ng_ref, o_ref,
                 st_ref, b_s, k_s, v_s, oi_s, qe_s, kv_s, dec_s):
    c = HG_CHUNK
    nsub = c // HG_SUB

    logits = lbl_ref[...]
    e = jnp.exp(logits - jnp.max(logits, axis=0, keepdims=True))
    lb = e[0:1] / jnp.sum(e, axis=0, keepdims=True)
    one_m_lb = jnp.sum(e[1:], axis=0, keepdims=True) / jnp.sum(e, axis=0, keepdims=True)
    ng = ng_ref[...]

    row = lax.broadcasted_iota(I32, (c, c), 0)
    col = lax.broadcasted_iota(I32, (c, c), 1)
    tri = jnp.where(row >= col, 1.0, 0.0).astype(BF16)
    sub_row = lax.broadcasted_iota(I32, (HG_SUB, HG_DK), 0)
    masked = jnp.float32(-1e30)

    def bcast_row(ref, r, rows):
        return jnp.broadcast_to(ref[pl.ds(r, 1), :], (rows, HG_DK))

    def prep(ci, slot):
        r0 = pl.multiple_of(ci * c, c)
        z = f_ref[pl.ds(r0, c), :]
        ez = jnp.exp(-jnp.abs(z))
        rz = 1.0 / (1.0 + ez)
        pos = z >= 0.0
        logf = jnp.log2(lb + one_m_lb * jnp.where(pos, rz, ez * rz))
        kk = one_m_lb * jnp.where(pos, ez * rz, rz)
        q = _silu(q_ref[pl.ds(r0, c), :].astype(F32))
        v = i_ref[pl.ds(r0, c), :].astype(F32)
        k_s[slot] = kk
        v_s[slot] = v
        return dict(ci=ci, r0=r0, slot=slot, kk=kk, q=q, v=v, v16=v.astype(BF16), parts=_split3(logf))

    def cumulate(s):
        hi, mid, lo = s.pop("parts")
        b = (_dot(tri, lo) + _dot(tri, mid)) + _dot(tri, hi)
        b_s[s["slot"]] = b
        qe_s[pl.ds(s["r0"], c), :] = (s["q"] * jnp.exp2(b)).astype(BF16)
        s["b"] = b
        return s

    def off_diagonal(s):
        b, q, kk = s["b"], s["q"], s["kk"]
        bs_ref = b_s.at[s["slot"]]
        squares = []
        size = c // 2
        while size >= HG_SUB:
            for r0 in range(size, c, 2 * size):
                edge = bcast_row(bs_ref, r0 - 1, size)
                qs = q[r0:r0 + size, :] * jnp.exp2(b[r0:r0 + size, :] - edge)
                ks = kk[r0 - size:r0, :] * jnp.exp2(edge - b[r0 - size:r0, :])
                squares.append((r0, size, _dot_nt(qs.astype(BF16), ks.astype(BF16))))
            size //= 2
        s["squares"] = squares
        return s

    def apply_values(s):
        b, kk, v = s["b"], s["kk"], s["v"]
        blast = b[c - 1:c, :]
        rows = [jnp.zeros((HG_SUB, HG_DK), F32) for _ in range(nsub)]
        for r0, size, a in s.pop("squares"):
            part = _dot(a.astype(BF16), v[r0 - size:r0, :].astype(BF16))
            for i in range(size // HG_SUB):
                rows[r0 // HG_SUB + i] = rows[r0 // HG_SUB + i] + part[i * HG_SUB:(i + 1) * HG_SUB, :]
        s["o"] = jnp.concatenate(rows, axis=0)
        kd = kk * jnp.exp2(blast - b)
        kv_s[s["ci"]] = _dot_tn(s["v16"], kd.astype(BF16))
        dec_s[s["ci"]] = jnp.broadcast_to(jnp.exp2(blast), (HG_SUB, HG_DK))
        return s

    def diagonal(s):
        b, q = s["b"], s["q"]
        bs_ref, ks_ref, vs_ref = b_s.at[s["slot"]], k_s.at[s["slot"]], v_s.at[s["slot"]]
        d_blocks = []
        for i in range(nsub):
            sl = slice(i * HG_SUB, (i + 1) * HG_SUB)
            bt, qt = b[sl, :], q[sl, :]
            acc = jnp.zeros((HG_SUB, HG_DK), F32)
            for j in range(HG_SUB):
                r = i * HG_SUB + j
                arg = jnp.where(sub_row >= j, bt - bcast_row(bs_ref, r, HG_SUB), masked)
                g = jnp.exp2(arg) * (qt * bcast_row(ks_ref, r, HG_SUB))
                acc = acc + jnp.sum(g, axis=-1, keepdims=True) * bcast_row(vs_ref, r, HG_SUB)
            d_blocks.append(acc)
        oi_s[pl.ds(s["r0"], c), :] = s["o"] + jnp.concatenate(d_blocks, axis=0)

    def carried_group(gi):
        st = st_ref[...]
        outs = []
        for j in range(HG_STATE_UNROLL):
            ci = gi * HG_STATE_UNROLL + j
            r0 = pl.multiple_of(ci * c, c)
            outs.append((r0, _dot_nt(qe_s[pl.ds(r0, c), :], st.astype(BF16))))
            st = st * dec_s[ci][0:1, :] + kv_s[ci]
        st_ref[...] = st
        for r0, os in outs:
            oi_s[pl.ds(r0, c), :] += os

    def finish(ri):
        r0 = pl.multiple_of(ri * HG_NORM_ROWS, HG_NORM_ROWS)
        y = _rms(oi_s[pl.ds(r0, HG_NORM_ROWS), :], ng) * _silu(g_ref[pl.ds(r0, HG_NORM_ROWS), :].astype(F32))
        o_ref[pl.ds(r0, HG_NORM_ROWS), :] = y.astype(o_ref.dtype)

    return prep, (cumulate, off_diagonal, apply_values, diagonal), carried_group, finish


HG_SCRATCH = [
    pltpu.VMEM((HG_DK, HG_DK), F32),
    pltpu.VMEM((HG_UNROLL, HG_CHUNK, HG_DK), F32),
    pltpu.VMEM((HG_UNROLL, HG_CHUNK, HG_DK), F32),
    pltpu.VMEM((HG_UNROLL, HG_CHUNK, HG_DK), F32),
    pltpu.VMEM((SEQ_TILE, HG_DK), F32),
    pltpu.VMEM((SEQ_TILE, HG_DK), BF16),
    pltpu.VMEM((SEQ_TILE // HG_CHUNK, HG_DK, HG_DK), F32),
    pltpu.VMEM((SEQ_TILE // HG_CHUNK, HG_SUB, HG_DK), F32),
]


def _ret_stages(q_ref, k_ref, v_ref, g_ref, cos_ref, sin_ref, ng_ref, o_ref,
                r_ref, oi_s, q16_s, kv_s):
    c = RET_CHUNK

    hf = jnp.full((1, 1), pl.program_id(1), I32).astype(F32)
    lg = jnp.log1p(-jnp.exp2(-5.0 - hf))
    ti = lax.broadcasted_iota(I32, (c, c), 0)
    si = lax.broadcasted_iota(I32, (c, c), 1)
    rel = (ti - si).astype(F32)
    intra = jnp.where(ti >= si, jnp.exp(jnp.maximum(rel, 0.0) * lg), 0.0)
    idx = lax.broadcasted_iota(I32, (c, 1), 0).astype(F32)
    inter = jnp.exp((idx + 1.0) * lg)
    to_state = jnp.exp((c - 1.0 - idx) * lg)
    chunk_decay = jnp.exp(float(c) * lg)
    ng = ng_ref[...]
    half = RET_DK // 2

    def prep(ci):
        r0 = pl.multiple_of(ci * c, c)
        cos = cos_ref[pl.ds(r0, c), :]
        sin = sin_ref[pl.ds(r0, c), :]
        q = q_ref[pl.ds(r0, c), :].astype(F32)
        k = k_ref[pl.ds(r0, c), :].astype(F32)
        qr = (q * cos + pltpu.roll(q, half, 1) * sin) * (RET_DK ** -0.5)
        kr = k * cos + pltpu.roll(k, half, 1) * sin
        q16 = qr.astype(BF16)
        q16_s[pl.ds(r0, c), :] = q16
        return dict(ci=ci, r0=r0, q16=q16, k16=kr.astype(BF16), kts=(kr * to_state).astype(BF16),
                    v16=v_ref[pl.ds(r0, c), :])

    def scores(s):
        s["att"] = (_dot_nt(s.pop("q16"), s.pop("k16")) * intra).astype(BF16)
        return s

    def apply_values(s):
        oi_s[pl.ds(s["r0"], c), :] = _dot(s["att"], s["v16"])
        kv_s[s["ci"]] = _dot_tn(s["kts"], s["v16"])
        return s

    def carried_group(gi):
        r = r_ref[...]
        outs = []
        for j in range(RET_STATE_UNROLL):
            ci = gi * RET_STATE_UNROLL + j
            r0 = pl.multiple_of(ci * c, c)
            outs.append((r0, _dot(q16_s[pl.ds(r0, c), :], r.astype(BF16))))
            r = chunk_decay * r + kv_s[ci]
        r_ref[...] = r
        for r0, qr_state in outs:
            o = oi_s[pl.ds(r0, c), :] + qr_state * inter
            y = _rms(o, ng) * _silu(g_ref[pl.ds(r0, c), :].astype(F32))
            o_ref[pl.ds(r0, c), :] = y.astype(o_ref.dtype)

    return prep, (scores, apply_values), carried_group


RET_SCRATCH = [
    pltpu.VMEM((RET_DK, RET_DV), F32),
    pltpu.VMEM((SEQ_TILE, RET_DV), F32),
    pltpu.VMEM((SEQ_TILE, RET_DK), BF16),
    pltpu.VMEM((SEQ_TILE // RET_CHUNK, RET_DK, RET_DV), F32),
]


def _mixers_kernel(*refs):
    hg_refs = refs[:6] + refs[13:14] + refs[15:15 + len(HG_SCRATCH)]
    ret_refs = refs[6:13] + refs[14:15] + refs[15 + len(HG_SCRATCH):]
    st_ref, r_ref = hg_refs[7], ret_refs[8]
    n_tokens = refs[1].shape[0]

    @pl.when(pl.program_id(2) == 0)
    def _():
        st_ref[...] = jnp.zeros_like(st_ref)
        r_ref[...] = jnp.zeros_like(r_ref)

    hg_prep, hg_local, hg_carried, hg_finish = _hgrn_stages(*hg_refs)
    ret_prep, ret_local, ret_carried = _ret_stages(*ret_refs)
    group_tokens = HG_UNROLL * HG_CHUNK
    ret_unroll = group_tokens // RET_CHUNK

    def local_group(gi, carry):
        hs = [hg_prep(gi * HG_UNROLL + slot, slot) for slot in range(HG_UNROLL)]
        rs = [ret_prep(gi * ret_unroll + j) for j in range(ret_unroll)]
        cumulate, off_diagonal, apply_values, diagonal = hg_local
        scores, ret_apply = ret_local
        hs = [cumulate(s) for s in hs]
        rs = [scores(s) for s in rs]
        hs = [off_diagonal(s) for s in hs]
        rs = [ret_apply(s) for s in rs]
        hs = [apply_values(s) for s in hs]
        for s in hs:
            diagonal(s)
        return carry

    lax.fori_loop(0, n_tokens // group_tokens, local_group, 0)

    state_tokens = HG_STATE_UNROLL * HG_CHUNK
    assert state_tokens == RET_STATE_UNROLL * RET_CHUNK

    def carried_group(gi, carry):
        hg_carried(gi)
        ret_carried(gi)
        return carry

    lax.fori_loop(0, n_tokens // state_tokens, carried_group, 0)

    def finish(ri, carry):
        hg_finish(ri)
        return carry

    lax.fori_loop(0, n_tokens // HG_NORM_ROWS, finish, 0)


def _mixers(proj, hf, lb_logits, hg_norm, cos2, sin2, ret_norm, batch, seq):
    ns = seq // SEQ_TILE
    tok = lambda b, h, s: b * ns + s
    hg_col = lambda base: pl.BlockSpec((SEQ_TILE, HG_DK), lambda b, h, s: (tok(b, h, s), base + h))
    return pl.pallas_call(
        _mixers_kernel,
        grid=(batch, HG_HEADS, ns),
        in_specs=[
            pl.BlockSpec((2, HG_DK), lambda b, h, s: (0, h)),
            hg_col(COL_HQ), hg_col(0), hg_col(COL_HI), hg_col(COL_HG),
            pl.BlockSpec((1, HG_DK), lambda b, h, s: (0, 0)),
            hg_col(COL_RQ), hg_col(COL_RK),
            pl.BlockSpec((SEQ_TILE, RET_DV), lambda b, h, s: (tok(b, h, s), COL_RV // 2 + h)),
            pl.BlockSpec((SEQ_TILE, RET_DV), lambda b, h, s: (tok(b, h, s), COL_RG // 2 + h)),
            pl.BlockSpec((SEQ_TILE, RET_DK), lambda b, h, s: (s, 0)),
            pl.BlockSpec((SEQ_TILE, RET_DK), lambda b, h, s: (s, 0)),
            pl.BlockSpec((1, RET_DV), lambda b, h, s: (0, 0)),
        ],
        out_specs=[
            pl.BlockSpec((SEQ_TILE, HG_DK), lambda b, h, s: (tok(b, h, s), h)),
            pl.BlockSpec((SEQ_TILE, RET_DV), lambda b, h, s: (tok(b, h, s), h)),
        ],
        out_shape=[
            jax.ShapeDtypeStruct((batch * seq, HG_WIDTH), BF16),
            jax.ShapeDtypeStruct((batch * seq, RET_HEADS * RET_DV), BF16),
        ],
        scratch_shapes=HG_SCRATCH + RET_SCRATCH,
        compiler_params=_cparams(("arbitrary", "arbitrary", "arbitrary")),
        name="mixers",
    )(lb_logits, proj, hf, proj, proj, hg_norm, proj, proj, proj, proj, cos2, sin2, ret_norm)


def _merge_kernel(x_ref, ya_ref, yb_ref, ga_ref, gb_ref, wa_ref, wb_ref, wo_ref, fg_ref,
                  wr_ref, br_ref, x1_ref, xs_ref, route_ref, cnt_ref, h2b_s, rows_s):
    tm = x_ref.shape[0]

    @pl.when(pl.program_id(0) == 0)
    def _():
        h2b_s[...] = jnp.zeros_like(h2b_s)
        rows_s[...] = jnp.full(rows_s.shape, -1.0, F32)

    def sort_previous(lo, hi):
        slab_row = lax.broadcasted_iota(I32, (SORT_SLAB, tm), 0).astype(F32)
        for r0 in range(lo, hi, SORT_SLAB):
            sel = ((slab_row == rows_s[0:1, :] - float(r0)) | (slab_row == rows_s[1:2, :] - float(r0)))
            xs_ref[r0:r0 + SORT_SLAB, :] = _dot(jnp.where(sel, 1.0, 0.0).astype(BF16),
                                                h2b_s[...]).astype(BF16)

    merged = (_sigmoid(ga_ref[...].astype(F32)) * _dot(ya_ref[...], wa_ref[...])
              + _sigmoid(gb_ref[...].astype(F32)) * _dot(yb_ref[...], wb_ref[...]))
    x1 = x_ref[...] + _dot(merged.astype(BF16), wo_ref[...])
    x1_ref[...] = x1
    h2 = _rms(x1, fg_ref[...])

    h2b = h2.astype(BF16)
    logits = _dot(h2b, wr_ref[...]) + br_ref[...]
    sort_previous(0, LOCAL_ROWS)

    lane = lax.broadcasted_iota(I32, (tm, LANES), 1)
    neg = jnp.float32(-jnp.inf)
    big = jnp.int32(1 << 30)
    is_g = lane < N_GROUPS
    gl = jnp.where(is_g, logits, neg)
    gmax = jnp.max(gl, axis=-1, keepdims=True)
    g_idx = jnp.min(jnp.where(gl == gmax, lane, big), axis=-1, keepdims=True)
    g_w = 1.0 / jnp.sum(jnp.where(is_g, jnp.exp(gl - gmax), 0.0), axis=-1, keepdims=True)

    ex = lane - N_GROUPS
    in_grp = (ex >= g_idx * EXPERTS_PER_GROUP) & (ex < (g_idx + 1) * EXPERTS_PER_GROUP)
    el = jnp.where(in_grp, logits, neg)
    m1 = jnp.max(el, axis=-1, keepdims=True)
    e1 = jnp.min(jnp.where(el == m1, ex, big), axis=-1, keepdims=True)
    el2 = jnp.where(ex == e1, neg, el)
    m2 = jnp.max(el2, axis=-1, keepdims=True)
    e2 = jnp.min(jnp.where(el2 == m2, ex, big), axis=-1, keepdims=True)
    p2 = jnp.exp(m2 - m1)
    w1 = g_w / (1.0 + p2)
    w2 = g_w * p2 / (1.0 + p2)

    oh1 = ex == e1
    oh2 = ex == e2
    oh = jnp.where(oh1 | oh2, 1.0, 0.0)
    ri = lax.broadcasted_iota(I32, (tm, tm), 0)
    ci = lax.broadcasted_iota(I32, (tm, tm), 1)
    strict = jnp.where(ri > ci, 1.0, 0.0).astype(BF16)
    local_rank = _dot(strict, oh.astype(BF16))
    cnt = jnp.sum(oh, axis=0, keepdims=True)
    run_chunks = jnp.floor((cnt + (ROW_CHUNK - 1.0)) * (1.0 / ROW_CHUNK))
    ui = lax.broadcasted_iota(I32, (LANES, LANES), 0)
    uj = lax.broadcasted_iota(I32, (LANES, LANES), 1)
    before = jnp.where(ui < uj, 1.0, 0.0).astype(BF16)
    run_start = _dot(jnp.broadcast_to(run_chunks, (8, LANES)).astype(BF16), before)[0:1] * ROW_CHUNK
    slot = run_start + local_rank
    pos1 = jnp.sum(jnp.where(oh1, slot, 0.0), axis=-1, keepdims=True)
    pos2 = jnp.sum(jnp.where(oh2, slot, 0.0), axis=-1, keepdims=True)
    cnt_ref[0] = jnp.broadcast_to(cnt, (8, LANES))

    route = jnp.where(lane == 0, e1.astype(F32), 0.0)
    route = jnp.where(lane == 1, e2.astype(F32), route)
    route = jnp.where(lane == 2, w1, route)
    route = jnp.where(lane == 3, w2, route)
    route = jnp.where(lane == 4, pos1, route)
    route = jnp.where(lane == 5, pos2, route)
    route_ref[...] = route

    hi1 = jnp.floor(pos1 * (1.0 / 256.0))
    hi2 = jnp.floor(pos2 * (1.0 / 256.0))
    digits = jnp.where(lane == 0, hi1, 0.0)
    digits = jnp.where(lane == 1, pos1 - 256.0 * hi1, digits)
    digits = jnp.where(lane == 2, hi2, digits)
    digits = jnp.where(lane == 3, pos2 - 256.0 * hi2, digits)
    pick = jnp.where(lax.broadcasted_iota(I32, (8, LANES), 0) == lax.broadcasted_iota(I32, (8, LANES), 1),
                     1.0, 0.0).astype(BF16)
    rows = _dot_nt(pick, digits.astype(BF16))

    h2b_s[...] = h2b
    rows_s[0:1, :] = rows[0:1] * 256.0 + rows[1:2]
    rows_s[1:2, :] = rows[2:3] * 256.0 + rows[3:4]


def _merge(x2d, ya, yb, proj, wa, wb, wo, ffn_g, w_router, b_router):
    t = x2d.shape[0]
    nb = t // TOK_TILE
    const = lambda *shape: pl.BlockSpec(shape, lambda i: (0,) * len(shape))
    cur = lambda i: jnp.minimum(i, nb - 1)
    prev = lambda i: jnp.maximum(i - 1, 0)
    return pl.pallas_call(
        _merge_kernel,
        grid=(nb + 1,),
        in_specs=[
            pl.BlockSpec((TOK_TILE, D_MODEL), lambda i: (cur(i), 0)),
            pl.BlockSpec((TOK_TILE, HG_WIDTH), lambda i: (cur(i), 0)),
            pl.BlockSpec((TOK_TILE, D_MODEL), lambda i: (cur(i), 0)),
            pl.BlockSpec((TOK_TILE, D_MODEL), lambda i: (cur(i), COL_GA)),
            pl.BlockSpec((TOK_TILE, D_MODEL), lambda i: (cur(i), COL_GB)),
            const(HG_WIDTH, D_MODEL),
            const(D_MODEL, D_MODEL),
            const(D_MODEL, D_MODEL),
            const(1, D_MODEL),
            const(D_MODEL, LANES),
            const(1, LANES),
        ],
        out_specs=[
            pl.BlockSpec((TOK_TILE, D_MODEL), lambda i: (cur(i), 0)),
            pl.BlockSpec((LOCAL_ROWS, D_MODEL), lambda i: (prev(i), 0)),
            pl.BlockSpec((TOK_TILE, LANES), lambda i: (cur(i), 0)),
            pl.BlockSpec((1, 8, LANES), lambda i: (cur(i), 0, 0)),
        ],
        out_shape=[
            jax.ShapeDtypeStruct((t, D_MODEL), F32),
            jax.ShapeDtypeStruct((t // TOK_TILE * LOCAL_ROWS, D_MODEL), BF16),
            jax.ShapeDtypeStruct((t, LANES), F32),
            jax.ShapeDtypeStruct((t // TOK_TILE, 8, LANES), F32),
        ],
        scratch_shapes=[pltpu.VMEM((TOK_TILE, D_MODEL), BF16), pltpu.VMEM((8, TOK_TILE), F32)],
        compiler_params=_cparams(("arbitrary",)),
        name="merge_route",
    )(x2d, ya, yb, proj, proj, wa, wb, wo, ffn_g, w_router, b_router)


def _expert_kernel(te_ref, nt_ref, src_ref, xs_ref, wg_ref, wu_ref, wd_ref, ys_ref, xbuf, sem):
    i = pl.program_id(0)
    nt = nt_ref[0]

    def gather(tile, slot):
        copies = []
        for c in range(CHUNKS_PER_TILE):
            row = pl.multiple_of(src_ref[tile * CHUNKS_PER_TILE + c] * ROW_CHUNK, ROW_CHUNK)
            copies.append(pltpu.make_async_copy(
                xs_ref.at[pl.ds(row, ROW_CHUNK)],
                xbuf.at[slot, pl.ds(c * ROW_CHUNK, ROW_CHUNK)], sem.at[slot]))
        return copies

    @pl.when(i == 0)
    def _():
        for k in range(EXP_AHEAD):
            for cp in gather(k, k):
                cp.start()

    @pl.when(i < nt + EXP_AHEAD)
    def _():
        for cp in gather(i, i % EXP_SLOTS):
            cp.wait()

    @pl.when(i < nt)
    def _():
        slot = i % EXP_SLOTS
        wg, wu, wd = wg_ref[0], wu_ref[0], wd_ref[0]
        subs = [pl.ds(r, EXP_SUB) for r in range(0, EXP_TILE, EXP_SUB)]
        xs = [xbuf[slot, s, :] for s in subs]
        gates = [(_dot(x, wg), _dot(x, wu)) for x in xs]
        for c, cp in enumerate(gather(i + EXP_AHEAD, (i + EXP_AHEAD) % EXP_SLOTS)):
            cp.start(priority=c % 2)
        hidden = [(_silu(a) * u).astype(BF16) for a, u in gates]
        for s, h in zip(subs, hidden):
            ys_ref[s, :] = _dot(h, wd).astype(ys_ref.dtype)

    @pl.when(i >= nt)
    def _():
        ys_ref[...] = jnp.zeros_like(ys_ref)


def _experts(tile_expert, n_tiles_used, src_chunk, xs, wg, wu, wd, n_tiles):
    grid_spec = pltpu.PrefetchScalarGridSpec(
        num_scalar_prefetch=3,
        grid=(n_tiles,),
        in_specs=[
            pl.BlockSpec(memory_space=pl.ANY),
            pl.BlockSpec((1, D_MODEL, D_EXPERT), lambda i, te, nt, src: (te[i], 0, 0)),
            pl.BlockSpec((1, D_MODEL, D_EXPERT), lambda i, te, nt, src: (te[i], 0, 0)),
            pl.BlockSpec((1, D_EXPERT, D_MODEL), lambda i, te, nt, src: (te[i], 0, 0)),
        ],
        out_specs=pl.BlockSpec((EXP_TILE, D_MODEL), lambda i, te, nt, src: (i, 0)),
        scratch_shapes=[
            pltpu.VMEM((EXP_SLOTS, EXP_TILE, D_MODEL), BF16),
            pltpu.SemaphoreType.DMA((EXP_SLOTS,)),
        ],
    )
    return pl.pallas_call(
        _expert_kernel,
        grid_spec=grid_spec,
        out_shape=jax.ShapeDtypeStruct((n_tiles * EXP_TILE, D_MODEL), BF16),
        compiler_params=_cparams(("arbitrary",)),
        name="experts",
    )(tile_expert, n_tiles_used, src_chunk, xs, wg, wu, wd)


def _final_kernel(src_ref, x1_ref, route_ref, p_ref, ys_ref, pg_ref, wpg_ref, wpp_ref, fg_ref,
                  o_ref, ybuf, sem):
    tm = x1_ref.shape[0]
    i = pl.program_id(0)
    nb = pl.num_programs(0)

    def gather(block, slot):
        copies = []
        for c in range(LOCAL_CHUNKS):
            row = pl.multiple_of(src_ref[block * LOCAL_CHUNKS + c] * ROW_CHUNK, ROW_CHUNK)
            copies.append(pltpu.make_async_copy(
                ys_ref.at[pl.ds(row, ROW_CHUNK)],
                ybuf.at[slot, pl.ds(c * ROW_CHUNK, ROW_CHUNK)], sem.at[slot]))
        return copies

    @pl.when(i == 0)
    def _():
        for k in range(FIN_AHEAD):
            for cp in gather(k % nb, k):
                cp.start()

    slot = i % FIN_SLOTS
    for cp in gather(i, slot):
        cp.wait()

    ple = _dot(p_ref[...].astype(BF16), wpp_ref[...])

    route = route_ref[...]
    w1, w2 = route[:, 2:3], route[:, 3:4]
    pos1, pos2 = route[:, 4:5], route[:, 5:6]
    slab_col = lax.broadcasted_iota(I32, (tm, SORT_SLAB), 1).astype(F32)
    moe = jnp.zeros((tm, D_MODEL), F32)
    for k0 in range(0, LOCAL_ROWS, SORT_SLAB):
        sel = jnp.where(slab_col == pos1 - float(k0), w1,
                        jnp.where(slab_col == pos2 - float(k0), w2, 0.0)).astype(BF16)
        moe = moe + _dot(sel, ybuf[slot, k0:k0 + SORT_SLAB, :])
    for c, cp in enumerate(gather((i + FIN_AHEAD) % nb, (i + FIN_AHEAD) % FIN_SLOTS)):
        cp.start(priority=c % 2)
    x2 = x1_ref[...] + moe
    hp = _rms(x2, pg_ref[...]).astype(BF16)
    gate = _sigmoid(_dot(hp, wpg_ref[...]))
    x3 = x2 + gate * ple
    o_ref[...] = _rms(x3, fg_ref[...])

    @pl.when(i == nb - 1)
    def _():
        for k in range(1, FIN_AHEAD + 1):
            for cp in gather(0, (i + k) % FIN_SLOTS):
                cp.wait()


def _final(src_chunk, x1, route, p2d, ys, ple_g, wpg, wpp, final_g):
    t = x1.shape[0]
    const = lambda *shape: pl.BlockSpec(shape, lambda i, src: (0,) * len(shape))
    grid_spec = pltpu.PrefetchScalarGridSpec(
        num_scalar_prefetch=1,
        grid=(t // TOK_TILE,),
        in_specs=[
            pl.BlockSpec((TOK_TILE, D_MODEL), lambda i, src: (i, 0)),
            pl.BlockSpec((TOK_TILE, LANES), lambda i, src: (i, 0)),
            pl.BlockSpec((TOK_TILE, PLE_DIM), lambda i, src: (i, 0)),
            pl.BlockSpec(memory_space=pl.ANY),
            const(1, D_MODEL),
            const(D_MODEL, D_MODEL),
            const(PLE_DIM, D_MODEL),
            const(1, D_MODEL),
        ],
        out_specs=pl.BlockSpec((TOK_TILE, D_MODEL), lambda i, src: (i, 0)),
        scratch_shapes=[pltpu.VMEM((FIN_SLOTS, LOCAL_ROWS, D_MODEL), BF16),
                        pltpu.SemaphoreType.DMA((FIN_SLOTS,))],
    )
    return pl.pallas_call(
        _final_kernel,
        grid_spec=grid_spec,
        out_shape=jax.ShapeDtypeStruct((t, D_MODEL), F32),
        compiler_params=_cparams(("arbitrary",)),
        name="combine_ple_final",
    )(src_chunk, x1, route, p2d, ys, ple_g, wpg, wpp, final_g)


def _rotary_tables(seq):
    inv = ROPE_BASE ** (-jnp.arange(0, RET_DK, 2, dtype=F32) / RET_DK)
    inv = jnp.concatenate([inv, inv])
    sign = jnp.where(jnp.arange(RET_DK) < RET_DK // 2, -1.0, 1.0).astype(F32)
    hi = (jnp.arange(seq // ROPE_SPLIT, dtype=F32) * ROPE_SPLIT)[:, None] * inv[None, :]
    lo = jnp.arange(ROPE_SPLIT, dtype=F32)[:, None] * inv[None, :]
    ch, sh = jnp.cos(hi)[:, None, :], jnp.sin(hi)[:, None, :]
    cl, sl = jnp.cos(lo)[None], jnp.sin(lo)[None]
    cos = (ch * cl - sh * sl).reshape(seq, RET_DK)
    sin = ((sh * cl + ch * sl) * sign).reshape(seq, RET_DK)
    return cos, sin


def _layer(x2d, p2d, batch, seq, mix_norm, w_in, hg_lb_logits, hg_norm, ret_norm, w_branch_a,
           w_branch_b, w_out, ffn_norm, w_rg, b_rg, w_re, b_re, w_gate, w_up, w_down, ple_norm,
           w_ple_gate, w_ple_proj, out_gain):
    t = x2d.shape[0]
    row = lambda v: v.reshape(1, -1).astype(F32)

    proj, hf, wg16, wu16, wd16 = _inproj(x2d, row(mix_norm), w_in.astype(BF16), w_gate, w_up, w_down)
    cos2, sin2 = _rotary_tables(seq)
    ya, yb = _mixers(proj, hf, hg_lb_logits.astype(F32), row(hg_norm), cos2, sin2, row(ret_norm), batch, seq)

    n_r = N_GROUPS + N_EXPERTS
    w_router = jnp.pad(jnp.concatenate([w_rg, w_re], axis=1).astype(BF16), ((0, 0), (0, LANES - n_r)))
    b_router = jnp.zeros((1, LANES), F32).at[0, :n_r].set(jnp.concatenate([b_rg, b_re]))
    x1, xs, route, counts = _merge(
        x2d, ya, yb, proj, w_branch_a.astype(BF16), w_branch_b.astype(BF16), w_out.astype(BF16),
        row(ffn_norm), w_router, b_router)

    n_blocks = t // TOK_TILE
    cnt = counts[:, 0, N_GROUPS:N_GROUPS + N_EXPERTS].astype(I32)
    run_chunks = (cnt + ROW_CHUNK - 1) // ROW_CHUNK
    earlier_e = jnp.tril(jnp.ones((N_EXPERTS, N_EXPERTS), I32), -1)
    earlier_b = jnp.tril(jnp.ones((n_blocks, n_blocks), I32), -1)
    run_local = jnp.sum(run_chunks[:, None, :] * earlier_e[None], axis=2)
    seg_chunks = jnp.sum(run_chunks, axis=0)
    tiles_per = (seg_chunks + CHUNKS_PER_TILE - 1) // CHUNKS_PER_TILE
    seg_start = jnp.sum(tiles_per[None, :] * earlier_e, axis=1) * CHUNKS_PER_TILE
    tile_end = seg_start // CHUNKS_PER_TILE + tiles_per
    run_global = seg_start[None, :] + jnp.sum(run_chunks.T[:, None, :] * earlier_b[None], axis=2).T
    max_chunks = (2 * t) // ROW_CHUNK + n_blocks * N_EXPERTS + N_EXPERTS * (CHUNKS_PER_TILE - 1)
    n_tiles = -(-max_chunks // CHUNKS_PER_TILE) + EXP_AHEAD
    tile_ids = jnp.arange(n_tiles, dtype=I32)
    tile_expert = jnp.minimum(jnp.sum((tile_end[None, :] <= tile_ids[:, None]).astype(I32), axis=1),
                              N_EXPERTS - 1)
    n_used = tile_end[-1:].astype(I32)

    block_ids = jnp.arange(n_blocks, dtype=I32)
    zero_local = LOCAL_CHUNKS - 1
    zero_global = n_tiles * CHUNKS_PER_TILE - 1
    g = jnp.arange(n_tiles * CHUNKS_PER_TILE, dtype=I32)[:, None]
    e_g = jnp.repeat(tile_expert, CHUNKS_PER_TILE)
    pick_e = (e_g[:, None] == jnp.arange(N_EXPERTS, dtype=I32)[None, :]).astype(I32)
    rg, rc, rl = (jnp.sum(pick_e[:, :, None] * tab.T[None, :, :], axis=1)
                  for tab in (run_global, run_chunks, run_local))
    inside = (rg <= g) & (g < rg + rc)
    gather_src = jnp.sum(jnp.where(inside, block_ids[None, :] * LOCAL_CHUNKS + rl + (g - rg), 0), axis=1)
    gather_src = jnp.where(jnp.any(inside, axis=1), gather_src, zero_local).astype(I32)

    lc = jnp.arange(LOCAL_CHUNKS, dtype=I32)[None, :, None]
    inside = (run_local[:, None, :] <= lc) & (lc < (run_local + run_chunks)[:, None, :])
    back_src = jnp.sum(jnp.where(inside, run_global[:, None, :] + lc - run_local[:, None, :], 0), axis=2)
    back_src = jnp.where(jnp.any(inside, axis=2), back_src, zero_global).astype(I32).reshape(-1)

    ys = _experts(tile_expert, n_used, gather_src, xs, wg16, wu16, wd16, n_tiles)
    return _final(back_src, x1, route, p2d, ys, row(ple_norm), w_ple_gate.astype(BF16),
                  w_ple_proj.astype(BF16), out_gain)


def kernel(x, p, mix_norm, w_in, hg_lb_logits, hg_norm, ret_norm, w_branch_a, w_branch_b, w_out,
           ffn_norm, w_router_group, b_router_group, w_router_expert, b_router_expert,
           w_expert_gate, w_expert_up, w_expert_down, ple_norm, w_ple_gate, w_ple_proj, final_norm):
    batch, seq, d = x.shape
    depth = p.shape[0]
    assert depth == 1, "the final rmsnorm is fused into the single layer"
    x2d = x.reshape(batch * seq, d)
    out = _layer(x2d, p[0].reshape(batch * seq, -1), batch, seq, mix_norm[0], w_in[0], hg_lb_logits,
                 hg_norm[0], ret_norm[0], w_branch_a[0], w_branch_b[0], w_out[0], ffn_norm[0],
                 w_router_group[0], b_router_group[0], w_router_expert[0], b_router_expert[0],
                 w_expert_gate[0], w_expert_up[0], w_expert_down[0], ple_norm[0], w_ple_gate[0],
                 w_ple_proj[0], final_norm.reshape(1, -1).astype(F32))
    return out.reshape(batch, seq, d)
```

```python
import jax
import jax.numpy as jnp
from jax import lax
from jax.experimental import pallas as pl
from jax.experimental.pallas import tpu as pltpu

F32 = jnp.float32
BF16 = jnp.bfloat16
I32 = jnp.int32

EPS = 1e-6
D_MODEL = 1024
PLE_DIM = 256
HG_HEADS = 4
HG_DK = 128
HG_WIDTH = HG_HEADS * HG_DK
RET_HEADS = 4
RET_DK = 128
RET_DV = 256
ROPE_BASE = 10000.0
ROPE_SPLIT = 64
IN_TOTAL = 7168
N_GROUPS = 4
EXPERTS_PER_GROUP = 8
N_EXPERTS = 32
D_EXPERT = 256

COL_HQ, COL_HF, COL_HI, COL_HG = 0, 4, 8, 12
COL_RQ, COL_RK = 16, 20
COL_RV, COL_RG = 24, 32
COL_GA, COL_GB = 5, 6

LANES = 128
VMEM_LIMIT = 56 * 1024 * 1024

HG_CHUNK = 64
HG_SUB = 8
HG_UNROLL = 16
HG_STATE_UNROLL = 16
HG_NORM_ROWS = 1024
RET_CHUNK = 128
RET_STATE_UNROLL = 8
SEQ_TILE = 1024
TOK_TILE = 512
EXP_TILE = 512
EXP_SUB = 512
EXP_AHEAD = 3
EXP_SLOTS = EXP_AHEAD + 1
ROW_CHUNK = 16
CHUNKS_PER_TILE = EXP_TILE // ROW_CHUNK
LOCAL_ROWS = 2 * TOK_TILE + N_EXPERTS * ROW_CHUNK
LOCAL_CHUNKS = LOCAL_ROWS // ROW_CHUNK
SORT_SLAB = 256
FIN_AHEAD = 2
FIN_SLOTS = FIN_AHEAD + 1


def _cparams(sem):
    return pltpu.CompilerParams(dimension_semantics=sem, vmem_limit_bytes=VMEM_LIMIT)


def _rms(x, g):
    return x * lax.rsqrt(jnp.mean(x * x, axis=-1, keepdims=True) + EPS) * g


def _sigmoid(x):
    return 1.0 / (1.0 + jnp.exp(-x))


def _silu(x):
    return x * _sigmoid(x)


def _split3(x):
    hi = x.astype(BF16)
    r1 = x - hi.astype(F32)
    mid = r1.astype(BF16)
    lo = (r1 - mid.astype(F32)).astype(BF16)
    return hi, mid, lo


def _dot(a, b):
    return jnp.dot(a, b, preferred_element_type=F32)


def _dot_nt(a, b):
    return lax.dot_general(a, b, (((1,), (1,)), ((), ())), preferred_element_type=F32)


def _dot_tn(a, b):
    return lax.dot_general(a, b, (((0,), (0,)), ((), ())), preferred_element_type=F32)


def _inproj_kernel(x_ref, g_ref, w_ref, wg_ref, wu_ref, wd_ref, proj_ref, hf_ref,
                   wg16_ref, wu16_ref, wd16_ref):
    wg16_ref[...] = wg_ref[...].astype(BF16)
    wu16_ref[...] = wu_ref[...].astype(BF16)
    wd16_ref[...] = wd_ref[...].astype(BF16)

    h = _rms(x_ref[...], g_ref[...]).astype(BF16)
    tn = HG_WIDTH
    for j in range(IN_TOTAL // tn):
        acc = _dot(h, w_ref[:, j * tn:(j + 1) * tn])
        proj_ref[:, j * tn:(j + 1) * tn] = acc.astype(BF16)
        if j * tn == COL_HF * LANES:
            hf_ref[...] = acc


def _inproj(x2d, gain, w_bf16, w_gate, w_up, w_down):
    t = x2d.shape[0]
    steps = t // TOK_TILE
    assert N_EXPERTS % steps == 0, "expert weights are converted in equal shares per grid step"
    share = N_EXPERTS // steps
    up_spec = pl.BlockSpec((share, D_MODEL, D_EXPERT), lambda i: (i, 0, 0))
    down_spec = pl.BlockSpec((share, D_EXPERT, D_MODEL), lambda i: (i, 0, 0))
    return pl.pallas_call(
        _inproj_kernel,
        grid=(steps,),
        in_specs=[
            pl.BlockSpec((TOK_TILE, D_MODEL), lambda i: (i, 0)),
            pl.BlockSpec((1, D_MODEL), lambda i: (0, 0)),
            pl.BlockSpec((D_MODEL, IN_TOTAL), lambda i: (0, 0), pipeline_mode=pl.Buffered(1)),
            up_spec, up_spec, down_spec,
        ],
        out_specs=[
            pl.BlockSpec((TOK_TILE, IN_TOTAL), lambda i: (i, 0)),
            pl.BlockSpec((TOK_TILE, HG_WIDTH), lambda i: (i, 0)),
            up_spec, up_spec, down_spec,
        ],
        out_shape=[
            jax.ShapeDtypeStruct((t, IN_TOTAL), BF16),
            jax.ShapeDtypeStruct((t, HG_WIDTH), F32),
            jax.ShapeDtypeStruct(w_gate.shape, BF16),
            jax.ShapeDtypeStruct(w_up.shape, BF16),
            jax.ShapeDtypeStruct(w_down.shape, BF16),
        ],
        compiler_params=_cparams(("arbitrary",)),
        name="inproj",
    )(x2d, gain, w_bf16, w_gate, w_up, w_down)


def _hgrn_stages(lbl_ref, q_ref, f_ref, i_ref, g_ref, ng_ref, o_ref,
                 st_ref, b_s, k_s, v_s, oi_s, qe_s, kv_s, dec_s):
    c = HG_CHUNK
    nsub = c // HG_SUB

    logits = lbl_ref[...]
    e = jnp.exp(logits - jnp.max(logits, axis=0, keepdims=True))
    lb = e[0:1] / jnp.sum(e, axis=0, keepdims=True)
    one_m_lb = jnp.sum(e[1:], axis=0, keepdims=True) / jnp.sum(e, axis=0, keepdims=True)
    ng = ng_ref[...]

    row = lax.broadcasted_iota(I32, (c, c), 0)
    col = lax.broadcasted_iota(I32, (c, c), 1)
    tri = jnp.where(row >= col, 1.0, 0.0).astype(BF16)
    sub_row = lax.broadcasted_iota(I32, (HG_SUB, HG_DK), 0)
    masked = jnp.float32(-1e30)

    def bcast_row(ref, r, rows):
        return jnp.broadcast_to(ref[pl.ds(r, 1), :], (rows, HG_DK))

    def prep(ci, slot):
        r0 = pl.multiple_of(ci * c, c)
        z = f_ref[pl.ds(r0, c), :]
        ez = jnp.exp(-jnp.abs(z))
        rz = 1.0 / (1.0 + ez)
        pos = z >= 0.0
        logf = jnp.log2(lb + one_m_lb * jnp.where(pos, rz, ez * rz))
        kk = one_m_lb * jnp.where(pos, ez * rz, rz)
        q = _silu(q_ref[pl.ds(r0, c), :].astype(F32))
        v = i_ref[pl.ds(r0, c), :].astype(F32)
        k_s[slot] = kk
        v_s[slot] = v
        return dict(ci=ci, r0=r0, slot=slot, kk=kk, q=q, v=v, v16=v.astype(BF16), parts=_split3(logf))

    def cumulate(s):
        hi, mid, lo = s.pop("parts")
        b = (_dot(tri, lo) + _dot(tri, mid)) + _dot(tri, hi)
        b_s[s["slot"]] = b
        qe_s[pl.ds(s["r0"], c), :] = (s["q"] * jnp.exp2(b)).astype(BF16)
        s["b"] = b
        return s

    def off_diagonal(s):
        b, q, kk = s["b"], s["q"], s["kk"]
        bs_ref = b_s.at[s["slot"]]
        squares = []
        size = c // 2
        while size >= HG_SUB:
            for r0 in range(size, c, 2 * size):
                edge = bcast_row(bs_ref, r0 - 1, size)
                qs = q[r0:r0 + size, :] * jnp.exp2(b[r0:r0 + size, :] - edge)
                ks = kk[r0 - size:r0, :] * jnp.exp2(edge - b[r0 - size:r0, :])
                squares.append((r0, size, _dot_nt(qs.astype(BF16), ks.astype(BF16))))
            size //= 2
        s["squares"] = squares
        return s

    def apply_values(s):
        b, kk, v = s["b"], s["kk"], s["v"]
        blast = b[c - 1:c, :]
        rows = [jnp.zeros((HG_SUB, HG_DK), F32) for _ in range(nsub)]
        for r0, size, a in s.pop("squares"):
            part = _dot(a.astype(BF16), v[r0 - size:r0, :].astype(BF16))
            for i in range(size // HG_SUB):
                rows[r0 // HG_SUB + i] = rows[r0 // HG_SUB + i] + part[i * HG_SUB:(i + 1) * HG_SUB, :]
        s["o"] = jnp.concatenate(rows, axis=0)
        kd = kk * jnp.exp2(blast - b)
        kv_s[s["ci"]] = _dot_tn(s["v16"], kd.astype(BF16))
        dec_s[s["ci"]] = jnp.broadcast_to(jnp.exp2(blast), (HG_SUB, HG_DK))
        return s

    def diagonal(s):
        b, q = s["b"], s["q"]
        bs_ref, ks_ref, vs_ref = b_s.at[s["slot"]], k_s.at[s["slot"]], v_s.at[s["slot"]]
        d_blocks = []
        for i in range(nsub):
            sl = slice(i * HG_SUB, (i + 1) * HG_SUB)
            bt, qt = b[sl, :], q[sl, :]
            acc = jnp.zeros((HG_SUB, HG_DK), F32)
            for j in range(HG_SUB):
                r = i * HG_SUB + j
                arg = bt - bcast_row(bs_ref, r, HG_SUB)
                if j > 0:
                    arg = jnp.where(sub_row >= j, arg, masked)
                g = jnp.exp2(arg) * (qt * bcast_row(ks_ref, r, HG_SUB))
                acc = acc + jnp.sum(g, axis=-1, keepdims=True) * bcast_row(vs_ref, r, HG_SUB)
            d_blocks.append(acc)
        oi_s[pl.ds(s["r0"], c), :] = s["o"] + jnp.concatenate(d_blocks, axis=0)

    def carried_group(gi):
        st = st_ref[...]
        outs = []
        for j in range(HG_STATE_UNROLL):
            ci = gi * HG_STATE_UNROLL + j
            r0 = pl.multiple_of(ci * c, c)
            outs.append((r0, _dot_nt(qe_s[pl.ds(r0, c), :], st.astype(BF16))))
            st = st * dec_s[ci][0:1, :] + kv_s[ci]
        st_ref[...] = st
        for r0, os in outs:
            oi_s[pl.ds(r0, c), :] += os

    def finish(ri):
        r0 = pl.multiple_of(ri * HG_NORM_ROWS, HG_NORM_ROWS)
        y = _rms(oi_s[pl.ds(r0, HG_NORM_ROWS), :], ng) * _silu(g_ref[pl.ds(r0, HG_NORM_ROWS), :].astype(F32))
        o_ref[pl.ds(r0, HG_NORM_ROWS), :] = y.astype(o_ref.dtype)

    return prep, (cumulate, off_diagonal, apply_values, diagonal), carried_group, finish


HG_SCRATCH = [
    pltpu.VMEM((HG_DK, HG_DK), F32),
    pltpu.VMEM((HG_UNROLL, HG_CHUNK, HG_DK), F32),
    pltpu.VMEM((HG_UNROLL, HG_CHUNK, HG_DK), F32),
    pltpu.VMEM((HG_UNROLL, HG_CHUNK, HG_DK), F32),
    pltpu.VMEM((SEQ_TILE, HG_DK), F32),
    pltpu.VMEM((SEQ_TILE, HG_DK), BF16),
    pltpu.VMEM((SEQ_TILE // HG_CHUNK, HG_DK, HG_DK), F32),
    pltpu.VMEM((SEQ_TILE // HG_CHUNK, HG_SUB, HG_DK), F32),
]


def _ret_stages(q_ref, k_ref, v_ref, g_ref, cos_ref, sin_ref, ng_ref, o_ref,
                r_ref, oi_s, q16_s, kv_s):
    c = RET_CHUNK

    hf = jnp.full((1, 1), pl.program_id(1), I32).astype(F32)
    lg = jnp.log1p(-jnp.exp2(-5.0 - hf))
    ti = lax.broadcasted_iota(I32, (c, c), 0)
    si = lax.broadcasted_iota(I32, (c, c), 1)
    rel = (ti - si).astype(F32)
    intra = jnp.where(ti >= si, jnp.exp(jnp.maximum(rel, 0.0) * lg), 0.0)
    idx = lax.broadcasted_iota(I32, (c, 1), 0).astype(F32)
    inter = jnp.exp((idx + 1.0) * lg)
    to_state = jnp.exp((c - 1.0 - idx) * lg)
    chunk_decay = jnp.exp(float(c) * lg)
    ng = ng_ref[...]
    half = RET_DK // 2

    def prep(ci):
        r0 = pl.multiple_of(ci * c, c)
        cos = cos_ref[pl.ds(r0, c), :]
        sin = sin_ref[pl.ds(r0, c), :]
        q = q_ref[pl.ds(r0, c), :].astype(F32)
        k = k_ref[pl.ds(r0, c), :].astype(F32)
        qr = (q * cos + pltpu.roll(q, half, 1) * sin) * (RET_DK ** -0.5)
        kr = k * cos + pltpu.roll(k, half, 1) * sin
        q16 = qr.astype(BF16)
        q16_s[pl.ds(r0, c), :] = q16
        return dict(ci=ci, r0=r0, q16=q16, k16=kr.astype(BF16), kts=(kr * to_state).astype(BF16),
                    v16=v_ref[pl.ds(r0, c), :])

    def scores(s):
        s["att"] = (_dot_nt(s.pop("q16"), s.pop("k16")) * intra).astype(BF16)
        return s

    def apply_values(s):
        oi_s[pl.ds(s["r0"], c), :] = _dot(s["att"], s["v16"])
        kv_s[s["ci"]] = _dot_tn(s["kts"], s["v16"])
        return s

    def carried_group(gi):
        r = r_ref[...]
        outs = []
        for j in range(RET_STATE_UNROLL):
            ci = gi * RET_STATE_UNROLL + j
            r0 = pl.multiple_of(ci * c, c)
            outs.append((r0, _dot(q16_s[pl.ds(r0, c), :], r.astype(BF16))))
            r = chunk_decay * r + kv_s[ci]
        r_ref[...] = r
        for r0, qr_state in outs:
            o = oi_s[pl.ds(r0, c), :] + qr_state * inter
            y = _rms(o, ng) * _silu(g_ref[pl.ds(r0, c), :].astype(F32))
            o_ref[pl.ds(r0, c), :] = y.astype(o_ref.dtype)

    return prep, (scores, apply_values), carried_group


RET_SCRATCH = [
    pltpu.VMEM((RET_DK, RET_DV), F32),
    pltpu.VMEM((SEQ_TILE, RET_DV), F32),
    pltpu.VMEM((SEQ_TILE, RET_DK), BF16),
    pltpu.VMEM((SEQ_TILE // RET_CHUNK, RET_DK, RET_DV), F32),
]


def _mixers_kernel(*refs):
    hg_refs = refs[:6] + refs[13:14] + refs[15:15 + len(HG_SCRATCH)]
    ret_refs = refs[6:13] + refs[14:15] + refs[15 + len(HG_SCRATCH):]
    st_ref, r_ref = hg_refs[7], ret_refs[8]
    n_tokens = refs[1].shape[0]

    @pl.when(pl.program_id(2) == 0)
    def _():
        st_ref[...] = jnp.zeros_like(st_ref)
        r_ref[...] = jnp.zeros_like(r_ref)

    hg_prep, hg_local, hg_carried, hg_finish = _hgrn_stages(*hg_refs)
    ret_prep, ret_local, ret_carried = _ret_stages(*ret_refs)
    group_tokens = HG_UNROLL * HG_CHUNK
    ret_unroll = group_tokens // RET_CHUNK

    def local_group(gi, carry):
        hs = [hg_prep(gi * HG_UNROLL + slot, slot) for slot in range(HG_UNROLL)]
        rs = [ret_prep(gi * ret_unroll + j) for j in range(ret_unroll)]
        cumulate, off_diagonal, apply_values, diagonal = hg_local
        scores, ret_apply = ret_local
        hs = [cumulate(s) for s in hs]
        rs = [scores(s) for s in rs]
        hs = [off_diagonal(s) for s in hs]
        rs = [ret_apply(s) for s in rs]
        hs = [apply_values(s) for s in hs]
        for s in hs:
            diagonal(s)
        return carry

    lax.fori_loop(0, n_tokens // group_tokens, local_group, 0)

    state_tokens = HG_STATE_UNROLL * HG_CHUNK
    assert state_tokens == RET_STATE_UNROLL * RET_CHUNK

    def carried_group(gi, carry):
        hg_carried(gi)
        ret_carried(gi)
        return carry

    lax.fori_loop(0, n_tokens // state_tokens, carried_group, 0)

    def finish(ri, carry):
        hg_finish(ri)
        return carry

    lax.fori_loop(0, n_tokens // HG_NORM_ROWS, finish, 0)


def _mixers(proj, hf, lb_logits, hg_norm, cos2, sin2, ret_norm, batch, seq):
    ns = seq // SEQ_TILE
    tok = lambda b, h, s: b * ns + s
    hg_col = lambda base: pl.BlockSpec((SEQ_TILE, HG_DK), lambda b, h, s: (tok(b, h, s), base + h))
    return pl.pallas_call(
        _mixers_kernel,
        grid=(batch, HG_HEADS, ns),
        in_specs=[
            pl.BlockSpec((2, HG_DK), lambda b, h, s: (0, h)),
            hg_col(COL_HQ), hg_col(0), hg_col(COL_HI), hg_col(COL_HG),
            pl.BlockSpec((1, HG_DK), lambda b, h, s: (0, 0)),
            hg_col(COL_RQ), hg_col(COL_RK),
            pl.BlockSpec((SEQ_TILE, RET_DV), lambda b, h, s: (tok(b, h, s), COL_RV // 2 + h)),
            pl.BlockSpec((SEQ_TILE, RET_DV), lambda b, h, s: (tok(b, h, s), COL_RG // 2 + h)),
            pl.BlockSpec((SEQ_TILE, RET_DK), lambda b, h, s: (s, 0)),
            pl.BlockSpec((SEQ_TILE, RET_DK), lambda b, h, s: (s, 0)),
            pl.BlockSpec((1, RET_DV), lambda b, h, s: (0, 0)),
        ],
        out_specs=[
            pl.BlockSpec((SEQ_TILE, HG_DK), lambda b, h, s: (tok(b, h, s), h)),
            pl.BlockSpec((SEQ_TILE, RET_DV), lambda b, h, s: (tok(b, h, s), h)),
        ],
        out_shape=[
            jax.ShapeDtypeStruct((batch * seq, HG_WIDTH), BF16),
            jax.ShapeDtypeStruct((batch * seq, RET_HEADS * RET_DV), BF16),
        ],
        scratch_shapes=HG_SCRATCH + RET_SCRATCH,
        compiler_params=_cparams(("arbitrary", "arbitrary", "arbitrary")),
        name="mixers",
    )(lb_logits, proj, hf, proj, proj, hg_norm, proj, proj, proj, proj, cos2, sin2, ret_norm)


def _merge_kernel(x_ref, ya_ref, yb_ref, ga_ref, gb_ref, wa_ref, wb_ref, wo_ref, fg_ref,
                  wr_ref, br_ref, x1_ref, xs_ref, route_ref, cnt_ref, h2b_s, rows_s):
    tm = x_ref.shape[0]

    @pl.when(pl.program_id(0) == 0)
    def _():
        h2b_s[...] = jnp.zeros_like(h2b_s)
        rows_s[...] = jnp.full(rows_s.shape, -1.0, F32)

    def sort_previous(lo, hi):
        slab_row = lax.broadcasted_iota(I32, (SORT_SLAB, tm), 0).astype(F32)
        for r0 in range(lo, hi, SORT_SLAB):
            sel = ((slab_row == rows_s[0:1, :] - float(r0)) | (slab_row == rows_s[1:2, :] - float(r0)))
            xs_ref[r0:r0 + SORT_SLAB, :] = _dot(jnp.where(sel, 1.0, 0.0).astype(BF16),
                                                h2b_s[...]).astype(BF16)

    merged = (_sigmoid(ga_ref[...].astype(F32)) * _dot(ya_ref[...], wa_ref[...])
              + _sigmoid(gb_ref[...].astype(F32)) * _dot(yb_ref[...], wb_ref[...]))
    x1 = x_ref[...] + _dot(merged.astype(BF16), wo_ref[...])
    x1_ref[...] = x1
    h2 = _rms(x1, fg_ref[...])

    h2b = h2.astype(BF16)
    logits = _dot(h2b, wr_ref[...]) + br_ref[...]
    sort_previous(0, LOCAL_ROWS)

    lane = lax.broadcasted_iota(I32, (tm, LANES), 1)
    neg = jnp.float32(-jnp.inf)
    big = jnp.int32(1 << 30)
    is_g = lane < N_GROUPS
    gl = jnp.where(is_g, logits, neg)
    gmax = jnp.max(gl, axis=-1, keepdims=True)
    g_idx = jnp.min(jnp.where(gl == gmax, lane, big), axis=-1, keepdims=True)
    g_w = 1.0 / jnp.sum(jnp.where(is_g, jnp.exp(gl - gmax), 0.0), axis=-1, keepdims=True)

    ex = lane - N_GROUPS
    in_grp = (ex >= g_idx * EXPERTS_PER_GROUP) & (ex < (g_idx + 1) * EXPERTS_PER_GROUP)
    el = jnp.where(in_grp, logits, neg)
    m1 = jnp.max(el, axis=-1, keepdims=True)
    e1 = jnp.min(jnp.where(el == m1, ex, big), axis=-1, keepdims=True)
    el2 = jnp.where(ex == e1, neg, el)
    m2 = jnp.max(el2, axis=-1, keepdims=True)
    e2 = jnp.min(jnp.where(el2 == m2, ex, big), axis=-1, keepdims=True)
    p2 = jnp.exp(m2 - m1)
    w1 = g_w / (1.0 + p2)
    w2 = g_w * p2 / (1.0 + p2)

    oh1 = ex == e1
    oh2 = ex == e2
    oh = jnp.where(oh1 | oh2, 1.0, 0.0)
    ri = lax.broadcasted_iota(I32, (tm, tm), 0)
    ci = lax.broadcasted_iota(I32, (tm, tm), 1)
    strict = jnp.where(ri > ci, 1.0, 0.0).astype(BF16)
    local_rank = _dot(strict, oh.astype(BF16))
    cnt = jnp.sum(oh, axis=0, keepdims=True)
    run_chunks = jnp.floor((cnt + (ROW_CHUNK - 1.0)) * (1.0 / ROW_CHUNK))
    ui = lax.broadcasted_iota(I32, (LANES, LANES), 0)
    uj = lax.broadcasted_iota(I32, (LANES, LANES), 1)
    before = jnp.where(ui < uj, 1.0, 0.0).astype(BF16)
    run_start = _dot(jnp.broadcast_to(run_chunks, (8, LANES)).astype(BF16), before)[0:1] * ROW_CHUNK
    slot = run_start + local_rank
    pos1 = jnp.sum(jnp.where(oh1, slot, 0.0), axis=-1, keepdims=True)
    pos2 = jnp.sum(jnp.where(oh2, slot, 0.0), axis=-1, keepdims=True)
    cnt_ref[0] = jnp.broadcast_to(cnt, (8, LANES))

    route = jnp.where(lane == 0, e1.astype(F32), 0.0)
    route = jnp.where(lane == 1, e2.astype(F32), route)
    route = jnp.where(lane == 2, w1, route)
    route = jnp.where(lane == 3, w2, route)
    route = jnp.where(lane == 4, pos1, route)
    route = jnp.where(lane == 5, pos2, route)
    route_ref[...] = route

    hi1 = jnp.floor(pos1 * (1.0 / 256.0))
    hi2 = jnp.floor(pos2 * (1.0 / 256.0))
    digits = jnp.where(lane == 0, hi1, 0.0)
    digits = jnp.where(lane == 1, pos1 - 256.0 * hi1, digits)
    digits = jnp.where(lane == 2, hi2, digits)
    digits = jnp.where(lane == 3, pos2 - 256.0 * hi2, digits)
    pick = jnp.where(lax.broadcasted_iota(I32, (8, LANES), 0) == lax.broadcasted_iota(I32, (8, LANES), 1),
                     1.0, 0.0).astype(BF16)
    rows = _dot_nt(pick, digits.astype(BF16))

    h2b_s[...] = h2b
    rows_s[0:1, :] = rows[0:1] * 256.0 + rows[1:2]
    rows_s[1:2, :] = rows[2:3] * 256.0 + rows[3:4]


def _merge(x2d, ya, yb, proj, wa, wb, wo, ffn_g, w_router, b_router):
    t = x2d.shape[0]
    nb = t // TOK_TILE
    const = lambda *shape: pl.BlockSpec(shape, lambda i: (0,) * len(shape))
    cur = lambda i: jnp.minimum(i, nb - 1)
    prev = lambda i: jnp.maximum(i - 1, 0)
    return pl.pallas_call(
        _merge_kernel,
        grid=(nb + 1,),
        in_specs=[
            pl.BlockSpec((TOK_TILE, D_MODEL), lambda i: (cur(i), 0)),
            pl.BlockSpec((TOK_TILE, HG_WIDTH), lambda i: (cur(i), 0)),
            pl.BlockSpec((TOK_TILE, D_MODEL), lambda i: (cur(i), 0)),
            pl.BlockSpec((TOK_TILE, D_MODEL), lambda i: (cur(i), COL_GA)),
            pl.BlockSpec((TOK_TILE, D_MODEL), lambda i: (cur(i), COL_GB)),
            const(HG_WIDTH, D_MODEL),
            const(D_MODEL, D_MODEL),
            const(D_MODEL, D_MODEL),
            const(1, D_MODEL),
            const(D_MODEL, LANES),
            const(1, LANES),
        ],
        out_specs=[
            pl.BlockSpec((TOK_TILE, D_MODEL), lambda i: (cur(i), 0)),
            pl.BlockSpec((LOCAL_ROWS, D_MODEL), lambda i: (prev(i), 0)),
            pl.BlockSpec((TOK_TILE, LANES), lambda i: (cur(i), 0)),
            pl.BlockSpec((1, 8, LANES), lambda i: (cur(i), 0, 0)),
        ],
        out_shape=[
            jax.ShapeDtypeStruct((t, D_MODEL), F32),
            jax.ShapeDtypeStruct((t // TOK_TILE * LOCAL_ROWS, D_MODEL), BF16),
            jax.ShapeDtypeStruct((t, LANES), F32),
            jax.ShapeDtypeStruct((t // TOK_TILE, 8, LANES), F32),
        ],
        scratch_shapes=[pltpu.VMEM((TOK_TILE, D_MODEL), BF16), pltpu.VMEM((8, TOK_TILE), F32)],
        compiler_params=_cparams(("arbitrary",)),
        name="merge_route",
    )(x2d, ya, yb, proj, proj, wa, wb, wo, ffn_g, w_router, b_router)


def _expert_kernel(te_ref, nt_ref, src_ref, xs_ref, wg_ref, wu_ref, wd_ref, ys_ref, xbuf, sem):
    i = pl.program_id(0)
    nt = nt_ref[0]

    def gather(tile, slot):
        copies = []
        for c in range(CHUNKS_PER_TILE):
            row = pl.multiple_of(src_ref[tile * CHUNKS_PER_TILE + c] * ROW_CHUNK, ROW_CHUNK)
            copies.append(pltpu.make_async_copy(
                xs_ref.at[pl.ds(row, ROW_CHUNK)],
                xbuf.at[slot, pl.ds(c * ROW_CHUNK, ROW_CHUNK)], sem.at[slot]))
        return copies

    @pl.when(i == 0)
    def _():
        for k in range(EXP_AHEAD):
            for cp in gather(k, k):
                cp.start()

    @pl.when(i < nt + EXP_AHEAD)
    def _():
        for cp in gather(i, i % EXP_SLOTS):
            cp.wait()

    @pl.when(i < nt)
    def _():
        slot = i % EXP_SLOTS
        wg, wu, wd = wg_ref[0], wu_ref[0], wd_ref[0]
        subs = [pl.ds(r, EXP_SUB) for r in range(0, EXP_TILE, EXP_SUB)]
        xs = [xbuf[slot, s, :] for s in subs]
        gates = [(_dot(x, wg), _dot(x, wu)) for x in xs]
        for c, cp in enumerate(gather(i + EXP_AHEAD, (i + EXP_AHEAD) % EXP_SLOTS)):
            cp.start(priority=c % 2)
        hidden = [(_silu(a) * u).astype(BF16) for a, u in gates]
        for s, h in zip(subs, hidden):
            ys_ref[s, :] = _dot(h, wd).astype(ys_ref.dtype)

    @pl.when(i >= nt)
    def _():
        ys_ref[...] = jnp.zeros_like(ys_ref)


def _experts(tile_expert, n_tiles_used, src_chunk, xs, wg, wu, wd, n_tiles):
    grid_spec = pltpu.PrefetchScalarGridSpec(
        num_scalar_prefetch=3,
        grid=(n_tiles,),
        in_specs=[
            pl.BlockSpec(memory_space=pl.ANY),
            pl.BlockSpec((1, D_MODEL, D_EXPERT), lambda i, te, nt, src: (te[i], 0, 0)),
            pl.BlockSpec((1, D_MODEL, D_EXPERT), lambda i, te, nt, src: (te[i], 0, 0)),
            pl.BlockSpec((1, D_EXPERT, D_MODEL), lambda i, te, nt, src: (te[i], 0, 0)),
        ],
        out_specs=pl.BlockSpec((EXP_TILE, D_MODEL), lambda i, te, nt, src: (i, 0)),
        scratch_shapes=[
            pltpu.VMEM((EXP_SLOTS, EXP_TILE, D_MODEL), BF16),
            pltpu.SemaphoreType.DMA((EXP_SLOTS,)),
        ],
    )
    return pl.pallas_call(
        _expert_kernel,
        grid_spec=grid_spec,
        out_shape=jax.ShapeDtypeStruct((n_tiles * EXP_TILE, D_MODEL), BF16),
        compiler_params=_cparams(("arbitrary",)),
        name="experts",
    )(tile_expert, n_tiles_used, src_chunk, xs, wg, wu, wd)


def _final_kernel(src_ref, x1_ref, route_ref, p_ref, ys_ref, pg_ref, wpg_ref, wpp_ref, fg_ref,
                  o_ref, ybuf, sem):
    tm = x1_ref.shape[0]
    i = pl.program_id(0)
    nb = pl.num_programs(0)

    def gather(block, slot):
        copies = []
        for c in range(LOCAL_CHUNKS):
            row = pl.multiple_of(src_ref[block * LOCAL_CHUNKS + c] * ROW_CHUNK, ROW_CHUNK)
            copies.append(pltpu.make_async_copy(
                ys_ref.at[pl.ds(row, ROW_CHUNK)],
                ybuf.at[slot, pl.ds(c * ROW_CHUNK, ROW_CHUNK)], sem.at[slot]))
        return copies

    @pl.when(i == 0)
    def _():
        for k in range(FIN_AHEAD):
            for cp in gather(k % nb, k):
                cp.start()

    slot = i % FIN_SLOTS
    for cp in gather(i, slot):
        cp.wait()

    ple = _dot(p_ref[...].astype(BF16), wpp_ref[...])

    route = route_ref[...]
    w1, w2 = route[:, 2:3], route[:, 3:4]
    pos1, pos2 = route[:, 4:5], route[:, 5:6]
    slab_col = lax.broadcasted_iota(I32, (tm, SORT_SLAB), 1).astype(F32)
    moe = jnp.zeros((tm, D_MODEL), F32)
    for k0 in range(0, LOCAL_ROWS, SORT_SLAB):
        sel = jnp.where(slab_col == pos1 - float(k0), w1,
                        jnp.where(slab_col == pos2 - float(k0), w2, 0.0)).astype(BF16)
        moe = moe + _dot(sel, ybuf[slot, k0:k0 + SORT_SLAB, :])
    for c, cp in enumerate(gather((i + FIN_AHEAD) % nb, (i + FIN_AHEAD) % FIN_SLOTS)):
        cp.start(priority=c % 2)
    x2 = x1_ref[...] + moe
    hp = _rms(x2, pg_ref[...]).astype(BF16)
    gate = _sigmoid(_dot(hp, wpg_ref[...]))
    x3 = x2 + gate * ple
    o_ref[...] = _rms(x3, fg_ref[...])

    @pl.when(i == nb - 1)
    def _():
        for k in range(1, FIN_AHEAD + 1):
            for cp in gather(0, (i + k) % FIN_SLOTS):
                cp.wait()


def _final(src_chunk, x1, route, p2d, ys, ple_g, wpg, wpp, final_g):
    t = x1.shape[0]
    const = lambda *shape: pl.BlockSpec(shape, lambda i, src: (0,) * len(shape))
    grid_spec = pltpu.PrefetchScalarGridSpec(
        num_scalar_prefetch=1,
        grid=(t // TOK_TILE,),
        in_specs=[
            pl.BlockSpec((TOK_TILE, D_MODEL), lambda i, src: (i, 0)),
            pl.BlockSpec((TOK_TILE, LANES), lambda i, src: (i, 0)),
            pl.BlockSpec((TOK_TILE, PLE_DIM), lambda i, src: (i, 0)),
            pl.BlockSpec(memory_space=pl.ANY),
            const(1, D_MODEL),
            const(D_MODEL, D_MODEL),
            const(PLE_DIM, D_MODEL),
            const(1, D_MODEL),
        ],
        out_specs=pl.BlockSpec((TOK_TILE, D_MODEL), lambda i, src: (i, 0)),
        scratch_shapes=[pltpu.VMEM((FIN_SLOTS, LOCAL_ROWS, D_MODEL), BF16),
                        pltpu.SemaphoreType.DMA((FIN_SLOTS,))],
    )
    return pl.pallas_call(
        _final_kernel,
        grid_spec=grid_spec,
        out_shape=jax.ShapeDtypeStruct((t, D_MODEL), F32),
        compiler_params=_cparams(("arbitrary",)),
        name="combine_ple_final",
    )(src_chunk, x1, route, p2d, ys, ple_g, wpg, wpp, final_g)


def _rotary_tables(seq):
    inv = ROPE_BASE ** (-jnp.arange(0, RET_DK, 2, dtype=F32) / RET_DK)
    inv = jnp.concatenate([inv, inv])
    sign = jnp.where(jnp.arange(RET_DK) < RET_DK // 2, -1.0, 1.0).astype(F32)
    hi = (jnp.arange(seq // ROPE_SPLIT, dtype=F32) * ROPE_SPLIT)[:, None] * inv[None, :]
    lo = jnp.arange(ROPE_SPLIT, dtype=F32)[:, None] * inv[None, :]
    ch, sh = jnp.cos(hi)[:, None, :], jnp.sin(hi)[:, None, :]
    cl, sl = jnp.cos(lo)[None], jnp.sin(lo)[None]
    cos = (ch * cl - sh * sl).reshape(seq, RET_DK)
    sin = ((sh * cl + ch * sl) * sign).reshape(seq, RET_DK)
    return cos, sin


def _layer(x2d, p2d, batch, seq, mix_norm, w_in, hg_lb_logits, hg_norm, ret_norm, w_branch_a,
           w_branch_b, w_out, ffn_norm, w_rg, b_rg, w_re, b_re, w_gate, w_up, w_down, ple_norm,
           w_ple_gate, w_ple_proj, out_gain):
    t = x2d.shape[0]
    row = lambda v: v.reshape(1, -1).astype(F32)

    proj, hf, wg16, wu16, wd16 = _inproj(x2d, row(mix_norm), w_in.astype(BF16), w_gate, w_up, w_down)
    cos2, sin2 = _rotary_tables(seq)
    ya, yb = _mixers(proj, hf, hg_lb_logits.astype(F32), row(hg_norm), cos2, sin2, row(ret_norm), batch, seq)

    n_r = N_GROUPS + N_EXPERTS
    w_router = jnp.pad(jnp.concatenate([w_rg, w_re], axis=1).astype(BF16), ((0, 0), (0, LANES - n_r)))
    b_router = jnp.zeros((1, LANES), F32).at[0, :n_r].set(jnp.concatenate([b_rg, b_re]))
    x1, xs, route, counts = _merge(
        x2d, ya, yb, proj, w_branch_a.astype(BF16), w_branch_b.astype(BF16), w_out.astype(BF16),
        row(ffn_norm), w_router, b_router)

    n_blocks = t // TOK_TILE
    cnt = counts[:, 0, N_GROUPS:N_GROUPS + N_EXPERTS].astype(I32)
    run_chunks = (cnt + ROW_CHUNK - 1) // ROW_CHUNK
    earlier_e = jnp.tril(jnp.ones((N_EXPERTS, N_EXPERTS), I32), -1)
    earlier_b = jnp.tril(jnp.ones((n_blocks, n_blocks), I32), -1)
    run_local = jnp.sum(run_chunks[:, None, :] * earlier_e[None], axis=2)
    seg_chunks = jnp.sum(run_chunks, axis=0)
    tiles_per = (seg_chunks + CHUNKS_PER_TILE - 1) // CHUNKS_PER_TILE
    seg_start = jnp.sum(tiles_per[None, :] * earlier_e, axis=1) * CHUNKS_PER_TILE
    tile_end = seg_start // CHUNKS_PER_TILE + tiles_per
    run_global = seg_start[None, :] + jnp.sum(run_chunks.T[:, None, :] * earlier_b[None], axis=2).T
    max_chunks = (2 * t) // ROW_CHUNK + n_blocks * N_EXPERTS + N_EXPERTS * (CHUNKS_PER_TILE - 1)
    n_tiles = -(-max_chunks // CHUNKS_PER_TILE) + EXP_AHEAD
    tile_ids = jnp.arange(n_tiles, dtype=I32)
    tile_expert = jnp.minimum(jnp.sum((tile_end[None, :] <= tile_ids[:, None]).astype(I32), axis=1),
                              N_EXPERTS - 1)
    n_used = tile_end[-1:].astype(I32)

    block_ids = jnp.arange(n_blocks, dtype=I32)
    zero_local = LOCAL_CHUNKS - 1
    zero_global = n_tiles * CHUNKS_PER_TILE - 1
    g = jnp.arange(n_tiles * CHUNKS_PER_TILE, dtype=I32)[:, None]
    e_g = jnp.repeat(tile_expert, CHUNKS_PER_TILE)
    pick_e = (e_g[:, None] == jnp.arange(N_EXPERTS, dtype=I32)[None, :]).astype(I32)
    rg, rc, rl = (jnp.sum(pick_e[:, :, None] * tab.T[None, :, :], axis=1)
                  for tab in (run_global, run_chunks, run_local))
    inside = (rg <= g) & (g < rg + rc)
    gather_src = jnp.sum(jnp.where(inside, block_ids[None, :] * LOCAL_CHUNKS + rl + (g - rg), 0), axis=1)
    gather_src = jnp.where(jnp.any(inside, axis=1), gather_src, zero_local).astype(I32)

    lc = jnp.arange(LOCAL_CHUNKS, dtype=I32)[None, :, None]
    inside = (run_local[:, None, :] <= lc) & (lc < (run_local + run_chunks)[:, None, :])
    back_src = jnp.sum(jnp.where(inside, run_global[:, None, :] + lc - run_local[:, None, :], 0), axis=2)
    back_src = jnp.where(jnp.any(inside, axis=2), back_src, zero_global).astype(I32).reshape(-1)

    ys = _experts(tile_expert, n_used, gather_src, xs, wg16, wu16, wd16, n_tiles)
    return _final(back_src, x1, route, p2d, ys, row(ple_norm), w_ple_gate.astype(BF16),
                  w_ple_proj.astype(BF16), out_gain)


def kernel(x, p, mix_norm, w_in, hg_lb_logits, hg_norm, ret_norm, w_branch_a, w_branch_b, w_out,
           ffn_norm, w_router_group, b_router_group, w_router_expert, b_router_expert,
           w_expert_gate, w_expert_up, w_expert_down, ple_norm, w_ple_gate, w_ple_proj, final_norm):
    batch, seq, d = x.shape
    depth = p.shape[0]
    assert depth == 1, "the final rmsnorm is fused into the single layer"
    x2d = x.reshape(batch * seq, d)
    out = _layer(x2d, p[0].reshape(batch * seq, -1), batch, seq, mix_norm[0], w_in[0], hg_lb_logits,
                 hg_norm[0], ret_norm[0], w_branch_a[0], w_branch_b[0], w_out[0], ffn_norm[0],
                 w_router_group[0], b_router_group[0], w_router_expert[0], b_router_expert[0],
                 w_expert_gate[0], w_expert_up[0], w_expert_down[0], ple_norm[0], w_ple_gate[0],
                 w_ple_proj[0], final_norm.reshape(1, -1).astype(F32))
    return out.reshape(batch, seq, d)
```

```python
import jax
import jax.numpy as jnp
from jax import lax
from jax.experimental import pallas as pl
from jax.experimental.pallas import tpu as pltpu

F32 = jnp.float32
BF16 = jnp.bfloat16
I32 = jnp.int32

EPS = 1e-6
D_MODEL = 1024
PLE_DIM = 256
HG_HEADS = 4
HG_DK = 128
HG_WIDTH = HG_HEADS * HG_DK
RET_HEADS = 4
RET_DK = 128
RET_DV = 256
ROPE_BASE = 10000.0
ROPE_SPLIT = 64
IN_TOTAL = 7168
N_GROUPS = 4
EXPERTS_PER_GROUP = 8
N_EXPERTS = 32
D_EXPERT = 256

COL_HQ, COL_HF, COL_HI, COL_HG = 0, 4, 8, 12
COL_RQ, COL_RK = 16, 20
COL_RV, COL_RG = 24, 32
COL_GA, COL_GB = 5, 6

LANES = 128
VMEM_LIMIT = 56 * 1024 * 1024

HG_CHUNK = 64
HG_SUB = 8
HG_UNROLL = 16
HG_STATE_UNROLL = 16
HG_NORM_ROWS = 1024
RET_CHUNK = 128
RET_STATE_UNROLL = 8
SEQ_TILE = 1024
TOK_TILE = 512
EXP_TILE = 512
EXP_SUB = 512
EXP_AHEAD = 3
EXP_SLOTS = EXP_AHEAD + 1
ROW_CHUNK = 16
CHUNKS_PER_TILE = EXP_TILE // ROW_CHUNK
LOCAL_ROWS = 2 * TOK_TILE + N_EXPERTS * ROW_CHUNK
LOCAL_CHUNKS = LOCAL_ROWS // ROW_CHUNK
SORT_SLAB = 256
FIN_AHEAD = 2
FIN_SLOTS = FIN_AHEAD + 1
FIN_PARTS = 2


def _cparams(sem):
    return pltpu.CompilerParams(dimension_semantics=sem, vmem_limit_bytes=VMEM_LIMIT)


def _rms(x, g):
    return x * lax.rsqrt(jnp.mean(x * x, axis=-1, keepdims=True) + EPS) * g


def _sigmoid(x):
    return 1.0 / (1.0 + jnp.exp(-x))


def _silu(x):
    return x * _sigmoid(x)


def _split3(x):
    hi = x.astype(BF16)
    r1 = x - hi.astype(F32)
    mid = r1.astype(BF16)
    lo = (r1 - mid.astype(F32)).astype(BF16)
    return hi, mid, lo


def _dot(a, b):
    return jnp.dot(a, b, preferred_element_type=F32)


def _dot_nt(a, b):
    return lax.dot_general(a, b, (((1,), (1,)), ((), ())), preferred_element_type=F32)


def _dot_tn(a, b):
    return lax.dot_general(a, b, (((0,), (0,)), ((), ())), preferred_element_type=F32)


def _inproj_kernel(x_ref, g_ref, w_ref, wg_ref, wu_ref, wd_ref, proj_ref, hf_ref,
                   wg16_ref, wu16_ref, wd16_ref):
    wg16_ref[...] = wg_ref[...].astype(BF16)
    wu16_ref[...] = wu_ref[...].astype(BF16)
    wd16_ref[...] = wd_ref[...].astype(BF16)

    h = _rms(x_ref[...], g_ref[...]).astype(BF16)
    tn = HG_WIDTH
    for j in range(IN_TOTAL // tn):
        acc = _dot(h, w_ref[:, j * tn:(j + 1) * tn])
        proj_ref[:, j * tn:(j + 1) * tn] = acc.astype(BF16)
        if j * tn == COL_HF * LANES:
            hf_ref[...] = acc


def _inproj(x2d, gain, w_bf16, w_gate, w_up, w_down):
    t = x2d.shape[0]
    steps = t // TOK_TILE
    assert N_EXPERTS % steps == 0, "expert weights are converted in equal shares per grid step"
    share = N_EXPERTS // steps
    up_spec = pl.BlockSpec((share, D_MODEL, D_EXPERT), lambda i: (i, 0, 0))
    down_spec = pl.BlockSpec((share, D_EXPERT, D_MODEL), lambda i: (i, 0, 0))
    return pl.pallas_call(
        _inproj_kernel,
        grid=(steps,),
        in_specs=[
            pl.BlockSpec((TOK_TILE, D_MODEL), lambda i: (i, 0)),
            pl.BlockSpec((1, D_MODEL), lambda i: (0, 0)),
            pl.BlockSpec((D_MODEL, IN_TOTAL), lambda i: (0, 0), pipeline_mode=pl.Buffered(1)),
            up_spec, up_spec, down_spec,
        ],
        out_specs=[
            pl.BlockSpec((TOK_TILE, IN_TOTAL), lambda i: (i, 0)),
            pl.BlockSpec((TOK_TILE, HG_WIDTH), lambda i: (i, 0)),
            up_spec, up_spec, down_spec,
        ],
        out_shape=[
            jax.ShapeDtypeStruct((t, IN_TOTAL), BF16),
            jax.ShapeDtypeStruct((t, HG_WIDTH), F32),
            jax.ShapeDtypeStruct(w_gate.shape, BF16),
            jax.ShapeDtypeStruct(w_up.shape, BF16),
            jax.ShapeDtypeStruct(w_down.shape, BF16),
        ],
        compiler_params=_cparams(("arbitrary",)),
        name="inproj",
    )(x2d, gain, w_bf16, w_gate, w_up, w_down)


def _hgrn_stages(lbl_ref, q_ref, f_ref, i_ref, g_ref, ng_ref, o_ref,
                 st_ref, b_s, k_s, v_s, oi_s, qe_s, kv_s, dec_s):
    c = HG_CHUNK
    nsub = c // HG_SUB

    logits = lbl_ref[...]
    e = jnp.exp(logits - jnp.max(logits, axis=0, keepdims=True))
    lb = e[0:1] / jnp.sum(e, axis=0, keepdims=True)
    one_m_lb = jnp.sum(e[1:], axis=0, keepdims=True) / jnp.sum(e, axis=0, keepdims=True)
    ng = ng_ref[...]

    row = lax.broadcasted_iota(I32, (c, c), 0)
    col = lax.broadcasted_iota(I32, (c, c), 1)
    tri = jnp.where(row >= col, 1.0, 0.0).astype(BF16)
    sub_row = lax.broadcasted_iota(I32, (HG_SUB, HG_DK), 0)
    masked = jnp.float32(-1e30)

    def bcast_row(ref, r, rows):
        return jnp.broadcast_to(ref[pl.ds(r, 1), :], (rows, HG_DK))

    def prep(ci, slot):
        r0 = pl.multiple_of(ci * c, c)
        z = f_ref[pl.ds(r0, c), :]
        ez = jnp.exp(-jnp.abs(z))
        rz = 1.0 / (1.0 + ez)
        pos = z >= 0.0
        logf = jnp.log2(lb + one_m_lb * jnp.where(pos, rz, ez * rz))
        kk = one_m_lb * jnp.where(pos, ez * rz, rz)
        q = _silu(q_ref[pl.ds(r0, c), :].astype(F32))
        v = i_ref[pl.ds(r0, c), :].astype(F32)
        k_s[slot] = kk
        v_s[slot] = v
        return dict(ci=ci, r0=r0, slot=slot, kk=kk, q=q, v=v, v16=v.astype(BF16), parts=_split3(logf))

    def cumulate(s):
        hi, mid, lo = s.pop("parts")
        b = (_dot(tri, lo) + _dot(tri, mid)) + _dot(tri, hi)
        b_s[s["slot"]] = b
        qe_s[pl.ds(s["r0"], c), :] = (s["q"] * jnp.exp2(b)).astype(BF16)
        s["b"] = b
        return s

    def off_diagonal(s):
        b, q, kk = s["b"], s["q"], s["kk"]
        bs_ref = b_s.at[s["slot"]]
        squares = []
        size = c // 2
        while size >= HG_SUB:
            for r0 in range(size, c, 2 * size):
                edge = bcast_row(bs_ref, r0 - 1, size)
                qs = q[r0:r0 + size, :] * jnp.exp2(b[r0:r0 + size, :] - edge)
                ks = kk[r0 - size:r0, :] * jnp.exp2(edge - b[r0 - size:r0, :])
                squares.append((r0, size, _dot_nt(qs.astype(BF16), ks.astype(BF16))))
            size //= 2
        s["squares"] = squares
        return s

    def apply_values(s):
        b, kk, v = s["b"], s["kk"], s["v"]
        blast = b[c - 1:c, :]
        rows = [jnp.zeros((HG_SUB, HG_DK), F32) for _ in range(nsub)]
        for r0, size, a in s.pop("squares"):
            part = _dot(a.astype(BF16), v[r0 - size:r0, :].astype(BF16))
            for i in range(size // HG_SUB):
                rows[r0 // HG_SUB + i] = rows[r0 // HG_SUB + i] + part[i * HG_SUB:(i + 1) * HG_SUB, :]
        s["o"] = jnp.concatenate(rows, axis=0)
        kd = kk * jnp.exp2(blast - b)
        kv_s[s["ci"]] = _dot_tn(s["v16"], kd.astype(BF16))
        dec_s[s["ci"]] = jnp.broadcast_to(jnp.exp2(blast), (HG_SUB, HG_DK))
        return s

    def diagonal(s):
        b, q = s["b"], s["q"]
        bs_ref, ks_ref, vs_ref = b_s.at[s["slot"]], k_s.at[s["slot"]], v_s.at[s["slot"]]
        d_blocks = []
        for i in range(nsub):
            sl = slice(i * HG_SUB, (i + 1) * HG_SUB)
            bt, qt = b[sl, :], q[sl, :]
            acc = jnp.zeros((HG_SUB, HG_DK), F32)
            for j in range(HG_SUB):
                r = i * HG_SUB + j
                arg = bt - bcast_row(bs_ref, r, HG_SUB)
                if j > 0:
                    arg = jnp.where(sub_row >= j, arg, masked)
                g = jnp.exp2(arg) * (qt * bcast_row(ks_ref, r, HG_SUB))
                acc = acc + jnp.sum(g, axis=-1, keepdims=True) * bcast_row(vs_ref, r, HG_SUB)
            d_blocks.append(acc)
        oi_s[pl.ds(s["r0"], c), :] = s["o"] + jnp.concatenate(d_blocks, axis=0)

    def carried_group(gi):
        st = st_ref[...]
        outs = []
        for j in range(HG_STATE_UNROLL):
            ci = gi * HG_STATE_UNROLL + j
            r0 = pl.multiple_of(ci * c, c)
            outs.append((r0, _dot_nt(qe_s[pl.ds(r0, c), :], st.astype(BF16))))
            st = st * dec_s[ci][0:1, :] + kv_s[ci]
        st_ref[...] = st
        for r0, os in outs:
            oi_s[pl.ds(r0, c), :] += os

    def finish(ri):
        r0 = pl.multiple_of(ri * HG_NORM_ROWS, HG_NORM_ROWS)
        y = _rms(oi_s[pl.ds(r0, HG_NORM_ROWS), :], ng) * _silu(g_ref[pl.ds(r0, HG_NORM_ROWS), :].astype(F32))
        o_ref[pl.ds(r0, HG_NORM_ROWS), :] = y.astype(o_ref.dtype)

    return prep, (cumulate, off_diagonal, apply_values, diagonal), carried_group, finish


HG_SCRATCH = [
    pltpu.VMEM((HG_DK, HG_DK), F32),
    pltpu.VMEM((HG_UNROLL, HG_CHUNK, HG_DK), F32),
    pltpu.VMEM((HG_UNROLL, HG_CHUNK, HG_DK), F32),
    pltpu.VMEM((HG_UNROLL, HG_CHUNK, HG_DK), F32),
    pltpu.VMEM((SEQ_TILE, HG_DK), F32),
    pltpu.VMEM((SEQ_TILE, HG_DK), BF16),
    pltpu.VMEM((SEQ_TILE // HG_CHUNK, HG_DK, HG_DK), F32),
    pltpu.VMEM((SEQ_TILE // HG_CHUNK, HG_SUB, HG_DK), F32),
]


def _ret_stages(q_ref, k_ref, v_ref, g_ref, cos_ref, sin_ref, ng_ref, o_ref,
                r_ref, oi_s, q16_s, kv_s):
    c = RET_CHUNK

    hf = jnp.full((1, 1), pl.program_id(1), I32).astype(F32)
    lg = jnp.log1p(-jnp.exp2(-5.0 - hf))
    ti = lax.broadcasted_iota(I32, (c, c), 0)
    si = lax.broadcasted_iota(I32, (c, c), 1)
    rel = (ti - si).astype(F32)
    intra = jnp.where(ti >= si, jnp.exp(jnp.maximum(rel, 0.0) * lg), 0.0)
    idx = lax.broadcasted_iota(I32, (c, 1), 0).astype(F32)
    inter = jnp.exp((idx + 1.0) * lg)
    to_state = jnp.exp((c - 1.0 - idx) * lg)
    chunk_decay = jnp.exp(float(c) * lg)
    ng = ng_ref[...]
    half = RET_DK // 2

    def prep(ci):
        r0 = pl.multiple_of(ci * c, c)
        cos = cos_ref[pl.ds(r0, c), :]
        sin = sin_ref[pl.ds(r0, c), :]
        q = q_ref[pl.ds(r0, c), :].astype(F32)
        k = k_ref[pl.ds(r0, c), :].astype(F32)
        qr = (q * cos + pltpu.roll(q, half, 1) * sin) * (RET_DK ** -0.5)
        kr = k * cos + pltpu.roll(k, half, 1) * sin
        q16 = qr.astype(BF16)
        q16_s[pl.ds(r0, c), :] = q16
        return dict(ci=ci, r0=r0, q16=q16, k16=kr.astype(BF16), kts=(kr * to_state).astype(BF16),
                    v16=v_ref[pl.ds(r0, c), :])

    def scores(s):
        s["att"] = (_dot_nt(s.pop("q16"), s.pop("k16")) * intra).astype(BF16)
        return s

    def apply_values(s):
        oi_s[pl.ds(s["r0"], c), :] = _dot(s["att"], s["v16"])
        kv_s[s["ci"]] = _dot_tn(s["kts"], s["v16"])
        return s

    def carried_group(gi):
        r = r_ref[...]
        outs = []
        for j in range(RET_STATE_UNROLL):
            ci = gi * RET_STATE_UNROLL + j
            r0 = pl.multiple_of(ci * c, c)
            outs.append((r0, _dot(q16_s[pl.ds(r0, c), :], r.astype(BF16))))
            r = chunk_decay * r + kv_s[ci]
        r_ref[...] = r
        for r0, qr_state in outs:
            o = oi_s[pl.ds(r0, c), :] + qr_state * inter
            y = _rms(o, ng) * _silu(g_ref[pl.ds(r0, c), :].astype(F32))
            o_ref[pl.ds(r0, c), :] = y.astype(o_ref.dtype)

    return prep, (scores, apply_values), carried_group


RET_SCRATCH = [
    pltpu.VMEM((RET_DK, RET_DV), F32),
    pltpu.VMEM((SEQ_TILE, RET_DV), F32),
    pltpu.VMEM((SEQ_TILE, RET_DK), BF16),
    pltpu.VMEM((SEQ_TILE // RET_CHUNK, RET_DK, RET_DV), F32),
]


def _mixers_kernel(*refs):
    hg_refs = refs[:6] + refs[13:14] + refs[15:15 + len(HG_SCRATCH)]
    ret_refs = refs[6:13] + refs[14:15] + refs[15 + len(HG_SCRATCH):]
    st_ref, r_ref = hg_refs[7], ret_refs[8]
    n_tokens = refs[1].shape[0]

    @pl.when(pl.program_id(2) == 0)
    def _():
        st_ref[...] = jnp.zeros_like(st_ref)
        r_ref[...] = jnp.zeros_like(r_ref)

    hg_prep, hg_local, hg_carried, hg_finish = _hgrn_stages(*hg_refs)
    ret_prep, ret_local, ret_carried = _ret_stages(*ret_refs)
    group_tokens = HG_UNROLL * HG_CHUNK
    ret_unroll = group_tokens // RET_CHUNK

    def local_group(gi, carry):
        hs = [hg_prep(gi * HG_UNROLL + slot, slot) for slot in range(HG_UNROLL)]
        rs = [ret_prep(gi * ret_unroll + j) for j in range(ret_unroll)]
        cumulate, off_diagonal, apply_values, diagonal = hg_local
        scores, ret_apply = ret_local
        hs = [cumulate(s) for s in hs]
        rs = [scores(s) for s in rs]
        hs = [off_diagonal(s) for s in hs]
        rs = [ret_apply(s) for s in rs]
        hs = [apply_values(s) for s in hs]
        for s in hs:
            diagonal(s)
        return carry

    lax.fori_loop(0, n_tokens // group_tokens, local_group, 0)

    state_tokens = HG_STATE_UNROLL * HG_CHUNK
    assert state_tokens == RET_STATE_UNROLL * RET_CHUNK

    def carried_group(gi, carry):
        hg_carried(gi)
        ret_carried(gi)
        return carry

    lax.fori_loop(0, n_tokens // state_tokens, carried_group, 0)

    def finish(ri, carry):
        hg_finish(ri)
        return carry

    lax.fori_loop(0, n_tokens // HG_NORM_ROWS, finish, 0)


def _mixers(proj, hf, lb_logits, hg_norm, cos2, sin2, ret_norm, batch, seq):
    ns = seq // SEQ_TILE
    tok = lambda b, h, s: b * ns + s
    hg_col = lambda base: pl.BlockSpec((SEQ_TILE, HG_DK), lambda b, h, s: (tok(b, h, s), base + h))
    return pl.pallas_call(
        _mixers_kernel,
        grid=(batch, HG_HEADS, ns),
        in_specs=[
            pl.BlockSpec((2, HG_DK), lambda b, h, s: (0, h)),
            hg_col(COL_HQ), hg_col(0), hg_col(COL_HI), hg_col(COL_HG),
            pl.BlockSpec((1, HG_DK), lambda b, h, s: (0, 0)),
            hg_col(COL_RQ), hg_col(COL_RK),
            pl.BlockSpec((SEQ_TILE, RET_DV), lambda b, h, s: (tok(b, h, s), COL_RV // 2 + h)),
            pl.BlockSpec((SEQ_TILE, RET_DV), lambda b, h, s: (tok(b, h, s), COL_RG // 2 + h)),
            pl.BlockSpec((SEQ_TILE, RET_DK), lambda b, h, s: (s, 0)),
            pl.BlockSpec((SEQ_TILE, RET_DK), lambda b, h, s: (s, 0)),
            pl.BlockSpec((1, RET_DV), lambda b, h, s: (0, 0)),
        ],
        out_specs=[
            pl.BlockSpec((SEQ_TILE, HG_DK), lambda b, h, s: (tok(b, h, s), h)),
            pl.BlockSpec((SEQ_TILE, RET_DV), lambda b, h, s: (tok(b, h, s), h)),
        ],
        out_shape=[
            jax.ShapeDtypeStruct((batch * seq, HG_WIDTH), BF16),
            jax.ShapeDtypeStruct((batch * seq, RET_HEADS * RET_DV), BF16),
        ],
        scratch_shapes=HG_SCRATCH + RET_SCRATCH,
        compiler_params=_cparams(("arbitrary", "arbitrary", "arbitrary")),
        name="mixers",
    )(lb_logits, proj, hf, proj, proj, hg_norm, proj, proj, proj, proj, cos2, sin2, ret_norm)


def _merge_kernel(x_ref, ya_ref, yb_ref, ga_ref, gb_ref, wa_ref, wb_ref, wo_ref, fg_ref,
                  wr_ref, br_ref, x1_ref, xs_ref, route_ref, cnt_ref, h2b_s, rows_s):
    tm = x_ref.shape[0]

    @pl.when(pl.program_id(0) == 0)
    def _():
        h2b_s[...] = jnp.zeros_like(h2b_s)
        rows_s[...] = jnp.full(rows_s.shape, -1.0, F32)

    def sort_previous(lo, hi):
        slab_row = lax.broadcasted_iota(I32, (SORT_SLAB, tm), 0).astype(F32)
        for r0 in range(lo, hi, SORT_SLAB):
            sel = ((slab_row == rows_s[0:1, :] - float(r0)) | (slab_row == rows_s[1:2, :] - float(r0)))
            xs_ref[r0:r0 + SORT_SLAB, :] = _dot(jnp.where(sel, 1.0, 0.0).astype(BF16),
                                                h2b_s[...]).astype(BF16)

    merged = (_sigmoid(ga_ref[...].astype(F32)) * _dot(ya_ref[...], wa_ref[...])
              + _sigmoid(gb_ref[...].astype(F32)) * _dot(yb_ref[...], wb_ref[...]))
    x1 = x_ref[...] + _dot(merged.astype(BF16), wo_ref[...])
    x1_ref[...] = x1
    h2 = _rms(x1, fg_ref[...])

    h2b = h2.astype(BF16)
    logits = _dot(h2b, wr_ref[...]) + br_ref[...]
    sort_previous(0, LOCAL_ROWS)

    lane = lax.broadcasted_iota(I32, (tm, LANES), 1)
    neg = jnp.float32(-jnp.inf)
    big = jnp.int32(1 << 30)
    is_g = lane < N_GROUPS
    gl = jnp.where(is_g, logits, neg)
    gmax = jnp.max(gl, axis=-1, keepdims=True)
    g_idx = jnp.min(jnp.where(gl == gmax, lane, big), axis=-1, keepdims=True)
    g_w = 1.0 / jnp.sum(jnp.where(is_g, jnp.exp(gl - gmax), 0.0), axis=-1, keepdims=True)

    ex = lane - N_GROUPS
    in_grp = (ex >= g_idx * EXPERTS_PER_GROUP) & (ex < (g_idx + 1) * EXPERTS_PER_GROUP)
    el = jnp.where(in_grp, logits, neg)
    m1 = jnp.max(el, axis=-1, keepdims=True)
    e1 = jnp.min(jnp.where(el == m1, ex, big), axis=-1, keepdims=True)
    el2 = jnp.where(ex == e1, neg, el)
    m2 = jnp.max(el2, axis=-1, keepdims=True)
    e2 = jnp.min(jnp.where(el2 == m2, ex, big), axis=-1, keepdims=True)
    p2 = jnp.exp(m2 - m1)
    w1 = g_w / (1.0 + p2)
    w2 = g_w * p2 / (1.0 + p2)

    oh1 = ex == e1
    oh2 = ex == e2
    oh = jnp.where(oh1 | oh2, 1.0, 0.0)
    ri = lax.broadcasted_iota(I32, (tm, tm), 0)
    ci = lax.broadcasted_iota(I32, (tm, tm), 1)
    strict = jnp.where(ri > ci, 1.0, 0.0).astype(BF16)
    local_rank = _dot(strict, oh.astype(BF16))
    cnt = jnp.sum(oh, axis=0, keepdims=True)
    run_chunks = jnp.floor((cnt + (ROW_CHUNK - 1.0)) * (1.0 / ROW_CHUNK))
    ui = lax.broadcasted_iota(I32, (LANES, LANES), 0)
    uj = lax.broadcasted_iota(I32, (LANES, LANES), 1)
    before = jnp.where(ui < uj, 1.0, 0.0).astype(BF16)
    run_start = _dot(jnp.broadcast_to(run_chunks, (8, LANES)).astype(BF16), before)[0:1] * ROW_CHUNK
    slot = run_start + local_rank
    pos1 = jnp.sum(jnp.where(oh1, slot, 0.0), axis=-1, keepdims=True)
    pos2 = jnp.sum(jnp.where(oh2, slot, 0.0), axis=-1, keepdims=True)
    cnt_ref[0] = jnp.broadcast_to(cnt, (8, LANES))

    route = jnp.where(lane == 0, e1.astype(F32), 0.0)
    route = jnp.where(lane == 1, e2.astype(F32), route)
    route = jnp.where(lane == 2, w1, route)
    route = jnp.where(lane == 3, w2, route)
    route = jnp.where(lane == 4, pos1, route)
    route = jnp.where(lane == 5, pos2, route)
    route_ref[...] = route

    hi1 = jnp.floor(pos1 * (1.0 / 256.0))
    hi2 = jnp.floor(pos2 * (1.0 / 256.0))
    digits = jnp.where(lane == 0, hi1, 0.0)
    digits = jnp.where(lane == 1, pos1 - 256.0 * hi1, digits)
    digits = jnp.where(lane == 2, hi2, digits)
    digits = jnp.where(lane == 3, pos2 - 256.0 * hi2, digits)
    pick = jnp.where(lax.broadcasted_iota(I32, (8, LANES), 0) == lax.broadcasted_iota(I32, (8, LANES), 1),
                     1.0, 0.0).astype(BF16)
    rows = _dot_nt(pick, digits.astype(BF16))

    h2b_s[...] = h2b
    rows_s[0:1, :] = rows[0:1] * 256.0 + rows[1:2]
    rows_s[1:2, :] = rows[2:3] * 256.0 + rows[3:4]


def _merge(x2d, ya, yb, proj, wa, wb, wo, ffn_g, w_router, b_router):
    t = x2d.shape[0]
    nb = t // TOK_TILE
    const = lambda *shape: pl.BlockSpec(shape, lambda i: (0,) * len(shape))
    cur = lambda i: jnp.minimum(i, nb - 1)
    prev = lambda i: jnp.maximum(i - 1, 0)
    return pl.pallas_call(
        _merge_kernel,
        grid=(nb + 1,),
        in_specs=[
            pl.BlockSpec((TOK_TILE, D_MODEL), lambda i: (cur(i), 0)),
            pl.BlockSpec((TOK_TILE, HG_WIDTH), lambda i: (cur(i), 0)),
            pl.BlockSpec((TOK_TILE, D_MODEL), lambda i: (cur(i), 0)),
            pl.BlockSpec((TOK_TILE, D_MODEL), lambda i: (cur(i), COL_GA)),
            pl.BlockSpec((TOK_TILE, D_MODEL), lambda i: (cur(i), COL_GB)),
            const(HG_WIDTH, D_MODEL),
            const(D_MODEL, D_MODEL),
            const(D_MODEL, D_MODEL),
            const(1, D_MODEL),
            const(D_MODEL, LANES),
            const(1, LANES),
        ],
        out_specs=[
            pl.BlockSpec((TOK_TILE, D_MODEL), lambda i: (cur(i), 0)),
            pl.BlockSpec((LOCAL_ROWS, D_MODEL), lambda i: (prev(i), 0)),
            pl.BlockSpec((TOK_TILE, LANES), lambda i: (cur(i), 0)),
            pl.BlockSpec((1, 8, LANES), lambda i: (cur(i), 0, 0)),
        ],
        out_shape=[
            jax.ShapeDtypeStruct((t, D_MODEL), F32),
            jax.ShapeDtypeStruct((t // TOK_TILE * LOCAL_ROWS, D_MODEL), BF16),
            jax.ShapeDtypeStruct((t, LANES), F32),
            jax.ShapeDtypeStruct((t // TOK_TILE, 8, LANES), F32),
        ],
        scratch_shapes=[pltpu.VMEM((TOK_TILE, D_MODEL), BF16), pltpu.VMEM((8, TOK_TILE), F32)],
        compiler_params=_cparams(("arbitrary",)),
        name="merge_route",
    )(x2d, ya, yb, proj, proj, wa, wb, wo, ffn_g, w_router, b_router)


def _expert_kernel(te_ref, nt_ref, src_ref, xs_ref, wg_ref, wu_ref, wd_ref, ys_ref, xbuf, sem):
    i = pl.program_id(0)
    nt = nt_ref[0]

    def gather(tile, slot):
        copies = []
        for c in range(CHUNKS_PER_TILE):
            row = pl.multiple_of(src_ref[tile * CHUNKS_PER_TILE + c] * ROW_CHUNK, ROW_CHUNK)
            copies.append(pltpu.make_async_copy(
                xs_ref.at[pl.ds(row, ROW_CHUNK)],
                xbuf.at[slot, pl.ds(c * ROW_CHUNK, ROW_CHUNK)], sem.at[slot]))
        return copies

    @pl.when(i == 0)
    def _():
        for k in range(EXP_AHEAD):
            for cp in gather(k, k):
                cp.start()

    @pl.when(i < nt + EXP_AHEAD)
    def _():
        for cp in gather(i, i % EXP_SLOTS):
            cp.wait()

    @pl.when(i < nt)
    def _():
        slot = i % EXP_SLOTS
        wg, wu, wd = wg_ref[0], wu_ref[0], wd_ref[0]
        subs = [pl.ds(r, EXP_SUB) for r in range(0, EXP_TILE, EXP_SUB)]
        xs = [xbuf[slot, s, :] for s in subs]
        gates = [(_dot(x, wg), _dot(x, wu)) for x in xs]
        for c, cp in enumerate(gather(i + EXP_AHEAD, (i + EXP_AHEAD) % EXP_SLOTS)):
            cp.start(priority=c % 2)
        hidden = [(_silu(a) * u).astype(BF16) for a, u in gates]
        for s, h in zip(subs, hidden):
            ys_ref[s, :] = _dot(h, wd).astype(ys_ref.dtype)

    @pl.when(i >= nt)
    def _():
        ys_ref[...] = jnp.zeros_like(ys_ref)


def _experts(tile_expert, n_tiles_used, src_chunk, xs, wg, wu, wd, n_tiles):
    grid_spec = pltpu.PrefetchScalarGridSpec(
        num_scalar_prefetch=3,
        grid=(n_tiles,),
        in_specs=[
            pl.BlockSpec(memory_space=pl.ANY),
            pl.BlockSpec((1, D_MODEL, D_EXPERT), lambda i, te, nt, src: (te[i], 0, 0)),
            pl.BlockSpec((1, D_MODEL, D_EXPERT), lambda i, te, nt, src: (te[i], 0, 0)),
            pl.BlockSpec((1, D_EXPERT, D_MODEL), lambda i, te, nt, src: (te[i], 0, 0)),
        ],
        out_specs=pl.BlockSpec((EXP_TILE, D_MODEL), lambda i, te, nt, src: (i, 0)),
        scratch_shapes=[
            pltpu.VMEM((EXP_SLOTS, EXP_TILE, D_MODEL), BF16),
            pltpu.SemaphoreType.DMA((EXP_SLOTS,)),
        ],
    )
    return pl.pallas_call(
        _expert_kernel,
        grid_spec=grid_spec,
        out_shape=jax.ShapeDtypeStruct((n_tiles * EXP_TILE, D_MODEL), BF16),
        compiler_params=_cparams(("arbitrary",)),
        name="experts",
    )(tile_expert, n_tiles_used, src_chunk, xs, wg, wu, wd)


def _final_kernel(src_ref, x1_ref, route_ref, p_ref, ys_ref, pg_ref, wpg_ref, wpp_ref, fg_ref,
                  o_ref, ybuf, sem):
    tm = x1_ref.shape[0]
    i = pl.program_id(0)
    nb = pl.num_programs(0)

    def gather(block, slot):
        copies = []
        for c in range(LOCAL_CHUNKS):
            row = pl.multiple_of(src_ref[block * LOCAL_CHUNKS + c] * ROW_CHUNK, ROW_CHUNK)
            copies.append(pltpu.make_async_copy(
                ys_ref.at[pl.ds(row, ROW_CHUNK)],
                ybuf.at[slot, pl.ds(c * ROW_CHUNK, ROW_CHUNK)], sem.at[slot]))
        return copies

    @pl.when(i == 0)
    def _():
        for k in range(FIN_AHEAD):
            for cp in gather(k % nb, k):
                cp.start()

    slot = i % FIN_SLOTS
    for cp in gather(i, slot):
        cp.wait()

    parts = [pl.ds(r, tm // FIN_PARTS) for r in range(0, tm, tm // FIN_PARTS)]
    route = route_ref[...]
    ple = [_dot(p_ref[rows, :].astype(BF16), wpp_ref[...]) for rows in parts]

    slab_col = lax.broadcasted_iota(I32, (tm // FIN_PARTS, SORT_SLAB), 1).astype(F32)
    moes = []
    for k in range(FIN_PARTS):
        rows = slice(k * (tm // FIN_PARTS), (k + 1) * (tm // FIN_PARTS))
        w1, w2 = route[rows, 2:3], route[rows, 3:4]
        pos1, pos2 = route[rows, 4:5], route[rows, 5:6]
        moe = jnp.zeros((tm // FIN_PARTS, D_MODEL), F32)
        for k0 in range(0, LOCAL_ROWS, SORT_SLAB):
            sel = jnp.where(slab_col == pos1 - float(k0), w1,
                            jnp.where(slab_col == pos2 - float(k0), w2, 0.0)).astype(BF16)
            moe = moe + _dot(sel, ybuf[slot, k0:k0 + SORT_SLAB, :])
        moes.append(moe)
    for c, cp in enumerate(gather((i + FIN_AHEAD) % nb, (i + FIN_AHEAD) % FIN_SLOTS)):
        cp.start(priority=c % 2)
    x2s = [x1_ref[rows, :] + moe for rows, moe in zip(parts, moes)]
    hps = [_rms(x2, pg_ref[...]).astype(BF16) for x2 in x2s]
    gates = [_sigmoid(_dot(hp, wpg_ref[...])) for hp in hps]
    for rows, x2, gate, pp in zip(parts, x2s, gates, ple):
        o_ref[rows, :] = _rms(x2 + gate * pp, fg_ref[...])

    @pl.when(i == nb - 1)
    def _():
        for k in range(1, FIN_AHEAD + 1):
            for cp in gather(0, (i + k) % FIN_SLOTS):
                cp.wait()


def _final(src_chunk, x1, route, p2d, ys, ple_g, wpg, wpp, final_g):
    t = x1.shape[0]
    const = lambda *shape: pl.BlockSpec(shape, lambda i, src: (0,) * len(shape))
    grid_spec = pltpu.PrefetchScalarGridSpec(
        num_scalar_prefetch=1,
        grid=(t // TOK_TILE,),
        in_specs=[
            pl.BlockSpec((TOK_TILE, D_MODEL), lambda i, src: (i, 0)),
            pl.BlockSpec((TOK_TILE, LANES), lambda i, src: (i, 0)),
            pl.BlockSpec((TOK_TILE, PLE_DIM), lambda i, src: (i, 0)),
            pl.BlockSpec(memory_space=pl.ANY),
            const(1, D_MODEL),
            const(D_MODEL, D_MODEL),
            const(PLE_DIM, D_MODEL),
            const(1, D_MODEL),
        ],
        out_specs=pl.BlockSpec((TOK_TILE, D_MODEL), lambda i, src: (i, 0)),
        scratch_shapes=[pltpu.VMEM((FIN_SLOTS, LOCAL_ROWS, D_MODEL), BF16),
                        pltpu.SemaphoreType.DMA((FIN_SLOTS,))],
    )
    return pl.pallas_call(
        _final_kernel,
        grid_spec=grid_spec,
        out_shape=jax.ShapeDtypeStruct((t, D_MODEL), F32),
        compiler_params=_cparams(("arbitrary",)),
        name="combine_ple_final",
    )(src_chunk, x1, route, p2d, ys, ple_g, wpg, wpp, final_g)


def _rotary_tables(seq):
    inv = ROPE_BASE ** (-jnp.arange(0, RET_DK, 2, dtype=F32) / RET_DK)
    inv = jnp.concatenate([inv, inv])
    sign = jnp.where(jnp.arange(RET_DK) < RET_DK // 2, -1.0, 1.0).astype(F32)
    hi = (jnp.arange(seq // ROPE_SPLIT, dtype=F32) * ROPE_SPLIT)[:, None] * inv[None, :]
    lo = jnp.arange(ROPE_SPLIT, dtype=F32)[:, None] * inv[None, :]
    ch, sh = jnp.cos(hi)[:, None, :], jnp.sin(hi)[:, None, :]
    cl, sl = jnp.cos(lo)[None], jnp.sin(lo)[None]
    cos = (ch * cl - sh * sl).reshape(seq, RET_DK)
    sin = ((sh * cl + ch * sl) * sign).reshape(seq, RET_DK)
    return cos, sin


def _layer(x2d, p2d, batch, seq, mix_norm, w_in, hg_lb_logits, hg_norm, ret_norm, w_branch_a,
           w_branch_b, w_out, ffn_norm, w_rg, b_rg, w_re, b_re, w_gate, w_up, w_down, ple_norm,
           w_ple_gate, w_ple_proj, out_gain):
    t = x2d.shape[0]
    row = lambda v: v.reshape(1, -1).astype(F32)

    proj, hf, wg16, wu16, wd16 = _inproj(x2d, row(mix_norm), w_in.astype(BF16), w_gate, w_up, w_down)
    cos2, sin2 = _rotary_tables(seq)
    ya, yb = _mixers(proj, hf, hg_lb_logits.astype(F32), row(hg_norm), cos2, sin2, row(ret_norm), batch, seq)

    n_r = N_GROUPS + N_EXPERTS
    w_router = jnp.pad(jnp.concatenate([w_rg, w_re], axis=1).astype(BF16), ((0, 0), (0, LANES - n_r)))
    b_router = jnp.zeros((1, LANES), F32).at[0, :n_r].set(jnp.concatenate([b_rg, b_re]))
    x1, xs, route, counts = _merge(
        x2d, ya, yb, proj, w_branch_a.astype(BF16), w_branch_b.astype(BF16), w_out.astype(BF16),
        row(ffn_norm), w_router, b_router)

    n_blocks = t // TOK_TILE
    cnt = counts[:, 0, N_GROUPS:N_GROUPS + N_EXPERTS].astype(I32)
    run_chunks = (cnt + ROW_CHUNK - 1) // ROW_CHUNK
    earlier_e = jnp.tril(jnp.ones((N_EXPERTS, N_EXPERTS), I32), -1)
    earlier_b = jnp.tril(jnp.ones((n_blocks, n_blocks), I32), -1)
    run_local = jnp.sum(run_chunks[:, None, :] * earlier_e[None], axis=2)
    seg_chunks = jnp.sum(run_chunks, axis=0)
    tiles_per = (seg_chunks + CHUNKS_PER_TILE - 1) // CHUNKS_PER_TILE
    seg_start = jnp.sum(tiles_per[None, :] * earlier_e, axis=1) * CHUNKS_PER_TILE
    tile_end = seg_start // CHUNKS_PER_TILE + tiles_per
    run_global = seg_start[None, :] + jnp.sum(run_chunks.T[:, None, :] * earlier_b[None], axis=2).T
    max_chunks = (2 * t) // ROW_CHUNK + n_blocks * N_EXPERTS + N_EXPERTS * (CHUNKS_PER_TILE - 1)
    n_tiles = -(-max_chunks // CHUNKS_PER_TILE) + EXP_AHEAD
    tile_ids = jnp.arange(n_tiles, dtype=I32)
    tile_expert = jnp.minimum(jnp.sum((tile_end[None, :] <= tile_ids[:, None]).astype(I32), axis=1),
                              N_EXPERTS - 1)
    n_used = tile_end[-1:].astype(I32)

    block_ids = jnp.arange(n_blocks, dtype=I32)
    zero_local = LOCAL_CHUNKS - 1
    zero_global = n_tiles * CHUNKS_PER_TILE - 1
    g = jnp.arange(n_tiles * CHUNKS_PER_TILE, dtype=I32)[:, None]
    e_g = jnp.repeat(tile_expert, CHUNKS_PER_TILE)
    pick_e = (e_g[:, None] == jnp.arange(N_EXPERTS, dtype=I32)[None, :]).astype(I32)
    rg, rc, rl = (jnp.sum(pick_e[:, :, None] * tab.T[None, :, :], axis=1)
                  for tab in (run_global, run_chunks, run_local))
    inside = (rg <= g) & (g < rg + rc)
    gather_src = jnp.sum(jnp.where(inside, block_ids[None, :] * LOCAL_CHUNKS + rl + (g - rg), 0), axis=1)
    gather_src = jnp.where(jnp.any(inside, axis=1), gather_src, zero_local).astype(I32)

    lc = jnp.arange(LOCAL_CHUNKS, dtype=I32)[None, :, None]
    inside = (run_local[:, None, :] <= lc) & (lc < (run_local + run_chunks)[:, None, :])
    back_src = jnp.sum(jnp.where(inside, run_global[:, None, :] + lc - run_local[:, None, :], 0), axis=2)
    back_src = jnp.where(jnp.any(inside, axis=2), back_src, zero_global).astype(I32).reshape(-1)

    ys = _experts(tile_expert, n_used, gather_src, xs, wg16, wu16, wd16, n_tiles)
    return _final(back_src, x1, route, p2d, ys, row(ple_norm), w_ple_gate.astype(BF16),
                  w_ple_proj.astype(BF16), out_gain)


def kernel(x, p, mix_norm, w_in, hg_lb_logits, hg_norm, ret_norm, w_branch_a, w_branch_b, w_out,
           ffn_norm, w_router_group, b_router_group, w_router_expert, b_router_expert,
           w_expert_gate, w_expert_up, w_expert_down, ple_norm, w_ple_gate, w_ple_proj, final_norm):
    batch, seq, d = x.shape
    depth = p.shape[0]
    assert depth == 1, "the final rmsnorm is fused into the single layer"
    x2d = x.reshape(batch * seq, d)
    out = _layer(x2d, p[0].reshape(batch * seq, -1), batch, seq, mix_norm[0], w_in[0], hg_lb_logits,
                 hg_norm[0], ret_norm[0], w_branch_a[0], w_branch_b[0], w_out[0], ffn_norm[0],
                 w_router_group[0], b_router_group[0], w_router_expert[0], b_router_expert[0],
                 w_expert_gate[0], w_expert_up[0], w_expert_down[0], ple_norm[0], w_ple_gate[0],
                 w_ple_proj[0], final_norm.reshape(1, -1).astype(F32))
    return out.reshape(batch, seq, d)
```

```python
import jax
import jax.numpy as jnp
from jax import lax
from jax.experimental import pallas as pl
from jax.experimental.pallas import tpu as pltpu

F32 = jnp.float32
BF16 = jnp.bfloat16
I32 = jnp.int32

EPS = 1e-6
D_MODEL = 1024
PLE_DIM = 256
HG_HEADS = 4
HG_DK = 128
HG_WIDTH = HG_HEADS * HG_DK
RET_HEADS = 4
RET_DK = 128
RET_DV = 256
ROPE_BASE = 10000.0
ROPE_SPLIT = 64
IN_TOTAL = 7168
N_GROUPS = 4
EXPERTS_PER_GROUP = 8
N_EXPERTS = 32
D_EXPERT = 256

COL_HQ, COL_HF, COL_HI, COL_HG = 0, 4, 8, 12
COL_RQ, COL_RK = 16, 20
COL_RV, COL_RG = 24, 32
COL_GA, COL_GB = 5, 6

LANES = 128
VMEM_LIMIT = 56 * 1024 * 1024

HG_CHUNK = 64
HG_SUB = 8
HG_UNROLL = 16
HG_STATE_UNROLL = 16
HG_NORM_ROWS = 1024
RET_CHUNK = 128
RET_STATE_UNROLL = 8
SEQ_TILE = 1024
TOK_TILE = 512
EXP_TILE = 512
EXP_SUB = 512
EXP_AHEAD = 3
EXP_SLOTS = EXP_AHEAD + 1
ROW_CHUNK = 16
CHUNKS_PER_TILE = EXP_TILE // ROW_CHUNK
LOCAL_ROWS = 2 * TOK_TILE + N_EXPERTS * ROW_CHUNK
LOCAL_CHUNKS = LOCAL_ROWS // ROW_CHUNK
SORT_SLAB = 256
MERGE_SLAB = LOCAL_ROWS
FIN_AHEAD = 2
FIN_SLOTS = FIN_AHEAD + 1
FIN_PARTS = 2


def _cparams(sem):
    return pltpu.CompilerParams(dimension_semantics=sem, vmem_limit_bytes=VMEM_LIMIT)


def _rms(x, g):
    return x * lax.rsqrt(jnp.mean(x * x, axis=-1, keepdims=True) + EPS) * g


def _sigmoid(x):
    return 1.0 / (1.0 + jnp.exp(-x))


def _silu(x):
    return x * _sigmoid(x)


def _split3(x):
    hi = x.astype(BF16)
    r1 = x - hi.astype(F32)
    mid = r1.astype(BF16)
    lo = (r1 - mid.astype(F32)).astype(BF16)
    return hi, mid, lo


def _dot(a, b):
    return jnp.dot(a, b, preferred_element_type=F32)


def _dot_nt(a, b):
    return lax.dot_general(a, b, (((1,), (1,)), ((), ())), preferred_element_type=F32)


def _dot_tn(a, b):
    return lax.dot_general(a, b, (((0,), (0,)), ((), ())), preferred_element_type=F32)


def _inproj_kernel(x_ref, g_ref, w_ref, wg_ref, wu_ref, wd_ref, proj_ref, hf_ref,
                   wg16_ref, wu16_ref, wd16_ref):
    wg16_ref[...] = wg_ref[...].astype(BF16)
    wu16_ref[...] = wu_ref[...].astype(BF16)
    wd16_ref[...] = wd_ref[...].astype(BF16)

    h = _rms(x_ref[...], g_ref[...]).astype(BF16)
    tn = HG_WIDTH
    for j in range(IN_TOTAL // tn):
        acc = _dot(h, w_ref[:, j * tn:(j + 1) * tn])
        proj_ref[:, j * tn:(j + 1) * tn] = acc.astype(BF16)
        if j * tn == COL_HF * LANES:
            hf_ref[...] = acc


def _inproj(x2d, gain, w_bf16, w_gate, w_up, w_down):
    t = x2d.shape[0]
    steps = t // TOK_TILE
    assert N_EXPERTS % steps == 0, "expert weights are converted in equal shares per grid step"
    share = N_EXPERTS // steps
    up_spec = pl.BlockSpec((share, D_MODEL, D_EXPERT), lambda i: (i, 0, 0))
    down_spec = pl.BlockSpec((share, D_EXPERT, D_MODEL), lambda i: (i, 0, 0))
    return pl.pallas_call(
        _inproj_kernel,
        grid=(steps,),
        in_specs=[
            pl.BlockSpec((TOK_TILE, D_MODEL), lambda i: (i, 0)),
            pl.BlockSpec((1, D_MODEL), lambda i: (0, 0)),
            pl.BlockSpec((D_MODEL, IN_TOTAL), lambda i: (0, 0), pipeline_mode=pl.Buffered(1)),
            up_spec, up_spec, down_spec,
        ],
        out_specs=[
            pl.BlockSpec((TOK_TILE, IN_TOTAL), lambda i: (i, 0)),
            pl.BlockSpec((TOK_TILE, HG_WIDTH), lambda i: (i, 0)),
            up_spec, up_spec, down_spec,
        ],
        out_shape=[
            jax.ShapeDtypeStruct((t, IN_TOTAL), BF16),
            jax.ShapeDtypeStruct((t, HG_WIDTH), F32),
            jax.ShapeDtypeStruct(w_gate.shape, BF16),
            jax.ShapeDtypeStruct(w_up.shape, BF16),
            jax.ShapeDtypeStruct(w_down.shape, BF16),
        ],
        compiler_params=_cparams(("arbitrary",)),
        name="inproj",
    )(x2d, gain, w_bf16, w_gate, w_up, w_down)


def _hgrn_stages(lbl_ref, q_ref, f_ref, i_ref, g_ref, ng_ref, o_ref,
                 st_ref, b_s, k_s, v_s, oi_s, qe_s, kv_s, dec_s):
    c = HG_CHUNK
    nsub = c // HG_SUB

    logits = lbl_ref[...]
    e = jnp.exp(logits - jnp.max(logits, axis=0, keepdims=True))
    lb = e[0:1] / jnp.sum(e, axis=0, keepdims=True)
    one_m_lb = jnp.sum(e[1:], axis=0, keepdims=True) / jnp.sum(e, axis=0, keepdims=True)
    ng = ng_ref[...]

    row = lax.broadcasted_iota(I32, (c, c), 0)
    col = lax.broadcasted_iota(I32, (c, c), 1)
    tri = jnp.where(row >= col, 1.0, 0.0).astype(BF16)
    sub_row = lax.broadcasted_iota(I32, (HG_SUB, HG_DK), 0)
    masked = jnp.float32(-1e30)

    def bcast_row(ref, r, rows):
        return jnp.broadcast_to(ref[pl.ds(r, 1), :], (rows, HG_DK))

    def prep(ci, slot):
        r0 = pl.multiple_of(ci * c, c)
        z = f_ref[pl.ds(r0, c), :]
        ez = jnp.exp(-jnp.abs(z))
        rz = 1.0 / (1.0 + ez)
        pos = z >= 0.0
        logf = jnp.log2(lb + one_m_lb * jnp.where(pos, rz, ez * rz))
        kk = one_m_lb * jnp.where(pos, ez * rz, rz)
        q = _silu(q_ref[pl.ds(r0, c), :].astype(F32))
        v = i_ref[pl.ds(r0, c), :].astype(F32)
        k_s[slot] = kk
        v_s[slot] = v
        return dict(ci=ci, r0=r0, slot=slot, kk=kk, q=q, v=v, v16=v.astype(BF16), parts=_split3(logf))

    def cumulate(s):
        hi, mid, lo = s.pop("parts")
        b = (_dot(tri, lo) + _dot(tri, mid)) + _dot(tri, hi)
        b_s[s["slot"]] = b
        qe_s[pl.ds(s["r0"], c), :] = (s["q"] * jnp.exp2(b)).astype(BF16)
        s["b"] = b
        return s

    def off_diagonal(s):
        b, q, kk = s["b"], s["q"], s["kk"]
        bs_ref = b_s.at[s["slot"]]
        squares = []
        size = c // 2
        while size >= HG_SUB:
            for r0 in range(size, c, 2 * size):
                edge = bcast_row(bs_ref, r0 - 1, size)
                qs = q[r0:r0 + size, :] * jnp.exp2(b[r0:r0 + size, :] - edge)
                ks = kk[r0 - size:r0, :] * jnp.exp2(edge - b[r0 - size:r0, :])
                squares.append((r0, size, _dot_nt(qs.astype(BF16), ks.astype(BF16))))
            size //= 2
        s["squares"] = squares
        return s

    def apply_values(s):
        b, kk, v = s["b"], s["kk"], s["v"]
        blast = b[c - 1:c, :]
        rows = [jnp.zeros((HG_SUB, HG_DK), F32) for _ in range(nsub)]
        for r0, size, a in s.pop("squares"):
            part = _dot(a.astype(BF16), v[r0 - size:r0, :].astype(BF16))
            for i in range(size // HG_SUB):
                rows[r0 // HG_SUB + i] = rows[r0 // HG_SUB + i] + part[i * HG_SUB:(i + 1) * HG_SUB, :]
        s["o"] = jnp.concatenate(rows, axis=0)
        kd = kk * jnp.exp2(blast - b)
        kv_s[s["ci"]] = _dot_tn(s["v16"], kd.astype(BF16))
        dec_s[s["ci"]] = jnp.broadcast_to(jnp.exp2(blast), (HG_SUB, HG_DK))
        return s

    def diagonal(s):
        b, q = s["b"], s["q"]
        bs_ref, ks_ref, vs_ref = b_s.at[s["slot"]], k_s.at[s["slot"]], v_s.at[s["slot"]]
        d_blocks = []
        for i in range(nsub):
            sl = slice(i * HG_SUB, (i + 1) * HG_SUB)
            bt, qt = b[sl, :], q[sl, :]
            acc = jnp.zeros((HG_SUB, HG_DK), F32)
            for j in range(HG_SUB):
                r = i * HG_SUB + j
                arg = bt - bcast_row(bs_ref, r, HG_SUB)
                if j > 0:
                    arg = jnp.where(sub_row >= j, arg, masked)
                g = jnp.exp2(arg) * (qt * bcast_row(ks_ref, r, HG_SUB))
                acc = acc + jnp.sum(g, axis=-1, keepdims=True) * bcast_row(vs_ref, r, HG_SUB)
            d_blocks.append(acc)
        oi_s[pl.ds(s["r0"], c), :] = s["o"] + jnp.concatenate(d_blocks, axis=0)

    def carried_group(gi):
        st = st_ref[...]
        outs = []
        for j in range(HG_STATE_UNROLL):
            ci = gi * HG_STATE_UNROLL + j
            r0 = pl.multiple_of(ci * c, c)
            outs.append((r0, _dot_nt(qe_s[pl.ds(r0, c), :], st.astype(BF16))))
            st = st * dec_s[ci][0:1, :] + kv_s[ci]
        st_ref[...] = st
        for r0, os in outs:
            oi_s[pl.ds(r0, c), :] += os

    def finish(ri):
        r0 = pl.multiple_of(ri * HG_NORM_ROWS, HG_NORM_ROWS)
        y = _rms(oi_s[pl.ds(r0, HG_NORM_ROWS), :], ng) * _silu(g_ref[pl.ds(r0, HG_NORM_ROWS), :].astype(F32))
        o_ref[pl.ds(r0, HG_NORM_ROWS), :] = y.astype(o_ref.dtype)

    return prep, (cumulate, off_diagonal, apply_values, diagonal), carried_group, finish


HG_SCRATCH = [
    pltpu.VMEM((HG_DK, HG_DK), F32),
    pltpu.VMEM((HG_UNROLL, HG_CHUNK, HG_DK), F32),
    pltpu.VMEM((HG_UNROLL, HG_CHUNK, HG_DK), F32),
    pltpu.VMEM((HG_UNROLL, HG_CHUNK, HG_DK), F32),
    pltpu.VMEM((SEQ_TILE, HG_DK), F32),
    pltpu.VMEM((SEQ_TILE, HG_DK), BF16),
    pltpu.VMEM((SEQ_TILE // HG_CHUNK, HG_DK, HG_DK), F32),
    pltpu.VMEM((SEQ_TILE // HG_CHUNK, HG_SUB, HG_DK), F32),
]


def _ret_stages(q_ref, k_ref, v_ref, g_ref, cos_ref, sin_ref, ng_ref, o_ref,
                r_ref, oi_s, q16_s, kv_s):
    c = RET_CHUNK

    hf = jnp.full((1, 1), pl.program_id(1), I32).astype(F32)
    lg = jnp.log1p(-jnp.exp2(-5.0 - hf))
    ti = lax.broadcasted_iota(I32, (c, c), 0)
    si = lax.broadcasted_iota(I32, (c, c), 1)
    rel = (ti - si).astype(F32)
    intra = jnp.where(ti >= si, jnp.exp(jnp.maximum(rel, 0.0) * lg), 0.0)
    idx = lax.broadcasted_iota(I32, (c, 1), 0).astype(F32)
    inter = jnp.exp((idx + 1.0) * lg)
    to_state = jnp.exp((c - 1.0 - idx) * lg)
    chunk_decay = jnp.exp(float(c) * lg)
    ng = ng_ref[...]
    half = RET_DK // 2

    def prep(ci):
        r0 = pl.multiple_of(ci * c, c)
        cos = cos_ref[pl.ds(r0, c), :]
        sin = sin_ref[pl.ds(r0, c), :]
        q = q_ref[pl.ds(r0, c), :].astype(F32)
        k = k_ref[pl.ds(r0, c), :].astype(F32)
        qr = (q * cos + pltpu.roll(q, half, 1) * sin) * (RET_DK ** -0.5)
        kr = k * cos + pltpu.roll(k, half, 1) * sin
        q16 = qr.astype(BF16)
        q16_s[pl.ds(r0, c), :] = q16
        return dict(ci=ci, r0=r0, q16=q16, k16=kr.astype(BF16), kts=(kr * to_state).astype(BF16),
                    v16=v_ref[pl.ds(r0, c), :])

    def scores(s):
        s["att"] = (_dot_nt(s.pop("q16"), s.pop("k16")) * intra).astype(BF16)
        return s

    def apply_values(s):
        oi_s[pl.ds(s["r0"], c), :] = _dot(s["att"], s["v16"])
        kv_s[s["ci"]] = _dot_tn(s["kts"], s["v16"])
        return s

    def carried_group(gi):
        r = r_ref[...]
        outs = []
        for j in range(RET_STATE_UNROLL):
            ci = gi * RET_STATE_UNROLL + j
            r0 = pl.multiple_of(ci * c, c)
            outs.append((r0, _dot(q16_s[pl.ds(r0, c), :], r.astype(BF16))))
            r = chunk_decay * r + kv_s[ci]
        r_ref[...] = r
        for r0, qr_state in outs:
            o = oi_s[pl.ds(r0, c), :] + qr_state * inter
            y = _rms(o, ng) * _silu(g_ref[pl.ds(r0, c), :].astype(F32))
            o_ref[pl.ds(r0, c), :] = y.astype(o_ref.dtype)

    return prep, (scores, apply_values), carried_group


RET_SCRATCH = [
    pltpu.VMEM((RET_DK, RET_DV), F32),
    pltpu.VMEM((SEQ_TILE, RET_DV), F32),
    pltpu.VMEM((SEQ_TILE, RET_DK), BF16),
    pltpu.VMEM((SEQ_TILE // RET_CHUNK, RET_DK, RET_DV), F32),
]


def _mixers_kernel(*refs):
    hg_refs = refs[:6] + refs[13:14] + refs[15:15 + len(HG_SCRATCH)]
    ret_refs = refs[6:13] + refs[14:15] + refs[15 + len(HG_SCRATCH):]
    st_ref, r_ref = hg_refs[7], ret_refs[8]
    n_tokens = refs[1].shape[0]

    @pl.when(pl.program_id(2) == 0)
    def _():
        st_ref[...] = jnp.zeros_like(st_ref)
        r_ref[...] = jnp.zeros_like(r_ref)

    hg_prep, hg_local, hg_carried, hg_finish = _hgrn_stages(*hg_refs)
    ret_prep, ret_local, ret_carried = _ret_stages(*ret_refs)
    group_tokens = HG_UNROLL * HG_CHUNK
    ret_unroll = group_tokens // RET_CHUNK

    def local_group(gi, carry):
        hs = [hg_prep(gi * HG_UNROLL + slot, slot) for slot in range(HG_UNROLL)]
        rs = [ret_prep(gi * ret_unroll + j) for j in range(ret_unroll)]
        cumulate, off_diagonal, apply_values, diagonal = hg_local
        scores, ret_apply = ret_local
        hs = [cumulate(s) for s in hs]
        rs = [scores(s) for s in rs]
        hs = [off_diagonal(s) for s in hs]
        rs = [ret_apply(s) for s in rs]
        hs = [apply_values(s) for s in hs]
        for s in hs:
            diagonal(s)
        return carry

    lax.fori_loop(0, n_tokens // group_tokens, local_group, 0)

    state_tokens = HG_STATE_UNROLL * HG_CHUNK
    assert state_tokens == RET_STATE_UNROLL * RET_CHUNK

    def carried_group(gi, carry):
        hg_carried(gi)
        ret_carried(gi)
        return carry

    lax.fori_loop(0, n_tokens // state_tokens, carried_group, 0)

    def finish(ri, carry):
        hg_finish(ri)
        return carry

    lax.fori_loop(0, n_tokens // HG_NORM_ROWS, finish, 0)


def _mixers(proj, hf, lb_logits, hg_norm, cos2, sin2, ret_norm, batch, seq):
    ns = seq // SEQ_TILE
    tok = lambda b, h, s: b * ns + s
    hg_col = lambda base: pl.BlockSpec((SEQ_TILE, HG_DK), lambda b, h, s: (tok(b, h, s), base + h))
    return pl.pallas_call(
        _mixers_kernel,
        grid=(batch, HG_HEADS, ns),
        in_specs=[
            pl.BlockSpec((2, HG_DK), lambda b, h, s: (0, h)),
            hg_col(COL_HQ), hg_col(0), hg_col(COL_HI), hg_col(COL_HG),
            pl.BlockSpec((1, HG_DK), lambda b, h, s: (0, 0)),
            hg_col(COL_RQ), hg_col(COL_RK),
            pl.BlockSpec((SEQ_TILE, RET_DV), lambda b, h, s: (tok(b, h, s), COL_RV // 2 + h)),
            pl.BlockSpec((SEQ_TILE, RET_DV), lambda b, h, s: (tok(b, h, s), COL_RG // 2 + h)),
            pl.BlockSpec((SEQ_TILE, RET_DK), lambda b, h, s: (s, 0)),
            pl.BlockSpec((SEQ_TILE, RET_DK), lambda b, h, s: (s, 0)),
            pl.BlockSpec((1, RET_DV), lambda b, h, s: (0, 0)),
        ],
        out_specs=[
            pl.BlockSpec((SEQ_TILE, HG_DK), lambda b, h, s: (tok(b, h, s), h)),
            pl.BlockSpec((SEQ_TILE, RET_DV), lambda b, h, s: (tok(b, h, s), h)),
        ],
        out_shape=[
            jax.ShapeDtypeStruct((batch * seq, HG_WIDTH), BF16),
            jax.ShapeDtypeStruct((batch * seq, RET_HEADS * RET_DV), BF16),
        ],
        scratch_shapes=HG_SCRATCH + RET_SCRATCH,
        compiler_params=_cparams(("arbitrary", "arbitrary", "arbitrary")),
        name="mixers",
    )(lb_logits, proj, hf, proj, proj, hg_norm, proj, proj, proj, proj, cos2, sin2, ret_norm)


def _merge_kernel(x_ref, ya_ref, yb_ref, ga_ref, gb_ref, wa_ref, wb_ref, wo_ref, fg_ref,
                  wr_ref, br_ref, x1_ref, xs_ref, route_ref, cnt_ref, h2b_s, rows_s):
    tm = x_ref.shape[0]

    @pl.when(pl.program_id(0) == 0)
    def _():
        h2b_s[...] = jnp.zeros_like(h2b_s)
        rows_s[...] = jnp.full(rows_s.shape, -1.0, F32)

    def sort_previous(lo, hi):
        slab_row = lax.broadcasted_iota(I32, (MERGE_SLAB, tm), 0).astype(F32)
        for r0 in range(lo, hi, MERGE_SLAB):
            sel = ((slab_row == rows_s[0:1, :] - float(r0)) | (slab_row == rows_s[1:2, :] - float(r0)))
            xs_ref[r0:r0 + MERGE_SLAB, :] = _dot(jnp.where(sel, 1.0, 0.0).astype(BF16),
                                                 h2b_s[...]).astype(BF16)

    merged = (_sigmoid(ga_ref[...].astype(F32)) * _dot(ya_ref[...], wa_ref[...])
              + _sigmoid(gb_ref[...].astype(F32)) * _dot(yb_ref[...], wb_ref[...]))
    x1 = x_ref[...] + _dot(merged.astype(BF16), wo_ref[...])
    x1_ref[...] = x1
    h2 = _rms(x1, fg_ref[...])

    h2b = h2.astype(BF16)
    logits = _dot(h2b, wr_ref[...]) + br_ref[...]
    sort_previous(0, LOCAL_ROWS)

    lane = lax.broadcasted_iota(I32, (tm, LANES), 1)
    neg = jnp.float32(-jnp.inf)
    big = jnp.int32(1 << 30)
    is_g = lane < N_GROUPS
    gl = jnp.where(is_g, logits, neg)
    gmax = jnp.max(gl, axis=-1, keepdims=True)
    g_idx = jnp.min(jnp.where(gl == gmax, lane, big), axis=-1, keepdims=True)
    g_w = 1.0 / jnp.sum(jnp.where(is_g, jnp.exp(gl - gmax), 0.0), axis=-1, keepdims=True)

    ex = lane - N_GROUPS
    in_grp = (ex >= g_idx * EXPERTS_PER_GROUP) & (ex < (g_idx + 1) * EXPERTS_PER_GROUP)
    el = jnp.where(in_grp, logits, neg)
    m1 = jnp.max(el, axis=-1, keepdims=True)
    e1 = jnp.min(jnp.where(el == m1, ex, big), axis=-1, keepdims=True)
    el2 = jnp.where(ex == e1, neg, el)
    m2 = jnp.max(el2, axis=-1, keepdims=True)
    e2 = jnp.min(jnp.where(el2 == m2, ex, big), axis=-1, keepdims=True)
    p2 = jnp.exp(m2 - m1)
    w1 = g_w / (1.0 + p2)
    w2 = g_w * p2 / (1.0 + p2)

    oh1 = ex == e1
    oh2 = ex == e2
    oh = jnp.where(oh1 | oh2, 1.0, 0.0)
    ri = lax.broadcasted_iota(I32, (tm, tm), 0)
    ci = lax.broadcasted_iota(I32, (tm, tm), 1)
    strict = jnp.where(ri > ci, 1.0, 0.0).astype(BF16)
    local_rank = _dot(strict, oh.astype(BF16))
    cnt = jnp.sum(oh, axis=0, keepdims=True)
    run_chunks = jnp.floor((cnt + (ROW_CHUNK - 1.0)) * (1.0 / ROW_CHUNK))
    ui = lax.broadcasted_iota(I32, (LANES, LANES), 0)
    uj = lax.broadcasted_iota(I32, (LANES, LANES), 1)
    before = jnp.where(ui < uj, 1.0, 0.0).astype(BF16)
    run_start = _dot(jnp.broadcast_to(run_chunks, (8, LANES)).astype(BF16), before)[0:1] * ROW_CHUNK
    slot = run_start + local_rank
    pos1 = jnp.sum(jnp.where(oh1, slot, 0.0), axis=-1, keepdims=True)
    pos2 = jnp.sum(jnp.where(oh2, slot, 0.0), axis=-1, keepdims=True)
    cnt_ref[0] = jnp.broadcast_to(cnt, (8, LANES))

    route = jnp.where(lane == 0, e1.astype(F32), 0.0)
    route = jnp.where(lane == 1, e2.astype(F32), route)
    route = jnp.where(lane == 2, w1, route)
    route = jnp.where(lane == 3, w2, route)
    route = jnp.where(lane == 4, pos1, route)
    route = jnp.where(lane == 5, pos2, route)
    route_ref[...] = route

    hi1 = jnp.floor(pos1 * (1.0 / 256.0))
    hi2 = jnp.floor(pos2 * (1.0 / 256.0))
    digits = jnp.where(lane == 0, hi1, 0.0)
    digits = jnp.where(lane == 1, pos1 - 256.0 * hi1, digits)
    digits = jnp.where(lane == 2, hi2, digits)
    digits = jnp.where(lane == 3, pos2 - 256.0 * hi2, digits)
    pick = jnp.where(lax.broadcasted_iota(I32, (8, LANES), 0) == lax.broadcasted_iota(I32, (8, LANES), 1),
                     1.0, 0.0).astype(BF16)
    rows = _dot_nt(pick, digits.astype(BF16))

    h2b_s[...] = h2b
    rows_s[0:1, :] = rows[0:1] * 256.0 + rows[1:2]
    rows_s[1:2, :] = rows[2:3] * 256.0 + rows[3:4]


def _merge(x2d, ya, yb, proj, wa, wb, wo, ffn_g, w_router, b_router):
    t = x2d.shape[0]
    nb = t // TOK_TILE
    const = lambda *shape: pl.BlockSpec(shape, lambda i: (0,) * len(shape))
    cur = lambda i: jnp.minimum(i, nb - 1)
    prev = lambda i: jnp.maximum(i - 1, 0)
    return pl.pallas_call(
        _merge_kernel,
        grid=(nb + 1,),
        in_specs=[
            pl.BlockSpec((TOK_TILE, D_MODEL), lambda i: (cur(i), 0)),
            pl.BlockSpec((TOK_TILE, HG_WIDTH), lambda i: (cur(i), 0)),
            pl.BlockSpec((TOK_TILE, D_MODEL), lambda i: (cur(i), 0)),
            pl.BlockSpec((TOK_TILE, D_MODEL), lambda i: (cur(i), COL_GA)),
            pl.BlockSpec((TOK_TILE, D_MODEL), lambda i: (cur(i), COL_GB)),
            const(HG_WIDTH, D_MODEL),
            const(D_MODEL, D_MODEL),
            const(D_MODEL, D_MODEL),
            const(1, D_MODEL),
            const(D_MODEL, LANES),
            const(1, LANES),
        ],
        out_specs=[
            pl.BlockSpec((TOK_TILE, D_MODEL), lambda i: (cur(i), 0)),
            pl.BlockSpec((LOCAL_ROWS, D_MODEL), lambda i: (prev(i), 0)),
            pl.BlockSpec((TOK_TILE, LANES), lambda i: (cur(i), 0)),
            pl.BlockSpec((1, 8, LANES), lambda i: (cur(i), 0, 0)),
        ],
        out_shape=[
            jax.ShapeDtypeStruct((t, D_MODEL), F32),
            jax.ShapeDtypeStruct((t // TOK_TILE * LOCAL_ROWS, D_MODEL), BF16),
            jax.ShapeDtypeStruct((t, LANES), F32),
            jax.ShapeDtypeStruct((t // TOK_TILE, 8, LANES), F32),
        ],
        scratch_shapes=[pltpu.VMEM((TOK_TILE, D_MODEL), BF16), pltpu.VMEM((8, TOK_TILE), F32)],
        compiler_params=_cparams(("arbitrary",)),
        name="merge_route",
    )(x2d, ya, yb, proj, proj, wa, wb, wo, ffn_g, w_router, b_router)


def _expert_kernel(te_ref, nt_ref, src_ref, xs_ref, wg_ref, wu_ref, wd_ref, ys_ref, xbuf, sem):
    i = pl.program_id(0)
    nt = nt_ref[0]

    def gather(tile, slot):
        copies = []
        for c in range(CHUNKS_PER_TILE):
            row = pl.multiple_of(src_ref[tile * CHUNKS_PER_TILE + c] * ROW_CHUNK, ROW_CHUNK)
            copies.append(pltpu.make_async_copy(
                xs_ref.at[pl.ds(row, ROW_CHUNK)],
                xbuf.at[slot, pl.ds(c * ROW_CHUNK, ROW_CHUNK)], sem.at[slot]))
        return copies

    @pl.when(i == 0)
    def _():
        for k in range(EXP_AHEAD):
            for cp in gather(k, k):
                cp.start()

    @pl.when(i < nt + EXP_AHEAD)
    def _():
        for cp in gather(i, i % EXP_SLOTS):
            cp.wait()

    @pl.when(i < nt)
    def _():
        slot = i % EXP_SLOTS
        wg, wu, wd = wg_ref[0], wu_ref[0], wd_ref[0]
        subs = [pl.ds(r, EXP_SUB) for r in range(0, EXP_TILE, EXP_SUB)]
        xs = [xbuf[slot, s, :] for s in subs]
        gates = [(_dot(x, wg), _dot(x, wu)) for x in xs]
        for c, cp in enumerate(gather(i + EXP_AHEAD, (i + EXP_AHEAD) % EXP_SLOTS)):
            cp.start(priority=c % 2)
        hidden = [(_silu(a) * u).astype(BF16) for a, u in gates]
        for s, h in zip(subs, hidden):
            ys_ref[s, :] = _dot(h, wd).astype(ys_ref.dtype)

    @pl.when(i >= nt)
    def _():
        ys_ref[...] = jnp.zeros_like(ys_ref)


def _experts(tile_expert, n_tiles_used, src_chunk, xs, wg, wu, wd, n_tiles):
    grid_spec = pltpu.PrefetchScalarGridSpec(
        num_scalar_prefetch=3,
        grid=(n_tiles,),
        in_specs=[
            pl.BlockSpec(memory_space=pl.ANY),
            pl.BlockSpec((1, D_MODEL, D_EXPERT), lambda i, te, nt, src: (te[i], 0, 0)),
            pl.BlockSpec((1, D_MODEL, D_EXPERT), lambda i, te, nt, src: (te[i], 0, 0)),
            pl.BlockSpec((1, D_EXPERT, D_MODEL), lambda i, te, nt, src: (te[i], 0, 0)),
        ],
        out_specs=pl.BlockSpec((EXP_TILE, D_MODEL), lambda i, te, nt, src: (i, 0)),
        scratch_shapes=[
            pltpu.VMEM((EXP_SLOTS, EXP_TILE, D_MODEL), BF16),
            pltpu.SemaphoreType.DMA((EXP_SLOTS,)),
        ],
    )
    return pl.pallas_call(
        _expert_kernel,
        grid_spec=grid_spec,
        out_shape=jax.ShapeDtypeStruct((n_tiles * EXP_TILE, D_MODEL), BF16),
        compiler_params=_cparams(("arbitrary",)),
        name="experts",
    )(tile_expert, n_tiles_used, src_chunk, xs, wg, wu, wd)


def _final_kernel(src_ref, x1_ref, route_ref, p_ref, ys_ref, pg_ref, wpg_ref, wpp_ref, fg_ref,
                  o_ref, ybuf, sem):
    tm = x1_ref.shape[0]
    i = pl.program_id(0)
    nb = pl.num_programs(0)

    def gather(block, slot):
        copies = []
        for c in range(LOCAL_CHUNKS):
            row = pl.multiple_of(src_ref[block * LOCAL_CHUNKS + c] * ROW_CHUNK, ROW_CHUNK)
            copies.append(pltpu.make_async_copy(
                ys_ref.at[pl.ds(row, ROW_CHUNK)],
                ybuf.at[slot, pl.ds(c * ROW_CHUNK, ROW_CHUNK)], sem.at[slot]))
        return copies

    @pl.when(i == 0)
    def _():
        for k in range(FIN_AHEAD):
            for cp in gather(k % nb, k):
                cp.start()

    slot = i % FIN_SLOTS
    for cp in gather(i, slot):
        cp.wait()

    parts = [pl.ds(r, tm // FIN_PARTS) for r in range(0, tm, tm // FIN_PARTS)]
    route = route_ref[...]
    ple = [_dot(p_ref[rows, :].astype(BF16), wpp_ref[...]) for rows in parts]

    slab_col = lax.broadcasted_iota(I32, (tm // FIN_PARTS, SORT_SLAB), 1).astype(F32)
    moes = []
    for k in range(FIN_PARTS):
        rows = slice(k * (tm // FIN_PARTS), (k + 1) * (tm // FIN_PARTS))
        w1, w2 = route[rows, 2:3], route[rows, 3:4]
        pos1, pos2 = route[rows, 4:5], route[rows, 5:6]
        moe = jnp.zeros((tm // FIN_PARTS, D_MODEL), F32)
        for k0 in range(0, LOCAL_ROWS, SORT_SLAB):
            sel = jnp.where(slab_col == pos1 - float(k0), w1,
                            jnp.where(slab_col == pos2 - float(k0), w2, 0.0)).astype(BF16)
            moe = moe + _dot(sel, ybuf[slot, k0:k0 + SORT_SLAB, :])
        moes.append(moe)
    for c, cp in enumerate(gather((i + FIN_AHEAD) % nb, (i + FIN_AHEAD) % FIN_SLOTS)):
        cp.start(priority=c % 2)
    x2s = [x1_ref[rows, :] + moe for rows, moe in zip(parts, moes)]
    hps = [_rms(x2, pg_ref[...]).astype(BF16) for x2 in x2s]
    gates = [_sigmoid(_dot(hp, wpg_ref[...])) for hp in hps]
    for rows, x2, gate, pp in zip(parts, x2s, gates, ple):
        o_ref[rows, :] = _rms(x2 + gate * pp, fg_ref[...])

    @pl.when(i == nb - 1)
    def _():
        for k in range(1, FIN_AHEAD + 1):
            for cp in gather(0, (i + k) % FIN_SLOTS):
                cp.wait()


def _final(src_chunk, x1, route, p2d, ys, ple_g, wpg, wpp, final_g):
    t = x1.shape[0]
    const = lambda *shape: pl.BlockSpec(shape, lambda i, src: (0,) * len(shape))
    grid_spec = pltpu.PrefetchScalarGridSpec(
        num_scalar_prefetch=1,
        grid=(t // TOK_TILE,),
        in_specs=[
            pl.BlockSpec((TOK_TILE, D_MODEL), lambda i, src: (i, 0)),
            pl.BlockSpec((TOK_TILE, LANES), lambda i, src: (i, 0)),
            pl.BlockSpec((TOK_TILE, PLE_DIM), lambda i, src: (i, 0)),
            pl.BlockSpec(memory_space=pl.ANY),
            const(1, D_MODEL),
            const(D_MODEL, D_MODEL),
            const(PLE_DIM, D_MODEL),
            const(1, D_MODEL),
        ],
        out_specs=pl.BlockSpec((TOK_TILE, D_MODEL), lambda i, src: (i, 0)),
        scratch_shapes=[pltpu.VMEM((FIN_SLOTS, LOCAL_ROWS, D_MODEL), BF16),
                        pltpu.SemaphoreType.DMA((FIN_SLOTS,))],
    )
    return pl.pallas_call(
        _final_kernel,
        grid_spec=grid_spec,
        out_shape=jax.ShapeDtypeStruct((t, D_MODEL), F32),
        compiler_params=_cparams(("arbitrary",)),
        name="combine_ple_final",
    )(src_chunk, x1, route, p2d, ys, ple_g, wpg, wpp, final_g)


def _rotary_tables(seq):
    inv = ROPE_BASE ** (-jnp.arange(0, RET_DK, 2, dtype=F32) / RET_DK)
    inv = jnp.concatenate([inv, inv])
    sign = jnp.where(jnp.arange(RET_DK) < RET_DK // 2, -1.0, 1.0).astype(F32)
    hi = (jnp.arange(seq // ROPE_SPLIT, dtype=F32) * ROPE_SPLIT)[:, None] * inv[None, :]
    lo = jnp.arange(ROPE_SPLIT, dtype=F32)[:, None] * inv[None, :]
    ch, sh = jnp.cos(hi)[:, None, :], jnp.sin(hi)[:, None, :]
    cl, sl = jnp.cos(lo)[None], jnp.sin(lo)[None]
    cos = (ch * cl - sh * sl).reshape(seq, RET_DK)
    sin = ((sh * cl + ch * sl) * sign).reshape(seq, RET_DK)
    return cos, sin


def _layer(x2d, p2d, batch, seq, mix_norm, w_in, hg_lb_logits, hg_norm, ret_norm, w_branch_a,
           w_branch_b, w_out, ffn_norm, w_rg, b_rg, w_re, b_re, w_gate, w_up, w_down, ple_norm,
           w_ple_gate, w_ple_proj, out_gain):
    t = x2d.shape[0]
    row = lambda v: v.reshape(1, -1).astype(F32)

    proj, hf, wg16, wu16, wd16 = _inproj(x2d, row(mix_norm), w_in.astype(BF16), w_gate, w_up, w_down)
    cos2, sin2 = _rotary_tables(seq)
    ya, yb = _mixers(proj, hf, hg_lb_logits.astype(F32), row(hg_norm), cos2, sin2, row(ret_norm), batch, seq)

    n_r = N_GROUPS + N_EXPERTS
    w_router = jnp.pad(jnp.concatenate([w_rg, w_re], axis=1).astype(BF16), ((0, 0), (0, LANES - n_r)))
    b_router = jnp.zeros((1, LANES), F32).at[0, :n_r].set(jnp.concatenate([b_rg, b_re]))
    x1, xs, route, counts = _merge(
        x2d, ya, yb, proj, w_branch_a.astype(BF16), w_branch_b.astype(BF16), w_out.astype(BF16),
        row(ffn_norm), w_router, b_router)

    n_blocks = t // TOK_TILE
    cnt = counts[:, 0, N_GROUPS:N_GROUPS + N_EXPERTS].astype(I32)
    run_chunks = (cnt + ROW_CHUNK - 1) // ROW_CHUNK
    earlier_e = jnp.tril(jnp.ones((N_EXPERTS, N_EXPERTS), I32), -1)
    earlier_b = jnp.tril(jnp.ones((n_blocks, n_blocks), I32), -1)
    run_local = jnp.sum(run_chunks[:, None, :] * earlier_e[None], axis=2)
    seg_chunks = jnp.sum(run_chunks, axis=0)
    tiles_per = (seg_chunks + CHUNKS_PER_TILE - 1) // CHUNKS_PER_TILE
    seg_start = jnp.sum(tiles_per[None, :] * earlier_e, axis=1) * CHUNKS_PER_TILE
    tile_end = seg_start // CHUNKS_PER_TILE + tiles_per
    run_global = seg_start[None, :] + jnp.sum(run_chunks.T[:, None, :] * earlier_b[None], axis=2).T
    max_chunks = (2 * t) // ROW_CHUNK + n_blocks * N_EXPERTS + N_EXPERTS * (CHUNKS_PER_TILE - 1)
    n_tiles = -(-max_chunks // CHUNKS_PER_TILE) + EXP_AHEAD
    tile_ids = jnp.arange(n_tiles, dtype=I32)
    tile_expert = jnp.minimum(jnp.sum((tile_end[None, :] <= tile_ids[:, None]).astype(I32), axis=1),
                              N_EXPERTS - 1)
    n_used = tile_end[-1:].astype(I32)

    block_ids = jnp.arange(n_blocks, dtype=I32)
    zero_local = LOCAL_CHUNKS - 1
    zero_global = n_tiles * CHUNKS_PER_TILE - 1
    g = jnp.arange(n_tiles * CHUNKS_PER_TILE, dtype=I32)[:, None]
    e_g = jnp.repeat(tile_expert, CHUNKS_PER_TILE)
    pick_e = (e_g[:, None] == jnp.arange(N_EXPERTS, dtype=I32)[None, :]).astype(I32)
    rg, rc, rl = (jnp.sum(pick_e[:, :, None] * tab.T[None, :, :], axis=1)
                  for tab in (run_global, run_chunks, run_local))
    inside = (rg <= g) & (g < rg + rc)
    gather_src = jnp.sum(jnp.where(inside, block_ids[None, :] * LOCAL_CHUNKS + rl + (g - rg), 0), axis=1)
    gather_src = jnp.where(jnp.any(inside, axis=1), gather_src, zero_local).astype(I32)

    lc = jnp.arange(LOCAL_CHUNKS, dtype=I32)[None, :, None]
    inside = (run_local[:, None, :] <= lc) & (lc < (run_local + run_chunks)[:, None, :])
    back_src = jnp.sum(jnp.where(inside, run_global[:, None, :] + lc - run_local[:, None, :], 0), axis=2)
    back_src = jnp.where(jnp.any(inside, axis=2), back_src, zero_global).astype(I32).reshape(-1)

    ys = _experts(tile_expert, n_used, gather_src, xs, wg16, wu16, wd16, n_tiles)
    return _final(back_src, x1, route, p2d, ys, row(ple_norm), w_ple_gate.astype(BF16),
                  w_ple_proj.astype(BF16), out_gain)


def kernel(x, p, mix_norm, w_in, hg_lb_logits, hg_norm, ret_norm, w_branch_a, w_branch_b, w_out,
           ffn_norm, w_router_group, b_router_group, w_router_expert, b_router_expert,
           w_expert_gate, w_expert_up, w_expert_down, ple_norm, w_ple_gate, w_ple_proj, final_norm):
    batch, seq, d = x.shape
    depth = p.shape[0]
    assert depth == 1, "the final rmsnorm is fused into the single layer"
    x2d = x.reshape(batch * seq, d)
    out = _layer(x2d, p[0].reshape(batch * seq, -1), batch, seq, mix_norm[0], w_in[0], hg_lb_logits,
                 hg_norm[0], ret_norm[0], w_branch_a[0], w_branch_b[0], w_out[0], ffn_norm[0],
                 w_router_group[0], b_router_group[0], w_router_expert[0], b_router_expert[0],
                 w_expert_gate[0], w_expert_up[0], w_expert_down[0], ple_norm[0], w_ple_gate[0],
                 w_ple_proj[0], final_norm.reshape(1, -1).astype(F32))
    return out.reshape(batch, seq, d)
```

```python
import jax
import jax.numpy as jnp
from jax import lax
from jax.experimental import pallas as pl
from jax.experimental.pallas import tpu as pltpu

F32 = jnp.float32
BF16 = jnp.bfloat16
I32 = jnp.int32

EPS = 1e-6
D_MODEL = 1024
PLE_DIM = 256
HG_HEADS = 4
HG_DK = 128
HG_WIDTH = HG_HEADS * HG_DK
RET_HEADS = 4
RET_DK = 128
RET_DV = 256
ROPE_BASE = 10000.0
ROPE_SPLIT = 64
IN_TOTAL = 7168
N_GROUPS = 4
EXPERTS_PER_GROUP = 8
N_EXPERTS = 32
D_EXPERT = 256

COL_HQ, COL_HF, COL_HI, COL_HG = 0, 4, 8, 12
COL_RQ, COL_RK = 16, 20
COL_RV, COL_RG = 24, 32
COL_GA, COL_GB = 5, 6

LANES = 128
VMEM_LIMIT = 56 * 1024 * 1024

HG_CHUNK = 64
HG_SUB = 8
HG_UNROLL = 16
HG_STATE_UNROLL = 16
HG_NORM_ROWS = 1024
RET_CHUNK = 128
RET_STATE_UNROLL = 8
SEQ_TILE = 1024
TOK_TILE = 512
EXP_TILE = 256
EXP_SUB = 256
EXP_AHEAD = 6
EXP_SLOTS = EXP_AHEAD + 1
ROW_CHUNK = 16
CHUNKS_PER_TILE = EXP_TILE // ROW_CHUNK
LOCAL_ROWS = 2 * TOK_TILE + N_EXPERTS * ROW_CHUNK
LOCAL_CHUNKS = LOCAL_ROWS // ROW_CHUNK
SORT_SLAB = 256
MERGE_SLAB = LOCAL_ROWS
FIN_AHEAD = 2
FIN_SLOTS = FIN_AHEAD + 1
FIN_PARTS = 2


def _cparams(sem):
    return pltpu.CompilerParams(dimension_semantics=sem, vmem_limit_bytes=VMEM_LIMIT)


def _rms(x, g):
    return x * lax.rsqrt(jnp.mean(x * x, axis=-1, keepdims=True) + EPS) * g


def _sigmoid(x):
    return 1.0 / (1.0 + jnp.exp(-x))


def _silu(x):
    return x * _sigmoid(x)


def _split3(x):
    hi = x.astype(BF16)
    r1 = x - hi.astype(F32)
    mid = r1.astype(BF16)
    lo = (r1 - mid.astype(F32)).astype(BF16)
    return hi, mid, lo


def _dot(a, b):
    return jnp.dot(a, b, preferred_element_type=F32)


def _dot_nt(a, b):
    return lax.dot_general(a, b, (((1,), (1,)), ((), ())), preferred_element_type=F32)


def _dot_tn(a, b):
    return lax.dot_general(a, b, (((0,), (0,)), ((), ())), preferred_element_type=F32)


def _inproj_kernel(x_ref, g_ref, w_ref, wg_ref, wu_ref, wd_ref, proj_ref, hf_ref,
                   wg16_ref, wu16_ref, wd16_ref):
    wg16_ref[...] = wg_ref[...].astype(BF16)
    wu16_ref[...] = wu_ref[...].astype(BF16)
    wd16_ref[...] = wd_ref[...].astype(BF16)

    h = _rms(x_ref[...], g_ref[...]).astype(BF16)
    tn = HG_WIDTH
    for j in range(IN_TOTAL // tn):
        acc = _dot(h, w_ref[:, j * tn:(j + 1) * tn])
        proj_ref[:, j * tn:(j + 1) * tn] = acc.astype(BF16)
        if j * tn == COL_HF * LANES:
            hf_ref[...] = acc


def _inproj(x2d, gain, w_bf16, w_gate, w_up, w_down):
    t = x2d.shape[0]
    steps = t // TOK_TILE
    assert N_EXPERTS % steps == 0, "expert weights are converted in equal shares per grid step"
    share = N_EXPERTS // steps
    up_spec = pl.BlockSpec((share, D_MODEL, D_EXPERT), lambda i: (i, 0, 0))
    down_spec = pl.BlockSpec((share, D_EXPERT, D_MODEL), lambda i: (i, 0, 0))
    return pl.pallas_call(
        _inproj_kernel,
        grid=(steps,),
        in_specs=[
            pl.BlockSpec((TOK_TILE, D_MODEL), lambda i: (i, 0)),
            pl.BlockSpec((1, D_MODEL), lambda i: (0, 0)),
            pl.BlockSpec((D_MODEL, IN_TOTAL), lambda i: (0, 0), pipeline_mode=pl.Buffered(1)),
            up_spec, up_spec, down_spec,
        ],
        out_specs=[
            pl.BlockSpec((TOK_TILE, IN_TOTAL), lambda i: (i, 0)),
            pl.BlockSpec((TOK_TILE, HG_WIDTH), lambda i: (i, 0)),
            up_spec, up_spec, down_spec,
        ],
        out_shape=[
            jax.ShapeDtypeStruct((t, IN_TOTAL), BF16),
            jax.ShapeDtypeStruct((t, HG_WIDTH), F32),
            jax.ShapeDtypeStruct(w_gate.shape, BF16),
            jax.ShapeDtypeStruct(w_up.shape, BF16),
            jax.ShapeDtypeStruct(w_down.shape, BF16),
        ],
        compiler_params=_cparams(("arbitrary",)),
        name="inproj",
    )(x2d, gain, w_bf16, w_gate, w_up, w_down)


def _hgrn_stages(lbl_ref, q_ref, f_ref, i_ref, g_ref, ng_ref, o_ref,
                 st_ref, b_s, k_s, v_s, oi_s, qe_s, kv_s, dec_s):
    c = HG_CHUNK
    nsub = c // HG_SUB

    logits = lbl_ref[...]
    e = jnp.exp(logits - jnp.max(logits, axis=0, keepdims=True))
    lb = e[0:1] / jnp.sum(e, axis=0, keepdims=True)
    one_m_lb = jnp.sum(e[1:], axis=0, keepdims=True) / jnp.sum(e, axis=0, keepdims=True)
    ng = ng_ref[...]

    row = lax.broadcasted_iota(I32, (c, c), 0)
    col = lax.broadcasted_iota(I32, (c, c), 1)
    tri = jnp.where(row >= col, 1.0, 0.0).astype(BF16)
    sub_row = lax.broadcasted_iota(I32, (HG_SUB, HG_DK), 0)
    masked = jnp.float32(-1e30)

    def bcast_row(ref, r, rows):
        return jnp.broadcast_to(ref[pl.ds(r, 1), :], (rows, HG_DK))

    def prep(ci, slot):
        r0 = pl.multiple_of(ci * c, c)
        z = f_ref[pl.ds(r0, c), :]
        ez = jnp.exp(-jnp.abs(z))
        rz = 1.0 / (1.0 + ez)
        pos = z >= 0.0
        logf = jnp.log2(lb + one_m_lb * jnp.where(pos, rz, ez * rz))
        kk = one_m_lb * jnp.where(pos, ez * rz, rz)
        q = _silu(q_ref[pl.ds(r0, c), :].astype(F32))
        v = i_ref[pl.ds(r0, c), :].astype(F32)
        k_s[slot] = kk
        v_s[slot] = v
        return dict(ci=ci, r0=r0, slot=slot, kk=kk, q=q, v=v, v16=v.astype(BF16), parts=_split3(logf))

    def cumulate(s):
        hi, mid, lo = s.pop("parts")
        b = (_dot(tri, lo) + _dot(tri, mid)) + _dot(tri, hi)
        b_s[s["slot"]] = b
        qe_s[pl.ds(s["r0"], c), :] = (s["q"] * jnp.exp2(b)).astype(BF16)
        s["b"] = b
        return s

    def off_diagonal(s):
        b, q, kk = s["b"], s["q"], s["kk"]
        bs_ref = b_s.at[s["slot"]]
        squares = []
        size = c // 2
        while size >= HG_SUB:
            for r0 in range(size, c, 2 * size):
                edge = bcast_row(bs_ref, r0 - 1, size)
                qs = q[r0:r0 + size, :] * jnp.exp2(b[r0:r0 + size, :] - edge)
                ks = kk[r0 - size:r0, :] * jnp.exp2(edge - b[r0 - size:r0, :])
                squares.append((r0, size, _dot_nt(qs.astype(BF16), ks.astype(BF16))))
            size //= 2
        s["squares"] = squares
        return s

    def apply_values(s):
        b, kk, v = s["b"], s["kk"], s["v"]
        blast = b[c - 1:c, :]
        rows = [jnp.zeros((HG_SUB, HG_DK), F32) for _ in range(nsub)]
        for r0, size, a in s.pop("squares"):
            part = _dot(a.astype(BF16), v[r0 - size:r0, :].astype(BF16))
            for i in range(size // HG_SUB):
                rows[r0 // HG_SUB + i] = rows[r0 // HG_SUB + i] + part[i * HG_SUB:(i + 1) * HG_SUB, :]
        s["o"] = jnp.concatenate(rows, axis=0)
        kd = kk * jnp.exp2(blast - b)
        kv_s[s["ci"]] = _dot_tn(s["v16"], kd.astype(BF16))
        dec_s[s["ci"]] = jnp.broadcast_to(jnp.exp2(blast), (HG_SUB, HG_DK))
        return s

    def diagonal(s):
        b, q = s["b"], s["q"]
        bs_ref, ks_ref, vs_ref = b_s.at[s["slot"]], k_s.at[s["slot"]], v_s.at[s["slot"]]
        d_blocks = []
        for i in range(nsub):
            sl = slice(i * HG_SUB, (i + 1) * HG_SUB)
            bt, qt = b[sl, :], q[sl, :]
            acc = jnp.zeros((HG_SUB, HG_DK), F32)
            for j in range(HG_SUB):
                r = i * HG_SUB + j
                arg = bt - bcast_row(bs_ref, r, HG_SUB)
                if j > 0:
                    arg = jnp.where(sub_row >= j, arg, masked)
                g = jnp.exp2(arg) * (qt * bcast_row(ks_ref, r, HG_SUB))
                acc = acc + jnp.sum(g, axis=-1, keepdims=True) * bcast_row(vs_ref, r, HG_SUB)
            d_blocks.append(acc)
        oi_s[pl.ds(s["r0"], c), :] = s["o"] + jnp.concatenate(d_blocks, axis=0)

    def carried_group(gi):
        st = st_ref[...]
        outs = []
        for j in range(HG_STATE_UNROLL):
            ci = gi * HG_STATE_UNROLL + j
            r0 = pl.multiple_of(ci * c, c)
            outs.append((r0, _dot_nt(qe_s[pl.ds(r0, c), :], st.astype(BF16))))
            st = st * dec_s[ci][0:1, :] + kv_s[ci]
        st_ref[...] = st
        for r0, os in outs:
            oi_s[pl.ds(r0, c), :] += os

    def finish(ri):
        r0 = pl.multiple_of(ri * HG_NORM_ROWS, HG_NORM_ROWS)
        y = _rms(oi_s[pl.ds(r0, HG_NORM_ROWS), :], ng) * _silu(g_ref[pl.ds(r0, HG_NORM_ROWS), :].astype(F32))
        o_ref[pl.ds(r0, HG_NORM_ROWS), :] = y.astype(o_ref.dtype)

    return prep, (cumulate, off_diagonal, apply_values, diagonal), carried_group, finish


HG_SCRATCH = [
    pltpu.VMEM((HG_DK, HG_DK), F32),
    pltpu.VMEM((HG_UNROLL, HG_CHUNK, HG_DK), F32),
    pltpu.VMEM((HG_UNROLL, HG_CHUNK, HG_DK), F32),
    pltpu.VMEM((HG_UNROLL, HG_CHUNK, HG_DK), F32),
    pltpu.VMEM((SEQ_TILE, HG_DK), F32),
    pltpu.VMEM((SEQ_TILE, HG_DK), BF16),
    pltpu.VMEM((SEQ_TILE // HG_CHUNK, HG_DK, HG_DK), F32),
    pltpu.VMEM((SEQ_TILE // HG_CHUNK, HG_SUB, HG_DK), F32),
]


def _ret_stages(q_ref, k_ref, v_ref, g_ref, cos_ref, sin_ref, ng_ref, o_ref,
                r_ref, oi_s, q16_s, kv_s):
    c = RET_CHUNK

    hf = jnp.full((1, 1), pl.program_id(1), I32).astype(F32)
    lg = jnp.log1p(-jnp.exp2(-5.0 - hf))
    ti = lax.broadcasted_iota(I32, (c, c), 0)
    si = lax.broadcasted_iota(I32, (c, c), 1)
    rel = (ti - si).astype(F32)
    intra = jnp.where(ti >= si, jnp.exp(jnp.maximum(rel, 0.0) * lg), 0.0)
    idx = lax.broadcasted_iota(I32, (c, 1), 0).astype(F32)
    inter = jnp.exp((idx + 1.0) * lg)
    to_state = jnp.exp((c - 1.0 - idx) * lg)
    chunk_decay = jnp.exp(float(c) * lg)
    ng = ng_ref[...]
    half = RET_DK // 2

    def prep(ci):
        r0 = pl.multiple_of(ci * c, c)
        cos = cos_ref[pl.ds(r0, c), :]
        sin = sin_ref[pl.ds(r0, c), :]
        q = q_ref[pl.ds(r0, c), :].astype(F32)
        k = k_ref[pl.ds(r0, c), :].astype(F32)
        qr = (q * cos + pltpu.roll(q, half, 1) * sin) * (RET_DK ** -0.5)
        kr = k * cos + pltpu.roll(k, half, 1) * sin
        q16 = qr.astype(BF16)
        q16_s[pl.ds(r0, c), :] = q16
        return dict(ci=ci, r0=r0, q16=q16, k16=kr.astype(BF16), kts=(kr * to_state).astype(BF16),
                    v16=v_ref[pl.ds(r0, c), :])

    def scores(s):
        s["att"] = (_dot_nt(s.pop("q16"), s.pop("k16")) * intra).astype(BF16)
        return s

    def apply_values(s):
        oi_s[pl.ds(s["r0"], c), :] = _dot(s["att"], s["v16"])
        kv_s[s["ci"]] = _dot_tn(s["kts"], s["v16"])
        return s

    def carried_group(gi):
        r = r_ref[...]
        outs = []
        for j in range(RET_STATE_UNROLL):
            ci = gi * RET_STATE_UNROLL + j
            r0 = pl.multiple_of(ci * c, c)
            outs.append((r0, _dot(q16_s[pl.ds(r0, c), :], r.astype(BF16))))
            r = chunk_decay * r + kv_s[ci]
        r_ref[...] = r
        for r0, qr_state in outs:
            o = oi_s[pl.ds(r0, c), :] + qr_state * inter
            y = _rms(o, ng) * _silu(g_ref[pl.ds(r0, c), :].astype(F32))
            o_ref[pl.ds(r0, c), :] = y.astype(o_ref.dtype)

    return prep, (scores, apply_values), carried_group


RET_SCRATCH = [
    pltpu.VMEM((RET_DK, RET_DV), F32),
    pltpu.VMEM((SEQ_TILE, RET_DV), F32),
    pltpu.VMEM((SEQ_TILE, RET_DK), BF16),
    pltpu.VMEM((SEQ_TILE // RET_CHUNK, RET_DK, RET_DV), F32),
]


def _mixers_kernel(*refs):
    hg_refs = refs[:6] + refs[13:14] + refs[15:15 + len(HG_SCRATCH)]
    ret_refs = refs[6:13] + refs[14:15] + refs[15 + len(HG_SCRATCH):]
    st_ref, r_ref = hg_refs[7], ret_refs[8]
    n_tokens = refs[1].shape[0]

    @pl.when(pl.program_id(2) == 0)
    def _():
        st_ref[...] = jnp.zeros_like(st_ref)
        r_ref[...] = jnp.zeros_like(r_ref)

    hg_prep, hg_local, hg_carried, hg_finish = _hgrn_stages(*hg_refs)
    ret_prep, ret_local, ret_carried = _ret_stages(*ret_refs)
    group_tokens = HG_UNROLL * HG_CHUNK
    ret_unroll = group_tokens // RET_CHUNK

    def local_group(gi, carry):
        hs = [hg_prep(gi * HG_UNROLL + slot, slot) for slot in range(HG_UNROLL)]
        rs = [ret_prep(gi * ret_unroll + j) for j in range(ret_unroll)]
        cumulate, off_diagonal, apply_values, diagonal = hg_local
        scores, ret_apply = ret_local
        hs = [cumulate(s) for s in hs]
        rs = [scores(s) for s in rs]
        hs = [off_diagonal(s) for s in hs]
        rs = [ret_apply(s) for s in rs]
        hs = [apply_values(s) for s in hs]
        for s in hs:
            diagonal(s)
        return carry

    lax.fori_loop(0, n_tokens // group_tokens, local_group, 0)

    state_tokens = HG_STATE_UNROLL * HG_CHUNK
    assert state_tokens == RET_STATE_UNROLL * RET_CHUNK

    def carried_group(gi, carry):
        hg_carried(gi)
        ret_carried(gi)
        return carry

    lax.fori_loop(0, n_tokens // state_tokens, carried_group, 0)

    def finish(ri, carry):
        hg_finish(ri)
        return carry

    lax.fori_loop(0, n_tokens // HG_NORM_ROWS, finish, 0)


def _mixers(proj, hf, lb_logits, hg_norm, cos2, sin2, ret_norm, batch, seq):
    ns = seq // SEQ_TILE
    tok = lambda b, h, s: b * ns + s
    hg_col = lambda base: pl.BlockSpec((SEQ_TILE, HG_DK), lambda b, h, s: (tok(b, h, s), base + h))
    return pl.pallas_call(
        _mixers_kernel,
        grid=(batch, HG_HEADS, ns),
        in_specs=[
            pl.BlockSpec((2, HG_DK), lambda b, h, s: (0, h)),
            hg_col(COL_HQ), hg_col(0), hg_col(COL_HI), hg_col(COL_HG),
            pl.BlockSpec((1, HG_DK), lambda b, h, s: (0, 0)),
            hg_col(COL_RQ), hg_col(COL_RK),
            pl.BlockSpec((SEQ_TILE, RET_DV), lambda b, h, s: (tok(b, h, s), COL_RV // 2 + h)),
            pl.BlockSpec((SEQ_TILE, RET_DV), lambda b, h, s: (tok(b, h, s), COL_RG // 2 + h)),
            pl.BlockSpec((SEQ_TILE, RET_DK), lambda b, h, s: (s, 0)),
            pl.BlockSpec((SEQ_TILE, RET_DK), lambda b, h, s: (s, 0)),
            pl.BlockSpec((1, RET_DV), lambda b, h, s: (0, 0)),
        ],
        out_specs=[
            pl.BlockSpec((SEQ_TILE, HG_DK), lambda b, h, s: (tok(b, h, s), h)),
            pl.BlockSpec((SEQ_TILE, RET_DV), lambda b, h, s: (tok(b, h, s), h)),
        ],
        out_shape=[
            jax.ShapeDtypeStruct((batch * seq, HG_WIDTH), BF16),
            jax.ShapeDtypeStruct((batch * seq, RET_HEADS * RET_DV), BF16),
        ],
        scratch_shapes=HG_SCRATCH + RET_SCRATCH,
        compiler_params=_cparams(("arbitrary", "arbitrary", "arbitrary")),
        name="mixers",
    )(lb_logits, proj, hf, proj, proj, hg_norm, proj, proj, proj, proj, cos2, sin2, ret_norm)


def _merge_kernel(x_ref, ya_ref, yb_ref, ga_ref, gb_ref, wa_ref, wb_ref, wo_ref, fg_ref,
                  wr_ref, br_ref, x1_ref, xs_ref, route_ref, cnt_ref, h2b_s, rows_s):
    tm = x_ref.shape[0]

    @pl.when(pl.program_id(0) == 0)
    def _():
        h2b_s[...] = jnp.zeros_like(h2b_s)
        rows_s[...] = jnp.full(rows_s.shape, -1.0, F32)

    def sort_previous(lo, hi):
        slab_row = lax.broadcasted_iota(I32, (MERGE_SLAB, tm), 0).astype(F32)
        for r0 in range(lo, hi, MERGE_SLAB):
            sel = ((slab_row == rows_s[0:1, :] - float(r0)) | (slab_row == rows_s[1:2, :] - float(r0)))
            xs_ref[r0:r0 + MERGE_SLAB, :] = _dot(jnp.where(sel, 1.0, 0.0).astype(BF16),
                                                 h2b_s[...]).astype(BF16)

    merged = (_sigmoid(ga_ref[...].astype(F32)) * _dot(ya_ref[...], wa_ref[...])
              + _sigmoid(gb_ref[...].astype(F32)) * _dot(yb_ref[...], wb_ref[...]))
    x1 = x_ref[...] + _dot(merged.astype(BF16), wo_ref[...])
    x1_ref[...] = x1
    h2 = _rms(x1, fg_ref[...])

    h2b = h2.astype(BF16)
    logits = _dot(h2b, wr_ref[...]) + br_ref[...]
    sort_previous(0, LOCAL_ROWS)

    lane = lax.broadcasted_iota(I32, (tm, LANES), 1)
    neg = jnp.float32(-jnp.inf)
    big = jnp.int32(1 << 30)
    is_g = lane < N_GROUPS
    gl = jnp.where(is_g, logits, neg)
    gmax = jnp.max(gl, axis=-1, keepdims=True)
    g_idx = jnp.min(jnp.where(gl == gmax, lane, big), axis=-1, keepdims=True)
    g_w = 1.0 / jnp.sum(jnp.where(is_g, jnp.exp(gl - gmax), 0.0), axis=-1, keepdims=True)

    ex = lane - N_GROUPS
    in_grp = (ex >= g_idx * EXPERTS_PER_GROUP) & (ex < (g_idx + 1) * EXPERTS_PER_GROUP)
    el = jnp.where(in_grp, logits, neg)
    m1 = jnp.max(el, axis=-1, keepdims=True)
    e1 = jnp.min(jnp.where(el == m1, ex, big), axis=-1, keepdims=True)
    el2 = jnp.where(ex == e1, neg, el)
    m2 = jnp.max(el2, axis=-1, keepdims=True)
    e2 = jnp.min(jnp.where(el2 == m2, ex, big), axis=-1, keepdims=True)
    p2 = jnp.exp(m2 - m1)
    w1 = g_w / (1.0 + p2)
    w2 = g_w * p2 / (1.0 + p2)

    oh1 = ex == e1
    oh2 = ex == e2
    oh = jnp.where(oh1 | oh2, 1.0, 0.0)
    ri = lax.broadcasted_iota(I32, (tm, tm), 0)
    ci = lax.broadcasted_iota(I32, (tm, tm), 1)
    strict = jnp.where(ri > ci, 1.0, 0.0).astype(BF16)
    local_rank = _dot(strict, oh.astype(BF16))
    cnt = jnp.sum(oh, axis=0, keepdims=True)
    run_chunks = jnp.floor((cnt + (ROW_CHUNK - 1.0)) * (1.0 / ROW_CHUNK))
    ui = lax.broadcasted_iota(I32, (LANES, LANES), 0)
    uj = lax.broadcasted_iota(I32, (LANES, LANES), 1)
    before = jnp.where(ui < uj, 1.0, 0.0).astype(BF16)
    run_start = _dot(jnp.broadcast_to(run_chunks, (8, LANES)).astype(BF16), before)[0:1] * ROW_CHUNK
    slot = run_start + local_rank
    pos1 = jnp.sum(jnp.where(oh1, slot, 0.0), axis=-1, keepdims=True)
    pos2 = jnp.sum(jnp.where(oh2, slot, 0.0), axis=-1, keepdims=True)
    cnt_ref[0] = jnp.broadcast_to(cnt, (8, LANES))

    route = jnp.where(lane == 0, e1.astype(F32), 0.0)
    route = jnp.where(lane == 1, e2.astype(F32), route)
    route = jnp.where(lane == 2, w1, route)
    route = jnp.where(lane == 3, w2, route)
    route = jnp.where(lane == 4, pos1, route)
    route = jnp.where(lane == 5, pos2, route)
    route_ref[...] = route

    hi1 = jnp.floor(pos1 * (1.0 / 256.0))
    hi2 = jnp.floor(pos2 * (1.0 / 256.0))
    digits = jnp.where(lane == 0, hi1, 0.0)
    digits = jnp.where(lane == 1, pos1 - 256.0 * hi1, digits)
    digits = jnp.where(lane == 2, hi2, digits)
    digits = jnp.where(lane == 3, pos2 - 256.0 * hi2, digits)
    pick = jnp.where(lax.broadcasted_iota(I32, (8, LANES), 0) == lax.broadcasted_iota(I32, (8, LANES), 1),
                     1.0, 0.0).astype(BF16)
    rows = _dot_nt(pick, digits.astype(BF16))

    h2b_s[...] = h2b
    rows_s[0:1, :] = rows[0:1] * 256.0 + rows[1:2]
    rows_s[1:2, :] = rows[2:3] * 256.0 + rows[3:4]


def _merge(x2d, ya, yb, proj, wa, wb, wo, ffn_g, w_router, b_router):
    t = x2d.shape[0]
    nb = t // TOK_TILE
    const = lambda *shape: pl.BlockSpec(shape, lambda i: (0,) * len(shape))
    cur = lambda i: jnp.minimum(i, nb - 1)
    prev = lambda i: jnp.maximum(i - 1, 0)
    return pl.pallas_call(
        _merge_kernel,
        grid=(nb + 1,),
        in_specs=[
            pl.BlockSpec((TOK_TILE, D_MODEL), lambda i: (cur(i), 0)),
            pl.BlockSpec((TOK_TILE, HG_WIDTH), lambda i: (cur(i), 0)),
            pl.BlockSpec((TOK_TILE, D_MODEL), lambda i: (cur(i), 0)),
            pl.BlockSpec((TOK_TILE, D_MODEL), lambda i: (cur(i), COL_GA)),
            pl.BlockSpec((TOK_TILE, D_MODEL), lambda i: (cur(i), COL_GB)),
            const(HG_WIDTH, D_MODEL),
            const(D_MODEL, D_MODEL),
            const(D_MODEL, D_MODEL),
            const(1, D_MODEL),
            const(D_MODEL, LANES),
            const(1, LANES),
        ],
        out_specs=[
            pl.BlockSpec((TOK_TILE, D_MODEL), lambda i: (cur(i), 0)),
            pl.BlockSpec((LOCAL_ROWS, D_MODEL), lambda i: (prev(i), 0)),
            pl.BlockSpec((TOK_TILE, LANES), lambda i: (cur(i), 0)),
            pl.BlockSpec((1, 8, LANES), lambda i: (cur(i), 0, 0)),
        ],
        out_shape=[
            jax.ShapeDtypeStruct((t, D_MODEL), F32),
            jax.ShapeDtypeStruct((t // TOK_TILE * LOCAL_ROWS, D_MODEL), BF16),
            jax.ShapeDtypeStruct((t, LANES), F32),
            jax.ShapeDtypeStruct((t // TOK_TILE, 8, LANES), F32),
        ],
        scratch_shapes=[pltpu.VMEM((TOK_TILE, D_MODEL), BF16), pltpu.VMEM((8, TOK_TILE), F32)],
        compiler_params=_cparams(("arbitrary",)),
        name="merge_route",
    )(x2d, ya, yb, proj, proj, wa, wb, wo, ffn_g, w_router, b_router)


def _expert_kernel(te_ref, nt_ref, src_ref, xs_ref, wg_ref, wu_ref, wd_ref, ys_ref, xbuf, sem):
    i = pl.program_id(0)
    nt = nt_ref[0]

    def gather(tile, slot):
        copies = []
        for c in range(CHUNKS_PER_TILE):
            row = pl.multiple_of(src_ref[tile * CHUNKS_PER_TILE + c] * ROW_CHUNK, ROW_CHUNK)
            copies.append(pltpu.make_async_copy(
                xs_ref.at[pl.ds(row, ROW_CHUNK)],
                xbuf.at[slot, pl.ds(c * ROW_CHUNK, ROW_CHUNK)], sem.at[slot]))
        return copies

    @pl.when(i == 0)
    def _():
        for k in range(EXP_AHEAD):
            for cp in gather(k, k):
                cp.start()

    @pl.when(i < nt + EXP_AHEAD)
    def _():
        for cp in gather(i, i % EXP_SLOTS):
            cp.wait()

    @pl.when(i < nt)
    def _():
        slot = i % EXP_SLOTS
        wg, wu, wd = wg_ref[0], wu_ref[0], wd_ref[0]
        subs = [pl.ds(r, EXP_SUB) for r in range(0, EXP_TILE, EXP_SUB)]
        xs = [xbuf[slot, s, :] for s in subs]
        gates = [(_dot(x, wg), _dot(x, wu)) for x in xs]
        for c, cp in enumerate(gather(i + EXP_AHEAD, (i + EXP_AHEAD) % EXP_SLOTS)):
            cp.start(priority=c % 2)
        hidden = [(_silu(a) * u).astype(BF16) for a, u in gates]
        for s, h in zip(subs, hidden):
            ys_ref[s, :] = _dot(h, wd).astype(ys_ref.dtype)

    @pl.when(i >= nt)
    def _():
        ys_ref[...] = jnp.zeros_like(ys_ref)


def _experts(tile_expert, n_tiles_used, src_chunk, xs, wg, wu, wd, n_tiles):
    grid_spec = pltpu.PrefetchScalarGridSpec(
        num_scalar_prefetch=3,
        grid=(n_tiles,),
        in_specs=[
            pl.BlockSpec(memory_space=pl.ANY),
            pl.BlockSpec((1, D_MODEL, D_EXPERT), lambda i, te, nt, src: (te[i], 0, 0)),
            pl.BlockSpec((1, D_MODEL, D_EXPERT), lambda i, te, nt, src: (te[i], 0, 0)),
            pl.BlockSpec((1, D_EXPERT, D_MODEL), lambda i, te, nt, src: (te[i], 0, 0)),
        ],
        out_specs=pl.BlockSpec((EXP_TILE, D_MODEL), lambda i, te, nt, src: (i, 0)),
        scratch_shapes=[
            pltpu.VMEM((EXP_SLOTS, EXP_TILE, D_MODEL), BF16),
            pltpu.SemaphoreType.DMA((EXP_SLOTS,)),
        ],
    )
    return pl.pallas_call(
        _expert_kernel,
        grid_spec=grid_spec,
        out_shape=jax.ShapeDtypeStruct((n_tiles * EXP_TILE, D_MODEL), BF16),
        compiler_params=_cparams(("arbitrary",)),
        name="experts",
    )(tile_expert, n_tiles_used, src_chunk, xs, wg, wu, wd)


def _final_kernel(src_ref, x1_ref, route_ref, p_ref, ys_ref, pg_ref, wpg_ref, wpp_ref, fg_ref,
                  o_ref, ybuf, sem):
    tm = x1_ref.shape[0]
    i = pl.program_id(0)
    nb = pl.num_programs(0)

    def gather(block, slot):
        copies = []
        for c in range(LOCAL_CHUNKS):
            row = pl.multiple_of(src_ref[block * LOCAL_CHUNKS + c] * ROW_CHUNK, ROW_CHUNK)
            copies.append(pltpu.make_async_copy(
                ys_ref.at[pl.ds(row, ROW_CHUNK)],
                ybuf.at[slot, pl.ds(c * ROW_CHUNK, ROW_CHUNK)], sem.at[slot]))
        return copies

    @pl.when(i == 0)
    def _():
        for k in range(FIN_AHEAD):
            for cp in gather(k % nb, k):
                cp.start()

    slot = i % FIN_SLOTS
    for cp in gather(i, slot):
        cp.wait()

    parts = [pl.ds(r, tm // FIN_PARTS) for r in range(0, tm, tm // FIN_PARTS)]
    route = route_ref[...]
    ple = [_dot(p_ref[rows, :].astype(BF16), wpp_ref[...]) for rows in parts]

    slab_col = lax.broadcasted_iota(I32, (tm // FIN_PARTS, SORT_SLAB), 1).astype(F32)
    moes = []
    for k in range(FIN_PARTS):
        rows = slice(k * (tm // FIN_PARTS), (k + 1) * (tm // FIN_PARTS))
        w1, w2 = route[rows, 2:3], route[rows, 3:4]
        pos1, pos2 = route[rows, 4:5], route[rows, 5:6]
        moe = jnp.zeros((tm // FIN_PARTS, D_MODEL), F32)
        for k0 in range(0, LOCAL_ROWS, SORT_SLAB):
            sel = jnp.where(slab_col == pos1 - float(k0), w1,
                            jnp.where(slab_col == pos2 - float(k0), w2, 0.0)).astype(BF16)
            moe = moe + _dot(sel, ybuf[slot, k0:k0 + SORT_SLAB, :])
        moes.append(moe)
    for c, cp in enumerate(gather((i + FIN_AHEAD) % nb, (i + FIN_AHEAD) % FIN_SLOTS)):
        cp.start(priority=c % 2)
    x2s = [x1_ref[rows, :] + moe for rows, moe in zip(parts, moes)]
    hps = [_rms(x2, pg_ref[...]).astype(BF16) for x2 in x2s]
    gates = [_sigmoid(_dot(hp, wpg_ref[...])) for hp in hps]
    for rows, x2, gate, pp in zip(parts, x2s, gates, ple):
        o_ref[rows, :] = _rms(x2 + gate * pp, fg_ref[...])

    @pl.when(i == nb - 1)
    def _():
        for k in range(1, FIN_AHEAD + 1):
            for cp in gather(0, (i + k) % FIN_SLOTS):
                cp.wait()


def _final(src_chunk, x1, route, p2d, ys, ple_g, wpg, wpp, final_g):
    t = x1.shape[0]
    const = lambda *shape: pl.BlockSpec(shape, lambda i, src: (0,) * len(shape))
    grid_spec = pltpu.PrefetchScalarGridSpec(
        num_scalar_prefetch=1,
        grid=(t // TOK_TILE,),
        in_specs=[
            pl.BlockSpec((TOK_TILE, D_MODEL), lambda i, src: (i, 0)),
            pl.BlockSpec((TOK_TILE, LANES), lambda i, src: (i, 0)),
            pl.BlockSpec((TOK_TILE, PLE_DIM), lambda i, src: (i, 0)),
            pl.BlockSpec(memory_space=pl.ANY),
            const(1, D_MODEL),
            const(D_MODEL, D_MODEL),
            const(PLE_DIM, D_MODEL),
            const(1, D_MODEL),
        ],
        out_specs=pl.BlockSpec((TOK_TILE, D_MODEL), lambda i, src: (i, 0)),
        scratch_shapes=[pltpu.VMEM((FIN_SLOTS, LOCAL_ROWS, D_MODEL), BF16),
                        pltpu.SemaphoreType.DMA((FIN_SLOTS,))],
    )
    return pl.pallas_call(
        _final_kernel,
        grid_spec=grid_spec,
        out_shape=jax.ShapeDtypeStruct((t, D_MODEL), F32),
        compiler_params=_cparams(("arbitrary",)),
        name="combine_ple_final",
    )(src_chunk, x1, route, p2d, ys, ple_g, wpg, wpp, final_g)


def _rotary_tables(seq):
    inv = ROPE_BASE ** (-jnp.arange(0, RET_DK, 2, dtype=F32) / RET_DK)
    inv = jnp.concatenate([inv, inv])
    sign = jnp.where(jnp.arange(RET_DK) < RET_DK // 2, -1.0, 1.0).astype(F32)
    hi = (jnp.arange(seq // ROPE_SPLIT, dtype=F32) * ROPE_SPLIT)[:, None] * inv[None, :]
    lo = jnp.arange(ROPE_SPLIT, dtype=F32)[:, None] * inv[None, :]
    ch, sh = jnp.cos(hi)[:, None, :], jnp.sin(hi)[:, None, :]
    cl, sl = jnp.cos(lo)[None], jnp.sin(lo)[None]
    cos = (ch * cl - sh * sl).reshape(seq, RET_DK)
    sin = ((sh * cl + ch * sl) * sign).reshape(seq, RET_DK)
    return cos, sin


def _layer(x2d, p2d, batch, seq, mix_norm, w_in, hg_lb_logits, hg_norm, ret_norm, w_branch_a,
           w_branch_b, w_out, ffn_norm, w_rg, b_rg, w_re, b_re, w_gate, w_up, w_down, ple_norm,
           w_ple_gate, w_ple_proj, out_gain):
    t = x2d.shape[0]
    row = lambda v: v.reshape(1, -1).astype(F32)

    proj, hf, wg16, wu16, wd16 = _inproj(x2d, row(mix_norm), w_in.astype(BF16), w_gate, w_up, w_down)
    cos2, sin2 = _rotary_tables(seq)
    ya, yb = _mixers(proj, hf, hg_lb_logits.astype(F32), row(hg_norm), cos2, sin2, row(ret_norm), batch, seq)

    n_r = N_GROUPS + N_EXPERTS
    w_router = jnp.pad(jnp.concatenate([w_rg, w_re], axis=1).astype(BF16), ((0, 0), (0, LANES - n_r)))
    b_router = jnp.zeros((1, LANES), F32).at[0, :n_r].set(jnp.concatenate([b_rg, b_re]))
    x1, xs, route, counts = _merge(
        x2d, ya, yb, proj, w_branch_a.astype(BF16), w_branch_b.astype(BF16), w_out.astype(BF16),
        row(ffn_norm), w_router, b_router)

    n_blocks = t // TOK_TILE
    cnt = counts[:, 0, N_GROUPS:N_GROUPS + N_EXPERTS].astype(I32)
    run_chunks = (cnt + ROW_CHUNK - 1) // ROW_CHUNK
    earlier_e = jnp.tril(jnp.ones((N_EXPERTS, N_EXPERTS), I32), -1)
    earlier_b = jnp.tril(jnp.ones((n_blocks, n_blocks), I32), -1)
    run_local = jnp.sum(run_chunks[:, None, :] * earlier_e[None], axis=2)
    seg_chunks = jnp.sum(run_chunks, axis=0)
    tiles_per = (seg_chunks + CHUNKS_PER_TILE - 1) // CHUNKS_PER_TILE
    seg_start = jnp.sum(tiles_per[None, :] * earlier_e, axis=1) * CHUNKS_PER_TILE
    tile_end = seg_start // CHUNKS_PER_TILE + tiles_per
    run_global = seg_start[None, :] + jnp.sum(run_chunks.T[:, None, :] * earlier_b[None], axis=2).T
    max_chunks = (2 * t) // ROW_CHUNK + n_blocks * N_EXPERTS + N_EXPERTS * (CHUNKS_PER_TILE - 1)
    n_tiles = -(-max_chunks // CHUNKS_PER_TILE) + EXP_AHEAD
    tile_ids = jnp.arange(n_tiles, dtype=I32)
    tile_expert = jnp.minimum(jnp.sum((tile_end[None, :] <= tile_ids[:, None]).astype(I32), axis=1),
                              N_EXPERTS - 1)
    n_used = tile_end[-1:].astype(I32)

    block_ids = jnp.arange(n_blocks, dtype=I32)
    zero_local = LOCAL_CHUNKS - 1
    zero_global = n_tiles * CHUNKS_PER_TILE - 1
    g = jnp.arange(n_tiles * CHUNKS_PER_TILE, dtype=I32)[:, None]
    e_g = jnp.repeat(tile_expert, CHUNKS_PER_TILE)
    pick_e = (e_g[:, None] == jnp.arange(N_EXPERTS, dtype=I32)[None, :]).astype(I32)
    rg, rc, rl = (jnp.sum(pick_e[:, :, None] * tab.T[None, :, :], axis=1)
                  for tab in (run_global, run_chunks, run_local))
    inside = (rg <= g) & (g < rg + rc)
    gather_src = jnp.sum(jnp.where(inside, block_ids[None, :] * LOCAL_CHUNKS + rl + (g - rg), 0), axis=1)
    gather_src = jnp.where(jnp.any(inside, axis=1), gather_src, zero_local).astype(I32)

    lc = jnp.arange(LOCAL_CHUNKS, dtype=I32)[None, :, None]
    inside = (run_local[:, None, :] <= lc) & (lc < (run_local + run_chunks)[:, None, :])
    back_src = jnp.sum(jnp.where(inside, run_global[:, None, :] + lc - run_local[:, None, :], 0), axis=2)
    back_src = jnp.where(jnp.any(inside, axis=2), back_src, zero_global).astype(I32).reshape(-1)

    ys = _experts(tile_expert, n_used, gather_src, xs, wg16, wu16, wd16, n_tiles)
    return _final(back_src, x1, route, p2d, ys, row(ple_norm), w_ple_gate.astype(BF16),
                  w_ple_proj.astype(BF16), out_gain)


def kernel(x, p, mix_norm, w_in, hg_lb_logits, hg_norm, ret_norm, w_branch_a, w_branch_b, w_out,
           ffn_norm, w_router_group, b_router_group, w_router_expert, b_router_expert,
           w_expert_gate, w_expert_up, w_expert_down, ple_norm, w_ple_gate, w_ple_proj, final_norm):
    batch, seq, d = x.shape
    depth = p.shape[0]
    assert depth == 1, "the final rmsnorm is fused into the single layer"
    x2d = x.reshape(batch * seq, d)
    out = _layer(x2d, p[0].reshape(batch * seq, -1), batch, seq, mix_norm[0], w_in[0], hg_lb_logits,
                 hg_norm[0], ret_norm[0], w_branch_a[0], w_branch_b[0], w_out[0], ffn_norm[0],
                 w_router_group[0], b_router_group[0], w_router_expert[0], b_router_expert[0],
                 w_expert_gate[0], w_expert_up[0], w_expert_down[0], ple_norm[0], w_ple_gate[0],
                 w_ple_proj[0], final_norm.reshape(1, -1).astype(F32))
    return out.reshape(batch, seq, d)
```

```python
import jax
import jax.numpy as jnp
from jax import lax
from jax.experimental import pallas as pl
from jax.experimental.pallas import tpu as pltpu

F32 = jnp.float32
BF16 = jnp.bfloat16
I32 = jnp.int32

EPS = 1e-6
D_MODEL = 1024
PLE_DIM = 256
HG_HEADS = 4
HG_DK = 128
HG_WIDTH = HG_HEADS * HG_DK
RET_HEADS = 4
RET_DK = 128
RET_DV = 256
ROPE_BASE = 10000.0
ROPE_SPLIT = 64
IN_TOTAL = 7168
N_GROUPS = 4
EXPERTS_PER_GROUP = 8
N_EXPERTS = 32
D_EXPERT = 256

COL_HQ, COL_HF, COL_HI, COL_HG = 0, 4, 8, 12
COL_RQ, COL_RK = 16, 20
COL_RV, COL_RG = 24, 32
COL_GA, COL_GB = 5, 6

LANES = 128
VMEM_LIMIT = 56 * 1024 * 1024

HG_CHUNK = 64
HG_SUB = 8
HG_UNROLL = 16
HG_STATE_UNROLL = 16
HG_NORM_ROWS = 1024
RET_CHUNK = 128
RET_STATE_UNROLL = 8
SEQ_TILE = 1024
TOK_TILE = 512
EXP_TILE = 512
EXP_SUB = 512
EXP_AHEAD = 3
EXP_SLOTS = EXP_AHEAD + 1
ROW_CHUNK = 16
CHUNKS_PER_TILE = EXP_TILE // ROW_CHUNK
LOCAL_ROWS = 2 * TOK_TILE + N_EXPERTS * ROW_CHUNK
LOCAL_CHUNKS = LOCAL_ROWS // ROW_CHUNK
SORT_SLAB = 256
MERGE_SLAB = LOCAL_ROWS
FIN_AHEAD = 2
FIN_SLOTS = FIN_AHEAD + 1
FIN_PARTS = 2


def _cparams(sem):
    return pltpu.CompilerParams(dimension_semantics=sem, vmem_limit_bytes=VMEM_LIMIT)


def _rms(x, g):
    return x * lax.rsqrt(jnp.mean(x * x, axis=-1, keepdims=True) + EPS) * g


def _sigmoid(x):
    return 1.0 / (1.0 + jnp.exp(-x))


def _silu(x):
    return x * _sigmoid(x)


def _split3(x):
    hi = x.astype(BF16)
    r1 = x - hi.astype(F32)
    mid = r1.astype(BF16)
    lo = (r1 - mid.astype(F32)).astype(BF16)
    return hi, mid, lo


def _dot(a, b):
    return jnp.dot(a, b, preferred_element_type=F32)


def _dot_nt(a, b):
    return lax.dot_general(a, b, (((1,), (1,)), ((), ())), preferred_element_type=F32)


def _dot_tn(a, b):
    return lax.dot_general(a, b, (((0,), (0,)), ((), ())), preferred_element_type=F32)


def _inproj_kernel(x_ref, g_ref, w_ref, wg_ref, wu_ref, wd_ref, proj_ref, hf_ref,
                   wg16_ref, wu16_ref, wd16_ref):
    wg16_ref[...] = wg_ref[...].astype(BF16)
    wu16_ref[...] = wu_ref[...].astype(BF16)
    wd16_ref[...] = wd_ref[...].astype(BF16)

    h = _rms(x_ref[...], g_ref[...]).astype(BF16)
    tn = HG_WIDTH
    for j in range(IN_TOTAL // tn):
        acc = _dot(h, w_ref[:, j * tn:(j + 1) * tn])
        proj_ref[:, j * tn:(j + 1) * tn] = acc.astype(BF16)
        if j * tn == COL_HF * LANES:
            hf_ref[...] = acc


def _inproj(x2d, gain, w_bf16, w_gate, w_up, w_down):
    t = x2d.shape[0]
    steps = t // TOK_TILE
    assert N_EXPERTS % steps == 0, "expert weights are converted in equal shares per grid step"
    share = N_EXPERTS // steps
    up_spec = pl.BlockSpec((share, D_MODEL, D_EXPERT), lambda i: (i, 0, 0))
    down_spec = pl.BlockSpec((share, D_EXPERT, D_MODEL), lambda i: (i, 0, 0))
    return pl.pallas_call(
        _inproj_kernel,
        grid=(steps,),
        in_specs=[
            pl.BlockSpec((TOK_TILE, D_MODEL), lambda i: (i, 0)),
            pl.BlockSpec((1, D_MODEL), lambda i: (0, 0)),
            pl.BlockSpec((D_MODEL, IN_TOTAL), lambda i: (0, 0), pipeline_mode=pl.Buffered(1)),
            up_spec, up_spec, down_spec,
        ],
        out_specs=[
            pl.BlockSpec((TOK_TILE, IN_TOTAL), lambda i: (i, 0)),
            pl.BlockSpec((TOK_TILE, HG_WIDTH), lambda i: (i, 0)),
            up_spec, up_spec, down_spec,
        ],
        out_shape=[
            jax.ShapeDtypeStruct((t, IN_TOTAL), BF16),
            jax.ShapeDtypeStruct((t, HG_WIDTH), F32),
            jax.ShapeDtypeStruct(w_gate.shape, BF16),
            jax.ShapeDtypeStruct(w_up.shape, BF16),
            jax.ShapeDtypeStruct(w_down.shape, BF16),
        ],
        compiler_params=_cparams(("arbitrary",)),
        name="inproj",
    )(x2d, gain, w_bf16, w_gate, w_up, w_down)


def _hgrn_stages(lbl_ref, q_ref, f_ref, i_ref, g_ref, ng_ref, o_ref,
                 st_ref, b_s, k_s, v_s, oi_s, qe_s, kv_s, dec_s):
    c = HG_CHUNK
    nsub = c // HG_SUB

    logits = lbl_ref[...]
    e = jnp.exp(logits - jnp.max(logits, axis=0, keepdims=True))
    lb = e[0:1] / jnp.sum(e, axis=0, keepdims=True)
    one_m_lb = jnp.sum(e[1:], axis=0, keepdims=True) / jnp.sum(e, axis=0, keepdims=True)
    ng = ng_ref[...]

    row = lax.broadcasted_iota(I32, (c, c), 0)
    col = lax.broadcasted_iota(I32, (c, c), 1)
    tri = jnp.where(row >= col, 1.0, 0.0).astype(BF16)
    sub_row = lax.broadcasted_iota(I32, (HG_SUB, HG_DK), 0)
    masked = jnp.float32(-1e30)

    def bcast_row(ref, r, rows):
        return jnp.broadcast_to(ref[pl.ds(r, 1), :], (rows, HG_DK))

    def prep(ci, slot):
        r0 = pl.multiple_of(ci * c, c)
        z = f_ref[pl.ds(r0, c), :]
        ez = jnp.exp(-jnp.abs(z))
        rz = 1.0 / (1.0 + ez)
        pos = z >= 0.0
        logf = jnp.log2(lb + one_m_lb * jnp.where(pos, rz, ez * rz))
        kk = one_m_lb * jnp.where(pos, ez * rz, rz)
        q = _silu(q_ref[pl.ds(r0, c), :].astype(F32))
        v = i_ref[pl.ds(r0, c), :].astype(F32)
        k_s[slot] = kk
        v_s[slot] = v
        return dict(ci=ci, r0=r0, slot=slot, kk=kk, q=q, v=v, v16=v.astype(BF16), parts=_split3(logf))

    def cumulate(s):
        hi, mid, lo = s.pop("parts")
        b = (_dot(tri, lo) + _dot(tri, mid)) + _dot(tri, hi)
        b_s[s["slot"]] = b
        qe_s[pl.ds(s["r0"], c), :] = (s["q"] * jnp.exp2(b)).astype(BF16)
        s["b"] = b
        return s

    def off_diagonal(s):
        b, q, kk = s["b"], s["q"], s["kk"]
        bs_ref = b_s.at[s["slot"]]
        squares = []
        size = c // 2
        while size >= HG_SUB:
            for r0 in range(size, c, 2 * size):
                edge = bcast_row(bs_ref, r0 - 1, size)
                qs = q[r0:r0 + size, :] * jnp.exp2(b[r0:r0 + size, :] - edge)
                ks = kk[r0 - size:r0, :] * jnp.exp2(edge - b[r0 - size:r0, :])
                squares.append((r0, size, _dot_nt(qs.astype(BF16), ks.astype(BF16))))
            size //= 2
        s["squares"] = squares
        return s

    def apply_values(s):
        b, kk, v = s["b"], s["kk"], s["v"]
        blast = b[c - 1:c, :]
        rows = [jnp.zeros((HG_SUB, HG_DK), F32) for _ in range(nsub)]
        for r0, size, a in s.pop("squares"):
            part = _dot(a.astype(BF16), v[r0 - size:r0, :].astype(BF16))
            for i in range(size // HG_SUB):
                rows[r0 // HG_SUB + i] = rows[r0 // HG_SUB + i] + part[i * HG_SUB:(i + 1) * HG_SUB, :]
        s["o"] = jnp.concatenate(rows, axis=0)
        kd = kk * jnp.exp2(blast - b)
        kv_s[s["ci"]] = _dot_tn(s["v16"], kd.astype(BF16))
        dec_s[s["ci"]] = jnp.broadcast_to(jnp.exp2(blast), (HG_SUB, HG_DK))
        return s

    def diagonal(s):
        b, q = s["b"], s["q"]
        bs_ref, ks_ref, vs_ref = b_s.at[s["slot"]], k_s.at[s["slot"]], v_s.at[s["slot"]]
        d_blocks = []
        for i in range(nsub):
            sl = slice(i * HG_SUB, (i + 1) * HG_SUB)
            bt, qt = b[sl, :], q[sl, :]
            acc = jnp.zeros((HG_SUB, HG_DK), F32)
            for j in range(HG_SUB):
                r = i * HG_SUB + j
                arg = bt - bcast_row(bs_ref, r, HG_SUB)
                if j > 0:
                    arg = jnp.where(sub_row >= j, arg, masked)
                g = jnp.exp2(arg) * (qt * bcast_row(ks_ref, r, HG_SUB))
                acc = acc + jnp.sum(g, axis=-1, keepdims=True) * bcast_row(vs_ref, r, HG_SUB)
            d_blocks.append(acc)
        oi_s[pl.ds(s["r0"], c), :] = s["o"] + jnp.concatenate(d_blocks, axis=0)

    def carried_group(gi):
        st = st_ref[...]
        outs = []
        for j in range(HG_STATE_UNROLL):
            ci = gi * HG_STATE_UNROLL + j
            r0 = pl.multiple_of(ci * c, c)
            outs.append((r0, _dot_nt(qe_s[pl.ds(r0, c), :], st.astype(BF16))))
            st = st * dec_s[ci][0:1, :] + kv_s[ci]
        st_ref[...] = st
        for r0, os in outs:
            oi_s[pl.ds(r0, c), :] += os

    def finish(ri):
        r0 = pl.multiple_of(ri * HG_NORM_ROWS, HG_NORM_ROWS)
        y = _rms(oi_s[pl.ds(r0, HG_NORM_ROWS), :], ng) * _silu(g_ref[pl.ds(r0, HG_NORM_ROWS), :].astype(F32))
        o_ref[pl.ds(r0, HG_NORM_ROWS), :] = y.astype(o_ref.dtype)

    return prep, (cumulate, off_diagonal, apply_values, diagonal), carried_group, finish


HG_SCRATCH = [
    pltpu.VMEM((HG_DK, HG_DK), F32),
    pltpu.VMEM((HG_UNROLL, HG_CHUNK, HG_DK), F32),
    pltpu.VMEM((HG_UNROLL, HG_CHUNK, HG_DK), F32),
    pltpu.VMEM((HG_UNROLL, HG_CHUNK, HG_DK), F32),
    pltpu.VMEM((SEQ_TILE, HG_DK), F32),
    pltpu.VMEM((SEQ_TILE, HG_DK), BF16),
    pltpu.VMEM((SEQ_TILE // HG_CHUNK, HG_DK, HG_DK), F32),
    pltpu.VMEM((SEQ_TILE // HG_CHUNK, HG_SUB, HG_DK), F32),
]


def _ret_stages(q_ref, k_ref, v_ref, g_ref, cos_ref, sin_ref, ng_ref, o_ref,
                r_ref, oi_s, q16_s, kv_s):
    c = RET_CHUNK

    hf = jnp.full((1, 1), pl.program_id(1), I32).astype(F32)
    lg = jnp.log1p(-jnp.exp2(-5.0 - hf))
    ti = lax.broadcasted_iota(I32, (c, c), 0)
    si = lax.broadcasted_iota(I32, (c, c), 1)
    rel = (ti - si).astype(F32)
    intra = jnp.where(ti >= si, jnp.exp(jnp.maximum(rel, 0.0) * lg), 0.0)
    idx = lax.broadcasted_iota(I32, (c, 1), 0).astype(F32)
    inter = jnp.exp((idx + 1.0) * lg)
    to_state = jnp.exp((c - 1.0 - idx) * lg)
    chunk_decay = jnp.exp(float(c) * lg)
    ng = ng_ref[...]
    half = RET_DK // 2

    def prep(ci):
        r0 = pl.multiple_of(ci * c, c)
        cos = cos_ref[pl.ds(r0, c), :]
        sin = sin_ref[pl.ds(r0, c), :]
        q = q_ref[pl.ds(r0, c), :].astype(F32)
        k = k_ref[pl.ds(r0, c), :].astype(F32)
        qr = (q * cos + pltpu.roll(q, half, 1) * sin) * (RET_DK ** -0.5)
        kr = k * cos + pltpu.roll(k, half, 1) * sin
        q16 = qr.astype(BF16)
        q16_s[pl.ds(r0, c), :] = q16
        return dict(ci=ci, r0=r0, q16=q16, k16=kr.astype(BF16), kts=(kr * to_state).astype(BF16),
                    v16=v_ref[pl.ds(r0, c), :])

    def scores(s):
        s["att"] = (_dot_nt(s.pop("q16"), s.pop("k16")) * intra).astype(BF16)
        return s

    def apply_values(s):
        oi_s[pl.ds(s["r0"], c), :] = _dot(s["att"], s["v16"])
        kv_s[s["ci"]] = _dot_tn(s["kts"], s["v16"])
        return s

    def carried_group(gi):
        r = r_ref[...]
        outs = []
        for j in range(RET_STATE_UNROLL):
            ci = gi * RET_STATE_UNROLL + j
            r0 = pl.multiple_of(ci * c, c)
            outs.append((r0, _dot(q16_s[pl.ds(r0, c), :], r.astype(BF16))))
            r = chunk_decay * r + kv_s[ci]
        r_ref[...] = r
        for r0, qr_state in outs:
            o = oi_s[pl.ds(r0, c), :] + qr_state * inter
            y = _rms(o, ng) * _silu(g_ref[pl.ds(r0, c), :].astype(F32))
            o_ref[pl.ds(r0, c), :] = y.astype(o_ref.dtype)

    return prep, (scores, apply_values), carried_group


RET_SCRATCH = [
    pltpu.VMEM((RET_DK, RET_DV), F32),
    pltpu.VMEM((SEQ_TILE, RET_DV), F32),
    pltpu.VMEM((SEQ_TILE, RET_DK), BF16),
    pltpu.VMEM((SEQ_TILE // RET_CHUNK, RET_DK, RET_DV), F32),
]


def _mixers_kernel(*refs):
    hg_refs = refs[:6] + refs[13:14] + refs[15:15 + len(HG_SCRATCH)]
    ret_refs = refs[6:13] + refs[14:15] + refs[15 + len(HG_SCRATCH):]
    st_ref, r_ref = hg_refs[7], ret_refs[8]
    n_tokens = refs[1].shape[0]

    @pl.when(pl.program_id(2) == 0)
    def _():
        st_ref[...] = jnp.zeros_like(st_ref)
        r_ref[...] = jnp.zeros_like(r_ref)

    hg_prep, hg_local, hg_carried, hg_finish = _hgrn_stages(*hg_refs)
    ret_prep, ret_local, ret_carried = _ret_stages(*ret_refs)
    group_tokens = HG_UNROLL * HG_CHUNK
    ret_unroll = group_tokens // RET_CHUNK

    def local_group(gi, carry):
        hs = [hg_prep(gi * HG_UNROLL + slot, slot) for slot in range(HG_UNROLL)]
        rs = [ret_prep(gi * ret_unroll + j) for j in range(ret_unroll)]
        cumulate, off_diagonal, apply_values, diagonal = hg_local
        scores, ret_apply = ret_local
        hs = [cumulate(s) for s in hs]
        rs = [scores(s) for s in rs]
        hs = [off_diagonal(s) for s in hs]
        rs = [ret_apply(s) for s in rs]
        hs = [apply_values(s) for s in hs]
        for s in hs:
            diagonal(s)
        return carry

    lax.fori_loop(0, n_tokens // group_tokens, local_group, 0)

    state_tokens = HG_STATE_UNROLL * HG_CHUNK
    assert state_tokens == RET_STATE_UNROLL * RET_CHUNK

    def carried_group(gi, carry):
        hg_carried(gi)
        ret_carried(gi)
        return carry

    lax.fori_loop(0, n_tokens // state_tokens, carried_group, 0)

    def finish(ri, carry):
        hg_finish(ri)
        return carry

    lax.fori_loop(0, n_tokens // HG_NORM_ROWS, finish, 0)


def _mixers(proj, hf, lb_logits, hg_norm, cos2, sin2, ret_norm, batch, seq):
    ns = seq // SEQ_TILE
    tok = lambda b, h, s: b * ns + s
    hg_col = lambda base: pl.BlockSpec((SEQ_TILE, HG_DK), lambda b, h, s: (tok(b, h, s), base + h))
    return pl.pallas_call(
        _mixers_kernel,
        grid=(batch, HG_HEADS, ns),
        in_specs=[
            pl.BlockSpec((2, HG_DK), lambda b, h, s: (0, h)),
            hg_col(COL_HQ), hg_col(0), hg_col(COL_HI), hg_col(COL_HG),
            pl.BlockSpec((1, HG_DK), lambda b, h, s: (0, 0)),
            hg_col(COL_RQ), hg_col(COL_RK),
            pl.BlockSpec((SEQ_TILE, RET_DV), lambda b, h, s: (tok(b, h, s), COL_RV // 2 + h)),
            pl.BlockSpec((SEQ_TILE, RET_DV), lambda b, h, s: (tok(b, h, s), COL_RG // 2 + h)),
            pl.BlockSpec((SEQ_TILE, RET_DK), lambda b, h, s: (s, 0)),
            pl.BlockSpec((SEQ_TILE, RET_DK), lambda b, h, s: (s, 0)),
            pl.BlockSpec((1, RET_DV), lambda b, h, s: (0, 0)),
        ],
        out_specs=[
            pl.BlockSpec((SEQ_TILE, HG_DK), lambda b, h, s: (tok(b, h, s), h)),
            pl.BlockSpec((SEQ_TILE, RET_DV), lambda b, h, s: (tok(b, h, s), h)),
        ],
        out_shape=[
            jax.ShapeDtypeStruct((batch * seq, HG_WIDTH), BF16),
            jax.ShapeDtypeStruct((batch * seq, RET_HEADS * RET_DV), BF16),
        ],
        scratch_shapes=HG_SCRATCH + RET_SCRATCH,
        compiler_params=_cparams(("arbitrary", "arbitrary", "arbitrary")),
        name="mixers",
    )(lb_logits, proj, hf, proj, proj, hg_norm, proj, proj, proj, proj, cos2, sin2, ret_norm)


def _merge_kernel(x_ref, ya_ref, yb_ref, ga_ref, gb_ref, wa_ref, wb_ref, wo_ref, fg_ref,
                  wr_ref, br_ref, x1_ref, xs_ref, route_ref, cnt_ref, h2b_s, rows_s):
    tm = x_ref.shape[0]

    @pl.when(pl.program_id(0) == 0)
    def _():
        h2b_s[...] = jnp.zeros_like(h2b_s)
        rows_s[...] = jnp.full(rows_s.shape, -1.0, F32)

    def sort_previous(lo, hi):
        slab_row = lax.broadcasted_iota(I32, (MERGE_SLAB, tm), 0).astype(F32)
        for r0 in range(lo, hi, MERGE_SLAB):
            sel = ((slab_row == rows_s[0:1, :] - float(r0)) | (slab_row == rows_s[1:2, :] - float(r0)))
            xs_ref[r0:r0 + MERGE_SLAB, :] = _dot(jnp.where(sel, 1.0, 0.0).astype(BF16),
                                                 h2b_s[...]).astype(BF16)

    merged = (_sigmoid(ga_ref[...].astype(F32)) * _dot(ya_ref[...], wa_ref[...])
              + _sigmoid(gb_ref[...].astype(F32)) * _dot(yb_ref[...], wb_ref[...]))
    x1 = x_ref[...] + _dot(merged.astype(BF16), wo_ref[...])
    x1_ref[...] = x1
    h2 = _rms(x1, fg_ref[...])

    h2b = h2.astype(BF16)
    logits = _dot(h2b, wr_ref[...]) + br_ref[...]
    sort_previous(0, LOCAL_ROWS)

    lane = lax.broadcasted_iota(I32, (tm, LANES), 1)
    neg = jnp.float32(-jnp.inf)
    big = jnp.int32(1 << 30)
    is_g = lane < N_GROUPS
    gl = jnp.where(is_g, logits, neg)
    gmax = jnp.max(gl, axis=-1, keepdims=True)
    g_idx = jnp.min(jnp.where(gl == gmax, lane, big), axis=-1, keepdims=True)
    g_w = 1.0 / jnp.sum(jnp.where(is_g, jnp.exp(gl - gmax), 0.0), axis=-1, keepdims=True)

    ex = lane - N_GROUPS
    in_grp = (ex >= g_idx * EXPERTS_PER_GROUP) & (ex < (g_idx + 1) * EXPERTS_PER_GROUP)
    el = jnp.where(in_grp, logits, neg)
    m1 = jnp.max(el, axis=-1, keepdims=True)
    e1 = jnp.min(jnp.where(el == m1, ex, big), axis=-1, keepdims=True)
    el2 = jnp.where(ex == e1, neg, el)
    m2 = jnp.max(el2, axis=-1, keepdims=True)
    e2 = jnp.min(jnp.where(el2 == m2, ex, big), axis=-1, keepdims=True)
    p2 = jnp.exp(m2 - m1)
    w1 = g_w / (1.0 + p2)
    w2 = g_w * p2 / (1.0 + p2)

    oh1 = ex == e1
    oh2 = ex == e2
    oh = jnp.where(oh1 | oh2, 1.0, 0.0)
    ri = lax.broadcasted_iota(I32, (tm, tm), 0)
    ci = lax.broadcasted_iota(I32, (tm, tm), 1)
    strict = jnp.where(ri > ci, 1.0, 0.0).astype(BF16)
    local_rank = _dot(strict, oh.astype(BF16))
    cnt = jnp.sum(oh, axis=0, keepdims=True)
    run_chunks = jnp.floor((cnt + (ROW_CHUNK - 1.0)) * (1.0 / ROW_CHUNK))
    ui = lax.broadcasted_iota(I32, (LANES, LANES), 0)
    uj = lax.broadcasted_iota(I32, (LANES, LANES), 1)
    before = jnp.where(ui < uj, 1.0, 0.0).astype(BF16)
    run_start = _dot(jnp.broadcast_to(run_chunks, (8, LANES)).astype(BF16), before)[0:1] * ROW_CHUNK
    slot = run_start + local_rank
    pos1 = jnp.sum(jnp.where(oh1, slot, 0.0), axis=-1, keepdims=True)
    pos2 = jnp.sum(jnp.where(oh2, slot, 0.0), axis=-1, keepdims=True)
    cnt_ref[0] = jnp.broadcast_to(cnt, (8, LANES))

    route = jnp.where(lane == 0, e1.astype(F32), 0.0)
    route = jnp.where(lane == 1, e2.astype(F32), route)
    route = jnp.where(lane == 2, w1, route)
    route = jnp.where(lane == 3, w2, route)
    route = jnp.where(lane == 4, pos1, route)
    route = jnp.where(lane == 5, pos2, route)
    route_ref[...] = route

    hi1 = jnp.floor(pos1 * (1.0 / 256.0))
    hi2 = jnp.floor(pos2 * (1.0 / 256.0))
    digits = jnp.where(lane == 0, hi1, 0.0)
    digits = jnp.where(lane == 1, pos1 - 256.0 * hi1, digits)
    digits = jnp.where(lane == 2, hi2, digits)
    digits = jnp.where(lane == 3, pos2 - 256.0 * hi2, digits)
    pick = jnp.where(lax.broadcasted_iota(I32, (8, LANES), 0) == lax.broadcasted_iota(I32, (8, LANES), 1),
                     1.0, 0.0).astype(BF16)
    rows = _dot_nt(pick, digits.astype(BF16))

    h2b_s[...] = h2b
    rows_s[0:1, :] = rows[0:1] * 256.0 + rows[1:2]
    rows_s[1:2, :] = rows[2:3] * 256.0 + rows[3:4]


def _merge(x2d, ya, yb, proj, wa, wb, wo, ffn_g, w_router, b_router):
    t = x2d.shape[0]
    nb = t // TOK_TILE
    const = lambda *shape: pl.BlockSpec(shape, lambda i: (0,) * len(shape))
    cur = lambda i: jnp.minimum(i, nb - 1)
    prev = lambda i: jnp.maximum(i - 1, 0)
    return pl.pallas_call(
        _merge_kernel,
        grid=(nb + 1,),
        in_specs=[
            pl.BlockSpec((TOK_TILE, D_MODEL), lambda i: (cur(i), 0)),
            pl.BlockSpec((TOK_TILE, HG_WIDTH), lambda i: (cur(i), 0)),
            pl.BlockSpec((TOK_TILE, D_MODEL), lambda i: (cur(i), 0)),
            pl.BlockSpec((TOK_TILE, D_MODEL), lambda i: (cur(i), COL_GA)),
            pl.BlockSpec((TOK_TILE, D_MODEL), lambda i: (cur(i), COL_GB)),
            const(HG_WIDTH, D_MODEL),
            const(D_MODEL, D_MODEL),
            const(D_MODEL, D_MODEL),
            const(1, D_MODEL),
            const(D_MODEL, LANES),
            const(1, LANES),
        ],
        out_specs=[
            pl.BlockSpec((TOK_TILE, D_MODEL), lambda i: (cur(i), 0)),
            pl.BlockSpec((LOCAL_ROWS, D_MODEL), lambda i: (prev(i), 0)),
            pl.BlockSpec((TOK_TILE, LANES), lambda i: (cur(i), 0)),
            pl.BlockSpec((1, 8, LANES), lambda i: (cur(i), 0, 0)),
        ],
        out_shape=[
            jax.ShapeDtypeStruct((t, D_MODEL), F32),
            jax.ShapeDtypeStruct((t // TOK_TILE * LOCAL_ROWS, D_MODEL), BF16),
            jax.ShapeDtypeStruct((t, LANES), F32),
            jax.ShapeDtypeStruct((t // TOK_TILE, 8, LANES), F32),
        ],
        scratch_shapes=[pltpu.VMEM((TOK_TILE, D_MODEL), BF16), pltpu.VMEM((8, TOK_TILE), F32)],
        compiler_params=_cparams(("arbitrary",)),
        name="merge_route",
    )(x2d, ya, yb, proj, proj, wa, wb, wo, ffn_g, w_router, b_router)


def _expert_kernel(te_ref, nt_ref, src_ref, xs_ref, wg_ref, wu_ref, wd_ref, ys_ref, xbuf, sem):
    i = pl.program_id(0)
    nt = nt_ref[0]

    def gather(tile, slot):
        copies = []
        for c in range(CHUNKS_PER_TILE):
            row = pl.multiple_of(src_ref[tile * CHUNKS_PER_TILE + c] * ROW_CHUNK, ROW_CHUNK)
            copies.append(pltpu.make_async_copy(
                xs_ref.at[pl.ds(row, ROW_CHUNK)],
                xbuf.at[slot, pl.ds(c * ROW_CHUNK, ROW_CHUNK)], sem.at[slot]))
        return copies

    @pl.when(i == 0)
    def _():
        for k in range(EXP_AHEAD):
            for cp in gather(k, k):
                cp.start()

    @pl.when(i < nt + EXP_AHEAD)
    def _():
        for cp in gather(i, i % EXP_SLOTS):
            cp.wait()

    @pl.when(i < nt)
    def _():
        slot = i % EXP_SLOTS
        wg, wu, wd = wg_ref[0], wu_ref[0], wd_ref[0]
        subs = [pl.ds(r, EXP_SUB) for r in range(0, EXP_TILE, EXP_SUB)]
        xs = [xbuf[slot, s, :] for s in subs]
        gates = [(_dot(x, wg), _dot(x, wu)) for x in xs]
        for c, cp in enumerate(gather(i + EXP_AHEAD, (i + EXP_AHEAD) % EXP_SLOTS)):
            cp.start(priority=c % 2)
        hidden = [(_silu(a) * u).astype(BF16) for a, u in gates]
        for s, h in zip(subs, hidden):
            ys_ref[s, :] = _dot(h, wd).astype(ys_ref.dtype)

    @pl.when(i >= nt)
    def _():
        ys_ref[...] = jnp.zeros_like(ys_ref)


def _experts(tile_expert, n_tiles_used, src_chunk, xs, wg, wu, wd, n_tiles):
    grid_spec = pltpu.PrefetchScalarGridSpec(
        num_scalar_prefetch=3,
        grid=(n_tiles,),
        in_specs=[
            pl.BlockSpec(memory_space=pl.ANY),
            pl.BlockSpec((1, D_MODEL, D_EXPERT), lambda i, te, nt, src: (te[i], 0, 0)),
            pl.BlockSpec((1, D_MODEL, D_EXPERT), lambda i, te, nt, src: (te[i], 0, 0)),
            pl.BlockSpec((1, D_EXPERT, D_MODEL), lambda i, te, nt, src: (te[i], 0, 0)),
        ],
        out_specs=pl.BlockSpec((EXP_TILE, D_MODEL), lambda i, te, nt, src: (i, 0)),
        scratch_shapes=[
            pltpu.VMEM((EXP_SLOTS, EXP_TILE, D_MODEL), BF16),
            pltpu.SemaphoreType.DMA((EXP_SLOTS,)),
        ],
    )
    return pl.pallas_call(
        _expert_kernel,
        grid_spec=grid_spec,
        out_shape=jax.ShapeDtypeStruct((n_tiles * EXP_TILE, D_MODEL), BF16),
        compiler_params=_cparams(("arbitrary",)),
        name="experts",
    )(tile_expert, n_tiles_used, src_chunk, xs, wg, wu, wd)


def _final_kernel(src_ref, x1_ref, route_ref, p_ref, ys_ref, pg_ref, wpg_ref, wpp_ref, fg_ref,
                  o_ref, ybuf, sem):
    tm = x1_ref.shape[0]
    i = pl.program_id(0)
    nb = pl.num_programs(0)

    def gather(block, slot):
        copies = []
        for c in range(LOCAL_CHUNKS):
            row = pl.multiple_of(src_ref[block * LOCAL_CHUNKS + c] * ROW_CHUNK, ROW_CHUNK)
            copies.append(pltpu.make_async_copy(
                ys_ref.at[pl.ds(row, ROW_CHUNK)],
                ybuf.at[slot, pl.ds(c * ROW_CHUNK, ROW_CHUNK)], sem.at[slot]))
        return copies

    @pl.when(i == 0)
    def _():
        for k in range(FIN_AHEAD):
            for cp in gather(k % nb, k):
                cp.start()

    slot = i % FIN_SLOTS
    for cp in gather(i, slot):
        cp.wait()

    parts = [pl.ds(r, tm // FIN_PARTS) for r in range(0, tm, tm // FIN_PARTS)]
    route = route_ref[...]
    ple = [_dot(p_ref[rows, :].astype(BF16), wpp_ref[...]) for rows in parts]

    slab_col = lax.broadcasted_iota(I32, (tm // FIN_PARTS, SORT_SLAB), 1).astype(F32)
    moes = []
    for k in range(FIN_PARTS):
        rows = slice(k * (tm // FIN_PARTS), (k + 1) * (tm // FIN_PARTS))
        w1, w2 = route[rows, 2:3], route[rows, 3:4]
        pos1, pos2 = route[rows, 4:5], route[rows, 5:6]
        moe = jnp.zeros((tm // FIN_PARTS, D_MODEL), F32)
        for k0 in range(0, LOCAL_ROWS, SORT_SLAB):
            sel = jnp.where(slab_col == pos1 - float(k0), w1,
                            jnp.where(slab_col == pos2 - float(k0), w2, 0.0)).astype(BF16)
            moe = moe + _dot(sel, ybuf[slot, k0:k0 + SORT_SLAB, :])
        moes.append(moe)
    for c, cp in enumerate(gather((i + FIN_AHEAD) % nb, (i + FIN_AHEAD) % FIN_SLOTS)):
        cp.start(priority=c % 2)
    x2s = [x1_ref[rows, :] + moe for rows, moe in zip(parts, moes)]
    hps = [_rms(x2, pg_ref[...]).astype(BF16) for x2 in x2s]
    gates = [_sigmoid(_dot(hp, wpg_ref[...])) for hp in hps]
    for rows, x2, gate, pp in zip(parts, x2s, gates, ple):
        o_ref[rows, :] = _rms(x2 + gate * pp, fg_ref[...])

    @pl.when(i == nb - 1)
    def _():
        for k in range(1, FIN_AHEAD + 1):
            for cp in gather(0, (i + k) % FIN_SLOTS):
                cp.wait()


def _final(src_chunk, x1, route, p2d, ys, ple_g, wpg, wpp, final_g):
    t = x1.shape[0]
    const = lambda *shape: pl.BlockSpec(shape, lambda i, src: (0,) * len(shape))
    grid_spec = pltpu.PrefetchScalarGridSpec(
        num_scalar_prefetch=1,
        grid=(t // TOK_TILE,),
        in_specs=[
            pl.BlockSpec((TOK_TILE, D_MODEL), lambda i, src: (i, 0)),
            pl.BlockSpec((TOK_TILE, LANES), lambda i, src: (i, 0)),
            pl.BlockSpec((TOK_TILE, PLE_DIM), lambda i, src: (i, 0)),
            pl.BlockSpec(memory_space=pl.ANY),
            const(1, D_MODEL),
            const(D_MODEL, D_MODEL),
            const(PLE_DIM, D_MODEL),
            const(1, D_MODEL),
        ],
        out_specs=pl.BlockSpec((TOK_TILE, D_MODEL), lambda i, src: (i, 0)),
        scratch_shapes=[pltpu.VMEM((FIN_SLOTS, LOCAL_ROWS, D_MODEL), BF16),
                        pltpu.SemaphoreType.DMA((FIN_SLOTS,))],
    )
    return pl.pallas_call(
        _final_kernel,
        grid_spec=grid_spec,
        out_shape=jax.ShapeDtypeStruct((t, D_MODEL), F32),
        compiler_params=_cparams(("arbitrary",)),
        name="combine_ple_final",
    )(src_chunk, x1, route, p2d, ys, ple_g, wpg, wpp, final_g)


def _rotary_tables(seq):
    inv = ROPE_BASE ** (-jnp.arange(0, RET_DK, 2, dtype=F32) / RET_DK)
    inv = jnp.concatenate([inv, inv])
    sign = jnp.where(jnp.arange(RET_DK) < RET_DK // 2, -1.0, 1.0).astype(F32)
    hi = (jnp.arange(seq // ROPE_SPLIT, dtype=F32) * ROPE_SPLIT)[:, None] * inv[None, :]
    lo = jnp.arange(ROPE_SPLIT, dtype=F32)[:, None] * inv[None, :]
    ch, sh = jnp.cos(hi)[:, None, :], jnp.sin(hi)[:, None, :]
    cl, sl = jnp.cos(lo)[None], jnp.sin(lo)[None]
    cos = (ch * cl - sh * sl).reshape(seq, RET_DK)
    sin = ((sh * cl + ch * sl) * sign).reshape(seq, RET_DK)
    return cos, sin


def _layer(x2d, p2d, batch, seq, mix_norm, w_in, hg_lb_logits, hg_norm, ret_norm, w_branch_a,
           w_branch_b, w_out, ffn_norm, w_rg, b_rg, w_re, b_re, w_gate, w_up, w_down, ple_norm,
           w_ple_gate, w_ple_proj, out_gain):
    t = x2d.shape[0]
    row = lambda v: v.reshape(1, -1).astype(F32)

    proj, hf, wg16, wu16, wd16 = _inproj(x2d, row(mix_norm), w_in.astype(BF16), w_gate, w_up, w_down)
    cos2, sin2 = _rotary_tables(seq)
    ya, yb = _mixers(proj, hf, hg_lb_logits.astype(F32), row(hg_norm), cos2, sin2, row(ret_norm), batch, seq)

    n_r = N_GROUPS + N_EXPERTS
    w_router = jnp.pad(jnp.concatenate([w_rg, w_re], axis=1).astype(BF16), ((0, 0), (0, LANES - n_r)))
    b_router = jnp.zeros((1, LANES), F32).at[0, :n_r].set(jnp.concatenate([b_rg, b_re]))
    x1, xs, route, counts = _merge(
        x2d, ya, yb, proj, w_branch_a.astype(BF16), w_branch_b.astype(BF16), w_out.astype(BF16),
        row(ffn_norm), w_router, b_router)

    n_blocks = t // TOK_TILE
    cnt = counts[:, 0, N_GROUPS:N_GROUPS + N_EXPERTS].astype(I32)
    run_chunks = (cnt + ROW_CHUNK - 1) // ROW_CHUNK
    earlier_e = jnp.tril(jnp.ones((N_EXPERTS, N_EXPERTS), I32), -1)
    earlier_b = jnp.tril(jnp.ones((n_blocks, n_blocks), I32), -1)
    run_local = jnp.sum(run_chunks[:, None, :] * earlier_e[None], axis=2)
    seg_chunks = jnp.sum(run_chunks, axis=0)
    tiles_per = (seg_chunks + CHUNKS_PER_TILE - 1) // CHUNKS_PER_TILE
    seg_start = jnp.sum(tiles_per[None, :] * earlier_e, axis=1) * CHUNKS_PER_TILE
    tile_end = seg_start // CHUNKS_PER_TILE + tiles_per
    run_global = seg_start[None, :] + jnp.sum(run_chunks.T[:, None, :] * earlier_b[None], axis=2).T
    max_chunks = (2 * t) // ROW_CHUNK + n_blocks * N_EXPERTS + N_EXPERTS * (CHUNKS_PER_TILE - 1)
    n_tiles = -(-max_chunks // CHUNKS_PER_TILE) + EXP_AHEAD
    tile_ids = jnp.arange(n_tiles, dtype=I32)
    tile_expert = jnp.minimum(jnp.sum((tile_end[None, :] <= tile_ids[:, None]).astype(I32), axis=1),
                              N_EXPERTS - 1)
    n_used = tile_end[-1:].astype(I32)

    block_ids = jnp.arange(n_blocks, dtype=I32)
    zero_local = LOCAL_CHUNKS - 1
    zero_global = n_tiles * CHUNKS_PER_TILE - 1
    g = jnp.arange(n_tiles * CHUNKS_PER_TILE, dtype=I32)[:, None]
    pick_e = (tile_expert[:, None] == jnp.arange(N_EXPERTS, dtype=I32)[None, :]).astype(I32)
    rg, rc, rl = (jnp.repeat(jnp.sum(pick_e[:, :, None] * tab.T[None, :, :], axis=1),
                             CHUNKS_PER_TILE, axis=0)
                  for tab in (run_global, run_chunks, run_local))
    inside = (rg <= g) & (g < rg + rc)
    gather_src = jnp.sum(jnp.where(inside, block_ids[None, :] * LOCAL_CHUNKS + rl + (g - rg), 0), axis=1)
    gather_src = jnp.where(jnp.any(inside, axis=1), gather_src, zero_local).astype(I32)

    lc = jnp.arange(LOCAL_CHUNKS, dtype=I32)[None, :, None]
    inside = (run_local[:, None, :] <= lc) & (lc < (run_local + run_chunks)[:, None, :])
    back_src = jnp.sum(jnp.where(inside, run_global[:, None, :] + lc - run_local[:, None, :], 0), axis=2)
    back_src = jnp.where(jnp.any(inside, axis=2), back_src, zero_global).astype(I32).reshape(-1)

    ys = _experts(tile_expert, n_used, gather_src, xs, wg16, wu16, wd16, n_tiles)
    return _final(back_src, x1, route, p2d, ys, row(ple_norm), w_ple_gate.astype(BF16),
                  w_ple_proj.astype(BF16), out_gain)


def kernel(x, p, mix_norm, w_in, hg_lb_logits, hg_norm, ret_norm, w_branch_a, w_branch_b, w_out,
           ffn_norm, w_router_group, b_router_group, w_router_expert, b_router_expert,
           w_expert_gate, w_expert_up, w_expert_down, ple_norm, w_ple_gate, w_ple_proj, final_norm):
    batch, seq, d = x.shape
    depth = p.shape[0]
    assert depth == 1, "the final rmsnorm is fused into the single layer"
    x2d = x.reshape(batch * seq, d)
    out = _layer(x2d, p[0].reshape(batch * seq, -1), batch, seq, mix_norm[0], w_in[0], hg_lb_logits,
                 hg_norm[0], ret_norm[0], w_branch_a[0], w_branch_b[0], w_out[0], ffn_norm[0],
                 w_router_group[0], b_router_group[0], w_router_expert[0], b_router_expert[0],
                 w_expert_gate[0], w_expert_up[0], w_expert_down[0], ple_norm[0], w_ple_gate[0],
                 w_ple_proj[0], final_norm.reshape(1, -1).astype(F32))
    return out.reshape(batch, seq, d)
```

```python
import jax
import jax.numpy as jnp
from jax import lax
from jax.experimental import pallas as pl
from jax.experimental.pallas import tpu as pltpu

F32 = jnp.float32
BF16 = jnp.bfloat16
I32 = jnp.int32

EPS = 1e-6
D_MODEL = 1024
PLE_DIM = 256
HG_HEADS = 4
HG_DK = 128
HG_WIDTH = HG_HEADS * HG_DK
RET_HEADS = 4
RET_DK = 128
RET_DV = 256
ROPE_BASE = 10000.0
ROPE_SPLIT = 64
IN_TOTAL = 7168
N_GROUPS = 4
EXPERTS_PER_GROUP = 8
N_EXPERTS = 32
D_EXPERT = 256

COL_HQ, COL_HF, COL_HI, COL_HG = 0, 4, 8, 12
COL_RQ, COL_RK = 16, 20
COL_RV, COL_RG = 24, 32
COL_GA, COL_GB = 5, 6

LANES = 128
VMEM_LIMIT = 56 * 1024 * 1024

HG_CHUNK = 64
HG_SUB = 8
HG_UNROLL = 16
HG_STATE_UNROLL = 16
HG_NORM_ROWS = 1024
RET_CHUNK = 128
RET_STATE_UNROLL = 8
SEQ_TILE = 1024
TOK_TILE = 512
EXP_TILE = 512
EXP_SUB = 512
EXP_AHEAD = 3
EXP_SLOTS = EXP_AHEAD + 1
ROW_CHUNK = 16
CHUNKS_PER_TILE = EXP_TILE // ROW_CHUNK
LOCAL_ROWS = 2 * TOK_TILE + N_EXPERTS * ROW_CHUNK
LOCAL_CHUNKS = LOCAL_ROWS // ROW_CHUNK
SORT_SLAB = 256
MERGE_SLAB = LOCAL_ROWS
FIN_AHEAD = 2
FIN_SLOTS = FIN_AHEAD + 1
FIN_PARTS = 2


def _cparams(sem):
    return pltpu.CompilerParams(dimension_semantics=sem, vmem_limit_bytes=VMEM_LIMIT)


def _rms(x, g):
    return x * lax.rsqrt(jnp.mean(x * x, axis=-1, keepdims=True) + EPS) * g


def _sigmoid(x):
    return 1.0 / (1.0 + jnp.exp(-x))


def _silu(x):
    return x * _sigmoid(x)


def _split3(x):
    hi = x.astype(BF16)
    r1 = x - hi.astype(F32)
    mid = r1.astype(BF16)
    lo = (r1 - mid.astype(F32)).astype(BF16)
    return hi, mid, lo


def _dot(a, b):
    return jnp.dot(a, b, preferred_element_type=F32)


def _dot_nt(a, b):
    return lax.dot_general(a, b, (((1,), (1,)), ((), ())), preferred_element_type=F32)


def _dot_tn(a, b):
    return lax.dot_general(a, b, (((0,), (0,)), ((), ())), preferred_element_type=F32)


def _inproj_kernel(x_ref, g_ref, w_ref, wg_ref, wu_ref, wd_ref, proj_ref, hf_ref,
                   wg16_ref, wu16_ref, wd16_ref):
    wg16_ref[...] = wg_ref[...].astype(BF16)
    wu16_ref[...] = wu_ref[...].astype(BF16)
    wd16_ref[...] = wd_ref[...].astype(BF16)

    h = _rms(x_ref[...], g_ref[...]).astype(BF16)
    tn = HG_WIDTH
    for j in range(IN_TOTAL // tn):
        acc = _dot(h, w_ref[:, j * tn:(j + 1) * tn])
        proj_ref[:, j * tn:(j + 1) * tn] = acc.astype(BF16)
        if j * tn == COL_HF * LANES:
            hf_ref[...] = acc


def _inproj(x2d, gain, w_bf16, w_gate, w_up, w_down):
    t = x2d.shape[0]
    steps = t // TOK_TILE
    assert N_EXPERTS % steps == 0, "expert weights are converted in equal shares per grid step"
    share = N_EXPERTS // steps
    up_spec = pl.BlockSpec((share, D_MODEL, D_EXPERT), lambda i: (i, 0, 0))
    down_spec = pl.BlockSpec((share, D_EXPERT, D_MODEL), lambda i: (i, 0, 0))
    return pl.pallas_call(
        _inproj_kernel,
        grid=(steps,),
        in_specs=[
            pl.BlockSpec((TOK_TILE, D_MODEL), lambda i: (i, 0)),
            pl.BlockSpec((1, D_MODEL), lambda i: (0, 0)),
            pl.BlockSpec((D_MODEL, IN_TOTAL), lambda i: (0, 0), pipeline_mode=pl.Buffered(1)),
            up_spec, up_spec, down_spec,
        ],
        out_specs=[
            pl.BlockSpec((TOK_TILE, IN_TOTAL), lambda i: (i, 0)),
            pl.BlockSpec((TOK_TILE, HG_WIDTH), lambda i: (i, 0)),
            up_spec, up_spec, down_spec,
        ],
        out_shape=[
            jax.ShapeDtypeStruct((t, IN_TOTAL), BF16),
            jax.ShapeDtypeStruct((t, HG_WIDTH), F32),
            jax.ShapeDtypeStruct(w_gate.shape, BF16),
            jax.ShapeDtypeStruct(w_up.shape, BF16),
            jax.ShapeDtypeStruct(w_down.shape, BF16),
        ],
        compiler_params=_cparams(("arbitrary",)),
        name="inproj",
    )(x2d, gain, w_bf16, w_gate, w_up, w_down)


def _hgrn_stages(lbl_ref, q_ref, f_ref, i_ref, g_ref, ng_ref, o_ref,
                 st_ref, b_s, k_s, v_s, oi_s, qe_s, kv_s, dec_s):
    c = HG_CHUNK
    nsub = c // HG_SUB

    logits = lbl_ref[...]
    e = jnp.exp(logits - jnp.max(logits, axis=0, keepdims=True))
    lb = e[0:1] / jnp.sum(e, axis=0, keepdims=True)
    one_m_lb = jnp.sum(e[1:], axis=0, keepdims=True) / jnp.sum(e, axis=0, keepdims=True)
    ng = ng_ref[...]

    row = lax.broadcasted_iota(I32, (c, c), 0)
    col = lax.broadcasted_iota(I32, (c, c), 1)
    tri = jnp.where(row >= col, 1.0, 0.0).astype(BF16)
    sub_row = lax.broadcasted_iota(I32, (HG_SUB, HG_DK), 0)
    masked = jnp.float32(-1e30)

    def bcast_row(ref, r, rows):
        return jnp.broadcast_to(ref[pl.ds(r, 1), :], (rows, HG_DK))

    def prep(ci, slot):
        r0 = pl.multiple_of(ci * c, c)
        z = f_ref[pl.ds(r0, c), :]
        ez = jnp.exp(-jnp.abs(z))
        rz = 1.0 / (1.0 + ez)
        pos = z >= 0.0
        logf = jnp.log2(lb + one_m_lb * jnp.where(pos, rz, ez * rz))
        kk = one_m_lb * jnp.where(pos, ez * rz, rz)
        q = _silu(q_ref[pl.ds(r0, c), :].astype(F32))
        v = i_ref[pl.ds(r0, c), :].astype(F32)
        k_s[slot] = kk
        v_s[slot] = v
        return dict(ci=ci, r0=r0, slot=slot, kk=kk, q=q, v=v, v16=v.astype(BF16), parts=_split3(logf))

    def cumulate(s):
        hi, mid, lo = s.pop("parts")
        b = (_dot(tri, lo) + _dot(tri, mid)) + _dot(tri, hi)
        b_s[s["slot"]] = b
        qe_s[pl.ds(s["r0"], c), :] = (s["q"] * jnp.exp2(b)).astype(BF16)
        s["b"] = b
        return s

    def off_diagonal(s):
        b, q, kk = s["b"], s["q"], s["kk"]
        bs_ref = b_s.at[s["slot"]]
        squares = []
        size = c // 2
        while size >= HG_SUB:
            for r0 in range(size, c, 2 * size):
                edge = bcast_row(bs_ref, r0 - 1, size)
                qs = q[r0:r0 + size, :] * jnp.exp2(b[r0:r0 + size, :] - edge)
                ks = kk[r0 - size:r0, :] * jnp.exp2(edge - b[r0 - size:r0, :])
                squares.append((r0, size, _dot_nt(qs.astype(BF16), ks.astype(BF16))))
            size //= 2
        s["squares"] = squares
        return s

    def apply_values(s):
        b, kk, v = s["b"], s["kk"], s["v"]
        blast = b[c - 1:c, :]
        rows = [jnp.zeros((HG_SUB, HG_DK), F32) for _ in range(nsub)]
        for r0, size, a in s.pop("squares"):
            part = _dot(a.astype(BF16), v[r0 - size:r0, :].astype(BF16))
            for i in range(size // HG_SUB):
                rows[r0 // HG_SUB + i] = rows[r0 // HG_SUB + i] + part[i * HG_SUB:(i + 1) * HG_SUB, :]
        s["o"] = jnp.concatenate(rows, axis=0)
        kd = kk * jnp.exp2(blast - b)
        kv_s[s["ci"]] = _dot_tn(s["v16"], kd.astype(BF16))
        dec_s[s["ci"]] = jnp.broadcast_to(jnp.exp2(blast), (HG_SUB, HG_DK))
        return s

    def diagonal(s):
        b, q = s["b"], s["q"]
        bs_ref, ks_ref, vs_ref = b_s.at[s["slot"]], k_s.at[s["slot"]], v_s.at[s["slot"]]
        d_blocks = []
        for i in range(nsub):
            sl = slice(i * HG_SUB, (i + 1) * HG_SUB)
            bt, qt = b[sl, :], q[sl, :]
            acc = jnp.zeros((HG_SUB, HG_DK), F32)
            for j in range(HG_SUB):
                r = i * HG_SUB + j
                arg = bt - bcast_row(bs_ref, r, HG_SUB)
                if j > 0:
                    arg = jnp.where(sub_row >= j, arg, masked)
                g = jnp.exp2(arg) * (qt * bcast_row(ks_ref, r, HG_SUB))
                acc = acc + jnp.sum(g, axis=-1, keepdims=True) * bcast_row(vs_ref, r, HG_SUB)
            d_blocks.append(acc)
        oi_s[pl.ds(s["r0"], c), :] = s["o"] + jnp.concatenate(d_blocks, axis=0)

    def carried_group(gi):
        st = st_ref[...]
        outs = []
        for j in range(HG_STATE_UNROLL):
            ci = gi * HG_STATE_UNROLL + j
            r0 = pl.multiple_of(ci * c, c)
            outs.append((r0, _dot_nt(qe_s[pl.ds(r0, c), :], st.astype(BF16))))
            st = st * dec_s[ci][0:1, :] + kv_s[ci]
        st_ref[...] = st
        for r0, os in outs:
            oi_s[pl.ds(r0, c), :] += os

    def finish(ri):
        r0 = pl.multiple_of(ri * HG_NORM_ROWS, HG_NORM_ROWS)
        y = _rms(oi_s[pl.ds(r0, HG_NORM_ROWS), :], ng) * _silu(g_ref[pl.ds(r0, HG_NORM_ROWS), :].astype(F32))
        o_ref[pl.ds(r0, HG_NORM_ROWS), :] = y.astype(o_ref.dtype)

    return prep, (cumulate, off_diagonal, apply_values, diagonal), carried_group, finish


HG_SCRATCH = [
    pltpu.VMEM((HG_DK, HG_DK), F32),
    pltpu.VMEM((HG_UNROLL, HG_CHUNK, HG_DK), F32),
    pltpu.VMEM((HG_UNROLL, HG_CHUNK, HG_DK), F32),
    pltpu.VMEM((HG_UNROLL, HG_CHUNK, HG_DK), F32),
    pltpu.VMEM((SEQ_TILE, HG_DK), F32),
    pltpu.VMEM((SEQ_TILE, HG_DK), BF16),
    pltpu.VMEM((SEQ_TILE // HG_CHUNK, HG_DK, HG_DK), F32),
    pltpu.VMEM((SEQ_TILE // HG_CHUNK, HG_SUB, HG_DK), F32),
]


def _ret_stages(q_ref, k_ref, v_ref, g_ref, cos_ref, sin_ref, ng_ref, o_ref,
                r_ref, oi_s, q16_s, kv_s):
    c = RET_CHUNK

    hf = jnp.full((1, 1), pl.program_id(1), I32).astype(F32)
    lg = jnp.log1p(-jnp.exp2(-5.0 - hf))
    ti = lax.broadcasted_iota(I32, (c, c), 0)
    si = lax.broadcasted_iota(I32, (c, c), 1)
    rel = (ti - si).astype(F32)
    intra = jnp.where(ti >= si, jnp.exp(jnp.maximum(rel, 0.0) * lg), 0.0)
    idx = lax.broadcasted_iota(I32, (c, 1), 0).astype(F32)
    inter = jnp.exp((idx + 1.0) * lg)
    to_state = jnp.exp((c - 1.0 - idx) * lg)
    chunk_decay = jnp.exp(float(c) * lg)
    ng = ng_ref[...]
    half = RET_DK // 2

    def prep(ci):
        r0 = pl.multiple_of(ci * c, c)
        cos = cos_ref[pl.ds(r0, c), :]
        sin = sin_ref[pl.ds(r0, c), :]
        q = q_ref[pl.ds(r0, c), :].astype(F32)
        k = k_ref[pl.ds(r0, c), :].astype(F32)
        qr = (q * cos + pltpu.roll(q, half, 1) * sin) * (RET_DK ** -0.5)
        kr = k * cos + pltpu.roll(k, half, 1) * sin
        q16 = qr.astype(BF16)
        q16_s[pl.ds(r0, c), :] = q16
        return dict(ci=ci, r0=r0, q16=q16, k16=kr.astype(BF16), kts=(kr * to_state).astype(BF16),
                    v16=v_ref[pl.ds(r0, c), :])

    def scores(s):
        s["att"] = (_dot_nt(s.pop("q16"), s.pop("k16")) * intra).astype(BF16)
        return s

    def apply_values(s):
        oi_s[pl.ds(s["r0"], c), :] = _dot(s["att"], s["v16"])
        kv_s[s["ci"]] = _dot_tn(s["kts"], s["v16"])
        return s

    def carried_group(gi):
        r = r_ref[...]
        outs = []
        for j in range(RET_STATE_UNROLL):
            ci = gi * RET_STATE_UNROLL + j
            r0 = pl.multiple_of(ci * c, c)
            outs.append((r0, _dot(q16_s[pl.ds(r0, c), :], r.astype(BF16))))
            r = chunk_decay * r + kv_s[ci]
        r_ref[...] = r
        for r0, qr_state in outs:
            o = oi_s[pl.ds(r0, c), :] + qr_state * inter
            y = _rms(o, ng) * _silu(g_ref[pl.ds(r0, c), :].astype(F32))
            o_ref[pl.ds(r0, c), :] = y.astype(o_ref.dtype)

    return prep, (scores, apply_values), carried_group


RET_SCRATCH = [
    pltpu.VMEM((RET_DK, RET_DV), F32),
    pltpu.VMEM((SEQ_TILE, RET_DV), F32),
    pltpu.VMEM((SEQ_TILE, RET_DK), BF16),
    pltpu.VMEM((SEQ_TILE // RET_CHUNK, RET_DK, RET_DV), F32),
]


def _mixers_kernel(*refs):
    hg_refs = refs[:6] + refs[13:14] + refs[15:15 + len(HG_SCRATCH)]
    ret_refs = refs[6:13] + refs[14:15] + refs[15 + len(HG_SCRATCH):]
    st_ref, r_ref = hg_refs[7], ret_refs[8]
    n_tokens = refs[1].shape[0]

    @pl.when(pl.program_id(2) == 0)
    def _():
        st_ref[...] = jnp.zeros_like(st_ref)
        r_ref[...] = jnp.zeros_like(r_ref)

    hg_prep, hg_local, hg_carried, hg_finish = _hgrn_stages(*hg_refs)
    ret_prep, ret_local, ret_carried = _ret_stages(*ret_refs)
    group_tokens = HG_UNROLL * HG_CHUNK
    ret_unroll = group_tokens // RET_CHUNK

    def local_group(gi, carry):
        hs = [hg_prep(gi * HG_UNROLL + slot, slot) for slot in range(HG_UNROLL)]
        rs = [ret_prep(gi * ret_unroll + j) for j in range(ret_unroll)]
        cumulate, off_diagonal, apply_values, diagonal = hg_local
        scores, ret_apply = ret_local
        hs = [cumulate(s) for s in hs]
        rs = [scores(s) for s in rs]
        hs = [off_diagonal(s) for s in hs]
        rs = [ret_apply(s) for s in rs]
        hs = [apply_values(s) for s in hs]
        for s in hs:
            diagonal(s)
        return carry

    lax.fori_loop(0, n_tokens // group_tokens, local_group, 0)

    state_tokens = HG_STATE_UNROLL * HG_CHUNK
    assert state_tokens == RET_STATE_UNROLL * RET_CHUNK

    def carried_group(gi, carry):
        hg_carried(gi)
        ret_carried(gi)
        return carry

    lax.fori_loop(0, n_tokens // state_tokens, carried_group, 0)

    def finish(ri, carry):
        hg_finish(ri)
        return carry

    lax.fori_loop(0, n_tokens // HG_NORM_ROWS, finish, 0)


def _mixers(proj, hf, lb_logits, hg_norm, cos2, sin2, ret_norm, batch, seq):
    ns = seq // SEQ_TILE
    tok = lambda b, h, s: b * ns + s
    hg_col = lambda base: pl.BlockSpec((SEQ_TILE, HG_DK), lambda b, h, s: (tok(b, h, s), base + h))
    return pl.pallas_call(
        _mixers_kernel,
        grid=(batch, HG_HEADS, ns),
        in_specs=[
            pl.BlockSpec((2, HG_DK), lambda b, h, s: (0, h)),
            hg_col(COL_HQ), hg_col(0), hg_col(COL_HI), hg_col(COL_HG),
            pl.BlockSpec((1, HG_DK), lambda b, h, s: (0, 0)),
            hg_col(COL_RQ), hg_col(COL_RK),
            pl.BlockSpec((SEQ_TILE, RET_DV), lambda b, h, s: (tok(b, h, s), COL_RV // 2 + h)),
            pl.BlockSpec((SEQ_TILE, RET_DV), lambda b, h, s: (tok(b, h, s), COL_RG // 2 + h)),
            pl.BlockSpec((SEQ_TILE, RET_DK), lambda b, h, s: (s, 0)),
            pl.BlockSpec((SEQ_TILE, RET_DK), lambda b, h, s: (s, 0)),
            pl.BlockSpec((1, RET_DV), lambda b, h, s: (0, 0)),
        ],
        out_specs=[
            pl.BlockSpec((SEQ_TILE, HG_DK), lambda b, h, s: (tok(b, h, s), h)),
            pl.BlockSpec((SEQ_TILE, RET_DV), lambda b, h, s: (tok(b, h, s), h)),
        ],
        out_shape=[
            jax.ShapeDtypeStruct((batch * seq, HG_WIDTH), BF16),
            jax.ShapeDtypeStruct((batch * seq, RET_HEADS * RET_DV), BF16),
        ],
        scratch_shapes=HG_SCRATCH + RET_SCRATCH,
        compiler_params=_cparams(("arbitrary", "arbitrary", "arbitrary")),
        name="mixers",
    )(lb_logits, proj, hf, proj, proj, hg_norm, proj, proj, proj, proj, cos2, sin2, ret_norm)


def _merge_kernel(x_ref, ya_ref, yb_ref, ga_ref, gb_ref, wa_ref, wb_ref, wo_ref, fg_ref,
                  wr_ref, br_ref, x1_ref, xs_ref, route_ref, cnt_ref, h2b_s, rows_s):
    tm = x_ref.shape[0]

    @pl.when(pl.program_id(0) == 0)
    def _():
        h2b_s[...] = jnp.zeros_like(h2b_s)
        rows_s[...] = jnp.full(rows_s.shape, -1.0, F32)

    def sort_previous(lo, hi):
        slab_row = lax.broadcasted_iota(I32, (MERGE_SLAB, tm), 0).astype(F32)
        for r0 in range(lo, hi, MERGE_SLAB):
            sel = ((slab_row == rows_s[0:1, :] - float(r0)) | (slab_row == rows_s[1:2, :] - float(r0)))
            xs_ref[r0:r0 + MERGE_SLAB, :] = _dot(jnp.where(sel, 1.0, 0.0).astype(BF16),
                                                 h2b_s[...]).astype(BF16)

    merged = (_sigmoid(ga_ref[...].astype(F32)) * _dot(ya_ref[...], wa_ref[...])
              + _sigmoid(gb_ref[...].astype(F32)) * _dot(yb_ref[...], wb_ref[...]))
    x1 = x_ref[...] + _dot(merged.astype(BF16), wo_ref[...])
    x1_ref[...] = x1
    h2 = _rms(x1, fg_ref[...])

    h2b = h2.astype(BF16)
    logits = _dot(h2b, wr_ref[...]) + br_ref[...]
    sort_previous(0, LOCAL_ROWS)

    lane = lax.broadcasted_iota(I32, (tm, LANES), 1)
    neg = jnp.float32(-jnp.inf)
    big = jnp.int32(1 << 30)
    is_g = lane < N_GROUPS
    gl = jnp.where(is_g, logits, neg)
    gmax = jnp.max(gl, axis=-1, keepdims=True)
    g_idx = jnp.min(jnp.where(gl == gmax, lane, big), axis=-1, keepdims=True)
    g_w = 1.0 / jnp.sum(jnp.where(is_g, jnp.exp(gl - gmax), 0.0), axis=-1, keepdims=True)

    ex = lane - N_GROUPS
    in_grp = (ex >= g_idx * EXPERTS_PER_GROUP) & (ex < (g_idx + 1) * EXPERTS_PER_GROUP)
    el = jnp.where(in_grp, logits, neg)
    m1 = jnp.max(el, axis=-1, keepdims=True)
    e1 = jnp.min(jnp.where(el == m1, ex, big), axis=-1, keepdims=True)
    el2 = jnp.where(ex == e1, neg, el)
    m2 = jnp.max(el2, axis=-1, keepdims=True)
    e2 = jnp.min(jnp.where(el2 == m2, ex, big), axis=-1, keepdims=True)
    p2 = jnp.exp(m2 - m1)
    w1 = g_w / (1.0 + p2)
    w2 = g_w * p2 / (1.0 + p2)

    oh1 = ex == e1
    oh2 = ex == e2
    oh = jnp.where(oh1 | oh2, 1.0, 0.0)
    ri = lax.broadcasted_iota(I32, (tm, tm), 0)
    ci = lax.broadcasted_iota(I32, (tm, tm), 1)
    strict = jnp.where(ri > ci, 1.0, 0.0).astype(BF16)
    local_rank = _dot(strict, oh.astype(BF16))
    cnt = jnp.sum(oh, axis=0, keepdims=True)
    run_chunks = jnp.floor((cnt + (ROW_CHUNK - 1.0)) * (1.0 / ROW_CHUNK))
    ui = lax.broadcasted_iota(I32, (LANES, LANES), 0)
    uj = lax.broadcasted_iota(I32, (LANES, LANES), 1)
    before = jnp.where(ui < uj, 1.0, 0.0).astype(BF16)
    run_start = _dot(jnp.broadcast_to(run_chunks, (8, LANES)).astype(BF16), before)[0:1] * ROW_CHUNK
    slot = run_start + local_rank
    pos1 = jnp.sum(jnp.where(oh1, slot, 0.0), axis=-1, keepdims=True)
    pos2 = jnp.sum(jnp.where(oh2, slot, 0.0), axis=-1, keepdims=True)
    cnt_ref[0] = jnp.broadcast_to(cnt, (8, LANES))

    route = jnp.where(lane == 0, e1.astype(F32), 0.0)
    route = jnp.where(lane == 1, e2.astype(F32), route)
    route = jnp.where(lane == 2, w1, route)
    route = jnp.where(lane == 3, w2, route)
    route = jnp.where(lane == 4, pos1, route)
    route = jnp.where(lane == 5, pos2, route)
    route_ref[...] = route

    hi1 = jnp.floor(pos1 * (1.0 / 256.0))
    hi2 = jnp.floor(pos2 * (1.0 / 256.0))
    digits = jnp.where(lane == 0, hi1, 0.0)
    digits = jnp.where(lane == 1, pos1 - 256.0 * hi1, digits)
    digits = jnp.where(lane == 2, hi2, digits)
    digits = jnp.where(lane == 3, pos2 - 256.0 * hi2, digits)
    pick = jnp.where(lax.broadcasted_iota(I32, (8, LANES), 0) == lax.broadcasted_iota(I32, (8, LANES), 1),
                     1.0, 0.0).astype(BF16)
    rows = _dot_nt(pick, digits.astype(BF16))

    h2b_s[...] = h2b
    rows_s[0:1, :] = rows[0:1] * 256.0 + rows[1:2]
    rows_s[1:2, :] = rows[2:3] * 256.0 + rows[3:4]


def _merge(x2d, ya, yb, proj, wa, wb, wo, ffn_g, w_router, b_router):
    t = x2d.shape[0]
    nb = t // TOK_TILE
    const = lambda *shape: pl.BlockSpec(shape, lambda i: (0,) * len(shape))
    cur = lambda i: jnp.minimum(i, nb - 1)
    prev = lambda i: jnp.maximum(i - 1, 0)
    return pl.pallas_call(
        _merge_kernel,
        grid=(nb + 1,),
        in_specs=[
            pl.BlockSpec((TOK_TILE, D_MODEL), lambda i: (cur(i), 0)),
            pl.BlockSpec((TOK_TILE, HG_WIDTH), lambda i: (cur(i), 0)),
            pl.BlockSpec((TOK_TILE, D_MODEL), lambda i: (cur(i), 0)),
            pl.BlockSpec((TOK_TILE, D_MODEL), lambda i: (cur(i), COL_GA)),
            pl.BlockSpec((TOK_TILE, D_MODEL), lambda i: (cur(i), COL_GB)),
            const(HG_WIDTH, D_MODEL),
            const(D_MODEL, D_MODEL),
            const(D_MODEL, D_MODEL),
            const(1, D_MODEL),
            const(D_MODEL, LANES),
            const(1, LANES),
        ],
        out_specs=[
            pl.BlockSpec((TOK_TILE, D_MODEL), lambda i: (cur(i), 0)),
            pl.BlockSpec((LOCAL_ROWS, D_MODEL), lambda i: (prev(i), 0)),
            pl.BlockSpec((TOK_TILE, LANES), lambda i: (cur(i), 0)),
            pl.BlockSpec((1, 8, LANES), lambda i: (cur(i), 0, 0)),
        ],
        out_shape=[
            jax.ShapeDtypeStruct((t, D_MODEL), F32),
            jax.ShapeDtypeStruct((t // TOK_TILE * LOCAL_ROWS, D_MODEL), BF16),
            jax.ShapeDtypeStruct((t, LANES), F32),
            jax.ShapeDtypeStruct((t // TOK_TILE, 8, LANES), F32),
        ],
        scratch_shapes=[pltpu.VMEM((TOK_TILE, D_MODEL), BF16), pltpu.VMEM((8, TOK_TILE), F32)],
        compiler_params=_cparams(("arbitrary",)),
        name="merge_route",
    )(x2d, ya, yb, proj, proj, wa, wb, wo, ffn_g, w_router, b_router)


def _expert_kernel(te_ref, nt_ref, src_ref, wslot_ref, wnext_ref, xs_ref, wg_ref, wu_ref, wd_ref,
                   ys_ref, xbuf, wg_s, wu_s, wd_s, sem, wsem):
    i = pl.program_id(0)
    nt = nt_ref[0]

    def fetch_weights(e, slot):
        return [pltpu.make_async_copy(w_ref.at[e], w_s.at[slot], wsem.at[slot])
                for w_ref, w_s in ((wg_ref, wg_s), (wu_ref, wu_s), (wd_ref, wd_s))]

    first_tile = (i == 0) | (te_ref[i] != te_ref[jnp.maximum(i - 1, 0)])

    @pl.when(i == 0)
    def _():
        for cp in fetch_weights(te_ref[0], wslot_ref[0]):
            cp.start()

    @pl.when((i < nt) & first_tile)
    def _():
        for cp in fetch_weights(te_ref[i], wslot_ref[i]):
            cp.wait()

        @pl.when(wnext_ref[i] >= 0)
        def _():
            for cp in fetch_weights(wnext_ref[i], 1 - wslot_ref[i]):
                cp.start()

    def gather(tile, slot):
        copies = []
        for c in range(CHUNKS_PER_TILE):
            row = pl.multiple_of(src_ref[tile * CHUNKS_PER_TILE + c] * ROW_CHUNK, ROW_CHUNK)
            copies.append(pltpu.make_async_copy(
                xs_ref.at[pl.ds(row, ROW_CHUNK)],
                xbuf.at[slot, pl.ds(c * ROW_CHUNK, ROW_CHUNK)], sem.at[slot]))
        return copies

    @pl.when(i == 0)
    def _():
        for k in range(EXP_AHEAD):
            for cp in gather(k, k):
                cp.start()

    @pl.when(i < nt + EXP_AHEAD)
    def _():
        for cp in gather(i, i % EXP_SLOTS):
            cp.wait()

    @pl.when(i < nt)
    def _():
        slot = i % EXP_SLOTS
        ws = wslot_ref[i]
        wg, wu, wd = wg_s[ws], wu_s[ws], wd_s[ws]
        subs = [pl.ds(r, EXP_SUB) for r in range(0, EXP_TILE, EXP_SUB)]
        xs = [xbuf[slot, s, :] for s in subs]
        gates = [(_dot(x, wg), _dot(x, wu)) for x in xs]
        for c, cp in enumerate(gather(i + EXP_AHEAD, (i + EXP_AHEAD) % EXP_SLOTS)):
            cp.start(priority=c % 2)
        hidden = [(_silu(a) * u).astype(BF16) for a, u in gates]
        for s, h in zip(subs, hidden):
            ys_ref[s, :] = _dot(h, wd).astype(ys_ref.dtype)

    @pl.when(i >= nt)
    def _():
        ys_ref[...] = jnp.zeros_like(ys_ref)


def _experts(tile_expert, n_tiles_used, src_chunk, xs, wg, wu, wd, n_tiles):
    ids = jnp.arange(n_tiles, dtype=I32)
    first = (ids == 0) | (tile_expert != jnp.concatenate([tile_expert[:1], tile_expert[:-1]]))
    wslot = (jnp.sum((first[None, :] & (ids[None, :] <= ids[:, None])).astype(I32), axis=1) - 1) % 2
    later_first = first[None, :] & (ids[None, :] > ids[:, None]) & (ids[None, :] < n_tiles_used[0])
    nxt = jnp.min(jnp.where(later_first, ids[None, :], n_tiles), axis=1)
    wnext = jnp.sum(jnp.where(ids[None, :] == nxt[:, None], tile_expert[None, :], 0), axis=1)
    wnext = jnp.where(nxt < n_tiles, wnext, -1).astype(I32)

    hbm = pl.BlockSpec(memory_space=pl.ANY)
    grid_spec = pltpu.PrefetchScalarGridSpec(
        num_scalar_prefetch=5,
        grid=(n_tiles,),
        in_specs=[hbm, hbm, hbm, hbm],
        out_specs=pl.BlockSpec((EXP_TILE, D_MODEL), lambda i, *_: (i, 0)),
        scratch_shapes=[
            pltpu.VMEM((EXP_SLOTS, EXP_TILE, D_MODEL), BF16),
            pltpu.VMEM((2, D_MODEL, D_EXPERT), BF16),
            pltpu.VMEM((2, D_MODEL, D_EXPERT), BF16),
            pltpu.VMEM((2, D_EXPERT, D_MODEL), BF16),
            pltpu.SemaphoreType.DMA((EXP_SLOTS,)),
            pltpu.SemaphoreType.DMA((2,)),
        ],
    )
    return pl.pallas_call(
        _expert_kernel,
        grid_spec=grid_spec,
        out_shape=jax.ShapeDtypeStruct((n_tiles * EXP_TILE, D_MODEL), BF16),
        compiler_params=_cparams(("arbitrary",)),
        name="experts",
    )(tile_expert, n_tiles_used, src_chunk, wslot.astype(I32), wnext, xs, wg, wu, wd)


def _final_kernel(src_ref, x1_ref, route_ref, p_ref, ys_ref, pg_ref, wpg_ref, wpp_ref, fg_ref,
                  o_ref, ybuf, sem):
    tm = x1_ref.shape[0]
    i = pl.program_id(0)
    nb = pl.num_programs(0)

    def gather(block, slot):
        copies = []
        for c in range(LOCAL_CHUNKS):
            row = pl.multiple_of(src_ref[block * LOCAL_CHUNKS + c] * ROW_CHUNK, ROW_CHUNK)
            copies.append(pltpu.make_async_copy(
                ys_ref.at[pl.ds(row, ROW_CHUNK)],
                ybuf.at[slot, pl.ds(c * ROW_CHUNK, ROW_CHUNK)], sem.at[slot]))
        return copies

    @pl.when(i == 0)
    def _():
        for k in range(FIN_AHEAD):
            for cp in gather(k % nb, k):
                cp.start()

    slot = i % FIN_SLOTS
    for cp in gather(i, slot):
        cp.wait()

    parts = [pl.ds(r, tm // FIN_PARTS) for r in range(0, tm, tm // FIN_PARTS)]
    route = route_ref[...]
    ple = [_dot(p_ref[rows, :].astype(BF16), wpp_ref[...]) for rows in parts]

    slab_col = lax.broadcasted_iota(I32, (tm // FIN_PARTS, SORT_SLAB), 1).astype(F32)
    moes = []
    for k in range(FIN_PARTS):
        rows = slice(k * (tm // FIN_PARTS), (k + 1) * (tm // FIN_PARTS))
        w1, w2 = route[rows, 2:3], route[rows, 3:4]
        pos1, pos2 = route[rows, 4:5], route[rows, 5:6]
        moe = jnp.zeros((tm // FIN_PARTS, D_MODEL), F32)
        for k0 in range(0, LOCAL_ROWS, SORT_SLAB):
            sel = jnp.where(slab_col == pos1 - float(k0), w1,
                            jnp.where(slab_col == pos2 - float(k0), w2, 0.0)).astype(BF16)
            moe = moe + _dot(sel, ybuf[slot, k0:k0 + SORT_SLAB, :])
        moes.append(moe)
    for c, cp in enumerate(gather((i + FIN_AHEAD) % nb, (i + FIN_AHEAD) % FIN_SLOTS)):
        cp.start(priority=c % 2)
    x2s = [x1_ref[rows, :] + moe for rows, moe in zip(parts, moes)]
    hps = [_rms(x2, pg_ref[...]).astype(BF16) for x2 in x2s]
    gates = [_sigmoid(_dot(hp, wpg_ref[...])) for hp in hps]
    for rows, x2, gate, pp in zip(parts, x2s, gates, ple):
        o_ref[rows, :] = _rms(x2 + gate * pp, fg_ref[...])

    @pl.when(i == nb - 1)
    def _():
        for k in range(1, FIN_AHEAD + 1):
            for cp in gather(0, (i + k) % FIN_SLOTS):
                cp.wait()


def _final(src_chunk, x1, route, p2d, ys, ple_g, wpg, wpp, final_g):
    t = x1.shape[0]
    const = lambda *shape: pl.BlockSpec(shape, lambda i, src: (0,) * len(shape))
    grid_spec = pltpu.PrefetchScalarGridSpec(
        num_scalar_prefetch=1,
        grid=(t // TOK_TILE,),
        in_specs=[
            pl.BlockSpec((TOK_TILE, D_MODEL), lambda i, src: (i, 0)),
            pl.BlockSpec((TOK_TILE, LANES), lambda i, src: (i, 0)),
            pl.BlockSpec((TOK_TILE, PLE_DIM), lambda i, src: (i, 0)),
            pl.BlockSpec(memory_space=pl.ANY),
            const(1, D_MODEL),
            const(D_MODEL, D_MODEL),
            const(PLE_DIM, D_MODEL),
            const(1, D_MODEL),
        ],
        out_specs=pl.BlockSpec((TOK_TILE, D_MODEL), lambda i, src: (i, 0)),
        scratch_shapes=[pltpu.VMEM((FIN_SLOTS, LOCAL_ROWS, D_MODEL), BF16),
                        pltpu.SemaphoreType.DMA((FIN_SLOTS,))],
    )
    return pl.pallas_call(
        _final_kernel,
        grid_spec=grid_spec,
        out_shape=jax.ShapeDtypeStruct((t, D_MODEL), F32),
        compiler_params=_cparams(("arbitrary",)),
        name="combine_ple_final",
    )(src_chunk, x1, route, p2d, ys, ple_g, wpg, wpp, final_g)


def _rotary_tables(seq):
    inv = ROPE_BASE ** (-jnp.arange(0, RET_DK, 2, dtype=F32) / RET_DK)
    inv = jnp.concatenate([inv, inv])
    sign = jnp.where(jnp.arange(RET_DK) < RET_DK // 2, -1.0, 1.0).astype(F32)
    hi = (jnp.arange(seq // ROPE_SPLIT, dtype=F32) * ROPE_SPLIT)[:, None] * inv[None, :]
    lo = jnp.arange(ROPE_SPLIT, dtype=F32)[:, None] * inv[None, :]
    ch, sh = jnp.cos(hi)[:, None, :], jnp.sin(hi)[:, None, :]
    cl, sl = jnp.cos(lo)[None], jnp.sin(lo)[None]
    cos = (ch * cl - sh * sl).reshape(seq, RET_DK)
    sin = ((sh * cl + ch * sl) * sign).reshape(seq, RET_DK)
    return cos, sin


def _layer(x2d, p2d, batch, seq, mix_norm, w_in, hg_lb_logits, hg_norm, ret_norm, w_branch_a,
           w_branch_b, w_out, ffn_norm, w_rg, b_rg, w_re, b_re, w_gate, w_up, w_down, ple_norm,
           w_ple_gate, w_ple_proj, out_gain):
    t = x2d.shape[0]
    row = lambda v: v.reshape(1, -1).astype(F32)

    proj, hf, wg16, wu16, wd16 = _inproj(x2d, row(mix_norm), w_in.astype(BF16), w_gate, w_up, w_down)
    cos2, sin2 = _rotary_tables(seq)
    ya, yb = _mixers(proj, hf, hg_lb_logits.astype(F32), row(hg_norm), cos2, sin2, row(ret_norm), batch, seq)

    n_r = N_GROUPS + N_EXPERTS
    w_router = jnp.pad(jnp.concatenate([w_rg, w_re], axis=1).astype(BF16), ((0, 0), (0, LANES - n_r)))
    b_router = jnp.zeros((1, LANES), F32).at[0, :n_r].set(jnp.concatenate([b_rg, b_re]))
    x1, xs, route, counts = _merge(
        x2d, ya, yb, proj, w_branch_a.astype(BF16), w_branch_b.astype(BF16), w_out.astype(BF16),
        row(ffn_norm), w_router, b_router)

    n_blocks = t // TOK_TILE
    cnt = counts[:, 0, N_GROUPS:N_GROUPS + N_EXPERTS].astype(I32)
    run_chunks = (cnt + ROW_CHUNK - 1) // ROW_CHUNK
    earlier_e = jnp.tril(jnp.ones((N_EXPERTS, N_EXPERTS), I32), -1)
    earlier_b = jnp.tril(jnp.ones((n_blocks, n_blocks), I32), -1)
    run_local = jnp.sum(run_chunks[:, None, :] * earlier_e[None], axis=2)
    seg_chunks = jnp.sum(run_chunks, axis=0)
    tiles_per = (seg_chunks + CHUNKS_PER_TILE - 1) // CHUNKS_PER_TILE
    seg_start = jnp.sum(tiles_per[None, :] * earlier_e, axis=1) * CHUNKS_PER_TILE
    tile_end = seg_start // CHUNKS_PER_TILE + tiles_per
    run_global = seg_start[None, :] + jnp.sum(run_chunks.T[:, None, :] * earlier_b[None], axis=2).T
    max_chunks = (2 * t) // ROW_CHUNK + n_blocks * N_EXPERTS + N_EXPERTS * (CHUNKS_PER_TILE - 1)
    n_tiles = -(-max_chunks // CHUNKS_PER_TILE) + EXP_AHEAD
    tile_ids = jnp.arange(n_tiles, dtype=I32)
    tile_expert = jnp.minimum(jnp.sum((tile_end[None, :] <= tile_ids[:, None]).astype(I32), axis=1),
                              N_EXPERTS - 1)
    n_used = tile_end[-1:].astype(I32)

    block_ids = jnp.arange(n_blocks, dtype=I32)
    zero_local = LOCAL_CHUNKS - 1
    zero_global = n_tiles * CHUNKS_PER_TILE - 1
    g = jnp.arange(n_tiles * CHUNKS_PER_TILE, dtype=I32)[:, None]
    pick_e = (tile_expert[:, None] == jnp.arange(N_EXPERTS, dtype=I32)[None, :]).astype(I32)
    rg, rc, rl = (jnp.repeat(jnp.sum(pick_e[:, :, None] * tab.T[None, :, :], axis=1),
                             CHUNKS_PER_TILE, axis=0)
                  for tab in (run_global, run_chunks, run_local))
    inside = (rg <= g) & (g < rg + rc)
    gather_src = jnp.sum(jnp.where(inside, block_ids[None, :] * LOCAL_CHUNKS + rl + (g - rg), 0), axis=1)
    gather_src = jnp.where(jnp.any(inside, axis=1), gather_src, zero_local).astype(I32)

    lc = jnp.arange(LOCAL_CHUNKS, dtype=I32)[None, :, None]
    inside = (run_local[:, None, :] <= lc) & (lc < (run_local + run_chunks)[:, None, :])
    back_src = jnp.sum(jnp.where(inside, run_global[:, None, :] + lc - run_local[:, None, :], 0), axis=2)
    back_src = jnp.where(jnp.any(inside, axis=2), back_src, zero_global).astype(I32).reshape(-1)

    ys = _experts(tile_expert, n_used, gather_src, xs, wg16, wu16, wd16, n_tiles)
    return _final(back_src, x1, route, p2d, ys, row(ple_norm), w_ple_gate.astype(BF16),
                  w_ple_proj.astype(BF16), out_gain)


def kernel(x, p, mix_norm, w_in, hg_lb_logits, hg_norm, ret_norm, w_branch_a, w_branch_b, w_out,
           ffn_norm, w_router_group, b_router_group, w_router_expert, b_router_expert,
           w_expert_gate, w_expert_up, w_expert_down, ple_norm, w_ple_gate, w_ple_proj, final_norm):
    batch, seq, d = x.shape
    depth = p.shape[0]
    assert depth == 1, "the final rmsnorm is fused into the single layer"
    x2d = x.reshape(batch * seq, d)
    out = _layer(x2d, p[0].reshape(batch * seq, -1), batch, seq, mix_norm[0], w_in[0], hg_lb_logits,
                 hg_norm[0], ret_norm[0], w_branch_a[0], w_branch_b[0], w_out[0], ffn_norm[0],
                 w_router_group[0], b_router_group[0], w_router_expert[0], b_router_expert[0],
                 w_expert_gate[0], w_expert_up[0], w_expert_down[0], ple_norm[0], w_ple_gate[0],
                 w_ple_proj[0], final_norm.reshape(1, -1).astype(F32))
    return out.reshape(batch, seq, d)
```

```python
import jax
import jax.numpy as jnp
from jax import lax
from jax.experimental import pallas as pl
from jax.experimental.pallas import tpu as pltpu

F32 = jnp.float32
BF16 = jnp.bfloat16
I32 = jnp.int32

EPS = 1e-6
D_MODEL = 1024
PLE_DIM = 256
HG_HEADS = 4
HG_DK = 128
HG_WIDTH = HG_HEADS * HG_DK
RET_HEADS = 4
RET_DK = 128
RET_DV = 256
ROPE_BASE = 10000.0
ROPE_SPLIT = 64
IN_TOTAL = 7168
N_GROUPS = 4
EXPERTS_PER_GROUP = 8
N_EXPERTS = 32
D_EXPERT = 256

COL_HQ, COL_HF, COL_HI, COL_HG = 0, 4, 8, 12
COL_RQ, COL_RK = 16, 20
COL_RV, COL_RG = 24, 32
COL_GA, COL_GB = 5, 6

LANES = 128
VMEM_LIMIT = 56 * 1024 * 1024

HG_CHUNK = 64
HG_SUB = 8
HG_UNROLL = 16
HG_STATE_UNROLL = 16
HG_NORM_ROWS = 1024
RET_CHUNK = 128
RET_STATE_UNROLL = 8
SEQ_TILE = 1024
TOK_TILE = 512
EXP_TILE = 512
EXP_SUB = 512
EXP_AHEAD = 4
EXP_SLOTS = EXP_AHEAD + 1
ROW_CHUNK = 16
CHUNKS_PER_TILE = EXP_TILE // ROW_CHUNK
LOCAL_ROWS = 2 * TOK_TILE + N_EXPERTS * ROW_CHUNK
LOCAL_CHUNKS = LOCAL_ROWS // ROW_CHUNK
SORT_SLAB = 256
MERGE_SLAB = LOCAL_ROWS
FIN_AHEAD = 2
FIN_SLOTS = FIN_AHEAD + 1
FIN_PARTS = 2


def _cparams(sem):
    return pltpu.CompilerParams(dimension_semantics=sem, vmem_limit_bytes=VMEM_LIMIT)


def _rms(x, g):
    return x * lax.rsqrt(jnp.mean(x * x, axis=-1, keepdims=True) + EPS) * g


def _sigmoid(x):
    return 1.0 / (1.0 + jnp.exp(-x))


def _silu(x):
    return x * _sigmoid(x)


def _split3(x):
    hi = x.astype(BF16)
    r1 = x - hi.astype(F32)
    mid = r1.astype(BF16)
    lo = (r1 - mid.astype(F32)).astype(BF16)
    return hi, mid, lo


def _dot(a, b):
    return jnp.dot(a, b, preferred_element_type=F32)


def _dot_nt(a, b):
    return lax.dot_general(a, b, (((1,), (1,)), ((), ())), preferred_element_type=F32)


def _dot_tn(a, b):
    return lax.dot_general(a, b, (((0,), (0,)), ((), ())), preferred_element_type=F32)


def _inproj_kernel(x_ref, g_ref, w_ref, wg_ref, wu_ref, wd_ref, proj_ref, hf_ref,
                   wg16_ref, wu16_ref, wd16_ref):
    wg16_ref[...] = wg_ref[...].astype(BF16)
    wu16_ref[...] = wu_ref[...].astype(BF16)
    wd16_ref[...] = wd_ref[...].astype(BF16)

    h = _rms(x_ref[...], g_ref[...]).astype(BF16)
    tn = HG_WIDTH
    for j in range(IN_TOTAL // tn):
        acc = _dot(h, w_ref[:, j * tn:(j + 1) * tn])
        proj_ref[:, j * tn:(j + 1) * tn] = acc.astype(BF16)
        if j * tn == COL_HF * LANES:
            hf_ref[...] = acc


def _inproj(x2d, gain, w_bf16, w_gate, w_up, w_down):
    t = x2d.shape[0]
    steps = t // TOK_TILE
    assert N_EXPERTS % steps == 0, "expert weights are converted in equal shares per grid step"
    share = N_EXPERTS // steps
    up_spec = pl.BlockSpec((share, D_MODEL, D_EXPERT), lambda i: (i, 0, 0))
    down_spec = pl.BlockSpec((share, D_EXPERT, D_MODEL), lambda i: (i, 0, 0))
    return pl.pallas_call(
        _inproj_kernel,
        grid=(steps,),
        in_specs=[
            pl.BlockSpec((TOK_TILE, D_MODEL), lambda i: (i, 0)),
            pl.BlockSpec((1, D_MODEL), lambda i: (0, 0)),
            pl.BlockSpec((D_MODEL, IN_TOTAL), lambda i: (0, 0), pipeline_mode=pl.Buffered(1)),
            up_spec, up_spec, down_spec,
        ],
        out_specs=[
            pl.BlockSpec((TOK_TILE, IN_TOTAL), lambda i: (i, 0)),
            pl.BlockSpec((TOK_TILE, HG_WIDTH), lambda i: (i, 0)),
            up_spec, up_spec, down_spec,
        ],
        out_shape=[
            jax.ShapeDtypeStruct((t, IN_TOTAL), BF16),
            jax.ShapeDtypeStruct((t, HG_WIDTH), F32),
            jax.ShapeDtypeStruct(w_gate.shape, BF16),
            jax.ShapeDtypeStruct(w_up.shape, BF16),
            jax.ShapeDtypeStruct(w_down.shape, BF16),
        ],
        compiler_params=_cparams(("arbitrary",)),
        name="inproj",
    )(x2d, gain, w_bf16, w_gate, w_up, w_down)


def _hgrn_stages(lbl_ref, q_ref, f_ref, i_ref, g_ref, ng_ref, o_ref,
                 st_ref, b_s, k_s, v_s, oi_s, qe_s, kv_s, dec_s):
    c = HG_CHUNK
    nsub = c // HG_SUB

    logits = lbl_ref[...]
    e = jnp.exp(logits - jnp.max(logits, axis=0, keepdims=True))
    lb = e[0:1] / jnp.sum(e, axis=0, keepdims=True)
    one_m_lb = jnp.sum(e[1:], axis=0, keepdims=True) / jnp.sum(e, axis=0, keepdims=True)
    ng = ng_ref[...]

    row = lax.broadcasted_iota(I32, (c, c), 0)
    col = lax.broadcasted_iota(I32, (c, c), 1)
    tri = jnp.where(row >= col, 1.0, 0.0).astype(BF16)
    sub_row = lax.broadcasted_iota(I32, (HG_SUB, HG_DK), 0)
    masked = jnp.float32(-1e30)

    def bcast_row(ref, r, rows):
        return jnp.broadcast_to(ref[pl.ds(r, 1), :], (rows, HG_DK))

    def prep(ci, slot):
        r0 = pl.multiple_of(ci * c, c)
        z = f_ref[pl.ds(r0, c), :]
        ez = jnp.exp(-jnp.abs(z))
        rz = 1.0 / (1.0 + ez)
        pos = z >= 0.0
        logf = jnp.log2(lb + one_m_lb * jnp.where(pos, rz, ez * rz))
        kk = one_m_lb * jnp.where(pos, ez * rz, rz)
        q = _silu(q_ref[pl.ds(r0, c), :].astype(F32))
        v = i_ref[pl.ds(r0, c), :].astype(F32)
        k_s[slot] = kk
        v_s[slot] = v
        return dict(ci=ci, r0=r0, slot=slot, kk=kk, q=q, v=v, v16=v.astype(BF16), parts=_split3(logf))

    def cumulate(s):
        hi, mid, lo = s.pop("parts")
        b = (_dot(tri, lo) + _dot(tri, mid)) + _dot(tri, hi)
        b_s[s["slot"]] = b
        qe_s[pl.ds(s["r0"], c), :] = (s["q"] * jnp.exp2(b)).astype(BF16)
        s["b"] = b
        return s

    def off_diagonal(s):
        b, q, kk = s["b"], s["q"], s["kk"]
        bs_ref = b_s.at[s["slot"]]
        squares = []
        size = c // 2
        while size >= HG_SUB:
            for r0 in range(size, c, 2 * size):
                edge = bcast_row(bs_ref, r0 - 1, size)
                qs = q[r0:r0 + size, :] * jnp.exp2(b[r0:r0 + size, :] - edge)
                ks = kk[r0 - size:r0, :] * jnp.exp2(edge - b[r0 - size:r0, :])
                squares.append((r0, size, _dot_nt(qs.astype(BF16), ks.astype(BF16))))
            size //= 2
        s["squares"] = squares
        return s

    def apply_values(s):
        b, kk, v = s["b"], s["kk"], s["v"]
        blast = b[c - 1:c, :]
        rows = [jnp.zeros((HG_SUB, HG_DK), F32) for _ in range(nsub)]
        for r0, size, a in s.pop("squares"):
            part = _dot(a.astype(BF16), v[r0 - size:r0, :].astype(BF16))
            for i in range(size // HG_SUB):
                rows[r0 // HG_SUB + i] = rows[r0 // HG_SUB + i] + part[i * HG_SUB:(i + 1) * HG_SUB, :]
        s["o"] = jnp.concatenate(rows, axis=0)
        kd = kk * jnp.exp2(blast - b)
        kv_s[s["ci"]] = _dot_tn(s["v16"], kd.astype(BF16))
        dec_s[s["ci"]] = jnp.broadcast_to(jnp.exp2(blast), (HG_SUB, HG_DK))
        return s

    def diagonal(s):
        b, q = s["b"], s["q"]
        bs_ref, ks_ref, vs_ref = b_s.at[s["slot"]], k_s.at[s["slot"]], v_s.at[s["slot"]]
        d_blocks = []
        for i in range(nsub):
            sl = slice(i * HG_SUB, (i + 1) * HG_SUB)
            bt, qt = b[sl, :], q[sl, :]
            acc = jnp.zeros((HG_SUB, HG_DK), F32)
            for j in range(HG_SUB):
                r = i * HG_SUB + j
                arg = bt - bcast_row(bs_ref, r, HG_SUB)
                if j > 0:
                    arg = jnp.where(sub_row >= j, arg, masked)
                g = jnp.exp2(arg) * (qt * bcast_row(ks_ref, r, HG_SUB))
                acc = acc + jnp.sum(g, axis=-1, keepdims=True) * bcast_row(vs_ref, r, HG_SUB)
            d_blocks.append(acc)
        oi_s[pl.ds(s["r0"], c), :] = s["o"] + jnp.concatenate(d_blocks, axis=0)

    def carried_group(gi):
        st = st_ref[...]
        outs = []
        for j in range(HG_STATE_UNROLL):
            ci = gi * HG_STATE_UNROLL + j
            r0 = pl.multiple_of(ci * c, c)
            outs.append((r0, _dot_nt(qe_s[pl.ds(r0, c), :], st.astype(BF16))))
            st = st * dec_s[ci][0:1, :] + kv_s[ci]
        st_ref[...] = st
        for r0, os in outs:
            oi_s[pl.ds(r0, c), :] += os

    def finish(ri):
        r0 = pl.multiple_of(ri * HG_NORM_ROWS, HG_NORM_ROWS)
        y = _rms(oi_s[pl.ds(r0, HG_NORM_ROWS), :], ng) * _silu(g_ref[pl.ds(r0, HG_NORM_ROWS), :].astype(F32))
        o_ref[pl.ds(r0, HG_NORM_ROWS), :] = y.astype(o_ref.dtype)

    return prep, (cumulate, off_diagonal, apply_values, diagonal), carried_group, finish


HG_SCRATCH = [
    pltpu.VMEM((HG_DK, HG_DK), F32),
    pltpu.VMEM((HG_UNROLL, HG_CHUNK, HG_DK), F32),
    pltpu.VMEM((HG_UNROLL, HG_CHUNK, HG_DK), F32),
    pltpu.VMEM((HG_UNROLL, HG_CHUNK, HG_DK), F32),
    pltpu.VMEM((SEQ_TILE, HG_DK), F32),
    pltpu.VMEM((SEQ_TILE, HG_DK), BF16),
    pltpu.VMEM((SEQ_TILE // HG_CHUNK, HG_DK, HG_DK), F32),
    pltpu.VMEM((SEQ_TILE // HG_CHUNK, HG_SUB, HG_DK), F32),
]


def _ret_stages(q_ref, k_ref, v_ref, g_ref, cos_ref, sin_ref, ng_ref, o_ref,
                r_ref, oi_s, q16_s, kv_s):
    c = RET_CHUNK

    hf = jnp.full((1, 1), pl.program_id(1), I32).astype(F32)
    lg = jnp.log1p(-jnp.exp2(-5.0 - hf))
    ti = lax.broadcasted_iota(I32, (c, c), 0)
    si = lax.broadcasted_iota(I32, (c, c), 1)
    rel = (ti - si).astype(F32)
    intra = jnp.where(ti >= si, jnp.exp(jnp.maximum(rel, 0.0) * lg), 0.0)
    idx = lax.broadcasted_iota(I32, (c, 1), 0).astype(F32)
    inter = jnp.exp((idx + 1.0) * lg)
    to_state = jnp.exp((c - 1.0 - idx) * lg)
    chunk_decay = jnp.exp(float(c) * lg)
    ng = ng_ref[...]
    half = RET_DK // 2

    def prep(ci):
        r0 = pl.multiple_of(ci * c, c)
        cos = cos_ref[pl.ds(r0, c), :]
        sin = sin_ref[pl.ds(r0, c), :]
        q = q_ref[pl.ds(r0, c), :].astype(F32)
        k = k_ref[pl.ds(r0, c), :].astype(F32)
        qr = (q * cos + pltpu.roll(q, half, 1) * sin) * (RET_DK ** -0.5)
        kr = k * cos + pltpu.roll(k, half, 1) * sin
        q16 = qr.astype(BF16)
        q16_s[pl.ds(r0, c), :] = q16
        return dict(ci=ci, r0=r0, q16=q16, k16=kr.astype(BF16), kts=(kr * to_state).astype(BF16),
                    v16=v_ref[pl.ds(r0, c), :])

    def scores(s):
        s["att"] = (_dot_nt(s.pop("q16"), s.pop("k16")) * intra).astype(BF16)
        return s

    def apply_values(s):
        oi_s[pl.ds(s["r0"], c), :] = _dot(s["att"], s["v16"])
        kv_s[s["ci"]] = _dot_tn(s["kts"], s["v16"])
        return s

    def carried_group(gi):
        r = r_ref[...]
        outs = []
        for j in range(RET_STATE_UNROLL):
            ci = gi * RET_STATE_UNROLL + j
            r0 = pl.multiple_of(ci * c, c)
            outs.append((r0, _dot(q16_s[pl.ds(r0, c), :], r.astype(BF16))))
            r = chunk_decay * r + kv_s[ci]
        r_ref[...] = r
        for r0, qr_state in outs:
            o = oi_s[pl.ds(r0, c), :] + qr_state * inter
            y = _rms(o, ng) * _silu(g_ref[pl.ds(r0, c), :].astype(F32))
            o_ref[pl.ds(r0, c), :] = y.astype(o_ref.dtype)

    return prep, (scores, apply_values), carried_group


RET_SCRATCH = [
    pltpu.VMEM((RET_DK, RET_DV), F32),
    pltpu.VMEM((SEQ_TILE, RET_DV), F32),
    pltpu.VMEM((SEQ_TILE, RET_DK), BF16),
    pltpu.VMEM((SEQ_TILE // RET_CHUNK, RET_DK, RET_DV), F32),
]


def _mixers_kernel(*refs):
    hg_refs = refs[:6] + refs[13:14] + refs[15:15 + len(HG_SCRATCH)]
    ret_refs = refs[6:13] + refs[14:15] + refs[15 + len(HG_SCRATCH):]
    st_ref, r_ref = hg_refs[7], ret_refs[8]
    n_tokens = refs[1].shape[0]

    @pl.when(pl.program_id(2) == 0)
    def _():
        st_ref[...] = jnp.zeros_like(st_ref)
        r_ref[...] = jnp.zeros_like(r_ref)

    hg_prep, hg_local, hg_carried, hg_finish = _hgrn_stages(*hg_refs)
    ret_prep, ret_local, ret_carried = _ret_stages(*ret_refs)
    group_tokens = HG_UNROLL * HG_CHUNK
    ret_unroll = group_tokens // RET_CHUNK

    def local_group(gi, carry):
        hs = [hg_prep(gi * HG_UNROLL + slot, slot) for slot in range(HG_UNROLL)]
        rs = [ret_prep(gi * ret_unroll + j) for j in range(ret_unroll)]
        cumulate, off_diagonal, apply_values, diagonal = hg_local
        scores, ret_apply = ret_local
        hs = [cumulate(s) for s in hs]
        rs = [scores(s) for s in rs]
        hs = [off_diagonal(s) for s in hs]
        rs = [ret_apply(s) for s in rs]
        hs = [apply_values(s) for s in hs]
        for s in hs:
            diagonal(s)
        return carry

    lax.fori_loop(0, n_tokens // group_tokens, local_group, 0)

    state_tokens = HG_STATE_UNROLL * HG_CHUNK
    assert state_tokens == RET_STATE_UNROLL * RET_CHUNK

    def carried_group(gi, carry):
        hg_carried(gi)
        ret_carried(gi)
        return carry

    lax.fori_loop(0, n_tokens // state_tokens, carried_group, 0)

    def finish(ri, carry):
        hg_finish(ri)
        return carry

    lax.fori_loop(0, n_tokens // HG_NORM_ROWS, finish, 0)


def _mixers(proj, hf, lb_logits, hg_norm, cos2, sin2, ret_norm, batch, seq):
    ns = seq // SEQ_TILE
    tok = lambda b, h, s: b * ns + s
    hg_col = lambda base: pl.BlockSpec((SEQ_TILE, HG_DK), lambda b, h, s: (tok(b, h, s), base + h))
    return pl.pallas_call(
        _mixers_kernel,
        grid=(batch, HG_HEADS, ns),
        in_specs=[
            pl.BlockSpec((2, HG_DK), lambda b, h, s: (0, h)),
            hg_col(COL_HQ), hg_col(0), hg_col(COL_HI), hg_col(COL_HG),
            pl.BlockSpec((1, HG_DK), lambda b, h, s: (0, 0)),
            hg_col(COL_RQ), hg_col(COL_RK),
            pl.BlockSpec((SEQ_TILE, RET_DV), lambda b, h, s: (tok(b, h, s), COL_RV // 2 + h)),
            pl.BlockSpec((SEQ_TILE, RET_DV), lambda b, h, s: (tok(b, h, s), COL_RG // 2 + h)),
            pl.BlockSpec((SEQ_TILE, RET_DK), lambda b, h, s: (s, 0)),
            pl.BlockSpec((SEQ_TILE, RET_DK), lambda b, h, s: (s, 0)),
            pl.BlockSpec((1, RET_DV), lambda b, h, s: (0, 0)),
        ],
        out_specs=[
            pl.BlockSpec((SEQ_TILE, HG_DK), lambda b, h, s: (tok(b, h, s), h)),
            pl.BlockSpec((SEQ_TILE, RET_DV), lambda b, h, s: (tok(b, h, s), h)),
        ],
        out_shape=[
            jax.ShapeDtypeStruct((batch * seq, HG_WIDTH), BF16),
            jax.ShapeDtypeStruct((batch * seq, RET_HEADS * RET_DV), BF16),
        ],
        scratch_shapes=HG_SCRATCH + RET_SCRATCH,
        compiler_params=_cparams(("arbitrary", "arbitrary", "arbitrary")),
        name="mixers",
    )(lb_logits, proj, hf, proj, proj, hg_norm, proj, proj, proj, proj, cos2, sin2, ret_norm)


def _merge_kernel(x_ref, ya_ref, yb_ref, ga_ref, gb_ref, wa_ref, wb_ref, wo_ref, fg_ref,
                  wr_ref, br_ref, x1_ref, xs_ref, route_ref, cnt_ref, h2b_s, rows_s):
    tm = x_ref.shape[0]

    @pl.when(pl.program_id(0) == 0)
    def _():
        h2b_s[...] = jnp.zeros_like(h2b_s)
        rows_s[...] = jnp.full(rows_s.shape, -1.0, F32)

    def sort_previous(lo, hi):
        slab_row = lax.broadcasted_iota(I32, (MERGE_SLAB, tm), 0).astype(F32)
        for r0 in range(lo, hi, MERGE_SLAB):
            sel = ((slab_row == rows_s[0:1, :] - float(r0)) | (slab_row == rows_s[1:2, :] - float(r0)))
            xs_ref[r0:r0 + MERGE_SLAB, :] = _dot(jnp.where(sel, 1.0, 0.0).astype(BF16),
                                                 h2b_s[...]).astype(BF16)

    merged = (_sigmoid(ga_ref[...].astype(F32)) * _dot(ya_ref[...], wa_ref[...])
              + _sigmoid(gb_ref[...].astype(F32)) * _dot(yb_ref[...], wb_ref[...]))
    x1 = x_ref[...] + _dot(merged.astype(BF16), wo_ref[...])
    x1_ref[...] = x1
    h2 = _rms(x1, fg_ref[...])

    h2b = h2.astype(BF16)
    logits = _dot(h2b, wr_ref[...]) + br_ref[...]
    sort_previous(0, LOCAL_ROWS)

    lane = lax.broadcasted_iota(I32, (tm, LANES), 1)
    neg = jnp.float32(-jnp.inf)
    big = jnp.int32(1 << 30)
    is_g = lane < N_GROUPS
    gl = jnp.where(is_g, logits, neg)
    gmax = jnp.max(gl, axis=-1, keepdims=True)
    g_idx = jnp.min(jnp.where(gl == gmax, lane, big), axis=-1, keepdims=True)
    g_w = 1.0 / jnp.sum(jnp.where(is_g, jnp.exp(gl - gmax), 0.0), axis=-1, keepdims=True)

    ex = lane - N_GROUPS
    in_grp = (ex >= g_idx * EXPERTS_PER_GROUP) & (ex < (g_idx + 1) * EXPERTS_PER_GROUP)
    el = jnp.where(in_grp, logits, neg)
    m1 = jnp.max(el, axis=-1, keepdims=True)
    e1 = jnp.min(jnp.where(el == m1, ex, big), axis=-1, keepdims=True)
    el2 = jnp.where(ex == e1, neg, el)
    m2 = jnp.max(el2, axis=-1, keepdims=True)
    e2 = jnp.min(jnp.where(el2 == m2, ex, big), axis=-1, keepdims=True)
    p2 = jnp.exp(m2 - m1)
    w1 = g_w / (1.0 + p2)
    w2 = g_w * p2 / (1.0 + p2)

    oh1 = ex == e1
    oh2 = ex == e2
    oh = jnp.where(oh1 | oh2, 1.0, 0.0)
    ri = lax.broadcasted_iota(I32, (tm, tm), 0)
    ci = lax.broadcasted_iota(I32, (tm, tm), 1)
    strict = jnp.where(ri > ci, 1.0, 0.0).astype(BF16)
    local_rank = _dot(strict, oh.astype(BF16))
    cnt = jnp.sum(oh, axis=0, keepdims=True)
    run_chunks = jnp.floor((cnt + (ROW_CHUNK - 1.0)) * (1.0 / ROW_CHUNK))
    ui = lax.broadcasted_iota(I32, (LANES, LANES), 0)
    uj = lax.broadcasted_iota(I32, (LANES, LANES), 1)
    before = jnp.where(ui < uj, 1.0, 0.0).astype(BF16)
    run_start = _dot(jnp.broadcast_to(run_chunks, (8, LANES)).astype(BF16), before)[0:1] * ROW_CHUNK
    slot = run_start + local_rank
    pos1 = jnp.sum(jnp.where(oh1, slot, 0.0), axis=-1, keepdims=True)
    pos2 = jnp.sum(jnp.where(oh2, slot, 0.0), axis=-1, keepdims=True)
    cnt_ref[0] = jnp.broadcast_to(cnt, (8, LANES))

    route = jnp.where(lane == 0, e1.astype(F32), 0.0)
    route = jnp.where(lane == 1, e2.astype(F32), route)
    route = jnp.where(lane == 2, w1, route)
    route = jnp.where(lane == 3, w2, route)
    route = jnp.where(lane == 4, pos1, route)
    route = jnp.where(lane == 5, pos2, route)
    route_ref[...] = route

    hi1 = jnp.floor(pos1 * (1.0 / 256.0))
    hi2 = jnp.floor(pos2 * (1.0 / 256.0))
    digits = jnp.where(lane == 0, hi1, 0.0)
    digits = jnp.where(lane == 1, pos1 - 256.0 * hi1, digits)
    digits = jnp.where(lane == 2, hi2, digits)
    digits = jnp.where(lane == 3, pos2 - 256.0 * hi2, digits)
    pick = jnp.where(lax.broadcasted_iota(I32, (8, LANES), 0) == lax.broadcasted_iota(I32, (8, LANES), 1),
                     1.0, 0.0).astype(BF16)
    rows = _dot_nt(pick, digits.astype(BF16))

    h2b_s[...] = h2b
    rows_s[0:1, :] = rows[0:1] * 256.0 + rows[1:2]
    rows_s[1:2, :] = rows[2:3] * 256.0 + rows[3:4]


def _merge(x2d, ya, yb, proj, wa, wb, wo, ffn_g, w_router, b_router):
    t = x2d.shape[0]
    nb = t // TOK_TILE
    const = lambda *shape: pl.BlockSpec(shape, lambda i: (0,) * len(shape))
    cur = lambda i: jnp.minimum(i, nb - 1)
    prev = lambda i: jnp.maximum(i - 1, 0)
    return pl.pallas_call(
        _merge_kernel,
        grid=(nb + 1,),
        in_specs=[
            pl.BlockSpec((TOK_TILE, D_MODEL), lambda i: (cur(i), 0)),
            pl.BlockSpec((TOK_TILE, HG_WIDTH), lambda i: (cur(i), 0)),
            pl.BlockSpec((TOK_TILE, D_MODEL), lambda i: (cur(i), 0)),
            pl.BlockSpec((TOK_TILE, D_MODEL), lambda i: (cur(i), COL_GA)),
            pl.BlockSpec((TOK_TILE, D_MODEL), lambda i: (cur(i), COL_GB)),
            const(HG_WIDTH, D_MODEL),
            const(D_MODEL, D_MODEL),
            const(D_MODEL, D_MODEL),
            const(1, D_MODEL),
            const(D_MODEL, LANES),
            const(1, LANES),
        ],
        out_specs=[
            pl.BlockSpec((TOK_TILE, D_MODEL), lambda i: (cur(i), 0)),
            pl.BlockSpec((LOCAL_ROWS, D_MODEL), lambda i: (prev(i), 0)),
            pl.BlockSpec((TOK_TILE, LANES), lambda i: (cur(i), 0)),
            pl.BlockSpec((1, 8, LANES), lambda i: (cur(i), 0, 0)),
        ],
        out_shape=[
            jax.ShapeDtypeStruct((t, D_MODEL), F32),
            jax.ShapeDtypeStruct((t // TOK_TILE * LOCAL_ROWS, D_MODEL), BF16),
            jax.ShapeDtypeStruct((t, LANES), F32),
            jax.ShapeDtypeStruct((t // TOK_TILE, 8, LANES), F32),
        ],
        scratch_shapes=[pltpu.VMEM((TOK_TILE, D_MODEL), BF16), pltpu.VMEM((8, TOK_TILE), F32)],
        compiler_params=_cparams(("arbitrary",)),
        name="merge_route",
    )(x2d, ya, yb, proj, proj, wa, wb, wo, ffn_g, w_router, b_router)


def _expert_kernel(te_ref, nt_ref, src_ref, wslot_ref, wnext_ref, xs_ref, wg_ref, wu_ref, wd_ref,
                   ys_ref, xbuf, wg_s, wu_s, wd_s, sem, wsem):
    i = pl.program_id(0)
    nt = nt_ref[0]

    def fetch_weights(e, slot):
        return [pltpu.make_async_copy(w_ref.at[e], w_s.at[slot], wsem.at[slot])
                for w_ref, w_s in ((wg_ref, wg_s), (wu_ref, wu_s), (wd_ref, wd_s))]

    first_tile = (i == 0) | (te_ref[i] != te_ref[jnp.maximum(i - 1, 0)])

    @pl.when(i == 0)
    def _():
        for cp in fetch_weights(te_ref[0], wslot_ref[0]):
            cp.start()

    @pl.when((i < nt) & first_tile)
    def _():
        for cp in fetch_weights(te_ref[i], wslot_ref[i]):
            cp.wait()

        @pl.when(wnext_ref[i] >= 0)
        def _():
            for cp in fetch_weights(wnext_ref[i], 1 - wslot_ref[i]):
                cp.start()

    def gather(tile, slot):
        copies = []
        for c in range(CHUNKS_PER_TILE):
            row = pl.multiple_of(src_ref[tile * CHUNKS_PER_TILE + c] * ROW_CHUNK, ROW_CHUNK)
            copies.append(pltpu.make_async_copy(
                xs_ref.at[pl.ds(row, ROW_CHUNK)],
                xbuf.at[slot, pl.ds(c * ROW_CHUNK, ROW_CHUNK)], sem.at[slot]))
        return copies

    @pl.when(i == 0)
    def _():
        for k in range(EXP_AHEAD):
            for cp in gather(k, k):
                cp.start()

    @pl.when(i < nt + EXP_AHEAD)
    def _():
        for cp in gather(i, i % EXP_SLOTS):
            cp.wait()

    @pl.when(i < nt)
    def _():
        slot = i % EXP_SLOTS
        ws = wslot_ref[i]
        wg, wu, wd = wg_s[ws], wu_s[ws], wd_s[ws]
        subs = [pl.ds(r, EXP_SUB) for r in range(0, EXP_TILE, EXP_SUB)]
        xs = [xbuf[slot, s, :] for s in subs]
        gates = [(_dot(x, wg), _dot(x, wu)) for x in xs]
        for c, cp in enumerate(gather(i + EXP_AHEAD, (i + EXP_AHEAD) % EXP_SLOTS)):
            cp.start(priority=c % 2)
        hidden = [(_silu(a) * u).astype(BF16) for a, u in gates]
        for s, h in zip(subs, hidden):
            ys_ref[s, :] = _dot(h, wd).astype(ys_ref.dtype)

    @pl.when(i >= nt)
    def _():
        ys_ref[...] = jnp.zeros_like(ys_ref)


def _experts(tile_expert, n_tiles_used, src_chunk, xs, wg, wu, wd, n_tiles):
    ids = jnp.arange(n_tiles, dtype=I32)
    first = (ids == 0) | (tile_expert != jnp.concatenate([tile_expert[:1], tile_expert[:-1]]))
    wslot = (jnp.sum((first[None, :] & (ids[None, :] <= ids[:, None])).astype(I32), axis=1) - 1) % 2
    later_first = first[None, :] & (ids[None, :] > ids[:, None]) & (ids[None, :] < n_tiles_used[0])
    nxt = jnp.min(jnp.where(later_first, ids[None, :], n_tiles), axis=1)
    wnext = jnp.sum(jnp.where(ids[None, :] == nxt[:, None], tile_expert[None, :], 0), axis=1)
    wnext = jnp.where(nxt < n_tiles, wnext, -1).astype(I32)

    hbm = pl.BlockSpec(memory_space=pl.ANY)
    grid_spec = pltpu.PrefetchScalarGridSpec(
        num_scalar_prefetch=5,
        grid=(n_tiles,),
        in_specs=[hbm, hbm, hbm, hbm],
        out_specs=pl.BlockSpec((EXP_TILE, D_MODEL), lambda i, *_: (i, 0)),
        scratch_shapes=[
            pltpu.VMEM((EXP_SLOTS, EXP_TILE, D_MODEL), BF16),
            pltpu.VMEM((2, D_MODEL, D_EXPERT), BF16),
            pltpu.VMEM((2, D_MODEL, D_EXPERT), BF16),
            pltpu.VMEM((2, D_EXPERT, D_MODEL), BF16),
            pltpu.SemaphoreType.DMA((EXP_SLOTS,)),
            pltpu.SemaphoreType.DMA((2,)),
        ],
    )
    return pl.pallas_call(
        _expert_kernel,
        grid_spec=grid_spec,
        out_shape=jax.ShapeDtypeStruct((n_tiles * EXP_TILE, D_MODEL), BF16),
        compiler_params=_cparams(("arbitrary",)),
        name="experts",
    )(tile_expert, n_tiles_used, src_chunk, wslot.astype(I32), wnext, xs, wg, wu, wd)


def _final_kernel(src_ref, x1_ref, route_ref, p_ref, ys_ref, pg_ref, wpg_ref, wpp_ref, fg_ref,
                  o_ref, ybuf, sem):
    tm = x1_ref.shape[0]
    i = pl.program_id(0)
    nb = pl.num_programs(0)

    def gather(block, slot):
        copies = []
        for c in range(LOCAL_CHUNKS):
            row = pl.multiple_of(src_ref[block * LOCAL_CHUNKS + c] * ROW_CHUNK, ROW_CHUNK)
            copies.append(pltpu.make_async_copy(
                ys_ref.at[pl.ds(row, ROW_CHUNK)],
                ybuf.at[slot, pl.ds(c * ROW_CHUNK, ROW_CHUNK)], sem.at[slot]))
        return copies

    @pl.when(i == 0)
    def _():
        for k in range(FIN_AHEAD):
            for cp in gather(k % nb, k):
                cp.start()

    slot = i % FIN_SLOTS
    for cp in gather(i, slot):
        cp.wait()

    parts = [pl.ds(r, tm // FIN_PARTS) for r in range(0, tm, tm // FIN_PARTS)]
    route = route_ref[...]
    ple = [_dot(p_ref[rows, :].astype(BF16), wpp_ref[...]) for rows in parts]

    slab_col = lax.broadcasted_iota(I32, (tm // FIN_PARTS, SORT_SLAB), 1).astype(F32)
    moes = []
    for k in range(FIN_PARTS):
        rows = slice(k * (tm // FIN_PARTS), (k + 1) * (tm // FIN_PARTS))
        w1, w2 = route[rows, 2:3], route[rows, 3:4]
        pos1, pos2 = route[rows, 4:5], route[rows, 5:6]
        moe = jnp.zeros((tm // FIN_PARTS, D_MODEL), F32)
        for k0 in range(0, LOCAL_ROWS, SORT_SLAB):
            sel = jnp.where(slab_col == pos1 - float(k0), w1,
                            jnp.where(slab_col == pos2 - float(k0), w2, 0.0)).astype(BF16)
            moe = moe + _dot(sel, ybuf[slot, k0:k0 + SORT_SLAB, :])
        moes.append(moe)
    for c, cp in enumerate(gather((i + FIN_AHEAD) % nb, (i + FIN_AHEAD) % FIN_SLOTS)):
        cp.start(priority=c % 2)
    x2s = [x1_ref[rows, :] + moe for rows, moe in zip(parts, moes)]
    hps = [_rms(x2, pg_ref[...]).astype(BF16) for x2 in x2s]
    gates = [_sigmoid(_dot(hp, wpg_ref[...])) for hp in hps]
    for rows, x2, gate, pp in zip(parts, x2s, gates, ple):
        o_ref[rows, :] = _rms(x2 + gate * pp, fg_ref[...])

    @pl.when(i == nb - 1)
    def _():
        for k in range(1, FIN_AHEAD + 1):
            for cp in gather(0, (i + k) % FIN_SLOTS):
                cp.wait()


def _final(src_chunk, x1, route, p2d, ys, ple_g, wpg, wpp, final_g):
    t = x1.shape[0]
    const = lambda *shape: pl.BlockSpec(shape, lambda i, src: (0,) * len(shape))
    grid_spec = pltpu.PrefetchScalarGridSpec(
        num_scalar_prefetch=1,
        grid=(t // TOK_TILE,),
        in_specs=[
            pl.BlockSpec((TOK_TILE, D_MODEL), lambda i, src: (i, 0)),
            pl.BlockSpec((TOK_TILE, LANES), lambda i, src: (i, 0)),
            pl.BlockSpec((TOK_TILE, PLE_DIM), lambda i, src: (i, 0)),
            pl.BlockSpec(memory_space=pl.ANY),
            const(1, D_MODEL),
            const(D_MODEL, D_MODEL),
            const(PLE_DIM, D_MODEL),
            const(1, D_MODEL),
        ],
        out_specs=pl.BlockSpec((TOK_TILE, D_MODEL), lambda i, src: (i, 0)),
        scratch_shapes=[pltpu.VMEM((FIN_SLOTS, LOCAL_ROWS, D_MODEL), BF16),
                        pltpu.SemaphoreType.DMA((FIN_SLOTS,))],
    )
    return pl.pallas_call(
        _final_kernel,
        grid_spec=grid_spec,
        out_shape=jax.ShapeDtypeStruct((t, D_MODEL), F32),
        compiler_params=_cparams(("arbitrary",)),
        name="combine_ple_final",
    )(src_chunk, x1, route, p2d, ys, ple_g, wpg, wpp, final_g)


def _rotary_tables(seq):
    inv = ROPE_BASE ** (-jnp.arange(0, RET_DK, 2, dtype=F32) / RET_DK)
    inv = jnp.concatenate([inv, inv])
    sign = jnp.where(jnp.arange(RET_DK) < RET_DK // 2, -1.0, 1.0).astype(F32)
    hi = (jnp.arange(seq // ROPE_SPLIT, dtype=F32) * ROPE_SPLIT)[:, None] * inv[None, :]
    lo = jnp.arange(ROPE_SPLIT, dtype=F32)[:, None] * inv[None, :]
    ch, sh = jnp.cos(hi)[:, None, :], jnp.sin(hi)[:, None, :]
    cl, sl = jnp.cos(lo)[None], jnp.sin(lo)[None]
    cos = (ch * cl - sh * sl).reshape(seq, RET_DK)
    sin = ((sh * cl + ch * sl) * sign).reshape(seq, RET_DK)
    return cos, sin


def _layer(x2d, p2d, batch, seq, mix_norm, w_in, hg_lb_logits, hg_norm, ret_norm, w_branch_a,
           w_branch_b, w_out, ffn_norm, w_rg, b_rg, w_re, b_re, w_gate, w_up, w_down, ple_norm,
           w_ple_gate, w_ple_proj, out_gain):
    t = x2d.shape[0]
    row = lambda v: v.reshape(1, -1).astype(F32)

    proj, hf, wg16, wu16, wd16 = _inproj(x2d, row(mix_norm), w_in.astype(BF16), w_gate, w_up, w_down)
    cos2, sin2 = _rotary_tables(seq)
    ya, yb = _mixers(proj, hf, hg_lb_logits.astype(F32), row(hg_norm), cos2, sin2, row(ret_norm), batch, seq)

    n_r = N_GROUPS + N_EXPERTS
    w_router = jnp.pad(jnp.concatenate([w_rg, w_re], axis=1).astype(BF16), ((0, 0), (0, LANES - n_r)))
    b_router = jnp.zeros((1, LANES), F32).at[0, :n_r].set(jnp.concatenate([b_rg, b_re]))
    x1, xs, route, counts = _merge(
        x2d, ya, yb, proj, w_branch_a.astype(BF16), w_branch_b.astype(BF16), w_out.astype(BF16),
        row(ffn_norm), w_router, b_router)

    n_blocks = t // TOK_TILE
    cnt = counts[:, 0, N_GROUPS:N_GROUPS + N_EXPERTS].astype(I32)
    run_chunks = (cnt + ROW_CHUNK - 1) // ROW_CHUNK
    earlier_e = jnp.tril(jnp.ones((N_EXPERTS, N_EXPERTS), I32), -1)
    earlier_b = jnp.tril(jnp.ones((n_blocks, n_blocks), I32), -1)
    run_local = jnp.sum(run_chunks[:, None, :] * earlier_e[None], axis=2)
    seg_chunks = jnp.sum(run_chunks, axis=0)
    tiles_per = (seg_chunks + CHUNKS_PER_TILE - 1) // CHUNKS_PER_TILE
    seg_start = jnp.sum(tiles_per[None, :] * earlier_e, axis=1) * CHUNKS_PER_TILE
    tile_end = seg_start // CHUNKS_PER_TILE + tiles_per
    run_global = seg_start[None, :] + jnp.sum(run_chunks.T[:, None, :] * earlier_b[None], axis=2).T
    max_chunks = (2 * t) // ROW_CHUNK + n_blocks * N_EXPERTS + N_EXPERTS * (CHUNKS_PER_TILE - 1)
    n_tiles = -(-max_chunks // CHUNKS_PER_TILE) + EXP_AHEAD
    tile_ids = jnp.arange(n_tiles, dtype=I32)
    tile_expert = jnp.minimum(jnp.sum((tile_end[None, :] <= tile_ids[:, None]).astype(I32), axis=1),
                              N_EXPERTS - 1)
    n_used = tile_end[-1:].astype(I32)

    block_ids = jnp.arange(n_blocks, dtype=I32)
    zero_local = LOCAL_CHUNKS - 1
    zero_global = n_tiles * CHUNKS_PER_TILE - 1
    g = jnp.arange(n_tiles * CHUNKS_PER_TILE, dtype=I32)[:, None]
    pick_e = (tile_expert[:, None] == jnp.arange(N_EXPERTS, dtype=I32)[None, :]).astype(I32)
    rg, rc, rl = (jnp.repeat(jnp.sum(pick_e[:, :, None] * tab.T[None, :, :], axis=1),
                             CHUNKS_PER_TILE, axis=0)
                  for tab in (run_global, run_chunks, run_local))
    inside = (rg <= g) & (g < rg + rc)
    gather_src = jnp.sum(jnp.where(inside, block_ids[None, :] * LOCAL_CHUNKS + rl + (g - rg), 0), axis=1)
    gather_src = jnp.where(jnp.any(inside, axis=1), gather_src, zero_local).astype(I32)

    lc = jnp.arange(LOCAL_CHUNKS, dtype=I32)[None, :, None]
    inside = (run_local[:, None, :] <= lc) & (lc < (run_local + run_chunks)[:, None, :])
    back_src = jnp.sum(jnp.where(inside, run_global[:, None, :] + lc - run_local[:, None, :], 0), axis=2)
    back_src = jnp.where(jnp.any(inside, axis=2), back_src, zero_global).astype(I32).reshape(-1)

    ys = _experts(tile_expert, n_used, gather_src, xs, wg16, wu16, wd16, n_tiles)
    return _final(back_src, x1, route, p2d, ys, row(ple_norm), w_ple_gate.astype(BF16),
                  w_ple_proj.astype(BF16), out_gain)


def kernel(x, p, mix_norm, w_in, hg_lb_logits, hg_norm, ret_norm, w_branch_a, w_branch_b, w_out,
           ffn_norm, w_router_group, b_router_group, w_router_expert, b_router_expert,
           w_expert_gate, w_expert_up, w_expert_down, ple_norm, w_ple_gate, w_ple_proj, final_norm):
    batch, seq, d = x.shape
    depth = p.shape[0]
    assert depth == 1, "the final rmsnorm is fused into the single layer"
    x2d = x.reshape(batch * seq, d)
    out = _layer(x2d, p[0].reshape(batch * seq, -1), batch, seq, mix_norm[0], w_in[0], hg_lb_logits,
                 hg_norm[0], ret_norm[0], w_branch_a[0], w_branch_b[0], w_out[0], ffn_norm[0],
                 w_router_group[0], b_router_group[0], w_router_expert[0], b_router_expert[0],
                 w_expert_gate[0], w_expert_up[0], w_expert_down[0], ple_norm[0], w_ple_gate[0],
                 w_ple_proj[0], final_norm.reshape(1, -1).astype(F32))
    return out.reshape(batch, seq, d)
```

```python
import jax
import jax.numpy as jnp
from jax import lax
from jax.experimental import pallas as pl
from jax.experimental.pallas import tpu as pltpu

F32 = jnp.float32
BF16 = jnp.bfloat16
I32 = jnp.int32

EPS = 1e-6
D_MODEL = 1024
PLE_DIM = 256
HG_HEADS = 4
HG_DK = 128
HG_WIDTH = HG_HEADS * HG_DK
RET_HEADS = 4
RET_DK = 128
RET_DV = 256
ROPE_BASE = 10000.0
ROPE_SPLIT = 64
IN_TOTAL = 7168
N_GROUPS = 4
EXPERTS_PER_GROUP = 8
N_EXPERTS = 32
D_EXPERT = 256

COL_HQ, COL_HF, COL_HI, COL_HG = 0, 4, 8, 12
COL_RQ, COL_RK = 16, 20
COL_RV, COL_RG = 24, 32
COL_GA, COL_GB = 5, 6

LANES = 128
VMEM_LIMIT = 56 * 1024 * 1024

HG_CHUNK = 64
HG_SUB = 8
HG_UNROLL = 16
HG_STATE_UNROLL = 16
HG_NORM_ROWS = 1024
RET_CHUNK = 128
RET_STATE_UNROLL = 8
SEQ_TILE = 1024
TOK_TILE = 512
EXP_TILE = 512
EXP_SUB = 512
EXP_AHEAD = 6
EXP_SLOTS = EXP_AHEAD + 1
ROW_CHUNK = 16
CHUNKS_PER_TILE = EXP_TILE // ROW_CHUNK
LOCAL_ROWS = 2 * TOK_TILE + N_EXPERTS * ROW_CHUNK
LOCAL_CHUNKS = LOCAL_ROWS // ROW_CHUNK
SORT_SLAB = 256
MERGE_SLAB = LOCAL_ROWS
FIN_AHEAD = 2
FIN_SLOTS = FIN_AHEAD + 1
FIN_PARTS = 2


def _cparams(sem):
    return pltpu.CompilerParams(dimension_semantics=sem, vmem_limit_bytes=VMEM_LIMIT)


def _rms(x, g):
    return x * lax.rsqrt(jnp.mean(x * x, axis=-1, keepdims=True) + EPS) * g


def _sigmoid(x):
    return 1.0 / (1.0 + jnp.exp(-x))


def _silu(x):
    return x * _sigmoid(x)


def _split3(x):
    hi = x.astype(BF16)
    r1 = x - hi.astype(F32)
    mid = r1.astype(BF16)
    lo = (r1 - mid.astype(F32)).astype(BF16)
    return hi, mid, lo


def _dot(a, b):
    return jnp.dot(a, b, preferred_element_type=F32)


def _dot_nt(a, b):
    return lax.dot_general(a, b, (((1,), (1,)), ((), ())), preferred_element_type=F32)


def _dot_tn(a, b):
    return lax.dot_general(a, b, (((0,), (0,)), ((), ())), preferred_element_type=F32)


def _inproj_kernel(x_ref, g_ref, w_ref, wg_ref, wu_ref, wd_ref, proj_ref, hf_ref,
                   wg16_ref, wu16_ref, wd16_ref):
    wg16_ref[...] = wg_ref[...].astype(BF16)
    wu16_ref[...] = wu_ref[...].astype(BF16)
    wd16_ref[...] = wd_ref[...].astype(BF16)

    h = _rms(x_ref[...], g_ref[...]).astype(BF16)
    tn = HG_WIDTH
    for j in range(IN_TOTAL // tn):
        acc = _dot(h, w_ref[:, j * tn:(j + 1) * tn])
        proj_ref[:, j * tn:(j + 1) * tn] = acc.astype(BF16)
        if j * tn == COL_HF * LANES:
            hf_ref[...] = acc


def _inproj(x2d, gain, w_bf16, w_gate, w_up, w_down):
    t = x2d.shape[0]
    steps = t // TOK_TILE
    assert N_EXPERTS % steps == 0, "expert weights are converted in equal shares per grid step"
    share = N_EXPERTS // steps
    up_spec = pl.BlockSpec((share, D_MODEL, D_EXPERT), lambda i: (i, 0, 0))
    down_spec = pl.BlockSpec((share, D_EXPERT, D_MODEL), lambda i: (i, 0, 0))
    return pl.pallas_call(
        _inproj_kernel,
        grid=(steps,),
        in_specs=[
            pl.BlockSpec((TOK_TILE, D_MODEL), lambda i: (i, 0)),
            pl.BlockSpec((1, D_MODEL), lambda i: (0, 0)),
            pl.BlockSpec((D_MODEL, IN_TOTAL), lambda i: (0, 0), pipeline_mode=pl.Buffered(1)),
            up_spec, up_spec, down_spec,
        ],
        out_specs=[
            pl.BlockSpec((TOK_TILE, IN_TOTAL), lambda i: (i, 0)),
            pl.BlockSpec((TOK_TILE, HG_WIDTH), lambda i: (i, 0)),
            up_spec, up_spec, down_spec,
        ],
        out_shape=[
            jax.ShapeDtypeStruct((t, IN_TOTAL), BF16),
            jax.ShapeDtypeStruct((t, HG_WIDTH), F32),
            jax.ShapeDtypeStruct(w_gate.shape, BF16),
            jax.ShapeDtypeStruct(w_up.shape, BF16),
            jax.ShapeDtypeStruct(w_down.shape, BF16),
        ],
        compiler_params=_cparams(("arbitrary",)),
        name="inproj",
    )(x2d, gain, w_bf16, w_gate, w_up, w_down)


def _hgrn_stages(lbl_ref, q_ref, f_ref, i_ref, g_ref, ng_ref, o_ref,
                 st_ref, b_s, k_s, v_s, oi_s, qe_s, kv_s, dec_s):
    c = HG_CHUNK
    nsub = c // HG_SUB

    logits = lbl_ref[...]
    e = jnp.exp(logits - jnp.max(logits, axis=0, keepdims=True))
    lb = e[0:1] / jnp.sum(e, axis=0, keepdims=True)
    one_m_lb = jnp.sum(e[1:], axis=0, keepdims=True) / jnp.sum(e, axis=0, keepdims=True)
    ng = ng_ref[...]

    row = lax.broadcasted_iota(I32, (c, c), 0)
    col = lax.broadcasted_iota(I32, (c, c), 1)
    tri = jnp.where(row >= col, 1.0, 0.0).astype(BF16)
    sub_row = lax.broadcasted_iota(I32, (HG_SUB, HG_DK), 0)
    masked = jnp.float32(-1e30)

    def bcast_row(ref, r, rows):
        return jnp.broadcast_to(ref[pl.ds(r, 1), :], (rows, HG_DK))

    def prep(ci, slot):
        r0 = pl.multiple_of(ci * c, c)
        z = f_ref[pl.ds(r0, c), :]
        ez = jnp.exp(-jnp.abs(z))
        rz = 1.0 / (1.0 + ez)
        pos = z >= 0.0
        logf = jnp.log2(lb + one_m_lb * jnp.where(pos, rz, ez * rz))
        kk = one_m_lb * jnp.where(pos, ez * rz, rz)
        q = _silu(q_ref[pl.ds(r0, c), :].astype(F32))
        v = i_ref[pl.ds(r0, c), :].astype(F32)
        k_s[slot] = kk
        v_s[slot] = v
        return dict(ci=ci, r0=r0, slot=slot, kk=kk, q=q, v=v, v16=v.astype(BF16), parts=_split3(logf))

    def cumulate(s):
        hi, mid, lo = s.pop("parts")
        b = (_dot(tri, lo) + _dot(tri, mid)) + _dot(tri, hi)
        b_s[s["slot"]] = b
        qe_s[pl.ds(s["r0"], c), :] = (s["q"] * jnp.exp2(b)).astype(BF16)
        s["b"] = b
        return s

    def off_diagonal(s):
        b, q, kk = s["b"], s["q"], s["kk"]
        bs_ref = b_s.at[s["slot"]]
        squares = []
        size = c // 2
        while size >= HG_SUB:
            for r0 in range(size, c, 2 * size):
                edge = bcast_row(bs_ref, r0 - 1, size)
                qs = q[r0:r0 + size, :] * jnp.exp2(b[r0:r0 + size, :] - edge)
                ks = kk[r0 - size:r0, :] * jnp.exp2(edge - b[r0 - size:r0, :])
                squares.append((r0, size, _dot_nt(qs.astype(BF16), ks.astype(BF16))))
            size //= 2
        s["squares"] = squares
        return s

    def apply_values(s):
        b, kk, v = s["b"], s["kk"], s["v"]
        blast = b[c - 1:c, :]
        rows = [jnp.zeros((HG_SUB, HG_DK), F32) for _ in range(nsub)]
        for r0, size, a in s.pop("squares"):
            part = _dot(a.astype(BF16), v[r0 - size:r0, :].astype(BF16))
            for i in range(size // HG_SUB):
                rows[r0 // HG_SUB + i] = rows[r0 // HG_SUB + i] + part[i * HG_SUB:(i + 1) * HG_SUB, :]
        s["o"] = jnp.concatenate(rows, axis=0)
        kd = kk * jnp.exp2(blast - b)
        kv_s[s["ci"]] = _dot_tn(s["v16"], kd.astype(BF16))
        dec_s[s["ci"]] = jnp.broadcast_to(jnp.exp2(blast), (HG_SUB, HG_DK))
        return s

    def diagonal(s):
        b, q = s["b"], s["q"]
        bs_ref, ks_ref, vs_ref = b_s.at[s["slot"]], k_s.at[s["slot"]], v_s.at[s["slot"]]
        d_blocks = []
        for i in range(nsub):
            sl = slice(i * HG_SUB, (i + 1) * HG_SUB)
            bt, qt = b[sl, :], q[sl, :]
            acc = jnp.zeros((HG_SUB, HG_DK), F32)
            for j in range(HG_SUB):
                r = i * HG_SUB + j
                arg = bt - bcast_row(bs_ref, r, HG_SUB)
                if j > 0:
                    arg = jnp.where(sub_row >= j, arg, masked)
                g = jnp.exp2(arg) * (qt * bcast_row(ks_ref, r, HG_SUB))
                acc = acc + jnp.sum(g, axis=-1, keepdims=True) * bcast_row(vs_ref, r, HG_SUB)
            d_blocks.append(acc)
        oi_s[pl.ds(s["r0"], c), :] = s["o"] + jnp.concatenate(d_blocks, axis=0)

    def carried_group(gi):
        st = st_ref[...]
        outs = []
        for j in range(HG_STATE_UNROLL):
            ci = gi * HG_STATE_UNROLL + j
            r0 = pl.multiple_of(ci * c, c)
            outs.append((r0, _dot_nt(qe_s[pl.ds(r0, c), :], st.astype(BF16))))
            st = st * dec_s[ci][0:1, :] + kv_s[ci]
        st_ref[...] = st
        for r0, os in outs:
            oi_s[pl.ds(r0, c), :] += os

    def finish(ri):
        r0 = pl.multiple_of(ri * HG_NORM_ROWS, HG_NORM_ROWS)
        y = _rms(oi_s[pl.ds(r0, HG_NORM_ROWS), :], ng) * _silu(g_ref[pl.ds(r0, HG_NORM_ROWS), :].astype(F32))
        o_ref[pl.ds(r0, HG_NORM_ROWS), :] = y.astype(o_ref.dtype)

    return prep, (cumulate, off_diagonal, apply_values, diagonal), carried_group, finish


HG_SCRATCH = [
    pltpu.VMEM((HG_DK, HG_DK), F32),
    pltpu.VMEM((HG_UNROLL, HG_CHUNK, HG_DK), F32),
    pltpu.VMEM((HG_UNROLL, HG_CHUNK, HG_DK), F32),
    pltpu.VMEM((HG_UNROLL, HG_CHUNK, HG_DK), F32),
    pltpu.VMEM((SEQ_TILE, HG_DK), F32),
    pltpu.VMEM((SEQ_TILE, HG_DK), BF16),
    pltpu.VMEM((SEQ_TILE // HG_CHUNK, HG_DK, HG_DK), F32),
    pltpu.VMEM((SEQ_TILE // HG_CHUNK, HG_SUB, HG_DK), F32),
]


def _ret_stages(q_ref, k_ref, v_ref, g_ref, cos_ref, sin_ref, ng_ref, o_ref,
                r_ref, oi_s, q16_s, kv_s):
    c = RET_CHUNK

    hf = jnp.full((1, 1), pl.program_id(1), I32).astype(F32)
    lg = jnp.log1p(-jnp.exp2(-5.0 - hf))
    ti = lax.broadcasted_iota(I32, (c, c), 0)
    si = lax.broadcasted_iota(I32, (c, c), 1)
    rel = (ti - si).astype(F32)
    intra = jnp.where(ti >= si, jnp.exp(jnp.maximum(rel, 0.0) * lg), 0.0)
    idx = lax.broadcasted_iota(I32, (c, 1), 0).astype(F32)
    inter = jnp.exp((idx + 1.0) * lg)
    to_state = jnp.exp((c - 1.0 - idx) * lg)
    chunk_decay = jnp.exp(float(c) * lg)
    ng = ng_ref[...]
    half = RET_DK // 2

    def prep(ci):
        r0 = pl.multiple_of(ci * c, c)
        cos = cos_ref[pl.ds(r0, c), :]
        sin = sin_ref[pl.ds(r0, c), :]
        q = q_ref[pl.ds(r0, c), :].astype(F32)
        k = k_ref[pl.ds(r0, c), :].astype(F32)
        qr = (q * cos + pltpu.roll(q, half, 1) * sin) * (RET_DK ** -0.5)
        kr = k * cos + pltpu.roll(k, half, 1) * sin
        q16 = qr.astype(BF16)
        q16_s[pl.ds(r0, c), :] = q16
        return dict(ci=ci, r0=r0, q16=q16, k16=kr.astype(BF16), kts=(kr * to_state).astype(BF16),
                    v16=v_ref[pl.ds(r0, c), :])

    def scores(s):
        s["att"] = (_dot_nt(s.pop("q16"), s.pop("k16")) * intra).astype(BF16)
        return s

    def apply_values(s):
        oi_s[pl.ds(s["r0"], c), :] = _dot(s["att"], s["v16"])
        kv_s[s["ci"]] = _dot_tn(s["kts"], s["v16"])
        return s

    def carried_group(gi):
        r = r_ref[...]
        outs = []
        for j in range(RET_STATE_UNROLL):
            ci = gi * RET_STATE_UNROLL + j
            r0 = pl.multiple_of(ci * c, c)
            outs.append((r0, _dot(q16_s[pl.ds(r0, c), :], r.astype(BF16))))
            r = chunk_decay * r + kv_s[ci]
        r_ref[...] = r
        for r0, qr_state in outs:
            o = oi_s[pl.ds(r0, c), :] + qr_state * inter
            y = _rms(o, ng) * _silu(g_ref[pl.ds(r0, c), :].astype(F32))
            o_ref[pl.ds(r0, c), :] = y.astype(o_ref.dtype)

    return prep, (scores, apply_values), carried_group


RET_SCRATCH = [
    pltpu.VMEM((RET_DK, RET_DV), F32),
    pltpu.VMEM((SEQ_TILE, RET_DV), F32),
    pltpu.VMEM((SEQ_TILE, RET_DK), BF16),
    pltpu.VMEM((SEQ_TILE // RET_CHUNK, RET_DK, RET_DV), F32),
]


def _mixers_kernel(*refs):
    hg_refs = refs[:6] + refs[13:14] + refs[15:15 + len(HG_SCRATCH)]
    ret_refs = refs[6:13] + refs[14:15] + refs[15 + len(HG_SCRATCH):]
    st_ref, r_ref = hg_refs[7], ret_refs[8]
    n_tokens = refs[1].shape[0]

    @pl.when(pl.program_id(2) == 0)
    def _():
        st_ref[...] = jnp.zeros_like(st_ref)
        r_ref[...] = jnp.zeros_like(r_ref)

    hg_prep, hg_local, hg_carried, hg_finish = _hgrn_stages(*hg_refs)
    ret_prep, ret_local, ret_carried = _ret_stages(*ret_refs)
    group_tokens = HG_UNROLL * HG_CHUNK
    ret_unroll = group_tokens // RET_CHUNK

    def local_group(gi, carry):
        hs = [hg_prep(gi * HG_UNROLL + slot, slot) for slot in range(HG_UNROLL)]
        rs = [ret_prep(gi * ret_unroll + j) for j in range(ret_unroll)]
        cumulate, off_diagonal, apply_values, diagonal = hg_local
        scores, ret_apply = ret_local
        hs = [cumulate(s) for s in hs]
        rs = [scores(s) for s in rs]
        hs = [off_diagonal(s) for s in hs]
        rs = [ret_apply(s) for s in rs]
        hs = [apply_values(s) for s in hs]
        for s in hs:
            diagonal(s)
        return carry

    lax.fori_loop(0, n_tokens // group_tokens, local_group, 0)

    state_tokens = HG_STATE_UNROLL * HG_CHUNK
    assert state_tokens == RET_STATE_UNROLL * RET_CHUNK

    def carried_group(gi, carry):
        hg_carried(gi)
        ret_carried(gi)
        return carry

    lax.fori_loop(0, n_tokens // state_tokens, carried_group, 0)

    def finish(ri, carry):
        hg_finish(ri)
        return carry

    lax.fori_loop(0, n_tokens // HG_NORM_ROWS, finish, 0)


def _mixers(proj, hf, lb_logits, hg_norm, cos2, sin2, ret_norm, batch, seq):
    ns = seq // SEQ_TILE
    tok = lambda b, h, s: b * ns + s
    hg_col = lambda base: pl.BlockSpec((SEQ_TILE, HG_DK), lambda b, h, s: (tok(b, h, s), base + h))
    return pl.pallas_call(
        _mixers_kernel,
        grid=(batch, HG_HEADS, ns),
        in_specs=[
            pl.BlockSpec((2, HG_DK), lambda b, h, s: (0, h)),
            hg_col(COL_HQ), hg_col(0), hg_col(COL_HI), hg_col(COL_HG),
            pl.BlockSpec((1, HG_DK), lambda b, h, s: (0, 0)),
            hg_col(COL_RQ), hg_col(COL_RK),
            pl.BlockSpec((SEQ_TILE, RET_DV), lambda b, h, s: (tok(b, h, s), COL_RV // 2 + h)),
            pl.BlockSpec((SEQ_TILE, RET_DV), lambda b, h, s: (tok(b, h, s), COL_RG // 2 + h)),
            pl.BlockSpec((SEQ_TILE, RET_DK), lambda b, h, s: (s, 0)),
            pl.BlockSpec((SEQ_TILE, RET_DK), lambda b, h, s: (s, 0)),
            pl.BlockSpec((1, RET_DV), lambda b, h, s: (0, 0)),
        ],
        out_specs=[
            pl.BlockSpec((SEQ_TILE, HG_DK), lambda b, h, s: (tok(b, h, s), h)),
            pl.BlockSpec((SEQ_TILE, RET_DV), lambda b, h, s: (tok(b, h, s), h)),
        ],
        out_shape=[
            jax.ShapeDtypeStruct((batch * seq, HG_WIDTH), BF16),
            jax.ShapeDtypeStruct((batch * seq, RET_HEADS * RET_DV), BF16),
        ],
        scratch_shapes=HG_SCRATCH + RET_SCRATCH,
        compiler_params=_cparams(("arbitrary", "arbitrary", "arbitrary")),
        name="mixers",
    )(lb_logits, proj, hf, proj, proj, hg_norm, proj, proj, proj, proj, cos2, sin2, ret_norm)


def _merge_kernel(x_ref, ya_ref, yb_ref, ga_ref, gb_ref, wa_ref, wb_ref, wo_ref, fg_ref,
                  wr_ref, br_ref, x1_ref, xs_ref, route_ref, cnt_ref, h2b_s, rows_s):
    tm = x_ref.shape[0]

    @pl.when(pl.program_id(0) == 0)
    def _():
        h2b_s[...] = jnp.zeros_like(h2b_s)
        rows_s[...] = jnp.full(rows_s.shape, -1.0, F32)

    def sort_previous(lo, hi):
        slab_row = lax.broadcasted_iota(I32, (MERGE_SLAB, tm), 0).astype(F32)
        for r0 in range(lo, hi, MERGE_SLAB):
            sel = ((slab_row == rows_s[0:1, :] - float(r0)) | (slab_row == rows_s[1:2, :] - float(r0)))
            xs_ref[r0:r0 + MERGE_SLAB, :] = _dot(jnp.where(sel, 1.0, 0.0).astype(BF16),
                                                 h2b_s[...]).astype(BF16)

    merged = (_sigmoid(ga_ref[...].astype(F32)) * _dot(ya_ref[...], wa_ref[...])
              + _sigmoid(gb_ref[...].astype(F32)) * _dot(yb_ref[...], wb_ref[...]))
    x1 = x_ref[...] + _dot(merged.astype(BF16), wo_ref[...])
    x1_ref[...] = x1
    h2 = _rms(x1, fg_ref[...])

    h2b = h2.astype(BF16)
    logits = _dot(h2b, wr_ref[...]) + br_ref[...]
    sort_previous(0, LOCAL_ROWS)

    lane = lax.broadcasted_iota(I32, (tm, LANES), 1)
    neg = jnp.float32(-jnp.inf)
    big = jnp.int32(1 << 30)
    is_g = lane < N_GROUPS
    gl = jnp.where(is_g, logits, neg)
    gmax = jnp.max(gl, axis=-1, keepdims=True)
    g_idx = jnp.min(jnp.where(gl == gmax, lane, big), axis=-1, keepdims=True)
    g_w = 1.0 / jnp.sum(jnp.where(is_g, jnp.exp(gl - gmax), 0.0), axis=-1, keepdims=True)

    ex = lane - N_GROUPS
    in_grp = (ex >= g_idx * EXPERTS_PER_GROUP) & (ex < (g_idx + 1) * EXPERTS_PER_GROUP)
    el = jnp.where(in_grp, logits, neg)
    m1 = jnp.max(el, axis=-1, keepdims=True)
    e1 = jnp.min(jnp.where(el == m1, ex, big), axis=-1, keepdims=True)
    el2 = jnp.where(ex == e1, neg, el)
    m2 = jnp.max(el2, axis=-1, keepdims=True)
    e2 = jnp.min(jnp.where(el2 == m2, ex, big), axis=-1, keepdims=True)
    p2 = jnp.exp(m2 - m1)
    w1 = g_w / (1.0 + p2)
    w2 = g_w * p2 / (1.0 + p2)

    oh1 = ex == e1
    oh2 = ex == e2
    oh = jnp.where(oh1 | oh2, 1.0, 0.0)
    ri = lax.broadcasted_iota(I32, (tm, tm), 0)
    ci = lax.broadcasted_iota(I32, (tm, tm), 1)
    strict = jnp.where(ri > ci, 1.0, 0.0).astype(BF16)
    local_rank = _dot(strict, oh.astype(BF16))
    cnt = jnp.sum(oh, axis=0, keepdims=True)
    run_chunks = jnp.floor((cnt + (ROW_CHUNK - 1.0)) * (1.0 / ROW_CHUNK))
    ui = lax.broadcasted_iota(I32, (LANES, LANES), 0)
    uj = lax.broadcasted_iota(I32, (LANES, LANES), 1)
    before = jnp.where(ui < uj, 1.0, 0.0).astype(BF16)
    run_start = _dot(jnp.broadcast_to(run_chunks, (8, LANES)).astype(BF16), before)[0:1] * ROW_CHUNK
    slot = run_start + local_rank
    pos1 = jnp.sum(jnp.where(oh1, slot, 0.0), axis=-1, keepdims=True)
    pos2 = jnp.sum(jnp.where(oh2, slot, 0.0), axis=-1, keepdims=True)
    cnt_ref[0] = jnp.broadcast_to(cnt, (8, LANES))

    route = jnp.where(lane == 0, e1.astype(F32), 0.0)
    route = jnp.where(lane == 1, e2.astype(F32), route)
    route = jnp.where(lane == 2, w1, route)
    route = jnp.where(lane == 3, w2, route)
    route = jnp.where(lane == 4, pos1, route)
    route = jnp.where(lane == 5, pos2, route)
    route_ref[...] = route

    hi1 = jnp.floor(pos1 * (1.0 / 256.0))
    hi2 = jnp.floor(pos2 * (1.0 / 256.0))
    digits = jnp.where(lane == 0, hi1, 0.0)
    digits = jnp.where(lane == 1, pos1 - 256.0 * hi1, digits)
    digits = jnp.where(lane == 2, hi2, digits)
    digits = jnp.where(lane == 3, pos2 - 256.0 * hi2, digits)
    pick = jnp.where(lax.broadcasted_iota(I32, (8, LANES), 0) == lax.broadcasted_iota(I32, (8, LANES), 1),
                     1.0, 0.0).astype(BF16)
    rows = _dot_nt(pick, digits.astype(BF16))

    h2b_s[...] = h2b
    rows_s[0:1, :] = rows[0:1] * 256.0 + rows[1:2]
    rows_s[1:2, :] = rows[2:3] * 256.0 + rows[3:4]


def _merge(x2d, ya, yb, proj, wa, wb, wo, ffn_g, w_router, b_router):
    t = x2d.shape[0]
    nb = t // TOK_TILE
    const = lambda *shape: pl.BlockSpec(shape, lambda i: (0,) * len(shape))
    cur = lambda i: jnp.minimum(i, nb - 1)
    prev = lambda i: jnp.maximum(i - 1, 0)
    return pl.pallas_call(
        _merge_kernel,
        grid=(nb + 1,),
        in_specs=[
            pl.BlockSpec((TOK_TILE, D_MODEL), lambda i: (cur(i), 0)),
            pl.BlockSpec((TOK_TILE, HG_WIDTH), lambda i: (cur(i), 0)),
            pl.BlockSpec((TOK_TILE, D_MODEL), lambda i: (cur(i), 0)),
            pl.BlockSpec((TOK_TILE, D_MODEL), lambda i: (cur(i), COL_GA)),
            pl.BlockSpec((TOK_TILE, D_MODEL), lambda i: (cur(i), COL_GB)),
            const(HG_WIDTH, D_MODEL),
            const(D_MODEL, D_MODEL),
            const(D_MODEL, D_MODEL),
            const(1, D_MODEL),
            const(D_MODEL, LANES),
            const(1, LANES),
        ],
        out_specs=[
            pl.BlockSpec((TOK_TILE, D_MODEL), lambda i: (cur(i), 0)),
            pl.BlockSpec((LOCAL_ROWS, D_MODEL), lambda i: (prev(i), 0)),
            pl.BlockSpec((TOK_TILE, LANES), lambda i: (cur(i), 0)),
            pl.BlockSpec((1, 8, LANES), lambda i: (cur(i), 0, 0)),
        ],
        out_shape=[
            jax.ShapeDtypeStruct((t, D_MODEL), F32),
            jax.ShapeDtypeStruct((t // TOK_TILE * LOCAL_ROWS, D_MODEL), BF16),
            jax.ShapeDtypeStruct((t, LANES), F32),
            jax.ShapeDtypeStruct((t // TOK_TILE, 8, LANES), F32),
        ],
        scratch_shapes=[pltpu.VMEM((TOK_TILE, D_MODEL), BF16), pltpu.VMEM((8, TOK_TILE), F32)],
        compiler_params=_cparams(("arbitrary",)),
        name="merge_route",
    )(x2d, ya, yb, proj, proj, wa, wb, wo, ffn_g, w_router, b_router)


def _expert_kernel(te_ref, nt_ref, src_ref, wslot_ref, wnext_ref, xs_ref, wg_ref, wu_ref, wd_ref,
                   ys_ref, xbuf, wg_s, wu_s, wd_s, sem, wsem):
    i = pl.program_id(0)
    nt = nt_ref[0]

    def fetch_weights(e, slot):
        return [pltpu.make_async_copy(w_ref.at[e], w_s.at[slot], wsem.at[slot])
                for w_ref, w_s in ((wg_ref, wg_s), (wu_ref, wu_s), (wd_ref, wd_s))]

    first_tile = (i == 0) | (te_ref[i] != te_ref[jnp.maximum(i - 1, 0)])

    @pl.when(i == 0)
    def _():
        for cp in fetch_weights(te_ref[0], wslot_ref[0]):
            cp.start()

    @pl.when((i < nt) & first_tile)
    def _():
        for cp in fetch_weights(te_ref[i], wslot_ref[i]):
            cp.wait()

        @pl.when(wnext_ref[i] >= 0)
        def _():
            for cp in fetch_weights(wnext_ref[i], 1 - wslot_ref[i]):
                cp.start()

    def gather(tile, slot):
        copies = []
        for c in range(CHUNKS_PER_TILE):
            row = pl.multiple_of(src_ref[tile * CHUNKS_PER_TILE + c] * ROW_CHUNK, ROW_CHUNK)
            copies.append(pltpu.make_async_copy(
                xs_ref.at[pl.ds(row, ROW_CHUNK)],
                xbuf.at[slot, pl.ds(c * ROW_CHUNK, ROW_CHUNK)], sem.at[slot]))
        return copies

    @pl.when(i == 0)
    def _():
        for k in range(EXP_AHEAD):
            for cp in gather(k, k):
                cp.start()

    @pl.when(i < nt + EXP_AHEAD)
    def _():
        for cp in gather(i, i % EXP_SLOTS):
            cp.wait()

    @pl.when(i < nt)
    def _():
        slot = i % EXP_SLOTS
        ws = wslot_ref[i]
        wg, wu, wd = wg_s[ws], wu_s[ws], wd_s[ws]
        subs = [pl.ds(r, EXP_SUB) for r in range(0, EXP_TILE, EXP_SUB)]
        xs = [xbuf[slot, s, :] for s in subs]
        gates = [(_dot(x, wg), _dot(x, wu)) for x in xs]
        for c, cp in enumerate(gather(i + EXP_AHEAD, (i + EXP_AHEAD) % EXP_SLOTS)):
            cp.start(priority=c % 2)
        hidden = [(_silu(a) * u).astype(BF16) for a, u in gates]
        for s, h in zip(subs, hidden):
            ys_ref[s, :] = _dot(h, wd).astype(ys_ref.dtype)

    @pl.when(i >= nt)
    def _():
        ys_ref[...] = jnp.zeros_like(ys_ref)


def _experts(tile_expert, n_tiles_used, src_chunk, xs, wg, wu, wd, n_tiles):
    ids = jnp.arange(n_tiles, dtype=I32)
    first = (ids == 0) | (tile_expert != jnp.concatenate([tile_expert[:1], tile_expert[:-1]]))
    wslot = (jnp.sum((first[None, :] & (ids[None, :] <= ids[:, None])).astype(I32), axis=1) - 1) % 2
    later_first = first[None, :] & (ids[None, :] > ids[:, None]) & (ids[None, :] < n_tiles_used[0])
    nxt = jnp.min(jnp.where(later_first, ids[None, :], n_tiles), axis=1)
    wnext = jnp.sum(jnp.where(ids[None, :] == nxt[:, None], tile_expert[None, :], 0), axis=1)
    wnext = jnp.where(nxt < n_tiles, wnext, -1).astype(I32)

    hbm = pl.BlockSpec(memory_space=pl.ANY)
    grid_spec = pltpu.PrefetchScalarGridSpec(
        num_scalar_prefetch=5,
        grid=(n_tiles,),
        in_specs=[hbm, hbm, hbm, hbm],
        out_specs=pl.BlockSpec((EXP_TILE, D_MODEL), lambda i, *_: (i, 0)),
        scratch_shapes=[
            pltpu.VMEM((EXP_SLOTS, EXP_TILE, D_MODEL), BF16),
            pltpu.VMEM((2, D_MODEL, D_EXPERT), BF16),
            pltpu.VMEM((2, D_MODEL, D_EXPERT), BF16),
            pltpu.VMEM((2, D_EXPERT, D_MODEL), BF16),
            pltpu.SemaphoreType.DMA((EXP_SLOTS,)),
            pltpu.SemaphoreType.DMA((2,)),
        ],
    )
    return pl.pallas_call(
        _expert_kernel,
        grid_spec=grid_spec,
        out_shape=jax.ShapeDtypeStruct((n_tiles * EXP_TILE, D_MODEL), BF16),
        compiler_params=_cparams(("arbitrary",)),
        name="experts",
    )(tile_expert, n_tiles_used, src_chunk, wslot.astype(I32), wnext, xs, wg, wu, wd)


def _final_kernel(src_ref, x1_ref, route_ref, p_ref, ys_ref, pg_ref, wpg_ref, wpp_ref, fg_ref,
                  o_ref, ybuf, sem):
    tm = x1_ref.shape[0]
    i = pl.program_id(0)
    nb = pl.num_programs(0)

    def gather(block, slot):
        copies = []
        for c in range(LOCAL_CHUNKS):
            row = pl.multiple_of(src_ref[block * LOCAL_CHUNKS + c] * ROW_CHUNK, ROW_CHUNK)
            copies.append(pltpu.make_async_copy(
                ys_ref.at[pl.ds(row, ROW_CHUNK)],
                ybuf.at[slot, pl.ds(c * ROW_CHUNK, ROW_CHUNK)], sem.at[slot]))
        return copies

    @pl.when(i == 0)
    def _():
        for k in range(FIN_AHEAD):
            for cp in gather(k % nb, k):
                cp.start()

    slot = i % FIN_SLOTS
    for cp in gather(i, slot):
        cp.wait()

    parts = [pl.ds(r, tm // FIN_PARTS) for r in range(0, tm, tm // FIN_PARTS)]
    route = route_ref[...]
    ple = [_dot(p_ref[rows, :].astype(BF16), wpp_ref[...]) for rows in parts]

    slab_col = lax.broadcasted_iota(I32, (tm // FIN_PARTS, SORT_SLAB), 1).astype(F32)
    moes = []
    for k in range(FIN_PARTS):
        rows = slice(k * (tm // FIN_PARTS), (k + 1) * (tm // FIN_PARTS))
        w1, w2 = route[rows, 2:3], route[rows, 3:4]
        pos1, pos2 = route[rows, 4:5], route[rows, 5:6]
        moe = jnp.zeros((tm // FIN_PARTS, D_MODEL), F32)
        for k0 in range(0, LOCAL_ROWS, SORT_SLAB):
            sel = jnp.where(slab_col == pos1 - float(k0), w1,
                            jnp.where(slab_col == pos2 - float(k0), w2, 0.0)).astype(BF16)
            moe = moe + _dot(sel, ybuf[slot, k0:k0 + SORT_SLAB, :])
        moes.append(moe)
    for c, cp in enumerate(gather((i + FIN_AHEAD) % nb, (i + FIN_AHEAD) % FIN_SLOTS)):
        cp.start(priority=c % 2)
    x2s = [x1_ref[rows, :] + moe for rows, moe in zip(parts, moes)]
    hps = [_rms(x2, pg_ref[...]).astype(BF16) for x2 in x2s]
    gates = [_sigmoid(_dot(hp, wpg_ref[...])) for hp in hps]
    for rows, x2, gate, pp in zip(parts, x2s, gates, ple):
        o_ref[rows, :] = _rms(x2 + gate * pp, fg_ref[...])

    @pl.when(i == nb - 1)
    def _():
        for k in range(1, FIN_AHEAD + 1):
            for cp in gather(0, (i + k) % FIN_SLOTS):
                cp.wait()


def _final(src_chunk, x1, route, p2d, ys, ple_g, wpg, wpp, final_g):
    t = x1.shape[0]
    const = lambda *shape: pl.BlockSpec(shape, lambda i, src: (0,) * len(shape))
    grid_spec = pltpu.PrefetchScalarGridSpec(
        num_scalar_prefetch=1,
        grid=(t // TOK_TILE,),
        in_specs=[
            pl.BlockSpec((TOK_TILE, D_MODEL), lambda i, src: (i, 0)),
            pl.BlockSpec((TOK_TILE, LANES), lambda i, src: (i, 0)),
            pl.BlockSpec((TOK_TILE, PLE_DIM), lambda i, src: (i, 0)),
            pl.BlockSpec(memory_space=pl.ANY),
            const(1, D_MODEL),
            const(D_MODEL, D_MODEL),
            const(PLE_DIM, D_MODEL),
            const(1, D_MODEL),
        ],
        out_specs=pl.BlockSpec((TOK_TILE, D_MODEL), lambda i, src: (i, 0)),
        scratch_shapes=[pltpu.VMEM((FIN_SLOTS, LOCAL_ROWS, D_MODEL), BF16),
                        pltpu.SemaphoreType.DMA((FIN_SLOTS,))],
    )
    return pl.pallas_call(
        _final_kernel,
        grid_spec=grid_spec,
        out_shape=jax.ShapeDtypeStruct((t, D_MODEL), F32),
        compiler_params=_cparams(("arbitrary",)),
        name="combine_ple_final",
    )(src_chunk, x1, route, p2d, ys, ple_g, wpg, wpp, final_g)


def _rotary_tables(seq):
    inv = ROPE_BASE ** (-jnp.arange(0, RET_DK, 2, dtype=F32) / RET_DK)
    inv = jnp.concatenate([inv, inv])
    sign = jnp.where(jnp.arange(RET_DK) < RET_DK // 2, -1.0, 1.0).astype(F32)
    hi = (jnp.arange(seq // ROPE_SPLIT, dtype=F32) * ROPE_SPLIT)[:, None] * inv[None, :]
    lo = jnp.arange(ROPE_SPLIT, dtype=F32)[:, None] * inv[None, :]
    ch, sh = jnp.cos(hi)[:, None, :], jnp.sin(hi)[:, None, :]
    cl, sl = jnp.cos(lo)[None], jnp.sin(lo)[None]
    cos = (ch * cl - sh * sl).reshape(seq, RET_DK)
    sin = ((sh * cl + ch * sl) * sign).reshape(seq, RET_DK)
    return cos, sin


def _layer(x2d, p2d, batch, seq, mix_norm, w_in, hg_lb_logits, hg_norm, ret_norm, w_branch_a,
           w_branch_b, w_out, ffn_norm, w_rg, b_rg, w_re, b_re, w_gate, w_up, w_down, ple_norm,
           w_ple_gate, w_ple_proj, out_gain):
    t = x2d.shape[0]
    row = lambda v: v.reshape(1, -1).astype(F32)

    proj, hf, wg16, wu16, wd16 = _inproj(x2d, row(mix_norm), w_in.astype(BF16), w_gate, w_up, w_down)
    cos2, sin2 = _rotary_tables(seq)
    ya, yb = _mixers(proj, hf, hg_lb_logits.astype(F32), row(hg_norm), cos2, sin2, row(ret_norm), batch, seq)

    n_r = N_GROUPS + N_EXPERTS
    w_router = jnp.pad(jnp.concatenate([w_rg, w_re], axis=1).astype(BF16), ((0, 0), (0, LANES - n_r)))
    b_router = jnp.zeros((1, LANES), F32).at[0, :n_r].set(jnp.concatenate([b_rg, b_re]))
    x1, xs, route, counts = _merge(
        x2d, ya, yb, proj, w_branch_a.astype(BF16), w_branch_b.astype(BF16), w_out.astype(BF16),
        row(ffn_norm), w_router, b_router)

    n_blocks = t // TOK_TILE
    cnt = counts[:, 0, N_GROUPS:N_GROUPS + N_EXPERTS].astype(I32)
    run_chunks = (cnt + ROW_CHUNK - 1) // ROW_CHUNK
    earlier_e = jnp.tril(jnp.ones((N_EXPERTS, N_EXPERTS), I32), -1)
    earlier_b = jnp.tril(jnp.ones((n_blocks, n_blocks), I32), -1)
    run_local = jnp.sum(run_chunks[:, None, :] * earlier_e[None], axis=2)
    seg_chunks = jnp.sum(run_chunks, axis=0)
    tiles_per = (seg_chunks + CHUNKS_PER_TILE - 1) // CHUNKS_PER_TILE
    seg_start = jnp.sum(tiles_per[None, :] * earlier_e, axis=1) * CHUNKS_PER_TILE
    tile_end = seg_start // CHUNKS_PER_TILE + tiles_per
    run_global = seg_start[None, :] + jnp.sum(run_chunks.T[:, None, :] * earlier_b[None], axis=2).T
    max_chunks = (2 * t) // ROW_CHUNK + n_blocks * N_EXPERTS + N_EXPERTS * (CHUNKS_PER_TILE - 1)
    n_tiles = -(-max_chunks // CHUNKS_PER_TILE) + EXP_AHEAD
    tile_ids = jnp.arange(n_tiles, dtype=I32)
    tile_expert = jnp.minimum(jnp.sum((tile_end[None, :] <= tile_ids[:, None]).astype(I32), axis=1),
                              N_EXPERTS - 1)
    n_used = tile_end[-1:].astype(I32)

    block_ids = jnp.arange(n_blocks, dtype=I32)
    zero_local = LOCAL_CHUNKS - 1
    zero_global = n_tiles * CHUNKS_PER_TILE - 1
    g = jnp.arange(n_tiles * CHUNKS_PER_TILE, dtype=I32)[:, None]
    pick_e = (tile_expert[:, None] == jnp.arange(N_EXPERTS, dtype=I32)[None, :]).astype(I32)
    rg, rc, rl = (jnp.repeat(jnp.sum(pick_e[:, :, None] * tab.T[None, :, :], axis=1),
                             CHUNKS_PER_TILE, axis=0)
                  for tab in (run_global, run_chunks, run_local))
    inside = (rg <= g) & (g < rg + rc)
    gather_src = jnp.sum(jnp.where(inside, block_ids[None, :] * LOCAL_CHUNKS + rl + (g - rg), 0), axis=1)
    gather_src = jnp.where(jnp.any(inside, axis=1), gather_src, zero_local).astype(I32)

    lc = jnp.arange(LOCAL_CHUNKS, dtype=I32)[None, :, None]
    inside = (run_local[:, None, :] <= lc) & (lc < (run_local + run_chunks)[:, None, :])
    back_src = jnp.sum(jnp.where(inside, run_global[:, None, :] + lc - run_local[:, None, :], 0), axis=2)
    back_src = jnp.where(jnp.any(inside, axis=2), back_src, zero_global).astype(I32).reshape(-1)

    ys = _experts(tile_expert, n_used, gather_src, xs, wg16, wu16, wd16, n_tiles)
    return _final(back_src, x1, route, p2d, ys, row(ple_norm), w_ple_gate.astype(BF16),
                  w_ple_proj.astype(BF16), out_gain)


def kernel(x, p, mix_norm, w_in, hg_lb_logits, hg_norm, ret_norm, w_branch_a, w_branch_b, w_out,
           ffn_norm, w_router_group, b_router_group, w_router_expert, b_router_expert,
           w_expert_gate, w_expert_up, w_expert_down, ple_norm, w_ple_gate, w_ple_proj, final_norm):
    batch, seq, d = x.shape
    depth = p.shape[0]
    assert depth == 1, "the final rmsnorm is fused into the single layer"
    x2d = x.reshape(batch * seq, d)
    out = _layer(x2d, p[0].reshape(batch * seq, -1), batch, seq, mix_norm[0], w_in[0], hg_lb_logits,
                 hg_norm[0], ret_norm[0], w_branch_a[0], w_branch_b[0], w_out[0], ffn_norm[0],
                 w_router_group[0], b_router_group[0], w_router_expert[0], b_router_expert[0],
                 w_expert_gate[0], w_expert_up[0], w_expert_down[0], ple_norm[0], w_ple_gate[0],
                 w_ple_proj[0], final_norm.reshape(1, -1).astype(F32))
    return out.reshape(batch, seq, d)
```

```python
import jax
import jax.numpy as jnp
from jax import lax
from jax.experimental import pallas as pl
from jax.experimental.pallas import tpu as pltpu

F32 = jnp.float32
BF16 = jnp.bfloat16
I32 = jnp.int32

EPS = 1e-6
D_MODEL = 1024
PLE_DIM = 256
HG_HEADS = 4
HG_DK = 128
HG_WIDTH = HG_HEADS * HG_DK
RET_HEADS = 4
RET_DK = 128
RET_DV = 256
ROPE_BASE = 10000.0
ROPE_SPLIT = 64
IN_TOTAL = 7168
N_GROUPS = 4
EXPERTS_PER_GROUP = 8
N_EXPERTS = 32
D_EXPERT = 256

COL_HQ, COL_HF, COL_HI, COL_HG = 0, 4, 8, 12
COL_RQ, COL_RK = 16, 20
COL_RV, COL_RG = 24, 32
COL_GA, COL_GB = 5, 6

LANES = 128
VMEM_LIMIT = 56 * 1024 * 1024

HG_CHUNK = 64
HG_SUB = 8
HG_UNROLL = 16
HG_STATE_UNROLL = 16
HG_NORM_ROWS = 1024
RET_CHUNK = 128
RET_STATE_UNROLL = 8
SEQ_TILE = 1024
TOK_TILE = 512
EXP_TILE = 512
EXP_SUB = 512
EXP_AHEAD = 5
EXP_SLOTS = EXP_AHEAD + 1
ROW_CHUNK = 16
CHUNKS_PER_TILE = EXP_TILE // ROW_CHUNK
LOCAL_ROWS = 2 * TOK_TILE + N_EXPERTS * ROW_CHUNK
LOCAL_CHUNKS = LOCAL_ROWS // ROW_CHUNK
SORT_SLAB = 256
MERGE_SLAB = LOCAL_ROWS
FIN_AHEAD = 2
FIN_SLOTS = FIN_AHEAD + 1
FIN_PARTS = 2


def _cparams(sem):
    return pltpu.CompilerParams(dimension_semantics=sem, vmem_limit_bytes=VMEM_LIMIT)


def _rms(x, g):
    return x * lax.rsqrt(jnp.mean(x * x, axis=-1, keepdims=True) + EPS) * g


def _sigmoid(x):
    return 1.0 / (1.0 + jnp.exp(-x))


def _silu(x):
    return x * _sigmoid(x)


def _split3(x):
    hi = x.astype(BF16)
    r1 = x - hi.astype(F32)
    mid = r1.astype(BF16)
    lo = (r1 - mid.astype(F32)).astype(BF16)
    return hi, mid, lo


def _dot(a, b):
    return jnp.dot(a, b, preferred_element_type=F32)


def _dot_nt(a, b):
    return lax.dot_general(a, b, (((1,), (1,)), ((), ())), preferred_element_type=F32)


def _dot_tn(a, b):
    return lax.dot_general(a, b, (((0,), (0,)), ((), ())), preferred_element_type=F32)


def _inproj_kernel(x_ref, g_ref, w_ref, wg_ref, wu_ref, wd_ref, proj_ref, hf_ref,
                   wg16_ref, wu16_ref, wd16_ref):
    wg16_ref[...] = wg_ref[...].astype(BF16)
    wu16_ref[...] = wu_ref[...].astype(BF16)
    wd16_ref[...] = wd_ref[...].astype(BF16)

    h = _rms(x_ref[...], g_ref[...]).astype(BF16)
    tn = HG_WIDTH
    for j in range(IN_TOTAL // tn):
        acc = _dot(h, w_ref[:, j * tn:(j + 1) * tn])
        proj_ref[:, j * tn:(j + 1) * tn] = acc.astype(BF16)
        if j * tn == COL_HF * LANES:
            hf_ref[...] = acc


def _inproj(x2d, gain, w_bf16, w_gate, w_up, w_down):
    t = x2d.shape[0]
    steps = t // TOK_TILE
    assert N_EXPERTS % steps == 0, "expert weights are converted in equal shares per grid step"
    share = N_EXPERTS // steps
    up_spec = pl.BlockSpec((share, D_MODEL, D_EXPERT), lambda i: (i, 0, 0))
    down_spec = pl.BlockSpec((share, D_EXPERT, D_MODEL), lambda i: (i, 0, 0))
    return pl.pallas_call(
        _inproj_kernel,
        grid=(steps,),
        in_specs=[
            pl.BlockSpec((TOK_TILE, D_MODEL), lambda i: (i, 0)),
            pl.BlockSpec((1, D_MODEL), lambda i: (0, 0)),
            pl.BlockSpec((D_MODEL, IN_TOTAL), lambda i: (0, 0), pipeline_mode=pl.Buffered(1)),
            up_spec, up_spec, down_spec,
        ],
        out_specs=[
            pl.BlockSpec((TOK_TILE, IN_TOTAL), lambda i: (i, 0)),
            pl.BlockSpec((TOK_TILE, HG_WIDTH), lambda i: (i, 0)),
            up_spec, up_spec, down_spec,
        ],
        out_shape=[
            jax.ShapeDtypeStruct((t, IN_TOTAL), BF16),
            jax.ShapeDtypeStruct((t, HG_WIDTH), F32),
            jax.ShapeDtypeStruct(w_gate.shape, BF16),
            jax.ShapeDtypeStruct(w_up.shape, BF16),
            jax.ShapeDtypeStruct(w_down.shape, BF16),
        ],
        compiler_params=_cparams(("arbitrary",)),
        name="inproj",
    )(x2d, gain, w_bf16, w_gate, w_up, w_down)


def _hgrn_stages(lbl_ref, q_ref, f_ref, i_ref, g_ref, ng_ref, o_ref,
                 st_ref, b_s, k_s, v_s, oi_s, qe_s, kv_s, dec_s):
    c = HG_CHUNK
    nsub = c // HG_SUB

    logits = lbl_ref[...]
    e = jnp.exp(logits - jnp.max(logits, axis=0, keepdims=True))
    lb = e[0:1] / jnp.sum(e, axis=0, keepdims=True)
    one_m_lb = jnp.sum(e[1:], axis=0, keepdims=True) / jnp.sum(e, axis=0, keepdims=True)
    ng = ng_ref[...]

    row = lax.broadcasted_iota(I32, (c, c), 0)
    col = lax.broadcasted_iota(I32, (c, c), 1)
    tri = jnp.where(row >= col, 1.0, 0.0).astype(BF16)
    sub_row = lax.broadcasted_iota(I32, (HG_SUB, HG_DK), 0)
    masked = jnp.float32(-1e30)

    def bcast_row(ref, r, rows):
        return jnp.broadcast_to(ref[pl.ds(r, 1), :], (rows, HG_DK))

    def prep(ci, slot):
        r0 = pl.multiple_of(ci * c, c)
        z = f_ref[pl.ds(r0, c), :]
        ez = jnp.exp(-jnp.abs(z))
        rz = 1.0 / (1.0 + ez)
        pos = z >= 0.0
        logf = jnp.log2(lb + one_m_lb * jnp.where(pos, rz, ez * rz))
        kk = one_m_lb * jnp.where(pos, ez * rz, rz)
        q = _silu(q_ref[pl.ds(r0, c), :].astype(F32))
        v = i_ref[pl.ds(r0, c), :].astype(F32)
        k_s[slot] = kk
        v_s[slot] = v
        return dict(ci=ci, r0=r0, slot=slot, kk=kk, q=q, v=v, v16=v.astype(BF16), parts=_split3(logf))

    def cumulate(s):
        hi, mid, lo = s.pop("parts")
        b = (_dot(tri, lo) + _dot(tri, mid)) + _dot(tri, hi)
        b_s[s["slot"]] = b
        qe_s[pl.ds(s["r0"], c), :] = (s["q"] * jnp.exp2(b)).astype(BF16)
        s["b"] = b
        return s

    def off_diagonal(s):
        b, q, kk = s["b"], s["q"], s["kk"]
        bs_ref = b_s.at[s["slot"]]
        squares = []
        size = c // 2
        while size >= HG_SUB:
            for r0 in range(size, c, 2 * size):
                edge = bcast_row(bs_ref, r0 - 1, size)
                qs = q[r0:r0 + size, :] * jnp.exp2(b[r0:r0 + size, :] - edge)
                ks = kk[r0 - size:r0, :] * jnp.exp2(edge - b[r0 - size:r0, :])
                squares.append((r0, size, _dot_nt(qs.astype(BF16), ks.astype(BF16))))
            size //= 2
        s["squares"] = squares
        return s

    def apply_values(s):
        b, kk, v = s["b"], s["kk"], s["v"]
        blast = b[c - 1:c, :]
        rows = [jnp.zeros((HG_SUB, HG_DK), F32) for _ in range(nsub)]
        for r0, size, a in s.pop("squares"):
            part = _dot(a.astype(BF16), v[r0 - size:r0, :].astype(BF16))
            for i in range(size // HG_SUB):
                rows[r0 // HG_SUB + i] = rows[r0 // HG_SUB + i] + part[i * HG_SUB:(i + 1) * HG_SUB, :]
        s["o"] = jnp.concatenate(rows, axis=0)
        kd = kk * jnp.exp2(blast - b)
        kv_s[s["ci"]] = _dot_tn(s["v16"], kd.astype(BF16))
        dec_s[s["ci"]] = jnp.broadcast_to(jnp.exp2(blast), (HG_SUB, HG_DK))
        return s

    def diagonal(s):
        b, q = s["b"], s["q"]
        bs_ref, ks_ref, vs_ref = b_s.at[s["slot"]], k_s.at[s["slot"]], v_s.at[s["slot"]]
        d_blocks = []
        for i in range(nsub):
            sl = slice(i * HG_SUB, (i + 1) * HG_SUB)
            bt, qt = b[sl, :], q[sl, :]
            acc = jnp.zeros((HG_SUB, HG_DK), F32)
            for j in range(HG_SUB):
                r = i * HG_SUB + j
                arg = bt - bcast_row(bs_ref, r, HG_SUB)
                if j > 0:
                    arg = jnp.where(sub_row >= j, arg, masked)
                g = jnp.exp2(arg) * (qt * bcast_row(ks_ref, r, HG_SUB))
                acc = acc + jnp.sum(g, axis=-1, keepdims=True) * bcast_row(vs_ref, r, HG_SUB)
            d_blocks.append(acc)
        oi_s[pl.ds(s["r0"], c), :] = s["o"] + jnp.concatenate(d_blocks, axis=0)

    def carried_group(gi):
        st = st_ref[...]
        outs = []
        for j in range(HG_STATE_UNROLL):
            ci = gi * HG_STATE_UNROLL + j
            r0 = pl.multiple_of(ci * c, c)
            outs.append((r0, _dot_nt(qe_s[pl.ds(r0, c), :], st.astype(BF16))))
            st = st * dec_s[ci][0:1, :] + kv_s[ci]
        st_ref[...] = st
        for r0, os in outs:
            oi_s[pl.ds(r0, c), :] += os

    def finish(ri):
        r0 = pl.multiple_of(ri * HG_NORM_ROWS, HG_NORM_ROWS)
        y = _rms(oi_s[pl.ds(r0, HG_NORM_ROWS), :], ng) * _silu(g_ref[pl.ds(r0, HG_NORM_ROWS), :].astype(F32))
        o_ref[pl.ds(r0, HG_NORM_ROWS), :] = y.astype(o_ref.dtype)

    return prep, (cumulate, off_diagonal, apply_values, diagonal), carried_group, finish


HG_SCRATCH = [
    pltpu.VMEM((HG_DK, HG_DK), F32),
    pltpu.VMEM((HG_UNROLL, HG_CHUNK, HG_DK), F32),
    pltpu.VMEM((HG_UNROLL, HG_CHUNK, HG_DK), F32),
    pltpu.VMEM((HG_UNROLL, HG_CHUNK, HG_DK), F32),
    pltpu.VMEM((SEQ_TILE, HG_DK), F32),
    pltpu.VMEM((SEQ_TILE, HG_DK), BF16),
    pltpu.VMEM((SEQ_TILE // HG_CHUNK, HG_DK, HG_DK), F32),
    pltpu.VMEM((SEQ_TILE // HG_CHUNK, HG_SUB, HG_DK), F32),
]


def _ret_stages(q_ref, k_ref, v_ref, g_ref, cos_ref, sin_ref, ng_ref, o_ref,
                r_ref, oi_s, q16_s, kv_s):
    c = RET_CHUNK

    hf = jnp.full((1, 1), pl.program_id(1), I32).astype(F32)
    lg = jnp.log1p(-jnp.exp2(-5.0 - hf))
    ti = lax.broadcasted_iota(I32, (c, c), 0)
    si = lax.broadcasted_iota(I32, (c, c), 1)
    rel = (ti - si).astype(F32)
    intra = jnp.where(ti >= si, jnp.exp(jnp.maximum(rel, 0.0) * lg), 0.0)
    idx = lax.broadcasted_iota(I32, (c, 1), 0).astype(F32)
    inter = jnp.exp((idx + 1.0) * lg)
    to_state = jnp.exp((c - 1.0 - idx) * lg)
    chunk_decay = jnp.exp(float(c) * lg)
    ng = ng_ref[...]
    half = RET_DK // 2

    def prep(ci):
        r0 = pl.multiple_of(ci * c, c)
        cos = cos_ref[pl.ds(r0, c), :]
        sin = sin_ref[pl.ds(r0, c), :]
        q = q_ref[pl.ds(r0, c), :].astype(F32)
        k = k_ref[pl.ds(r0, c), :].astype(F32)
        qr = (q * cos + pltpu.roll(q, half, 1) * sin) * (RET_DK ** -0.5)
        kr = k * cos + pltpu.roll(k, half, 1) * sin
        q16 = qr.astype(BF16)
        q16_s[pl.ds(r0, c), :] = q16
        return dict(ci=ci, r0=r0, q16=q16, k16=kr.astype(BF16), kts=(kr * to_state).astype(BF16),
                    v16=v_ref[pl.ds(r0, c), :])

    def scores(s):
        s["att"] = (_dot_nt(s.pop("q16"), s.pop("k16")) * intra).astype(BF16)
        return s

    def apply_values(s):
        oi_s[pl.ds(s["r0"], c), :] = _dot(s["att"], s["v16"])
        kv_s[s["ci"]] = _dot_tn(s["kts"], s["v16"])
        return s

    def carried_group(gi):
        r = r_ref[...]
        outs = []
        for j in range(RET_STATE_UNROLL):
            ci = gi * RET_STATE_UNROLL + j
            r0 = pl.multiple_of(ci * c, c)
            outs.append((r0, _dot(q16_s[pl.ds(r0, c), :], r.astype(BF16))))
            r = chunk_decay * r + kv_s[ci]
        r_ref[...] = r
        for r0, qr_state in outs:
            o = oi_s[pl.ds(r0, c), :] + qr_state * inter
            y = _rms(o, ng) * _silu(g_ref[pl.ds(r0, c), :].astype(F32))
            o_ref[pl.ds(r0, c), :] = y.astype(o_ref.dtype)

    return prep, (scores, apply_values), carried_group


RET_SCRATCH = [
    pltpu.VMEM((RET_DK, RET_DV), F32),
    pltpu.VMEM((SEQ_TILE, RET_DV), F32),
    pltpu.VMEM((SEQ_TILE, RET_DK), BF16),
    pltpu.VMEM((SEQ_TILE // RET_CHUNK, RET_DK, RET_DV), F32),
]


def _mixers_kernel(*refs):
    hg_refs = refs[:6] + refs[13:14] + refs[15:15 + len(HG_SCRATCH)]
    ret_refs = refs[6:13] + refs[14:15] + refs[15 + len(HG_SCRATCH):]
    st_ref, r_ref = hg_refs[7], ret_refs[8]
    n_tokens = refs[1].shape[0]

    @pl.when(pl.program_id(2) == 0)
    def _():
        st_ref[...] = jnp.zeros_like(st_ref)
        r_ref[...] = jnp.zeros_like(r_ref)

    hg_prep, hg_local, hg_carried, hg_finish = _hgrn_stages(*hg_refs)
    ret_prep, ret_local, ret_carried = _ret_stages(*ret_refs)
    group_tokens = HG_UNROLL * HG_CHUNK
    ret_unroll = group_tokens // RET_CHUNK

    def local_group(gi, carry):
        hs = [hg_prep(gi * HG_UNROLL + slot, slot) for slot in range(HG_UNROLL)]
        rs = [ret_prep(gi * ret_unroll + j) for j in range(ret_unroll)]
        cumulate, off_diagonal, apply_values, diagonal = hg_local
        scores, ret_apply = ret_local
        hs = [cumulate(s) for s in hs]
        rs = [scores(s) for s in rs]
        hs = [off_diagonal(s) for s in hs]
        rs = [ret_apply(s) for s in rs]
        hs = [apply_values(s) for s in hs]
        for s in hs:
            diagonal(s)
        return carry

    lax.fori_loop(0, n_tokens // group_tokens, local_group, 0)

    state_tokens = HG_STATE_UNROLL * HG_CHUNK
    assert state_tokens == RET_STATE_UNROLL * RET_CHUNK

    def carried_group(gi, carry):
        hg_carried(gi)
        ret_carried(gi)
        return carry

    lax.fori_loop(0, n_tokens // state_tokens, carried_group, 0)

    def finish(ri, carry):
        hg_finish(ri)
        return carry

    lax.fori_loop(0, n_tokens // HG_NORM_ROWS, finish, 0)


def _mixers(proj, hf, lb_logits, hg_norm, cos2, sin2, ret_norm, batch, seq):
    ns = seq // SEQ_TILE
    tok = lambda b, h, s: b * ns + s
    hg_col = lambda base: pl.BlockSpec((SEQ_TILE, HG_DK), lambda b, h, s: (tok(b, h, s), base + h))
    return pl.pallas_call(
        _mixers_kernel,
        grid=(batch, HG_HEADS, ns),
        in_specs=[
            pl.BlockSpec((2, HG_DK), lambda b, h, s: (0, h)),
            hg_col(COL_HQ), hg_col(0), hg_col(COL_HI), hg_col(COL_HG),
            pl.BlockSpec((1, HG_DK), lambda b, h, s: (0, 0)),
            hg_col(COL_RQ), hg_col(COL_RK),
            pl.BlockSpec((SEQ_TILE, RET_DV), lambda b, h, s: (tok(b, h, s), COL_RV // 2 + h)),
            pl.BlockSpec((SEQ_TILE, RET_DV), lambda b, h, s: (tok(b, h, s), COL_RG // 2 + h)),
            pl.BlockSpec((SEQ_TILE, RET_DK), lambda b, h, s: (s, 0)),
            pl.BlockSpec((SEQ_TILE, RET_DK), lambda b, h, s: (s, 0)),
            pl.BlockSpec((1, RET_DV), lambda b, h, s: (0, 0)),
        ],
        out_specs=[
            pl.BlockSpec((SEQ_TILE, HG_DK), lambda b, h, s: (tok(b, h, s), h)),
            pl.BlockSpec((SEQ_TILE, RET_DV), lambda b, h, s: (tok(b, h, s), h)),
        ],
        out_shape=[
            jax.ShapeDtypeStruct((batch * seq, HG_WIDTH), BF16),
            jax.ShapeDtypeStruct((batch * seq, RET_HEADS * RET_DV), BF16),
        ],
        scratch_shapes=HG_SCRATCH + RET_SCRATCH,
        compiler_params=_cparams(("arbitrary", "arbitrary", "arbitrary")),
        name="mixers",
    )(lb_logits, proj, hf, proj, proj, hg_norm, proj, proj, proj, proj, cos2, sin2, ret_norm)


def _merge_kernel(x_ref, ya_ref, yb_ref, ga_ref, gb_ref, wa_ref, wb_ref, wo_ref, fg_ref,
                  wr_ref, br_ref, x1_ref, xs_ref, route_ref, cnt_ref, h2b_s, rows_s):
    tm = x_ref.shape[0]

    @pl.when(pl.program_id(0) == 0)
    def _():
        h2b_s[...] = jnp.zeros_like(h2b_s)
        rows_s[...] = jnp.full(rows_s.shape, -1.0, F32)

    def sort_previous(lo, hi):
        slab_row = lax.broadcasted_iota(I32, (MERGE_SLAB, tm), 0).astype(F32)
        for r0 in range(lo, hi, MERGE_SLAB):
            sel = ((slab_row == rows_s[0:1, :] - float(r0)) | (slab_row == rows_s[1:2, :] - float(r0)))
            xs_ref[r0:r0 + MERGE_SLAB, :] = _dot(jnp.where(sel, 1.0, 0.0).astype(BF16),
                                                 h2b_s[...]).astype(BF16)

    merged = (_sigmoid(ga_ref[...].astype(F32)) * _dot(ya_ref[...], wa_ref[...])
              + _sigmoid(gb_ref[...].astype(F32)) * _dot(yb_ref[...], wb_ref[...]))
    x1 = x_ref[...] + _dot(merged.astype(BF16), wo_ref[...])
    x1_ref[...] = x1
    h2 = _rms(x1, fg_ref[...])

    h2b = h2.astype(BF16)
    logits = _dot(h2b, wr_ref[...]) + br_ref[...]
    sort_previous(0, LOCAL_ROWS)

    lane = lax.broadcasted_iota(I32, (tm, LANES), 1)
    neg = jnp.float32(-jnp.inf)
    big = jnp.int32(1 << 30)
    is_g = lane < N_GROUPS
    gl = jnp.where(is_g, logits, neg)
    gmax = jnp.max(gl, axis=-1, keepdims=True)
    g_idx = jnp.min(jnp.where(gl == gmax, lane, big), axis=-1, keepdims=True)
    g_w = 1.0 / jnp.sum(jnp.where(is_g, jnp.exp(gl - gmax), 0.0), axis=-1, keepdims=True)

    ex = lane - N_GROUPS
    in_grp = (ex >= g_idx * EXPERTS_PER_GROUP) & (ex < (g_idx + 1) * EXPERTS_PER_GROUP)
    el = jnp.where(in_grp, logits, neg)
    m1 = jnp.max(el, axis=-1, keepdims=True)
    e1 = jnp.min(jnp.where(el == m1, ex, big), axis=-1, keepdims=True)
    el2 = jnp.where(ex == e1, neg, el)
    m2 = jnp.max(el2, axis=-1, keepdims=True)
    e2 = jnp.min(jnp.where(el2 == m2, ex, big), axis=-1, keepdims=True)
    p2 = jnp.exp(m2 - m1)
    w1 = g_w / (1.0 + p2)
    w2 = g_w * p2 / (1.0 + p2)

    oh1 = ex == e1
    oh2 = ex == e2
    oh = jnp.where(oh1 | oh2, 1.0, 0.0)
    ri = lax.broadcasted_iota(I32, (tm, tm), 0)
    ci = lax.broadcasted_iota(I32, (tm, tm), 1)
    strict = jnp.where(ri > ci, 1.0, 0.0).astype(BF16)
    local_rank = _dot(strict, oh.astype(BF16))
    cnt = jnp.sum(oh, axis=0, keepdims=True)
    run_chunks = jnp.floor((cnt + (ROW_CHUNK - 1.0)) * (1.0 / ROW_CHUNK))
    ui = lax.broadcasted_iota(I32, (LANES, LANES), 0)
    uj = lax.broadcasted_iota(I32, (LANES, LANES), 1)
    before = jnp.where(ui < uj, 1.0, 0.0).astype(BF16)
    run_start = _dot(jnp.broadcast_to(run_chunks, (8, LANES)).astype(BF16), before)[0:1] * ROW_CHUNK
    slot = run_start + local_rank
    pos1 = jnp.sum(jnp.where(oh1, slot, 0.0), axis=-1, keepdims=True)
    pos2 = jnp.sum(jnp.where(oh2, slot, 0.0), axis=-1, keepdims=True)
    cnt_ref[0] = jnp.broadcast_to(cnt, (8, LANES))

    route = jnp.where(lane == 0, e1.astype(F32), 0.0)
    route = jnp.where(lane == 1, e2.astype(F32), route)
    route = jnp.where(lane == 2, w1, route)
    route = jnp.where(lane == 3, w2, route)
    route = jnp.where(lane == 4, pos1, route)
    route = jnp.where(lane == 5, pos2, route)
    route_ref[...] = route

    hi1 = jnp.floor(pos1 * (1.0 / 256.0))
    hi2 = jnp.floor(pos2 * (1.0 / 256.0))
    digits = jnp.where(lane == 0, hi1, 0.0)
    digits = jnp.where(lane == 1, pos1 - 256.0 * hi1, digits)
    digits = jnp.where(lane == 2, hi2, digits)
    digits = jnp.where(lane == 3, pos2 - 256.0 * hi2, digits)
    pick = jnp.where(lax.broadcasted_iota(I32, (8, LANES), 0) == lax.broadcasted_iota(I32, (8, LANES), 1),
                     1.0, 0.0).astype(BF16)
    rows = _dot_nt(pick, digits.astype(BF16))

    h2b_s[...] = h2b
    rows_s[0:1, :] = rows[0:1] * 256.0 + rows[1:2]
    rows_s[1:2, :] = rows[2:3] * 256.0 + rows[3:4]


def _merge(x2d, ya, yb, proj, wa, wb, wo, ffn_g, w_router, b_router):
    t = x2d.shape[0]
    nb = t // TOK_TILE
    const = lambda *shape: pl.BlockSpec(shape, lambda i: (0,) * len(shape))
    cur = lambda i: jnp.minimum(i, nb - 1)
    prev = lambda i: jnp.maximum(i - 1, 0)
    return pl.pallas_call(
        _merge_kernel,
        grid=(nb + 1,),
        in_specs=[
            pl.BlockSpec((TOK_TILE, D_MODEL), lambda i: (cur(i), 0)),
            pl.BlockSpec((TOK_TILE, HG_WIDTH), lambda i: (cur(i), 0)),
            pl.BlockSpec((TOK_TILE, D_MODEL), lambda i: (cur(i), 0)),
            pl.BlockSpec((TOK_TILE, D_MODEL), lambda i: (cur(i), COL_GA)),
            pl.BlockSpec((TOK_TILE, D_MODEL), lambda i: (cur(i), COL_GB)),
            const(HG_WIDTH, D_MODEL),
            const(D_MODEL, D_MODEL),
            const(D_MODEL, D_MODEL),
            const(1, D_MODEL),
            const(D_MODEL, LANES),
            const(1, LANES),
        ],
        out_specs=[
            pl.BlockSpec((TOK_TILE, D_MODEL), lambda i: (cur(i), 0)),
            pl.BlockSpec((LOCAL_ROWS, D_MODEL), lambda i: (prev(i), 0)),
            pl.BlockSpec((TOK_TILE, LANES), lambda i: (cur(i), 0)),
            pl.BlockSpec((1, 8, LANES), lambda i: (cur(i), 0, 0)),
        ],
        out_shape=[
            jax.ShapeDtypeStruct((t, D_MODEL), F32),
            jax.ShapeDtypeStruct((t // TOK_TILE * LOCAL_ROWS, D_MODEL), BF16),
            jax.ShapeDtypeStruct((t, LANES), F32),
            jax.ShapeDtypeStruct((t // TOK_TILE, 8, LANES), F32),
        ],
        scratch_shapes=[pltpu.VMEM((TOK_TILE, D_MODEL), BF16), pltpu.VMEM((8, TOK_TILE), F32)],
        compiler_params=_cparams(("arbitrary",)),
        name="merge_route",
    )(x2d, ya, yb, proj, proj, wa, wb, wo, ffn_g, w_router, b_router)


def _expert_kernel(te_ref, nt_ref, src_ref, wslot_ref, wnext_ref, xs_ref, wg_ref, wu_ref, wd_ref,
                   ys_ref, xbuf, wg_s, wu_s, wd_s, sem, wsem):
    i = pl.program_id(0)
    nt = nt_ref[0]

    def fetch_weights(e, slot):
        return [pltpu.make_async_copy(w_ref.at[e], w_s.at[slot], wsem.at[slot])
                for w_ref, w_s in ((wg_ref, wg_s), (wu_ref, wu_s), (wd_ref, wd_s))]

    first_tile = (i == 0) | (te_ref[i] != te_ref[jnp.maximum(i - 1, 0)])

    @pl.when(i == 0)
    def _():
        for cp in fetch_weights(te_ref[0], wslot_ref[0]):
            cp.start()

    @pl.when((i < nt) & first_tile)
    def _():
        for cp in fetch_weights(te_ref[i], wslot_ref[i]):
            cp.wait()

        @pl.when(wnext_ref[i] >= 0)
        def _():
            for cp in fetch_weights(wnext_ref[i], 1 - wslot_ref[i]):
                cp.start()

    def gather(tile, slot):
        copies = []
        for c in range(CHUNKS_PER_TILE):
            row = pl.multiple_of(src_ref[tile * CHUNKS_PER_TILE + c] * ROW_CHUNK, ROW_CHUNK)
            copies.append(pltpu.make_async_copy(
                xs_ref.at[pl.ds(row, ROW_CHUNK)],
                xbuf.at[slot, pl.ds(c * ROW_CHUNK, ROW_CHUNK)], sem.at[slot]))
        return copies

    @pl.when(i == 0)
    def _():
        for k in range(EXP_AHEAD):
            for cp in gather(k, k):
                cp.start()

    @pl.when(i < nt + EXP_AHEAD)
    def _():
        for cp in gather(i, i % EXP_SLOTS):
            cp.wait()

    @pl.when(i < nt)
    def _():
        slot = i % EXP_SLOTS
        ws = wslot_ref[i]
        wg, wu, wd = wg_s[ws], wu_s[ws], wd_s[ws]
        subs = [pl.ds(r, EXP_SUB) for r in range(0, EXP_TILE, EXP_SUB)]
        xs = [xbuf[slot, s, :] for s in subs]
        gates = [(_dot(x, wg), _dot(x, wu)) for x in xs]
        for c, cp in enumerate(gather(i + EXP_AHEAD, (i + EXP_AHEAD) % EXP_SLOTS)):
            cp.start(priority=c % 2)
        hidden = [(_silu(a) * u).astype(BF16) for a, u in gates]
        for s, h in zip(subs, hidden):
            ys_ref[s, :] = _dot(h, wd).astype(ys_ref.dtype)

    @pl.when(i >= nt)
    def _():
        ys_ref[...] = jnp.zeros_like(ys_ref)


def _experts(tile_expert, n_tiles_used, src_chunk, xs, wg, wu, wd, n_tiles):
    ids = jnp.arange(n_tiles, dtype=I32)
    first = (ids == 0) | (tile_expert != jnp.concatenate([tile_expert[:1], tile_expert[:-1]]))
    wslot = (jnp.sum((first[None, :] & (ids[None, :] <= ids[:, None])).astype(I32), axis=1) - 1) % 2
    later_first = first[None, :] & (ids[None, :] > ids[:, None]) & (ids[None, :] < n_tiles_used[0])
    nxt = jnp.min(jnp.where(later_first, ids[None, :], n_tiles), axis=1)
    wnext = jnp.sum(jnp.where(ids[None, :] == nxt[:, None], tile_expert[None, :], 0), axis=1)
    wnext = jnp.where(nxt < n_tiles, wnext, -1).astype(I32)

    hbm = pl.BlockSpec(memory_space=pl.ANY)
    grid_spec = pltpu.PrefetchScalarGridSpec(
        num_scalar_prefetch=5,
        grid=(n_tiles,),
        in_specs=[hbm, hbm, hbm, hbm],
        out_specs=pl.BlockSpec((EXP_TILE, D_MODEL), lambda i, *_: (i, 0)),
        scratch_shapes=[
            pltpu.VMEM((EXP_SLOTS, EXP_TILE, D_MODEL), BF16),
            pltpu.VMEM((2, D_MODEL, D_EXPERT), BF16),
            pltpu.VMEM((2, D_MODEL, D_EXPERT), BF16),
            pltpu.VMEM((2, D_EXPERT, D_MODEL), BF16),
            pltpu.SemaphoreType.DMA((EXP_SLOTS,)),
            pltpu.SemaphoreType.DMA((2,)),
        ],
    )
    return pl.pallas_call(
        _expert_kernel,
        grid_spec=grid_spec,
        out_shape=jax.ShapeDtypeStruct((n_tiles * EXP_TILE, D_MODEL), BF16),
        compiler_params=_cparams(("arbitrary",)),
        name="experts",
    )(tile_expert, n_tiles_used, src_chunk, wslot.astype(I32), wnext, xs, wg, wu, wd)


def _final_kernel(src_ref, x1_ref, route_ref, p_ref, ys_ref, pg_ref, wpg_ref, wpp_ref, fg_ref,
                  o_ref, ybuf, sem):
    tm = x1_ref.shape[0]
    i = pl.program_id(0)
    nb = pl.num_programs(0)

    def gather(block, slot):
        copies = []
        for c in range(LOCAL_CHUNKS):
            row = pl.multiple_of(src_ref[block * LOCAL_CHUNKS + c] * ROW_CHUNK, ROW_CHUNK)
            copies.append(pltpu.make_async_copy(
                ys_ref.at[pl.ds(row, ROW_CHUNK)],
                ybuf.at[slot, pl.ds(c * ROW_CHUNK, ROW_CHUNK)], sem.at[slot]))
        return copies

    @pl.when(i == 0)
    def _():
        for k in range(FIN_AHEAD):
            for cp in gather(k % nb, k):
                cp.start()

    slot = i % FIN_SLOTS
    for cp in gather(i, slot):
        cp.wait()

    parts = [pl.ds(r, tm // FIN_PARTS) for r in range(0, tm, tm // FIN_PARTS)]
    route = route_ref[...]
    ple = [_dot(p_ref[rows, :].astype(BF16), wpp_ref[...]) for rows in parts]

    slab_col = lax.broadcasted_iota(I32, (tm // FIN_PARTS, SORT_SLAB), 1).astype(F32)
    moes = []
    for k in range(FIN_PARTS):
        rows = slice(k * (tm // FIN_PARTS), (k + 1) * (tm // FIN_PARTS))
        w1, w2 = route[rows, 2:3], route[rows, 3:4]
        pos1, pos2 = route[rows, 4:5], route[rows, 5:6]
        moe = jnp.zeros((tm // FIN_PARTS, D_MODEL), F32)
        for k0 in range(0, LOCAL_ROWS, SORT_SLAB):
            sel = jnp.where(slab_col == pos1 - float(k0), w1,
                            jnp.where(slab_col == pos2 - float(k0), w2, 0.0)).astype(BF16)
            moe = moe + _dot(sel, ybuf[slot, k0:k0 + SORT_SLAB, :])
        moes.append(moe)
    for c, cp in enumerate(gather((i + FIN_AHEAD) % nb, (i + FIN_AHEAD) % FIN_SLOTS)):
        cp.start(priority=c % 2)
    x2s = [x1_ref[rows, :] + moe for rows, moe in zip(parts, moes)]
    hps = [_rms(x2, pg_ref[...]).astype(BF16) for x2 in x2s]
    gates = [_sigmoid(_dot(hp, wpg_ref[...])) for hp in hps]
    for rows, x2, gate, pp in zip(parts, x2s, gates, ple):
        o_ref[rows, :] = _rms(x2 + gate * pp, fg_ref[...])

    @pl.when(i == nb - 1)
    def _():
        for k in range(1, FIN_AHEAD + 1):
            for cp in gather(0, (i + k) % FIN_SLOTS):
                cp.wait()


def _final(src_chunk, x1, route, p2d, ys, ple_g, wpg, wpp, final_g):
    t = x1.shape[0]
    const = lambda *shape: pl.BlockSpec(shape, lambda i, src: (0,) * len(shape))
    grid_spec = pltpu.PrefetchScalarGridSpec(
        num_scalar_prefetch=1,
        grid=(t // TOK_TILE,),
        in_specs=[
            pl.BlockSpec((TOK_TILE, D_MODEL), lambda i, src: (i, 0)),
            pl.BlockSpec((TOK_TILE, LANES), lambda i, src: (i, 0)),
            pl.BlockSpec((TOK_TILE, PLE_DIM), lambda i, src: (i, 0)),
            pl.BlockSpec(memory_space=pl.ANY),
            const(1, D_MODEL),
            const(D_MODEL, D_MODEL),
            const(PLE_DIM, D_MODEL),
            const(1, D_MODEL),
        ],
        out_specs=pl.BlockSpec((TOK_TILE, D_MODEL), lambda i, src: (i, 0)),
        scratch_shapes=[pltpu.VMEM((FIN_SLOTS, LOCAL_ROWS, D_MODEL), BF16),
                        pltpu.SemaphoreType.DMA((FIN_SLOTS,))],
    )
    return pl.pallas_call(
        _final_kernel,
        grid_spec=grid_spec,
        out_shape=jax.ShapeDtypeStruct((t, D_MODEL), F32),
        compiler_params=_cparams(("arbitrary",)),
        name="combine_ple_final",
    )(src_chunk, x1, route, p2d, ys, ple_g, wpg, wpp, final_g)


def _rotary_tables(seq):
    inv = ROPE_BASE ** (-jnp.arange(0, RET_DK, 2, dtype=F32) / RET_DK)
    inv = jnp.concatenate([inv, inv])
    sign = jnp.where(jnp.arange(RET_DK) < RET_DK // 2, -1.0, 1.0).astype(F32)
    hi = (jnp.arange(seq // ROPE_SPLIT, dtype=F32) * ROPE_SPLIT)[:, None] * inv[None, :]
    lo = jnp.arange(ROPE_SPLIT, dtype=F32)[:, None] * inv[None, :]
    ch, sh = jnp.cos(hi)[:, None, :], jnp.sin(hi)[:, None, :]
    cl, sl = jnp.cos(lo)[None], jnp.sin(lo)[None]
    cos = (ch * cl - sh * sl).reshape(seq, RET_DK)
    sin = ((sh * cl + ch * sl) * sign).reshape(seq, RET_DK)
    return cos, sin


def _layer(x2d, p2d, batch, seq, mix_norm, w_in, hg_lb_logits, hg_norm, ret_norm, w_branch_a,
           w_branch_b, w_out, ffn_norm, w_rg, b_rg, w_re, b_re, w_gate, w_up, w_down, ple_norm,
           w_ple_gate, w_ple_proj, out_gain):
    t = x2d.shape[0]
    row = lambda v: v.reshape(1, -1).astype(F32)

    proj, hf, wg16, wu16, wd16 = _inproj(x2d, row(mix_norm), w_in.astype(BF16), w_gate, w_up, w_down)
    cos2, sin2 = _rotary_tables(seq)
    ya, yb = _mixers(proj, hf, hg_lb_logits.astype(F32), row(hg_norm), cos2, sin2, row(ret_norm), batch, seq)

    n_r = N_GROUPS + N_EXPERTS
    w_router = jnp.pad(jnp.concatenate([w_rg, w_re], axis=1).astype(BF16), ((0, 0), (0, LANES - n_r)))
    b_router = jnp.zeros((1, LANES), F32).at[0, :n_r].set(jnp.concatenate([b_rg, b_re]))
    x1, xs, route, counts = _merge(
        x2d, ya, yb, proj, w_branch_a.astype(BF16), w_branch_b.astype(BF16), w_out.astype(BF16),
        row(ffn_norm), w_router, b_router)

    n_blocks = t // TOK_TILE
    cnt = counts[:, 0, N_GROUPS:N_GROUPS + N_EXPERTS].astype(I32)
    run_chunks = (cnt + ROW_CHUNK - 1) // ROW_CHUNK
    earlier_e = jnp.tril(jnp.ones((N_EXPERTS, N_EXPERTS), I32), -1)
    earlier_b = jnp.tril(jnp.ones((n_blocks, n_blocks), I32), -1)
    run_local = jnp.sum(run_chunks[:, None, :] * earlier_e[None], axis=2)
    seg_chunks = jnp.sum(run_chunks, axis=0)
    tiles_per = (seg_chunks + CHUNKS_PER_TILE - 1) // CHUNKS_PER_TILE
    seg_start = jnp.sum(tiles_per[None, :] * earlier_e, axis=1) * CHUNKS_PER_TILE
    tile_end = seg_start // CHUNKS_PER_TILE + tiles_per
    run_global = seg_start[None, :] + jnp.sum(run_chunks.T[:, None, :] * earlier_b[None], axis=2).T
    max_chunks = (2 * t) // ROW_CHUNK + n_blocks * N_EXPERTS + N_EXPERTS * (CHUNKS_PER_TILE - 1)
    n_tiles = -(-max_chunks // CHUNKS_PER_TILE) + EXP_AHEAD
    tile_ids = jnp.arange(n_tiles, dtype=I32)
    tile_expert = jnp.minimum(jnp.sum((tile_end[None, :] <= tile_ids[:, None]).astype(I32), axis=1),
                              N_EXPERTS - 1)
    n_used = tile_end[-1:].astype(I32)

    block_ids = jnp.arange(n_blocks, dtype=I32)
    zero_local = LOCAL_CHUNKS - 1
    zero_global = n_tiles * CHUNKS_PER_TILE - 1
    g = jnp.arange(n_tiles * CHUNKS_PER_TILE, dtype=I32)[:, None]
    pick_e = (tile_expert[:, None] == jnp.arange(N_EXPERTS, dtype=I32)[None, :]).astype(I32)
    rg, rc, rl = (jnp.repeat(jnp.sum(pick_e[:, :, None] * tab.T[None, :, :], axis=1),
                             CHUNKS_PER_TILE, axis=0)
                  for tab in (run_global, run_chunks, run_local))
    inside = (rg <= g) & (g < rg + rc)
    gather_src = jnp.sum(jnp.where(inside, block_ids[None, :] * LOCAL_CHUNKS + rl + (g - rg), 0), axis=1)
    gather_src = jnp.where(jnp.any(inside, axis=1), gather_src, zero_local).astype(I32)

    lc = jnp.arange(LOCAL_CHUNKS, dtype=I32)[None, :, None]
    inside = (run_local[:, None, :] <= lc) & (lc < (run_local + run_chunks)[:, None, :])
    back_src = jnp.sum(jnp.where(inside, run_global[:, None, :] + lc - run_local[:, None, :], 0), axis=2)
    back_src = jnp.where(jnp.any(inside, axis=2), back_src, zero_global).astype(I32).reshape(-1)

    ys = _experts(tile_expert, n_used, gather_src, xs, wg16, wu16, wd16, n_tiles)
    return _final(back_src, x1, route, p2d, ys, row(ple_norm), w_ple_gate.astype(BF16),
                  w_ple_proj.astype(BF16), out_gain)


def kernel(x, p, mix_norm, w_in, hg_lb_logits, hg_norm, ret_norm, w_branch_a, w_branch_b, w_out,
           ffn_norm, w_router_group, b_router_group, w_router_expert, b_router_expert,
           w_expert_gate, w_expert_up, w_expert_down, ple_norm, w_ple_gate, w_ple_proj, final_norm):
    batch, seq, d = x.shape
    depth = p.shape[0]
    assert depth == 1, "the final rmsnorm is fused into the single layer"
    x2d = x.reshape(batch * seq, d)
    out = _layer(x2d, p[0].reshape(batch * seq, -1), batch, seq, mix_norm[0], w_in[0], hg_lb_logits,
                 hg_norm[0], ret_norm[0], w_branch_a[0], w_branch_b[0], w_out[0], ffn_norm[0],
                 w_router_group[0], b_router_group[0], w_router_expert[0], b_router_expert[0],
                 w_expert_gate[0], w_expert_up[0], w_expert_down[0], ple_norm[0], w_ple_gate[0],
                 w_ple_proj[0], final_norm.reshape(1, -1).astype(F32))
    return out.reshape(batch, seq, d)
```
